```python
import math
import jax, jax.numpy as jnp
from jax import lax
import numpy as np

D_MODEL = 1024
BATCH = 16
SEQ = 2048
DEPTH = 1

SB_HEADS = 8
SB_HEAD_DIM = 64
SB_WIDTH = SB_HEADS * SB_HEAD_DIM
SB_BLOCK = 128
GDN_HEADS = 4
GDN_HEAD_DIM = 128
GDN_WIDTH = GDN_HEADS * GDN_HEAD_DIM
GDN_CONV = 4
GDN_CHUNK = 64
D_MIX = SB_WIDTH + GDN_WIDTH
D_IN = 3 * SB_WIDTH + 4 * GDN_WIDTH + 2 * GDN_HEADS
PEER_HEADS = 8
PEER_NKEYS = 128
PEER_EXPERTS = PEER_NKEYS * PEER_NKEYS
PEER_DKEY = 256
PEER_HALF = PEER_DKEY // 2
PEER_TOPK = 16
PEER_TOKEN_BLOCK = 128
EPS = 1e-6

kernel_name = 'hybrid_stickbreak_gdn_peer'


def rms_norm(x, g):
    xf = x.astype(jnp.float32)
    y = xf * lax.rsqrt(jnp.mean(xf * xf, axis=-1, keepdims=True) + EPS)
    return (y * g.astype(jnp.float32)).astype(x.dtype)


def l2_normalize(x):
    return x * lax.rsqrt(jnp.sum(x * x, axis=-1, keepdims=True) + EPS)


def stick_breaking_attention(q, k, v):
    dtype = v.dtype
    q, k, v = (t.astype(jnp.float32) for t in (q, k, v))
    seq, head_dim = q.shape[2], q.shape[3]
    scale = head_dim ** -0.5
    outs = []
    for blk in range(seq // SB_BLOCK):
        start, end = blk * SB_BLOCK, (blk + 1) * SB_BLOCK
        z = jnp.einsum('bhqd,bhkd->bhqk', q[:, :, start:end], k[:, :, :end]) * scale
        causal = jnp.arange(end)[None, :] < (start + jnp.arange(SB_BLOCK))[:, None]
        log_keep = jnp.where(causal, jax.nn.log_sigmoid(-z), 0.0)
        rest = lax.cumsum(log_keep, axis=3, reverse=True) - log_keep
        weights = jnp.where(causal, jnp.exp(jax.nn.log_sigmoid(z) + rest), 0.0)
        outs.append(jnp.einsum('bhqk,bhkd->bhqd', weights, v[:, :, :end]))
    return jnp.concatenate(outs, axis=2).astype(dtype)


def causal_depthwise_conv(x, w):
    width, seq = w.shape[0], x.shape[1]
    xp = jnp.pad(x, ((0, 0), (width - 1, 0), (0, 0)))
    return sum(w[i] * xp[:, i:i + seq] for i in range(width))


def gated_delta_rule_chunked(q, k, v, g, beta):
    bsz, heads, seq, dk = q.shape
    dv = v.shape[-1]
    c = GDN_CHUNK
    n = seq // c
    q, k, v = (t.reshape(bsz, heads, n, c, -1) for t in (q, k, v))
    g = g.reshape(bsz, heads, n, c)
    beta = beta.reshape(bsz, heads, n, c)
    gc = lax.cumsum(g, axis=3)
    lower_incl = jnp.tril(jnp.ones((c, c), dtype=bool))
    lower_strict = jnp.tril(jnp.ones((c, c), dtype=bool), -1)
    diff = gc[..., :, None] - gc[..., None, :]
    decay_in = jnp.exp(jnp.where(lower_incl, diff, -jnp.inf))
    k_beta = k * beta[..., None]
    a_mat = jnp.where(lower_strict, jnp.einsum('bhnid,bhnjd->bhnij', k_beta, k) * decay_in, 0.0)
    eye = jnp.eye(c, dtype=q.dtype)
    t_inv = lax.linalg.triangular_solve(a_mat + eye, jnp.broadcast_to(eye, a_mat.shape),
                                        left_side=True, lower=True, unit_diagonal=True)
    u = jnp.einsum('bhnij,bhnjd->bhnid', t_inv, v * beta[..., None])
    w = jnp.einsum('bhnij,bhnjd->bhnid', t_inv, k_beta * jnp.exp(gc)[..., None])
    attn_intra = jnp.where(lower_incl, jnp.einsum('bhnid,bhnjd->bhnij', q, k) * decay_in, 0.0)
    q_dec = q * jnp.exp(gc)[..., None]
    k_dec = k * jnp.exp(gc[..., -1:] - gc)[..., None]
    chunk_decay = jnp.exp(gc[..., -1])

    def step(state, inp):
        q_c, k_c, u_c, w_c, a_c, d_c = inp
        v_new = u_c - jnp.einsum('bhcd,bhde->bhce', w_c, state)
        o_c = jnp.einsum('bhcd,bhde->bhce', q_c, state) + jnp.einsum('bhij,bhje->bhie', a_c, v_new)
        state = state * d_c[..., None, None] + jnp.einsum('bhcd,bhce->bhde', k_c, v_new)
        return state, o_c

    xs = tuple(jnp.moveaxis(t, 2, 0) for t in (q_dec, k_dec, u, w, attn_intra, chunk_decay))
    state0 = jnp.zeros((bsz, heads, dk, dv), dtype=q.dtype)
    _, o = lax.scan(step, state0, xs)
    return jnp.moveaxis(o, 0, 2).reshape(bsz, heads, seq, dv)


def gated_deltanet(qkv, z, a, b, conv_w, a_log, dt_bias, out_g):
    bsz, seq, _ = qkv.shape
    dtype = qkv.dtype
    f32 = jnp.float32
    qkv = jax.nn.silu(causal_depthwise_conv(qkv.astype(f32), conv_w.astype(f32)))
    qkv = qkv.reshape(bsz, seq, 3, GDN_HEADS, GDN_HEAD_DIM).transpose(2, 0, 3, 1, 4)
    q = l2_normalize(qkv[0]) * GDN_HEAD_DIM ** -0.5
    k = l2_normalize(qkv[1])
    v = qkv[2]
    beta = jax.nn.sigmoid(b.astype(f32)).transpose(0, 2, 1)
    g = (-jnp.exp(a_log.astype(f32)) * jax.nn.softplus(a.astype(f32) + dt_bias.astype(f32))).transpose(0, 2, 1)
    o = gated_delta_rule_chunked(q, k, v, g, beta)
    o = rms_norm(o.transpose(0, 2, 1, 3), out_g) * jax.nn.silu(
        z.astype(f32).reshape(bsz, seq, GDN_HEADS, GDN_HEAD_DIM))
    return o.reshape(bsz, seq, GDN_WIDTH).astype(dtype)


def peer_ffn(h, w_q, keys1, keys2, u_tab, v_tab):
    bsz, seq, d = h.shape
    tokens = bsz * seq
    hf = h.reshape(tokens, d)
    q = (hf @ w_q).astype(jnp.float32).reshape(tokens, PEER_HEADS, 2, PEER_HALF)
    s1 = jnp.einsum('thd,hnd->thn', q[:, :, 0], keys1.astype(jnp.float32))
    s2 = jnp.einsum('thd,hnd->thn', q[:, :, 1], keys2.astype(jnp.float32))
    v1, i1 = lax.top_k(s1, PEER_TOPK)
    v2, i2 = lax.top_k(s2, PEER_TOPK)
    cand_score = (v1[..., :, None] + v2[..., None, :]).reshape(tokens, PEER_HEADS, PEER_TOPK * PEER_TOPK)
    cand_id = (i1[..., :, None] * PEER_NKEYS + i2[..., None, :]).reshape(tokens, PEER_HEADS, PEER_TOPK * PEER_TOPK)
    top_score, pos = lax.top_k(cand_score, PEER_TOPK)
    expert_id = jnp.take_along_axis(cand_id, pos, axis=-1)
    gate = jax.nn.softmax(top_score, axis=-1)
    n_blocks = tokens // PEER_TOKEN_BLOCK
    sel = PEER_HEADS * PEER_TOPK
    xs = (hf.reshape(n_blocks, PEER_TOKEN_BLOCK, d),
          expert_id.reshape(n_blocks, PEER_TOKEN_BLOCK, sel),
          gate.reshape(n_blocks, PEER_TOKEN_BLOCK, sel))

    def block(args):
        xb, idb, gb = args
        u_sel = u_tab[idb].astype(jnp.float32)
        act = jax.nn.gelu(jnp.einsum('tkd,td->tk', u_sel, xb.astype(jnp.float32)), approximate=False)
        return jnp.einsum('tk,tkd->td', gb * act, v_tab[idb].astype(jnp.float32))

    y = lax.map(block, xs)
    return y.reshape(bsz, seq, d).astype(h.dtype)


def hybrid_layer(x, norm1_g, w_in, sb_q_norm_g, sb_k_norm_g, sb_out_norm_g, gdn_conv_w, gdn_a_log,
                 gdn_dt_bias, gdn_out_norm_g, w_out, norm2_g, peer_w_q, peer_keys1, peer_keys2,
                 peer_u, peer_v):
    bsz, seq, _ = x.shape
    h = rms_norm(x, norm1_g)
    proj = h @ w_in
    o1 = 3 * SB_WIDTH
    o2 = o1 + 3 * GDN_WIDTH
    o3 = o2 + GDN_WIDTH
    sb = proj[..., :o1].reshape(bsz, seq, 3, SB_HEADS, SB_HEAD_DIM)
    gdn_qkv = proj[..., o1:o2]
    gdn_z = proj[..., o2:o3]
    gdn_a = proj[..., o3:o3 + GDN_HEADS]
    gdn_b = proj[..., o3 + GDN_HEADS:]
    q = rms_norm(sb[:, :, 0], sb_q_norm_g).transpose(0, 2, 1, 3)
    k = rms_norm(sb[:, :, 1], sb_k_norm_g).transpose(0, 2, 1, 3)
    v = sb[:, :, 2].transpose(0, 2, 1, 3)
    sb_o = stick_breaking_attention(q, k, v).transpose(0, 2, 1, 3)
    sb_o = rms_norm(sb_o, sb_out_norm_g).reshape(bsz, seq, SB_WIDTH)
    gdn_o = gated_deltanet(gdn_qkv, gdn_z, gdn_a, gdn_b, gdn_conv_w, gdn_a_log, gdn_dt_bias, gdn_out_norm_g)
    x = x + jnp.concatenate([sb_o, gdn_o], axis=-1) @ w_out
    x = x + peer_ffn(rms_norm(x, norm2_g), peer_w_q, peer_keys1, peer_keys2, peer_u, peer_v)
    return x


def setup_inputs(seed: int = 0) -> dict:
    key = jax.random.key(seed)
    ks = jax.random.split(key, 17)
    f32 = jnp.float32
    L = DEPTH

    def nrm(k, shape, scale):
        return jax.random.normal(k, shape, f32) * scale

    def gain(k, shape):
        return 1.0 + 0.02 * jax.random.normal(k, shape, f32)

    dt = jnp.exp(jax.random.uniform(ks[8], (L, GDN_HEADS), f32, math.log(1e-3), math.log(1e-1)))
    return {
        'x': jax.random.normal(ks[0], (BATCH, SEQ, D_MODEL), f32),
        'norm1_g': gain(ks[1], (L, D_MODEL)),
        'w_in': nrm(ks[2], (L, D_MODEL, D_IN), D_MODEL ** -0.5),
        'sb_q_norm_g': gain(ks[3], (L, SB_HEAD_DIM)),
        'sb_k_norm_g': gain(ks[4], (L, SB_HEAD_DIM)),
        'sb_out_norm_g': gain(ks[5], (L, SB_HEAD_DIM)),
        'gdn_conv_w': nrm(ks[6], (L, GDN_CONV, 3 * GDN_WIDTH), GDN_CONV ** -0.5),
        'gdn_a_log': jnp.log(jax.random.uniform(ks[7], (L, GDN_HEADS), f32, 1.0, 16.0)),
        'gdn_dt_bias': dt + jnp.log(-jnp.expm1(-dt)),
        'gdn_out_norm_g': gain(ks[9], (L, GDN_HEAD_DIM)),
        'w_out': nrm(ks[10], (L, D_MIX, D_MODEL), D_MIX ** -0.5),
        'norm2_g': gain(ks[11], (L, D_MODEL)),
        'peer_w_q': nrm(ks[12], (L, D_MODEL, PEER_HEADS * PEER_DKEY), D_MODEL ** -0.5),
        'peer_keys1': nrm(ks[13], (L, PEER_HEADS, PEER_NKEYS, PEER_HALF), PEER_HALF ** -0.5),
        'peer_keys2': nrm(ks[14], (L, PEER_HEADS, PEER_NKEYS, PEER_HALF), PEER_HALF ** -0.5),
        'peer_u': nrm(ks[15], (L, PEER_EXPERTS, D_MODEL), D_MODEL ** -0.5),
        'peer_v': nrm(ks[16], (L, PEER_EXPERTS, D_MODEL), PEER_TOPK ** -0.5),
    }


def reference(x, norm1_g, w_in, sb_q_norm_g, sb_k_norm_g, sb_out_norm_g, gdn_conv_w, gdn_a_log,
              gdn_dt_bias, gdn_out_norm_g, w_out, norm2_g, peer_w_q, peer_keys1, peer_keys2,
              peer_u, peer_v):
    for layer in range(DEPTH):
        x = hybrid_layer(x, norm1_g[layer], w_in[layer], sb_q_norm_g[layer], sb_k_norm_g[layer],
                         sb_out_norm_g[layer], gdn_conv_w[layer], gdn_a_log[layer], gdn_dt_bias[layer],
                         gdn_out_norm_g[layer], w_out[layer], norm2_g[layer], peer_w_q[layer],
                         peer_keys1[layer], peer_keys2[layer], peer_u[layer], peer_v[layer])
    return x
```

```python
import functools

import jax
import jax.numpy as jnp
from jax import lax
from jax.experimental import pallas as pl
from jax.experimental.pallas import tpu as pltpu

F32 = jnp.float32
BF16 = jnp.bfloat16
EPS = 1e-6

SB_HEADS = 8
SB_HEAD_DIM = 64
SB_WIDTH = SB_HEADS * SB_HEAD_DIM
GDN_HEADS = 4
GDN_HEAD_DIM = 128
GDN_WIDTH = GDN_HEADS * GDN_HEAD_DIM
GDN_CONV = 4
PEER_HEADS = 8
PEER_NKEYS = 128
PEER_HALF = 128
PEER_TOPK = 16
LANES = 128

VMEM_LIMIT = 56 * 1024 * 1024


def _cparams(sem):
    return pltpu.CompilerParams(dimension_semantics=sem, vmem_limit_bytes=VMEM_LIMIT)


def _dot(a, b):
    return jnp.dot(a, b, preferred_element_type=F32)


def _dot_nt(a, b):
    return lax.dot_general(a, b, (((1,), (1,)), ((), ())), preferred_element_type=F32)


def _dot_tn(a, b):
    return lax.dot_general(a, b, (((0,), (0,)), ((), ())), preferred_element_type=F32)


def _dot_f32(a, b):
    return jnp.dot(a, b, preferred_element_type=F32, precision=lax.Precision.HIGHEST)


def _split(a):
    hi = a.astype(BF16)
    return hi, (a - hi.astype(F32)).astype(BF16)


def _dot3(a, b):
    ah, al = _split(a)
    bh, bl = _split(b)
    return _dot(jnp.concatenate([ah, ah, al], axis=1), jnp.concatenate([bh, bl, bh], axis=0))


def _softplus(x):
    return jnp.maximum(x, 0.0) + jnp.log1p(jnp.exp(-jnp.abs(x)))


def _sigmoid(x):
    return 1.0 / (1.0 + jnp.exp(-x))


def _inproj_body(x_ref, g_ref, w_ref, wab_ref, proj_ref, ab_ref):
    x = x_ref[...]
    ms = jnp.mean(x * x, axis=-1, keepdims=True)
    h = (x * lax.rsqrt(ms + EPS) * g_ref[...]).astype(BF16)
    proj_ref[...] = _dot(h, w_ref[...])
    ab_ref[...] = _dot(h, wab_ref[...])


def _inproj(x2, g, w_main, w_ab, tm):
    t, d = x2.shape
    n = w_main.shape[1]
    return pl.pallas_call(
        _inproj_body,
        grid=(t // tm,),
        in_specs=[
            pl.BlockSpec((tm, d), lambda i: (i, 0)),
            pl.BlockSpec((1, d), lambda i: (0, 0)),
            pl.BlockSpec((d, n), lambda i: (0, 0)),
            pl.BlockSpec((d, LANES), lambda i: (0, 0)),
        ],
        out_specs=[
            pl.BlockSpec((tm, n), lambda i: (i, 0)),
            pl.BlockSpec((tm, LANES), lambda i: (i, 0)),
        ],
        out_shape=[
            jax.ShapeDtypeStruct((t, n), F32),
            jax.ShapeDtypeStruct((t, LANES), F32),
        ],
        compiler_params=_cparams(("parallel",)),
        name="inproj",
    )(x2, g, w_main, w_ab)


def _sb_body(q_ref, k_ref, v_ref, gq_ref, gk_ref, go_ref, m2_ref, o_ref, kn_scr, vb_scr, *, tq, tk):
    i = pl.program_id(2)
    lane = lax.broadcasted_iota(jnp.int32, (1, LANES), 1)
    is0 = lane < SB_HEAD_DIM

    def headnorm(x, g):
        x2 = x * x
        s0 = jnp.sum(jnp.where(is0, x2, 0.0), axis=-1, keepdims=True)
        s1 = jnp.sum(jnp.where(is0, 0.0, x2), axis=-1, keepdims=True)
        ms = jnp.where(is0, s0, s1) * (1.0 / SB_HEAD_DIM)
        return x * lax.rsqrt(ms + EPS) * g

    @pl.when(i == 0)
    def _():
        kn_scr[...] = headnorm(k_ref[...], gk_ref[...]).astype(BF16)
        vb_scr[...] = v_ref[...].astype(BF16)

    qn = headnorm(q_ref[...], gq_ref[...]) * (SB_HEAD_DIM ** -0.5)
    qh = (jnp.where(is0, qn, 0.0).astype(BF16), jnp.where(is0, 0.0, qn).astype(BF16))
    m2 = m2_ref[...]
    row = i * tq + lax.broadcasted_iota(jnp.int32, (tq, tk), 0)
    col0 = lax.broadcasted_iota(jnp.int32, (tq, tk), 1)
    nkb = (i + 1) * (tq // tk)

    def body(n, carry):
        j = nkb - 1 - n
        k0 = pl.multiple_of(j * tk, tk)
        kj = kn_scr[pl.ds(k0, tk), :]
        vj = vb_scr[pl.ds(k0, tk), :]
        causal = (col0 + j * tk) < row
        out = []
        for h in range(2):
            acc, rest0 = carry[2 * h], carry[2 * h + 1]
            z = _dot_nt(qh[h], kj)
            lk = jnp.where(causal, -_softplus(z), 0.0)
            hi = lk.astype(BF16)
            lo = (lk - hi.astype(F32)).astype(BF16)
            rt = _dot(jnp.concatenate([hi, lo], axis=1), m2)
            rest = rest0 + rt[:, :tk]
            w = jnp.where(causal, jnp.exp(z + lk + rest), 0.0)
            acc = acc + _dot(w.astype(BF16), vj)
            out += [acc, rest0 + rt[:, tk:]]
        return tuple(out)

    zero = jnp.zeros((tq, LANES), F32)
    res = lax.fori_loop(0, nkb, body, (zero, zero, zero, zero))
    o = jnp.where(is0, res[0], res[2])
    o_ref[...] = headnorm(o, go_ref[...])


def _sb_attention(proj3, gq, gk, go, m2, tq, tk):
    b, s, _ = proj3.shape
    hp = SB_HEADS // 2
    return pl.pallas_call(
        functools.partial(_sb_body, tq=tq, tk=tk),
        grid=(b, hp, s // tq),
        in_specs=[
            pl.BlockSpec((None, tq, LANES), lambda bi, h, i: (bi, i, h)),
            pl.BlockSpec((None, s, LANES), lambda bi, h, i: (bi, 0, hp + h)),
            pl.BlockSpec((None, s, LANES), lambda bi, h, i: (bi, 0, 2 * hp + h)),
            pl.BlockSpec((1, LANES), lambda bi, h, i: (0, 0)),
            pl.BlockSpec((1, LANES), lambda bi, h, i: (0, 0)),
            pl.BlockSpec((1, LANES), lambda bi, h, i: (0, 0)),
            pl.BlockSpec((2 * tk, 2 * tk), lambda bi, h, i: (0, 0)),
        ],
        out_specs=pl.BlockSpec((None, tq, LANES), lambda bi, h, i: (bi, i, h)),
        out_shape=jax.ShapeDtypeStruct((b, s, SB_WIDTH), F32),
        scratch_shapes=[pltpu.VMEM((s, LANES), BF16), pltpu.VMEM((s, LANES), BF16)],
        compiler_params=_cparams(("parallel", "parallel", "arbitrary")),
        name="sb_attention",
    )(proj3, proj3, proj3, gq, gk, go, m2)


GDN_BLOCK = 128


def _gdn_body(qx_ref, kx_ref, vx_ref, zx_ref, ab_ref, cwq_ref, cwk_ref, cwv_ref, alog_ref, dtb_ref,
              og_ref, o_ref, q_scr, k_scr, v_scr, g_scr, beta_scr, *, s):
    h = pl.program_id(1)
    c = GDN_BLOCK
    row = lax.broadcasted_iota(jnp.int32, (s, LANES), 0)
    lane = lax.broadcasted_iota(jnp.int32, (1, LANES), 1)

    def conv_silu(x, w):
        y = jnp.where(row >= 3, pltpu.roll(x, 3, 0), 0.0) * w[0:1, :]
        y = y + jnp.where(row >= 2, pltpu.roll(x, 2, 0), 0.0) * w[1:2, :]
        y = y + jnp.where(row >= 1, pltpu.roll(x, 1, 0), 0.0) * w[2:3, :]
        y = y + x * w[3:4, :]
        return y * _sigmoid(y)

    def l2n(x):
        return x * lax.rsqrt(jnp.sum(x * x, axis=-1, keepdims=True) + EPS)

    q_scr[...] = l2n(conv_silu(qx_ref[...], cwq_ref[...])) * (GDN_HEAD_DIM ** -0.5)
    k_scr[...] = l2n(conv_silu(kx_ref[...], cwk_ref[...]))
    v_scr[...] = conv_silu(vx_ref[...], cwv_ref[...])
    ab = ab_ref[...]
    a_col = jnp.sum(jnp.where(lane == h, ab, 0.0), axis=-1, keepdims=True)
    b_col = jnp.sum(jnp.where(lane == h + GDN_HEADS, ab, 0.0), axis=-1, keepdims=True)
    g_scr[...] = -jnp.exp(alog_ref[...]) * _softplus(a_col + dtb_ref[...])
    beta_scr[...] = jnp.broadcast_to(_sigmoid(b_col), (s, LANES))

    ri = lax.broadcasted_iota(jnp.int32, (c, c), 0)
    ci = lax.broadcasted_iota(jnp.int32, (c, c), 1)
    lower_incl = ci <= ri
    lower_strict = ci < ri
    ltri = lower_incl.astype(F32)
    eye = (ci == ri).astype(F32)
    ones = jnp.ones((c, c), F32)
    og = og_ref[...]

    def blk(n, state):
        r0 = pl.multiple_of(n * c, c)
        q = q_scr[pl.ds(r0, c), :]
        k = k_scr[pl.ds(r0, c), :]
        v = v_scr[pl.ds(r0, c), :]
        g = g_scr[pl.ds(r0, c), :]
        beta = beta_scr[pl.ds(r0, c), :]
        gc = _dot_f32(ltri, g)
        gc_row = _dot_f32(ones, gc * eye)
        decay = jnp.exp(jnp.where(lower_incl, gc - gc_row, -jnp.inf))
        kb = k * beta
        kbf = k.astype(BF16)
        a = jnp.where(lower_strict, _dot_nt(kb.astype(BF16), kbf) * decay, 0.0)
        nmat = -a
        xp = a
        for _ in range(6):
            xp = _dot3(xp, xp)
            nmat = nmat + xp + _dot3(nmat, xp)
        eg = jnp.exp(gc)
        rhs = jnp.concatenate([v * beta, kb * eg], axis=1)
        uw = rhs + _dot(nmat.astype(BF16), rhs.astype(BF16))
        u = uw[:, :LANES]
        w = uw[:, LANES:]
        attn = jnp.where(lower_incl, _dot_nt(q.astype(BF16), kbf) * decay, 0.0)
        gl = gc[c - 1:c, :]
        qd = q * eg
        kd = k * jnp.exp(gl - gc)
        sb = state.astype(BF16)
        v_new = u - _dot(w.astype(BF16), sb)
        vnb = v_new.astype(BF16)
        o = _dot(qd.astype(BF16), sb) + _dot(attn.astype(BF16), vnb)
        state = state * jnp.exp(gl) + _dot_tn(kd.astype(BF16), vnb)
        on = o * lax.rsqrt(jnp.mean(o * o, axis=-1, keepdims=True) + EPS) * og
        z = zx_ref[pl.ds(r0, c), :]
        o_ref[pl.ds(r0, c), :] = on * (z * _sigmoid(z))
        return state

    lax.fori_loop(0, s // c, blk, jnp.zeros((GDN_HEAD_DIM, GDN_HEAD_DIM), F32))


def _gdn(proj3, ab3, conv_w, alog_b, dtb_b, og):
    b, s, _ = proj3.shape
    base = 3 * SB_WIDTH // LANES
    nh = GDN_HEADS
    seq_spec = lambda off: pl.BlockSpec((None, s, LANES), lambda bi, h: (bi, 0, off + h))
    cw_spec = lambda off: pl.BlockSpec((GDN_CONV, LANES), lambda bi, h: (0, off + h))
    return pl.pallas_call(
        functools.partial(_gdn_body, s=s),
        grid=(b, nh),
        in_specs=[
            seq_spec(base), seq_spec(base + nh), seq_spec(base + 2 * nh), seq_spec(base + 3 * nh),
            pl.BlockSpec((None, s, LANES), lambda bi, h: (bi, 0, 0)),
            cw_spec(0), cw_spec(nh), cw_spec(2 * nh),
            pl.BlockSpec((None, 1, LANES), lambda bi, h: (h, 0, 0)),
            pl.BlockSpec((None, 1, LANES), lambda bi, h: (h, 0, 0)),
            pl.BlockSpec((1, LANES), lambda bi, h: (0, 0)),
        ],
        out_specs=pl.BlockSpec((None, s, LANES), lambda bi, h: (bi, 0, h)),
        out_shape=jax.ShapeDtypeStruct((b, s, GDN_WIDTH), F32),
        scratch_shapes=[pltpu.VMEM((s, LANES), F32) for _ in range(5)],
        compiler_params=_cparams(("parallel", "parallel")),
        name="gdn",
    )(proj3, proj3, proj3, proj3, ab3, conv_w, conv_w, conv_w, alog_b, dtb_b, og)


def _outproj_body(sb_ref, gd_ref, x_ref, wo1_ref, wo2_ref, g2_ref, wqt_ref, x1_ref, hnt_ref, pqt_ref):
    mix = _dot(sb_ref[...].astype(BF16), wo1_ref[...]) + _dot(gd_ref[...].astype(BF16), wo2_ref[...])
    x1 = x_ref[...] + mix
    x1_ref[...] = x1
    ms = jnp.mean(x1 * x1, axis=-1, keepdims=True)
    hn = x1 * lax.rsqrt(ms + EPS) * g2_ref[...]
    hnt = hn.T.astype(BF16)
    hnt_ref[...] = hnt
    pqt_ref[...] = _dot(wqt_ref[...], hnt)


def _outproj(sb_o, gd_o, x2, wo1, wo2, g2, wqt, tm):
    t, d = x2.shape
    nq = wqt.shape[0]
    return pl.pallas_call(
        _outproj_body,
        grid=(t // tm,),
        in_specs=[
            pl.BlockSpec((tm, SB_WIDTH), lambda i: (i, 0)),
            pl.BlockSpec((tm, GDN_WIDTH), lambda i: (i, 0)),
            pl.BlockSpec((tm, d), lambda i: (i, 0)),
            pl.BlockSpec((SB_WIDTH, d), lambda i: (0, 0)),
            pl.BlockSpec((GDN_WIDTH, d), lambda i: (0, 0)),
            pl.BlockSpec((1, d), lambda i: (0, 0)),
            pl.BlockSpec((nq, d), lambda i: (0, 0)),
        ],
        out_specs=[
            pl.BlockSpec((tm, d), lambda i: (i, 0)),
            pl.BlockSpec((d, tm), lambda i: (0, i)),
            pl.BlockSpec((nq, tm), lambda i: (0, i)),
        ],
        out_shape=[
            jax.ShapeDtypeStruct((t, d), F32),
            jax.ShapeDtypeStruct((d, t), BF16),
            jax.ShapeDtypeStruct((nq, t), F32),
        ],
        compiler_params=_cparams(("parallel",)),
        name="outproj",
    )(sb_o, gd_o, x2, wo1, wo2, g2, wqt)


def _extract_topk(s, k, track_rank):
    rows = s.shape[0]
    idx = lax.broadcasted_iota(jnp.int32, s.shape, 0)
    work = s
    vals = []
    sel = jnp.zeros(s.shape, jnp.bool_)
    rank = jnp.full(s.shape, float(k), F32) if track_rank else None
    for r in range(k):
        m = jnp.max(work, axis=0, keepdims=True)
        first = jnp.min(jnp.where(work == m, idx, rows), axis=0, keepdims=True)
        hit = idx == first
        if track_rank:
            rank = jnp.where(hit, float(r), rank)
        sel = jnp.logical_or(sel, hit)
        work = jnp.where(hit, -jnp.inf, work)
        vals.append(m)
    return vals, sel, rank


def _route_body(pqt_ref, k1_ref, k2_ref, rank2_ref, cnt_ref, e1_ref, e2z_ref):
    kk = PEER_TOPK
    for h in range(PEER_HEADS):
        q1 = pqt_ref[pl.ds(h * 2 * PEER_HALF, PEER_HALF), :].astype(BF16)
        q2 = pqt_ref[pl.ds(h * 2 * PEER_HALF + PEER_HALF, PEER_HALF), :].astype(BF16)
        s1 = _dot(k1_ref[h], q1)
        s2 = _dot(k2_ref[h], q2)
        v1, _, rank1 = _extract_topk(s1, kk, True)
        v2, _, rank2 = _extract_topk(s2, kk, True)
        cand = jnp.concatenate([v1[a] + jnp.concatenate(v2, axis=0) for a in range(kk)], axis=0)
        top, sel, _ = _extract_topk(cand, kk, False)
        zsum = jnp.ones_like(top[0])
        for r in range(1, kk):
            zsum = zsum + jnp.exp(top[r] - top[0])
        self_f = sel.astype(F32)
        cnt = jnp.zeros(s1.shape, F32)
        for a in range(kk):
            n_a = jnp.sum(self_f[a * kk:(a + 1) * kk, :], axis=0, keepdims=True)
            cnt = jnp.where(rank1 == float(a), n_a, cnt)
        rank2_ref[h] = rank2
        cnt_ref[h] = cnt
        e1_ref[h] = jnp.exp(s1 - v1[0])
        e2z_ref[h] = jnp.exp(s2 - v2[0]) / zsum


def _route(pqt, k1, k2, tn):
    nq, t = pqt.shape
    hk = (PEER_HEADS, PEER_NKEYS, PEER_HALF)
    out = jax.ShapeDtypeStruct((PEER_HEADS, PEER_NKEYS, t), F32)
    ospec = pl.BlockSpec((PEER_HEADS, PEER_NKEYS, tn), lambda i: (0, 0, i))
    return pl.pallas_call(
        _route_body,
        grid=(t // tn,),
        in_specs=[
            pl.BlockSpec((nq, tn), lambda i: (0, i)),
            pl.BlockSpec(hk, lambda i: (0, 0, 0)),
            pl.BlockSpec(hk, lambda i: (0, 0, 0)),
        ],
        out_specs=[ospec, ospec, ospec, ospec],
        out_shape=[out, out, out, out],
        compiler_params=_cparams(("parallel",)),
        name="peer_route",
    )(pqt, k1, k2)


PEER_I1_PER_STEP = 8


def _expert_body(hnt_ref, u_ref, vt_ref, rank2_ref, cnt_ref, e1_ref, e2z_ref, x1_ref, o_ref, acc_scr, p_scr):
    j = pl.program_id(1)

    @pl.when(j == 0)
    def _():
        acc_scr[...] = jnp.zeros_like(acc_scr)

    pre = _dot(u_ref[...], hnt_ref[...])
    for l in range(PEER_I1_PER_STEP):
        gate = None
        for h in range(PEER_HEADS):
            cnt = cnt_ref[h, l:l + 1, :]
            e1 = e1_ref[h, l:l + 1, :]
            term = jnp.where(rank2_ref[h] < cnt, e2z_ref[h] * e1, 0.0)
            gate = term if gate is None else gate + term
        pr = pre[l * PEER_NKEYS:(l + 1) * PEER_NKEYS, :]
        act = 0.5 * pr * (1.0 + lax.erf(pr * (2.0 ** -0.5)))
        p_scr[pl.ds(l * PEER_NKEYS, PEER_NKEYS), :] = (gate * act).astype(BF16)
    acc_scr[...] += _dot(vt_ref[...], p_scr[...])

    @pl.when(j == pl.num_programs(1) - 1)
    def _():
        o_ref[...] = x1_ref[...] + acc_scr[...].T


def _experts(hnt, u_b, vt_b, rank2, cnt, e1, e2z, x1, tn):
    d, t = hnt.shape
    ne = u_b.shape[0]
    eb = PEER_I1_PER_STEP * PEER_NKEYS
    full = pl.BlockSpec((PEER_HEADS, PEER_NKEYS, tn), lambda i, j: (0, 0, i))
    part = pl.BlockSpec((PEER_HEADS, PEER_I1_PER_STEP, tn), lambda i, j: (0, j, i))
    return pl.pallas_call(
        _expert_body,
        grid=(t // tn, ne // eb),
        in_specs=[
            pl.BlockSpec((d, tn), lambda i, j: (0, i)),
            pl.BlockSpec((eb, d), lambda i, j: (j, 0)),
            pl.BlockSpec((d, eb), lambda i, j: (0, j)),
            full, part, part, full,
            pl.BlockSpec((tn, d), lambda i, j: (i, 0)),
        ],
        out_specs=pl.BlockSpec((tn, d), lambda i, j: (i, 0)),
        out_shape=jax.ShapeDtypeStruct((t, d), F32),
        scratch_shapes=[pltpu.VMEM((d, tn), F32), pltpu.VMEM((eb, tn), BF16)],
        compiler_params=_cparams(("parallel", "arbitrary")),
        name="peer_experts",
    )(hnt, u_b, vt_b, rank2, cnt, e1, e2z, x1)


def _suffix_sum_matrix(tk):
    r = jnp.arange(2 * tk)[:, None] % tk
    c = jnp.arange(2 * tk)[None, :]
    return jnp.where(c < tk, r > c, True).astype(BF16)


def _layer(x, norm1_g, w_in, sb_q_g, sb_k_g, sb_o_g, conv_w, a_log, dt_bias, gdn_o_g, w_out, norm2_g,
           w_q, keys1, keys2, u_tab, v_tab):
    b, s, d = x.shape
    t = b * s
    n_main = 3 * SB_WIDTH + 4 * GDN_WIDTH
    x2 = x.reshape(t, d)
    w_main = w_in[:, :n_main].astype(BF16)
    w_ab = jnp.pad(w_in[:, n_main:], ((0, 0), (0, LANES - 2 * GDN_HEADS))).astype(BF16)
    tm = min(256, t)
    proj, ab = _inproj(x2, norm1_g.reshape(1, d), w_main, w_ab, tm)
    proj3 = proj.reshape(b, s, n_main)
    ab3 = ab.reshape(b, s, LANES)

    tq, tk = min(256, s), 128
    tile2 = lambda g: jnp.tile(g, 2).reshape(1, LANES)
    sb_o = _sb_attention(proj3, tile2(sb_q_g), tile2(sb_k_g), tile2(sb_o_g), _suffix_sum_matrix(tk), tq, tk)

    bcast = lambda p: jnp.broadcast_to(p[:, None, None], (GDN_HEADS, 1, LANES))
    gd_o = _gdn(proj3, ab3, conv_w, bcast(a_log), bcast(dt_bias), gdn_o_g.reshape(1, LANES))

    wo = w_out.astype(BF16)
    x1, hnt, pqt = _outproj(sb_o.reshape(t, SB_WIDTH), gd_o.reshape(t, GDN_WIDTH), x2, wo[:SB_WIDTH],
                            wo[SB_WIDTH:], norm2_g.reshape(1, d), w_q.T.astype(BF16), tm)

    rank2, cnt, e1, e2z = _route(pqt, keys1.astype(BF16), keys2.astype(BF16), min(256, t))
    y = _experts(hnt, u_tab.astype(BF16), v_tab.T.astype(BF16), rank2, cnt, e1, e2z, x1, min(512, t))
    return y.reshape(b, s, d)


def kernel(x, norm1_g, w_in, sb_q_norm_g, sb_k_norm_g, sb_out_norm_g, gdn_conv_w, gdn_a_log, gdn_dt_bias,
           gdn_out_norm_g, w_out, norm2_g, peer_w_q, peer_keys1, peer_keys2, peer_u, peer_v):
    for layer in range(norm1_g.shape[0]):
        x = _layer(x, norm1_g[layer], w_in[layer], sb_q_norm_g[layer], sb_k_norm_g[layer],
                   sb_out_norm_g[layer], gdn_conv_w[layer], gdn_a_log[layer], gdn_dt_bias[layer],
                   gdn_out_norm_g[layer], w_out[layer], norm2_g[layer], peer_w_q[layer],
                   peer_keys1[layer], peer_keys2[layer], peer_u[layer], peer_v[layer])
    return x
```

```python
import functools

import jax
import jax.numpy as jnp
from jax import lax
from jax.experimental import pallas as pl
from jax.experimental.pallas import tpu as pltpu

F32 = jnp.float32
BF16 = jnp.bfloat16
EPS = 1e-6

SB_HEADS = 8
SB_HEAD_DIM = 64
SB_WIDTH = SB_HEADS * SB_HEAD_DIM
GDN_HEADS = 4
GDN_HEAD_DIM = 128
GDN_WIDTH = GDN_HEADS * GDN_HEAD_DIM
GDN_CONV = 4
PEER_HEADS = 8
PEER_NKEYS = 128
PEER_HALF = 128
PEER_TOPK = 16
LANES = 128

VMEM_LIMIT = 56 * 1024 * 1024


def _cparams(sem):
    return pltpu.CompilerParams(dimension_semantics=sem, vmem_limit_bytes=VMEM_LIMIT)


def _dot(a, b):
    return jnp.dot(a, b, preferred_element_type=F32)


def _dot_nt(a, b):
    return lax.dot_general(a, b, (((1,), (1,)), ((), ())), preferred_element_type=F32)


def _dot_tn(a, b):
    return lax.dot_general(a, b, (((0,), (0,)), ((), ())), preferred_element_type=F32)


def _dot_f32(a, b):
    return jnp.dot(a, b, preferred_element_type=F32, precision=lax.Precision.HIGHEST)


def _split(a):
    hi = a.astype(BF16)
    return hi, (a - hi.astype(F32)).astype(BF16)


def _dot3(a, b):
    ah, al = _split(a)
    bh, bl = _split(b)
    return _dot(jnp.concatenate([ah, ah, al], axis=1), jnp.concatenate([bh, bl, bh], axis=0))


def _softplus(x):
    return jnp.maximum(x, 0.0) + jnp.log1p(jnp.exp(-jnp.abs(x)))


def _sigmoid(x):
    return 1.0 / (1.0 + jnp.exp(-x))


def _inproj_body(x_ref, g_ref, w_ref, wab_ref, proj_ref, ab_ref):
    x = x_ref[...]
    ms = jnp.mean(x * x, axis=-1, keepdims=True)
    h = (x * lax.rsqrt(ms + EPS) * g_ref[...]).astype(BF16)
    proj_ref[...] = _dot(h, w_ref[...])
    ab_ref[...] = _dot(h, wab_ref[...])


def _inproj(x2, g, w_main, w_ab, tm):
    t, d = x2.shape
    n = w_main.shape[1]
    return pl.pallas_call(
        _inproj_body,
        grid=(t // tm,),
        in_specs=[
            pl.BlockSpec((tm, d), lambda i: (i, 0)),
            pl.BlockSpec((1, d), lambda i: (0, 0)),
            pl.BlockSpec((d, n), lambda i: (0, 0)),
            pl.BlockSpec((d, LANES), lambda i: (0, 0)),
        ],
        out_specs=[
            pl.BlockSpec((tm, n), lambda i: (i, 0)),
            pl.BlockSpec((tm, LANES), lambda i: (i, 0)),
        ],
        out_shape=[
            jax.ShapeDtypeStruct((t, n), F32),
            jax.ShapeDtypeStruct((t, LANES), F32),
        ],
        compiler_params=_cparams(("parallel",)),
        name="inproj",
    )(x2, g, w_main, w_ab)


SB_DEAD_LOG = -104.0


def _sb_body(q_ref, k_ref, v_ref, gq_ref, gk_ref, go_ref, m2_ref, o_ref, kn_scr, vb_scr, *, tq, tk):
    i = pl.program_id(2)
    lane = lax.broadcasted_iota(jnp.int32, (1, LANES), 1)
    is0 = lane < SB_HEAD_DIM

    def headnorm(x, g):
        x2 = x * x
        s0 = jnp.sum(jnp.where(is0, x2, 0.0), axis=-1, keepdims=True)
        s1 = jnp.sum(jnp.where(is0, 0.0, x2), axis=-1, keepdims=True)
        ms = jnp.where(is0, s0, s1) * (1.0 / SB_HEAD_DIM)
        return x * lax.rsqrt(ms + EPS) * g

    @pl.when(i == 0)
    def _():
        kn_scr[...] = headnorm(k_ref[...], gk_ref[...]).astype(BF16)
        vb_scr[...] = v_ref[...].astype(BF16)

    qn = headnorm(q_ref[...], gq_ref[...]) * (SB_HEAD_DIM ** -0.5)
    qh = (jnp.where(is0, qn, 0.0).astype(BF16), jnp.where(is0, 0.0, qn).astype(BF16))
    m2 = m2_ref[...]
    row = i * tq + lax.broadcasted_iota(jnp.int32, (tq, tk), 0)
    col0 = lax.broadcasted_iota(jnp.int32, (tq, tk), 1)
    nkb = (i + 1) * (tq // tk)

    def body(state):
        n, _, carry = state
        j = nkb - 1 - n
        alive = (jnp.max(jnp.maximum(carry[1], carry[3])) > SB_DEAD_LOG).astype(jnp.int32)
        k0 = pl.multiple_of(j * tk, tk)
        kj = kn_scr[pl.ds(k0, tk), :]
        vj = vb_scr[pl.ds(k0, tk), :]
        causal = (col0 + j * tk) < row
        out = []
        for h in range(2):
            acc, rest0 = carry[2 * h], carry[2 * h + 1]
            z = _dot_nt(qh[h], kj)
            lk = jnp.where(causal, -_softplus(z), 0.0)
            hi = lk.astype(BF16)
            lo = (lk - hi.astype(F32)).astype(BF16)
            rt = _dot(jnp.concatenate([hi, lo], axis=1), m2)
            rest = rest0 + rt[:, :tk]
            w = jnp.where(causal, jnp.exp(z + lk + rest), 0.0)
            acc = acc + _dot(w.astype(BF16), vj)
            out += [acc, rest0 + rt[:, tk:]]
        return n + 1, alive, tuple(out)

    def cond(state):
        return jnp.logical_and(state[0] < nkb, state[1] > 0)

    zero = jnp.zeros((tq, LANES), F32)
    _, _, res = lax.while_loop(cond, body, (jnp.int32(0), jnp.int32(1), (zero, zero, zero, zero)))
    o = jnp.where(is0, res[0], res[2])
    o_ref[...] = headnorm(o, go_ref[...])


def _sb_attention(proj3, gq, gk, go, m2, tq, tk):
    b, s, _ = proj3.shape
    hp = SB_HEADS // 2
    return pl.pallas_call(
        functools.partial(_sb_body, tq=tq, tk=tk),
        grid=(b, hp, s // tq),
        in_specs=[
            pl.BlockSpec((None, tq, LANES), lambda bi, h, i: (bi, i, h)),
            pl.BlockSpec((None, s, LANES), lambda bi, h, i: (bi, 0, hp + h)),
            pl.BlockSpec((None, s, LANES), lambda bi, h, i: (bi, 0, 2 * hp + h)),
            pl.BlockSpec((1, LANES), lambda bi, h, i: (0, 0)),
            pl.BlockSpec((1, LANES), lambda bi, h, i: (0, 0)),
            pl.BlockSpec((1, LANES), lambda bi, h, i: (0, 0)),
            pl.BlockSpec((2 * tk, 2 * tk), lambda bi, h, i: (0, 0)),
        ],
        out_specs=pl.BlockSpec((None, tq, LANES), lambda bi, h, i: (bi, i, h)),
        out_shape=jax.ShapeDtypeStruct((b, s, SB_WIDTH), F32),
        scratch_shapes=[pltpu.VMEM((s, LANES), BF16), pltpu.VMEM((s, LANES), BF16)],
        compiler_params=_cparams(("parallel", "parallel", "arbitrary")),
        name="sb_attention",
    )(proj3, proj3, proj3, gq, gk, go, m2)


GDN_BLOCK = 128


def _gdn_body(qx_ref, kx_ref, vx_ref, zx_ref, ab_ref, cwq_ref, cwk_ref, cwv_ref, alog_ref, dtb_ref,
              og_ref, o_ref, q_scr, k_scr, v_scr, g_scr, beta_scr, *, s):
    h = pl.program_id(1)
    c = GDN_BLOCK
    row = lax.broadcasted_iota(jnp.int32, (s, LANES), 0)
    lane = lax.broadcasted_iota(jnp.int32, (1, LANES), 1)

    def conv_silu(x, w):
        y = jnp.where(row >= 3, pltpu.roll(x, 3, 0), 0.0) * w[0:1, :]
        y = y + jnp.where(row >= 2, pltpu.roll(x, 2, 0), 0.0) * w[1:2, :]
        y = y + jnp.where(row >= 1, pltpu.roll(x, 1, 0), 0.0) * w[2:3, :]
        y = y + x * w[3:4, :]
        return y * _sigmoid(y)

    def l2n(x):
        return x * lax.rsqrt(jnp.sum(x * x, axis=-1, keepdims=True) + EPS)

    q_scr[...] = l2n(conv_silu(qx_ref[...], cwq_ref[...])) * (GDN_HEAD_DIM ** -0.5)
    k_scr[...] = l2n(conv_silu(kx_ref[...], cwk_ref[...]))
    v_scr[...] = conv_silu(vx_ref[...], cwv_ref[...])
    ab = ab_ref[...]
    a_col = jnp.sum(jnp.where(lane == h, ab, 0.0), axis=-1, keepdims=True)
    b_col = jnp.sum(jnp.where(lane == h + GDN_HEADS, ab, 0.0), axis=-1, keepdims=True)
    g_scr[...] = -jnp.exp(alog_ref[...]) * _softplus(a_col + dtb_ref[...])
    beta_scr[...] = jnp.broadcast_to(_sigmoid(b_col), (s, LANES))

    ri = lax.broadcasted_iota(jnp.int32, (c, c), 0)
    ci = lax.broadcasted_iota(jnp.int32, (c, c), 1)
    lower_incl = ci <= ri
    lower_strict = ci < ri
    ltri = lower_incl.astype(F32)
    eye = (ci == ri).astype(F32)
    ones = jnp.ones((c, c), F32)
    og = og_ref[...]

    def blk(n, state):
        r0 = pl.multiple_of(n * c, c)
        q = q_scr[pl.ds(r0, c), :]
        k = k_scr[pl.ds(r0, c), :]
        v = v_scr[pl.ds(r0, c), :]
        g = g_scr[pl.ds(r0, c), :]
        beta = beta_scr[pl.ds(r0, c), :]
        gc = _dot_f32(ltri, g)
        gc_row = _dot_f32(ones, gc * eye)
        decay = jnp.exp(jnp.where(lower_incl, gc - gc_row, -jnp.inf))
        kb = k * beta
        kbf = k.astype(BF16)
        a = jnp.where(lower_strict, _dot_nt(kb.astype(BF16), kbf) * decay, 0.0)
        nmat = -a
        xp = a
        for _ in range(6):
            xp = _dot3(xp, xp)
            nmat = nmat + xp + _dot3(nmat, xp)
        eg = jnp.exp(gc)
        rhs = jnp.concatenate([v * beta, kb * eg], axis=1)
        uw = rhs + _dot(nmat.astype(BF16), rhs.astype(BF16))
        u = uw[:, :LANES]
        w = uw[:, LANES:]
        attn = jnp.where(lower_incl, _dot_nt(q.astype(BF16), kbf) * decay, 0.0)
        gl = gc[c - 1:c, :]
        qd = q * eg
        kd = k * jnp.exp(gl - gc)
        sb = state.astype(BF16)
        v_new = u - _dot(w.astype(BF16), sb)
        vnb = v_new.astype(BF16)
        o = _dot(qd.astype(BF16), sb) + _dot(attn.astype(BF16), vnb)
        state = state * jnp.exp(gl) + _dot_tn(kd.astype(BF16), vnb)
        on = o * lax.rsqrt(jnp.mean(o * o, axis=-1, keepdims=True) + EPS) * og
        z = zx_ref[pl.ds(r0, c), :]
        o_ref[pl.ds(r0, c), :] = on * (z * _sigmoid(z))
        return state

    lax.fori_loop(0, s // c, blk, jnp.zeros((GDN_HEAD_DIM, GDN_HEAD_DIM), F32))


def _gdn(proj3, ab3, conv_w, alog_b, dtb_b, og):
    b, s, _ = proj3.shape
    base = 3 * SB_WIDTH // LANES
    nh = GDN_HEADS
    seq_spec = lambda off: pl.BlockSpec((None, s, LANES), lambda bi, h: (bi, 0, off + h))
    cw_spec = lambda off: pl.BlockSpec((GDN_CONV, LANES), lambda bi, h: (0, off + h))
    return pl.pallas_call(
        functools.partial(_gdn_body, s=s),
        grid=(b, nh),
        in_specs=[
            seq_spec(base), seq_spec(base + nh), seq_spec(base + 2 * nh), seq_spec(base + 3 * nh),
            pl.BlockSpec((None, s, LANES), lambda bi, h: (bi, 0, 0)),
            cw_spec(0), cw_spec(nh), cw_spec(2 * nh),
            pl.BlockSpec((None, 1, LANES), lambda bi, h: (h, 0, 0)),
            pl.BlockSpec((None, 1, LANES), lambda bi, h: (h, 0, 0)),
            pl.BlockSpec((1, LANES), lambda bi, h: (0, 0)),
        ],
        out_specs=pl.BlockSpec((None, s, LANES), lambda bi, h: (bi, 0, h)),
        out_shape=jax.ShapeDtypeStruct((b, s, GDN_WIDTH), F32),
        scratch_shapes=[pltpu.VMEM((s, LANES), F32) for _ in range(5)],
        compiler_params=_cparams(("parallel", "parallel")),
        name="gdn",
    )(proj3, proj3, proj3, proj3, ab3, conv_w, conv_w, conv_w, alog_b, dtb_b, og)


def _outproj_body(sb_ref, gd_ref, x_ref, wo1_ref, wo2_ref, g2_ref, wqt_ref, x1_ref, hnt_ref, pqt_ref):
    mix = _dot(sb_ref[...].astype(BF16), wo1_ref[...]) + _dot(gd_ref[...].astype(BF16), wo2_ref[...])
    x1 = x_ref[...] + mix
    x1_ref[...] = x1
    ms = jnp.mean(x1 * x1, axis=-1, keepdims=True)
    hn = x1 * lax.rsqrt(ms + EPS) * g2_ref[...]
    hnt = hn.T.astype(BF16)
    hnt_ref[...] = hnt
    pqt_ref[...] = _dot(wqt_ref[...], hnt)


def _outproj(sb_o, gd_o, x2, wo1, wo2, g2, wqt, tm):
    t, d = x2.shape
    nq = wqt.shape[0]
    return pl.pallas_call(
        _outproj_body,
        grid=(t // tm,),
        in_specs=[
            pl.BlockSpec((tm, SB_WIDTH), lambda i: (i, 0)),
            pl.BlockSpec((tm, GDN_WIDTH), lambda i: (i, 0)),
            pl.BlockSpec((tm, d), lambda i: (i, 0)),
            pl.BlockSpec((SB_WIDTH, d), lambda i: (0, 0)),
            pl.BlockSpec((GDN_WIDTH, d), lambda i: (0, 0)),
            pl.BlockSpec((1, d), lambda i: (0, 0)),
            pl.BlockSpec((nq, d), lambda i: (0, 0)),
        ],
        out_specs=[
            pl.BlockSpec((tm, d), lambda i: (i, 0)),
            pl.BlockSpec((d, tm), lambda i: (0, i)),
            pl.BlockSpec((nq, tm), lambda i: (0, i)),
        ],
        out_shape=[
            jax.ShapeDtypeStruct((t, d), F32),
            jax.ShapeDtypeStruct((d, t), BF16),
            jax.ShapeDtypeStruct((nq, t), F32),
        ],
        compiler_params=_cparams(("parallel",)),
        name="outproj",
    )(sb_o, gd_o, x2, wo1, wo2, g2, wqt)


def _extract_topk(s, ids, k):
    big = jnp.int32(2 ** 30)
    work = s
    vals = []
    rank = jnp.full(s.shape, float(k), F32)
    for r in range(k):
        m = jnp.max(work, axis=0, keepdims=True)
        first = jnp.min(jnp.where(work == m, ids, big), axis=0, keepdims=True)
        hit = ids == first
        rank = jnp.where(hit, float(r), rank)
        work = jnp.where(hit, -jnp.inf, work)
        vals.append(m)
    return vals, rank


def _route_body(pqt_ref, k1_ref, k2_ref, rank2_ref, cnt_ref, e1_ref, e2z_ref):
    kk = PEER_TOPK
    tn = pqt_ref.shape[1]
    key_ids = lax.broadcasted_iota(jnp.int32, (PEER_NKEYS, tn), 0)
    i8 = lax.broadcasted_iota(jnp.int32, (8, tn), 0)
    i16 = lax.broadcasted_iota(jnp.int32, (kk, tn), 0)
    cand_ids = jnp.concatenate(
        [i16] + [i8 + a * kk for a in (1, 2, 3)]
        + [jnp.where(i8 >= 4, i8 * kk + b, kk * kk + i8 * kk + b) for b in (0, 1, 2)] + [(i8 + 8) * kk], axis=0)
    for h in range(PEER_HEADS):
        q1 = pqt_ref[pl.ds(h * 2 * PEER_HALF, PEER_HALF), :].astype(BF16)
        q2 = pqt_ref[pl.ds(h * 2 * PEER_HALF + PEER_HALF, PEER_HALF), :].astype(BF16)
        s1 = _dot(k1_ref[h], q1)
        s2 = _dot(k2_ref[h], q2)
        v1, rank1 = _extract_topk(s1, key_ids, kk)
        v2, rank2 = _extract_topk(s2, key_ids, kk)
        v1a = jnp.concatenate(v1, axis=0)
        v2a = jnp.concatenate(v2, axis=0)
        cand = jnp.concatenate(
            [v1[0] + v2a] + [v1[a] + v2a[:8] for a in (1, 2, 3)]
            + [jnp.where(i8 >= 4, v1a[:8] + v2[b], -jnp.inf) for b in (0, 1, 2)] + [v1a[8:] + v2[0]], axis=0)
        top, crank = _extract_topk(cand, cand_ids, kk)
        zsum = jnp.ones_like(top[0])
        for r in range(1, kk):
            zsum = zsum + jnp.exp(top[r] - top[0])
        sel = (crank < float(kk)).astype(F32)
        low = [jnp.sum(sel[0:16], axis=0, keepdims=True)] + [
            jnp.sum(sel[8 + 8 * a:16 + 8 * a], axis=0, keepdims=True) for a in (1, 2, 3)]
        mid = sel[40:48] + sel[48:56] + sel[56:64]
        high = sel[64:72]
        cnt = jnp.zeros(s1.shape, F32)
        for a in range(kk):
            n_a = low[a] if a < 4 else (mid[a:a + 1] if a < 8 else high[a - 8:a - 7])
            cnt = jnp.where(rank1 == float(a), n_a, cnt)
        rank2_ref[h] = rank2.astype(BF16)
        cnt_ref[h] = cnt
        e1_ref[h] = jnp.exp(s1 - v1[0])
        e2z_ref[h] = (jnp.exp(s2 - v2[0]) / zsum).astype(BF16)


def _route(pqt, k1, k2, tn):
    nq, t = pqt.shape
    hk = (PEER_HEADS, PEER_NKEYS, PEER_HALF)
    out = lambda dt: jax.ShapeDtypeStruct((PEER_HEADS, PEER_NKEYS, t), dt)
    ospec = pl.BlockSpec((PEER_HEADS, PEER_NKEYS, tn), lambda i: (0, 0, i))
    return pl.pallas_call(
        _route_body,
        grid=(t // tn,),
        in_specs=[
            pl.BlockSpec((nq, tn), lambda i: (0, i)),
            pl.BlockSpec(hk, lambda i: (0, 0, 0)),
            pl.BlockSpec(hk, lambda i: (0, 0, 0)),
        ],
        out_specs=[ospec, ospec, ospec, ospec],
        out_shape=[out(BF16), out(F32), out(F32), out(BF16)],
        compiler_params=_cparams(("parallel",)),
        name="peer_route",
    )(pqt, k1, k2)


PEER_I1_PER_STEP = 8
PEER_I1_PER_CHUNK = 2


def _expert_body(hnt_ref, u_ref, vt_ref, rank2_ref, cnt_ref, e1_ref, e2z_ref, x1_ref, o_ref, acc_scr):
    j = pl.program_id(1)

    @pl.when(j == 0)
    def _():
        acc_scr[...] = jnp.zeros_like(acc_scr)

    hnt = hnt_ref[...]
    ce = PEER_I1_PER_CHUNK * PEER_NKEYS
    total = None
    for c in range(PEER_I1_PER_STEP // PEER_I1_PER_CHUNK):
        pre = _dot(u_ref[pl.ds(c * ce, ce), :], hnt)
        act = (0.5 * pre * (1.0 + lax.erf(pre * (2.0 ** -0.5)))).astype(BF16)
        gates = []
        for l in range(c * PEER_I1_PER_CHUNK, (c + 1) * PEER_I1_PER_CHUNK):
            gate = None
            for h in range(PEER_HEADS):
                cnt = cnt_ref[h, l:l + 1, :].astype(BF16)
                e1 = e1_ref[h, l:l + 1, :].astype(BF16)
                term = jnp.where(rank2_ref[h] < cnt, e2z_ref[h] * e1, 0.0)
                gate = term if gate is None else gate + term
            gates.append(gate)
        p = jnp.concatenate(gates, axis=0) * act
        part = _dot(vt_ref[:, pl.ds(c * ce, ce)], p)
        total = part if total is None else total + part
    acc_scr[...] += total

    @pl.when(j == pl.num_programs(1) - 1)
    def _():
        o_ref[...] = x1_ref[...] + acc_scr[...].T


def _experts(hnt, u_b, vt_b, rank2, cnt, e1, e2z, x1, tn):
    d, t = hnt.shape
    ne = u_b.shape[0]
    eb = PEER_I1_PER_STEP * PEER_NKEYS
    full = pl.BlockSpec((PEER_HEADS, PEER_NKEYS, tn), lambda i, j: (0, 0, i))
    part = pl.BlockSpec((PEER_HEADS, PEER_I1_PER_STEP, tn), lambda i, j: (0, j, i))
    return pl.pallas_call(
        _expert_body,
        grid=(t // tn, ne // eb),
        in_specs=[
            pl.BlockSpec((d, tn), lambda i, j: (0, i)),
            pl.BlockSpec((eb, d), lambda i, j: (j, 0)),
            pl.BlockSpec((d, eb), lambda i, j: (0, j)),
            full, part, part, full,
            pl.BlockSpec((tn, d), lambda i, j: (i, 0)),
        ],
        out_specs=pl.BlockSpec((tn, d), lambda i, j: (i, 0)),
        out_shape=jax.ShapeDtypeStruct((t, d), F32),
        scratch_shapes=[pltpu.VMEM((d, tn), F32)],
        compiler_params=_cparams(("parallel", "arbitrary")),
        name="peer_experts",
    )(hnt, u_b, vt_b, rank2, cnt, e1, e2z, x1)


def _suffix_sum_matrix(tk):
    r = jnp.arange(2 * tk)[:, None] % tk
    c = jnp.arange(2 * tk)[None, :]
    return jnp.where(c < tk, r > c, True).astype(BF16)


def _layer(x, norm1_g, w_in, sb_q_g, sb_k_g, sb_o_g, conv_w, a_log, dt_bias, gdn_o_g, w_out, norm2_g,
           w_q, keys1, keys2, u_tab, v_tab):
    b, s, d = x.shape
    t = b * s
    n_main = 3 * SB_WIDTH + 4 * GDN_WIDTH
    x2 = x.reshape(t, d)
    w_main = w_in[:, :n_main].astype(BF16)
    w_ab = jnp.pad(w_in[:, n_main:], ((0, 0), (0, LANES - 2 * GDN_HEADS))).astype(BF16)
    tm = min(256, t)
    proj, ab = _inproj(x2, norm1_g.reshape(1, d), w_main, w_ab, tm)
    proj3 = proj.reshape(b, s, n_main)
    ab3 = ab.reshape(b, s, LANES)

    tq, tk = min(256, s), 128
    tile2 = lambda g: jnp.tile(g, 2).reshape(1, LANES)
    sb_o = _sb_attention(proj3, tile2(sb_q_g), tile2(sb_k_g), tile2(sb_o_g), _suffix_sum_matrix(tk), tq, tk)

    bcast = lambda p: jnp.broadcast_to(p[:, None, None], (GDN_HEADS, 1, LANES))
    gd_o = _gdn(proj3, ab3, conv_w, bcast(a_log), bcast(dt_bias), gdn_o_g.reshape(1, LANES))

    wo = w_out.astype(BF16)
    x1, hnt, pqt = _outproj(sb_o.reshape(t, SB_WIDTH), gd_o.reshape(t, GDN_WIDTH), x2, wo[:SB_WIDTH],
                            wo[SB_WIDTH:], norm2_g.reshape(1, d), w_q.T.astype(BF16), tm)

    rank2, cnt, e1, e2z = _route(pqt, keys1.astype(BF16), keys2.astype(BF16), min(256, t))
    y = _experts(hnt, u_tab.astype(BF16), v_tab.T.astype(BF16), rank2, cnt, e1, e2z, x1, min(512, t))
    return y.reshape(b, s, d)


def kernel(x, norm1_g, w_in, sb_q_norm_g, sb_k_norm_g, sb_out_norm_g, gdn_conv_w, gdn_a_log, gdn_dt_bias,
           gdn_out_norm_g, w_out, norm2_g, peer_w_q, peer_keys1, peer_keys2, peer_u, peer_v):
    for layer in range(norm1_g.shape[0]):
        x = _layer(x, norm1_g[layer], w_in[layer], sb_q_norm_g[layer], sb_k_norm_g[layer],
                   sb_out_norm_g[layer], gdn_conv_w[layer], gdn_a_log[layer], gdn_dt_bias[layer],
                   gdn_out_norm_g[layer], w_out[layer], norm2_g[layer], peer_w_q[layer],
                   peer_keys1[layer], peer_keys2[layer], peer_u[layer], peer_v[layer])
    return x
```

```python
import functools

import jax
import jax.numpy as jnp
from jax import lax
from jax.experimental import pallas as pl
from jax.experimental.pallas import tpu as pltpu

F32 = jnp.float32
BF16 = jnp.bfloat16
EPS = 1e-6

SB_HEADS = 8
SB_HEAD_DIM = 64
SB_WIDTH = SB_HEADS * SB_HEAD_DIM
GDN_HEADS = 4
GDN_HEAD_DIM = 128
GDN_WIDTH = GDN_HEADS * GDN_HEAD_DIM
GDN_CONV = 4
PEER_HEADS = 8
PEER_NKEYS = 128
PEER_HALF = 128
PEER_TOPK = 16
LANES = 128

VMEM_LIMIT = 56 * 1024 * 1024


def _cparams(sem):
    return pltpu.CompilerParams(dimension_semantics=sem, vmem_limit_bytes=VMEM_LIMIT)


def _dot(a, b):
    return jnp.dot(a, b, preferred_element_type=F32)


def _dot_nt(a, b):
    return lax.dot_general(a, b, (((1,), (1,)), ((), ())), preferred_element_type=F32)


def _dot_tn(a, b):
    return lax.dot_general(a, b, (((0,), (0,)), ((), ())), preferred_element_type=F32)


def _dot_f32(a, b):
    return jnp.dot(a, b, preferred_element_type=F32, precision=lax.Precision.HIGHEST)


def _split(a):
    hi = a.astype(BF16)
    return hi, (a - hi.astype(F32)).astype(BF16)


def _dot3(a, b):
    ah, al = _split(a)
    bh, bl = _split(b)
    return _dot(jnp.concatenate([ah, ah, al], axis=1), jnp.concatenate([bh, bl, bh], axis=0))


def _softplus(x):
    return jnp.maximum(x, 0.0) + jnp.log1p(jnp.exp(-jnp.abs(x)))


def _sigmoid(x):
    return 1.0 / (1.0 + jnp.exp(-x))


def _inproj_body(x_ref, g_ref, w_ref, wab_ref, proj_ref, ab_ref):
    x = x_ref[...]
    ms = jnp.mean(x * x, axis=-1, keepdims=True)
    h = (x * lax.rsqrt(ms + EPS) * g_ref[...]).astype(BF16)
    proj_ref[...] = _dot(h, w_ref[...])
    ab_ref[...] = _dot(h, wab_ref[...])


def _inproj(x2, g, w_main, w_ab, tm):
    t, d = x2.shape
    n = w_main.shape[1]
    return pl.pallas_call(
        _inproj_body,
        grid=(t // tm,),
        in_specs=[
            pl.BlockSpec((tm, d), lambda i: (i, 0)),
            pl.BlockSpec((1, d), lambda i: (0, 0)),
            pl.BlockSpec((d, n), lambda i: (0, 0)),
            pl.BlockSpec((d, LANES), lambda i: (0, 0)),
        ],
        out_specs=[
            pl.BlockSpec((tm, n), lambda i: (i, 0)),
            pl.BlockSpec((tm, LANES), lambda i: (i, 0)),
        ],
        out_shape=[
            jax.ShapeDtypeStruct((t, n), F32),
            jax.ShapeDtypeStruct((t, LANES), F32),
        ],
        compiler_params=_cparams(("parallel",)),
        name="inproj",
    )(x2, g, w_main, w_ab)


SB_DEAD_LOG = -104.0


def _sb_body(q_ref, k_ref, v_ref, gq_ref, gk_ref, go_ref, m2_ref, o_ref, kn_scr, vb_scr, *, tq, tk):
    i = pl.program_id(2)
    lane = lax.broadcasted_iota(jnp.int32, (1, LANES), 1)
    is0 = lane < SB_HEAD_DIM

    def headnorm(x, g):
        x2 = x * x
        s0 = jnp.sum(jnp.where(is0, x2, 0.0), axis=-1, keepdims=True)
        s1 = jnp.sum(jnp.where(is0, 0.0, x2), axis=-1, keepdims=True)
        ms = jnp.where(is0, s0, s1) * (1.0 / SB_HEAD_DIM)
        return x * lax.rsqrt(ms + EPS) * g

    @pl.when(i == 0)
    def _():
        kn_scr[...] = headnorm(k_ref[...], gk_ref[...]).astype(BF16)
        vb_scr[...] = v_ref[...].astype(BF16)

    qn = headnorm(q_ref[...], gq_ref[...]) * (SB_HEAD_DIM ** -0.5)
    qh = (jnp.where(is0, qn, 0.0).astype(BF16), jnp.where(is0, 0.0, qn).astype(BF16))
    m2 = m2_ref[...]
    row = i * tq + lax.broadcasted_iota(jnp.int32, (tq, tk), 0)
    col0 = lax.broadcasted_iota(jnp.int32, (tq, tk), 1)
    nkb = (i + 1) * (tq // tk)

    def scores(j):
        kj = kn_scr[pl.ds(pl.multiple_of(j * tk, tk), tk), :]
        return [_dot_nt(qh[h], kj) for h in range(2)]

    def logs(j, zs):
        causal = (col0 + j * tk) < row
        zls, cats = [], []
        for z in zs:
            lk = jnp.where(causal, -_softplus(z), 0.0)
            hi = lk.astype(BF16)
            lo = (lk - hi.astype(F32)).astype(BF16)
            zls.append(jnp.where(causal, z + lk, -jnp.inf))
            cats.append(jnp.concatenate([hi, lo], axis=1))
        return zls, cats

    def sums(cats):
        return [_dot(c, m2) for c in cats]

    def body(state):
        n, _, carry, zls, rts = state
        j = nkb - 1 - n
        alive = (jnp.max(jnp.maximum(carry[1], carry[3])) > SB_DEAD_LOG).astype(jnp.int32)
        jn = jnp.maximum(j - 1, 0)
        zs_next = scores(jn)
        vj = vb_scr[pl.ds(pl.multiple_of(j * tk, tk), tk), :]
        ws = [jnp.exp(zls[h] + (carry[2 * h + 1] + rts[h][:, :tk])).astype(BF16) for h in range(2)]
        pvs = [_dot(w, vj) for w in ws]
        zls_next, cats = logs(jn, zs_next)
        rts_next = sums(cats)
        out = []
        for h in range(2):
            out += [carry[2 * h] + pvs[h], carry[2 * h + 1] + rts[h][:, tk:]]
        return n + 1, alive, tuple(out), tuple(zls_next), tuple(rts_next)

    def cond(state):
        return jnp.logical_and(state[0] < nkb, state[1] > 0)

    zero = jnp.zeros((tq, LANES), F32)
    zls0, cats0 = logs(nkb - 1, scores(nkb - 1))
    init = (jnp.int32(0), jnp.int32(1), (zero, zero, zero, zero), tuple(zls0), tuple(sums(cats0)))
    res = lax.while_loop(cond, body, init)[2]
    o = jnp.where(is0, res[0], res[2])
    o_ref[...] = headnorm(o, go_ref[...])


def _sb_attention(proj3, gq, gk, go, m2, tq, tk):
    b, s, _ = proj3.shape
    hp = SB_HEADS // 2
    return pl.pallas_call(
        functools.partial(_sb_body, tq=tq, tk=tk),
        grid=(b, hp, s // tq),
        in_specs=[
            pl.BlockSpec((None, tq, LANES), lambda bi, h, i: (bi, i, h)),
            pl.BlockSpec((None, s, LANES), lambda bi, h, i: (bi, 0, hp + h)),
            pl.BlockSpec((None, s, LANES), lambda bi, h, i: (bi, 0, 2 * hp + h)),
            pl.BlockSpec((1, LANES), lambda bi, h, i: (0, 0)),
            pl.BlockSpec((1, LANES), lambda bi, h, i: (0, 0)),
            pl.BlockSpec((1, LANES), lambda bi, h, i: (0, 0)),
            pl.BlockSpec((2 * tk, 2 * tk), lambda bi, h, i: (0, 0)),
        ],
        out_specs=pl.BlockSpec((None, tq, LANES), lambda bi, h, i: (bi, i, h)),
        out_shape=jax.ShapeDtypeStruct((b, s, SB_WIDTH), F32),
        scratch_shapes=[pltpu.VMEM((s, LANES), BF16), pltpu.VMEM((s, LANES), BF16)],
        compiler_params=_cparams(("parallel", "parallel", "arbitrary")),
        name="sb_attention",
    )(proj3, proj3, proj3, gq, gk, go, m2)


GDN_BLOCK = 128


def _gdn_body(qx_ref, kx_ref, vx_ref, zx_ref, ab_ref, cwq_ref, cwk_ref, cwv_ref, alog_ref, dtb_ref,
              og_ref, o_ref, q_scr, k_scr, v_scr, g_scr, beta_scr, *, s):
    h = pl.program_id(1)
    c = GDN_BLOCK
    row = lax.broadcasted_iota(jnp.int32, (s, LANES), 0)
    lane = lax.broadcasted_iota(jnp.int32, (1, LANES), 1)

    def conv_silu(x, w):
        y = jnp.where(row >= 3, pltpu.roll(x, 3, 0), 0.0) * w[0:1, :]
        y = y + jnp.where(row >= 2, pltpu.roll(x, 2, 0), 0.0) * w[1:2, :]
        y = y + jnp.where(row >= 1, pltpu.roll(x, 1, 0), 0.0) * w[2:3, :]
        y = y + x * w[3:4, :]
        return y * _sigmoid(y)

    def l2n(x):
        return x * lax.rsqrt(jnp.sum(x * x, axis=-1, keepdims=True) + EPS)

    q_scr[...] = l2n(conv_silu(qx_ref[...], cwq_ref[...])) * (GDN_HEAD_DIM ** -0.5)
    k_scr[...] = l2n(conv_silu(kx_ref[...], cwk_ref[...]))
    v_scr[...] = conv_silu(vx_ref[...], cwv_ref[...])
    ab = ab_ref[...]
    a_col = jnp.sum(jnp.where(lane == h, ab, 0.0), axis=-1, keepdims=True)
    b_col = jnp.sum(jnp.where(lane == h + GDN_HEADS, ab, 0.0), axis=-1, keepdims=True)
    g_scr[...] = -jnp.exp(alog_ref[...]) * _softplus(a_col + dtb_ref[...])
    beta_scr[...] = jnp.broadcast_to(_sigmoid(b_col), (s, LANES))

    ri = lax.broadcasted_iota(jnp.int32, (c, c), 0)
    ci = lax.broadcasted_iota(jnp.int32, (c, c), 1)
    lower_incl = ci <= ri
    lower_strict = ci < ri
    ltri = lower_incl.astype(F32)
    eye = (ci == ri).astype(F32)
    ones = jnp.ones((c, c), F32)
    og = og_ref[...]

    def blk(n, state):
        r0 = pl.multiple_of(n * c, c)
        q = q_scr[pl.ds(r0, c), :]
        k = k_scr[pl.ds(r0, c), :]
        v = v_scr[pl.ds(r0, c), :]
        g = g_scr[pl.ds(r0, c), :]
        beta = beta_scr[pl.ds(r0, c), :]
        gc = _dot_f32(ltri, g)
        gc_row = _dot_f32(ones, gc * eye)
        decay = jnp.exp(jnp.where(lower_incl, gc - gc_row, -jnp.inf))
        kb = k * beta
        kbf = k.astype(BF16)
        a = jnp.where(lower_strict, _dot_nt(kb.astype(BF16), kbf) * decay, 0.0)
        nmat = -a
        xp = a
        for _ in range(6):
            xp = _dot3(xp, xp)
            nmat = nmat + xp + _dot3(nmat, xp)
        eg = jnp.exp(gc)
        rhs = jnp.concatenate([v * beta, kb * eg], axis=1)
        uw = rhs + _dot(nmat.astype(BF16), rhs.astype(BF16))
        u = uw[:, :LANES]
        w = uw[:, LANES:]
        attn = jnp.where(lower_incl, _dot_nt(q.astype(BF16), kbf) * decay, 0.0)
        gl = gc[c - 1:c, :]
        qd = q * eg
        kd = k * jnp.exp(gl - gc)
        sb = state.astype(BF16)
        v_new = u - _dot(w.astype(BF16), sb)
        vnb = v_new.astype(BF16)
        o = _dot(qd.astype(BF16), sb) + _dot(attn.astype(BF16), vnb)
        state = state * jnp.exp(gl) + _dot_tn(kd.astype(BF16), vnb)
        on = o * lax.rsqrt(jnp.mean(o * o, axis=-1, keepdims=True) + EPS) * og
        z = zx_ref[pl.ds(r0, c), :]
        o_ref[pl.ds(r0, c), :] = on * (z * _sigmoid(z))
        return state

    lax.fori_loop(0, s // c, blk, jnp.zeros((GDN_HEAD_DIM, GDN_HEAD_DIM), F32))


def _gdn(proj3, ab3, conv_w, alog_b, dtb_b, og):
    b, s, _ = proj3.shape
    base = 3 * SB_WIDTH // LANES
    nh = GDN_HEADS
    seq_spec = lambda off: pl.BlockSpec((None, s, LANES), lambda bi, h: (bi, 0, off + h))
    cw_spec = lambda off: pl.BlockSpec((GDN_CONV, LANES), lambda bi, h: (0, off + h))
    return pl.pallas_call(
        functools.partial(_gdn_body, s=s),
        grid=(b, nh),
        in_specs=[
            seq_spec(base), seq_spec(base + nh), seq_spec(base + 2 * nh), seq_spec(base + 3 * nh),
            pl.BlockSpec((None, s, LANES), lambda bi, h: (bi, 0, 0)),
            cw_spec(0), cw_spec(nh), cw_spec(2 * nh),
            pl.BlockSpec((None, 1, LANES), lambda bi, h: (h, 0, 0)),
            pl.BlockSpec((None, 1, LANES), lambda bi, h: (h, 0, 0)),
            pl.BlockSpec((1, LANES), lambda bi, h: (0, 0)),
        ],
        out_specs=pl.BlockSpec((None, s, LANES), lambda bi, h: (bi, 0, h)),
        out_shape=jax.ShapeDtypeStruct((b, s, GDN_WIDTH), F32),
        scratch_shapes=[pltpu.VMEM((s, LANES), F32) for _ in range(5)],
        compiler_params=_cparams(("parallel", "parallel")),
        name="gdn",
    )(proj3, proj3, proj3, proj3, ab3, conv_w, conv_w, conv_w, alog_b, dtb_b, og)


def _outproj_body(sb_ref, gd_ref, x_ref, wo1_ref, wo2_ref, g2_ref, wqt_ref, x1_ref, hnt_ref, pqt_ref):
    mix = _dot(sb_ref[...].astype(BF16), wo1_ref[...]) + _dot(gd_ref[...].astype(BF16), wo2_ref[...])
    x1 = x_ref[...] + mix
    x1_ref[...] = x1
    ms = jnp.mean(x1 * x1, axis=-1, keepdims=True)
    hn = x1 * lax.rsqrt(ms + EPS) * g2_ref[...]
    hnt = hn.T.astype(BF16)
    hnt_ref[...] = hnt
    pqt_ref[...] = _dot(wqt_ref[...], hnt)


def _outproj(sb_o, gd_o, x2, wo1, wo2, g2, wqt, tm):
    t, d = x2.shape
    nq = wqt.shape[0]
    return pl.pallas_call(
        _outproj_body,
        grid=(t // tm,),
        in_specs=[
            pl.BlockSpec((tm, SB_WIDTH), lambda i: (i, 0)),
            pl.BlockSpec((tm, GDN_WIDTH), lambda i: (i, 0)),
            pl.BlockSpec((tm, d), lambda i: (i, 0)),
            pl.BlockSpec((SB_WIDTH, d), lambda i: (0, 0)),
            pl.BlockSpec((GDN_WIDTH, d), lambda i: (0, 0)),
            pl.BlockSpec((1, d), lambda i: (0, 0)),
            pl.BlockSpec((nq, d), lambda i: (0, 0)),
        ],
        out_specs=[
            pl.BlockSpec((tm, d), lambda i: (i, 0)),
            pl.BlockSpec((d, tm), lambda i: (0, i)),
            pl.BlockSpec((nq, tm), lambda i: (0, i)),
        ],
        out_shape=[
            jax.ShapeDtypeStruct((t, d), F32),
            jax.ShapeDtypeStruct((d, t), BF16),
            jax.ShapeDtypeStruct((nq, t), F32),
        ],
        compiler_params=_cparams(("parallel",)),
        name="outproj",
    )(sb_o, gd_o, x2, wo1, wo2, g2, wqt)


def _extract_topk(s, ids, k):
    big = jnp.int32(2 ** 30)
    work = s
    vals = []
    rank = jnp.full(s.shape, float(k), F32)
    for r in range(k):
        m = jnp.max(work, axis=0, keepdims=True)
        first = jnp.min(jnp.where(work == m, ids, big), axis=0, keepdims=True)
        hit = ids == first
        rank = jnp.where(hit, float(r), rank)
        work = jnp.where(hit, -jnp.inf, work)
        vals.append(m)
    return vals, rank


def _route_body(pqt_ref, k1_ref, k2_ref, rank2_ref, cnt_ref, e1_ref, e2z_ref):
    kk = PEER_TOPK
    tn = pqt_ref.shape[1]
    key_ids = lax.broadcasted_iota(jnp.int32, (PEER_NKEYS, tn), 0)
    i8 = lax.broadcasted_iota(jnp.int32, (8, tn), 0)
    i16 = lax.broadcasted_iota(jnp.int32, (kk, tn), 0)
    cand_ids = jnp.concatenate(
        [i16] + [i8 + a * kk for a in (1, 2, 3)]
        + [jnp.where(i8 >= 4, i8 * kk + b, kk * kk + i8 * kk + b) for b in (0, 1, 2)] + [(i8 + 8) * kk], axis=0)
    for h in range(PEER_HEADS):
        q1 = pqt_ref[pl.ds(h * 2 * PEER_HALF, PEER_HALF), :].astype(BF16)
        q2 = pqt_ref[pl.ds(h * 2 * PEER_HALF + PEER_HALF, PEER_HALF), :].astype(BF16)
        s1 = _dot(k1_ref[h], q1)
        s2 = _dot(k2_ref[h], q2)
        v1, rank1 = _extract_topk(s1, key_ids, kk)
        v2, rank2 = _extract_topk(s2, key_ids, kk)
        v1a = jnp.concatenate(v1, axis=0)
        v2a = jnp.concatenate(v2, axis=0)
        cand = jnp.concatenate(
            [v1[0] + v2a] + [v1[a] + v2a[:8] for a in (1, 2, 3)]
            + [jnp.where(i8 >= 4, v1a[:8] + v2[b], -jnp.inf) for b in (0, 1, 2)] + [v1a[8:] + v2[0]], axis=0)
        top, crank = _extract_topk(cand, cand_ids, kk)
        zsum = jnp.ones_like(top[0])
        for r in range(1, kk):
            zsum = zsum + jnp.exp(top[r] - top[0])
        sel = (crank < float(kk)).astype(F32)
        low = [jnp.sum(sel[0:16], axis=0, keepdims=True)] + [
            jnp.sum(sel[8 + 8 * a:16 + 8 * a], axis=0, keepdims=True) for a in (1, 2, 3)]
        mid = sel[40:48] + sel[48:56] + sel[56:64]
        high = sel[64:72]
        cnt = jnp.zeros(s1.shape, F32)
        for a in range(kk):
            n_a = low[a] if a < 4 else (mid[a:a + 1] if a < 8 else high[a - 8:a - 7])
            cnt = jnp.where(rank1 == float(a), n_a, cnt)
        rank2_ref[h] = rank2.astype(BF16)
        cnt_ref[h] = cnt
        e1_ref[h] = jnp.exp(s1 - v1[0])
        e2z_ref[h] = (jnp.exp(s2 - v2[0]) / zsum).astype(BF16)


def _route(pqt, k1, k2, tn):
    nq, t = pqt.shape
    hk = (PEER_HEADS, PEER_NKEYS, PEER_HALF)
    out = lambda dt: jax.ShapeDtypeStruct((PEER_HEADS, PEER_NKEYS, t), dt)
    ospec = pl.BlockSpec((PEER_HEADS, PEER_NKEYS, tn), lambda i: (0, 0, i))
    return pl.pallas_call(
        _route_body,
        grid=(t // tn,),
        in_specs=[
            pl.BlockSpec((nq, tn), lambda i: (0, i)),
            pl.BlockSpec(hk, lambda i: (0, 0, 0)),
            pl.BlockSpec(hk, lambda i: (0, 0, 0)),
        ],
        out_specs=[ospec, ospec, ospec, ospec],
        out_shape=[out(BF16), out(F32), out(F32), out(BF16)],
        compiler_params=_cparams(("parallel",)),
        name="peer_route",
    )(pqt, k1, k2)


PEER_I1_PER_STEP = 8
PEER_I1_PER_CHUNK = 2


def _expert_body(hnt_ref, u_ref, vt_ref, rank2_ref, cnt_ref, e1_ref, e2z_ref, x1_ref, o_ref, acc_scr):
    j = pl.program_id(1)

    @pl.when(j == 0)
    def _():
        acc_scr[...] = jnp.zeros_like(acc_scr)

    hnt = hnt_ref[...]
    ce = PEER_I1_PER_CHUNK * PEER_NKEYS
    nchunk = PEER_I1_PER_STEP // PEER_I1_PER_CHUNK

    def pre_act(c):
        return _dot(u_ref[pl.ds(c * ce, ce), :], hnt)

    pres = [pre_act(0), pre_act(1)] + [None] * (nchunk - 2)
    total = None
    for c in range(nchunk):
        pre = pres[c]
        act = (0.5 * pre * (1.0 + lax.erf(pre * (2.0 ** -0.5)))).astype(BF16)
        if c + 2 < nchunk:
            pres[c + 2] = pre_act(c + 2)
        gates = []
        for l in range(c * PEER_I1_PER_CHUNK, (c + 1) * PEER_I1_PER_CHUNK):
            gate = None
            for h in range(PEER_HEADS):
                cnt = cnt_ref[h, l:l + 1, :].astype(BF16)
                e1 = e1_ref[h, l:l + 1, :].astype(BF16)
                term = jnp.where(rank2_ref[h] < cnt, e2z_ref[h] * e1, 0.0)
                gate = term if gate is None else gate + term
            gates.append(gate)
        p = jnp.concatenate(gates, axis=0) * act
        part = _dot(vt_ref[:, pl.ds(c * ce, ce)], p)
        total = part if total is None else total + part
    acc_scr[...] += total

    @pl.when(j == pl.num_programs(1) - 1)
    def _():
        o_ref[...] = x1_ref[...] + acc_scr[...].T


def _experts(hnt, u_b, vt_b, rank2, cnt, e1, e2z, x1, tn):
    d, t = hnt.shape
    ne = u_b.shape[0]
    eb = PEER_I1_PER_STEP * PEER_NKEYS
    full = pl.BlockSpec((PEER_HEADS, PEER_NKEYS, tn), lambda i, j: (0, 0, i))
    part = pl.BlockSpec((PEER_HEADS, PEER_I1_PER_STEP, tn), lambda i, j: (0, j, i))
    return pl.pallas_call(
        _expert_body,
        grid=(t // tn, ne // eb),
        in_specs=[
            pl.BlockSpec((d, tn), lambda i, j: (0, i)),
            pl.BlockSpec((eb, d), lambda i, j: (j, 0)),
            pl.BlockSpec((d, eb), lambda i, j: (0, j)),
            full, part, part, full,
            pl.BlockSpec((tn, d), lambda i, j: (i, 0)),
        ],
        out_specs=pl.BlockSpec((tn, d), lambda i, j: (i, 0)),
        out_shape=jax.ShapeDtypeStruct((t, d), F32),
        scratch_shapes=[pltpu.VMEM((d, tn), F32)],
        compiler_params=_cparams(("parallel", "arbitrary")),
        name="peer_experts",
    )(hnt, u_b, vt_b, rank2, cnt, e1, e2z, x1)


def _suffix_sum_matrix(tk):
    r = jnp.arange(2 * tk)[:, None] % tk
    c = jnp.arange(2 * tk)[None, :]
    return jnp.where(c < tk, r > c, True).astype(BF16)


def _layer(x, norm1_g, w_in, sb_q_g, sb_k_g, sb_o_g, conv_w, a_log, dt_bias, gdn_o_g, w_out, norm2_g,
           w_q, keys1, keys2, u_tab, v_tab):
    b, s, d = x.shape
    t = b * s
    n_main = 3 * SB_WIDTH + 4 * GDN_WIDTH
    x2 = x.reshape(t, d)
    w_main = w_in[:, :n_main].astype(BF16)
    w_ab = jnp.pad(w_in[:, n_main:], ((0, 0), (0, LANES - 2 * GDN_HEADS))).astype(BF16)
    tm = min(256, t)
    proj, ab = _inproj(x2, norm1_g.reshape(1, d), w_main, w_ab, tm)
    proj3 = proj.reshape(b, s, n_main)
    ab3 = ab.reshape(b, s, LANES)

    tq, tk = min(256, s), 128
    tile2 = lambda g: jnp.tile(g, 2).reshape(1, LANES)
    sb_o = _sb_attention(proj3, tile2(sb_q_g), tile2(sb_k_g), tile2(sb_o_g), _suffix_sum_matrix(tk), tq, tk)

    bcast = lambda p: jnp.broadcast_to(p[:, None, None], (GDN_HEADS, 1, LANES))
    gd_o = _gdn(proj3, ab3, conv_w, bcast(a_log), bcast(dt_bias), gdn_o_g.reshape(1, LANES))

    wo = w_out.astype(BF16)
    x1, hnt, pqt = _outproj(sb_o.reshape(t, SB_WIDTH), gd_o.reshape(t, GDN_WIDTH), x2, wo[:SB_WIDTH],
                            wo[SB_WIDTH:], norm2_g.reshape(1, d), w_q.T.astype(BF16), tm)

    rank2, cnt, e1, e2z = _route(pqt, keys1.astype(BF16), keys2.astype(BF16), min(256, t))
    y = _experts(hnt, u_tab.astype(BF16), v_tab.T.astype(BF16), rank2, cnt, e1, e2z, x1, min(512, t))
    return y.reshape(b, s, d)


def kernel(x, norm1_g, w_in, sb_q_norm_g, sb_k_norm_g, sb_out_norm_g, gdn_conv_w, gdn_a_log, gdn_dt_bias,
           gdn_out_norm_g, w_out, norm2_g, peer_w_q, peer_keys1, peer_keys2, peer_u, peer_v):
    for layer in range(norm1_g.shape[0]):
        x = _layer(x, norm1_g[layer], w_in[layer], sb_q_norm_g[layer], sb_k_norm_g[layer],
                   sb_out_norm_g[layer], gdn_conv_w[layer], gdn_a_log[layer], gdn_dt_bias[layer],
                   gdn_out_norm_g[layer], w_out[layer], norm2_g[layer], peer_w_q[layer],
                   peer_keys1[layer], peer_keys2[layer], peer_u[layer], peer_v[layer])
    return x
```

```python
import functools

import jax
import jax.numpy as jnp
from jax import lax
from jax.experimental import pallas as pl
from jax.experimental.pallas import tpu as pltpu

F32 = jnp.float32
BF16 = jnp.bfloat16
EPS = 1e-6

SB_HEADS = 8
SB_HEAD_DIM = 64
SB_WIDTH = SB_HEADS * SB_HEAD_DIM
GDN_HEADS = 4
GDN_HEAD_DIM = 128
GDN_WIDTH = GDN_HEADS * GDN_HEAD_DIM
GDN_CONV = 4
PEER_HEADS = 8
PEER_NKEYS = 128
PEER_HALF = 128
PEER_TOPK = 16
LANES = 128

VMEM_LIMIT = 56 * 1024 * 1024


def _cparams(sem):
    return pltpu.CompilerParams(dimension_semantics=sem, vmem_limit_bytes=VMEM_LIMIT)


def _dot(a, b):
    return jnp.dot(a, b, preferred_element_type=F32)


def _dot_nt(a, b):
    return lax.dot_general(a, b, (((1,), (1,)), ((), ())), preferred_element_type=F32)


def _dot_tn(a, b):
    return lax.dot_general(a, b, (((0,), (0,)), ((), ())), preferred_element_type=F32)


def _dot_f32(a, b):
    return jnp.dot(a, b, preferred_element_type=F32, precision=lax.Precision.HIGHEST)


def _split(a):
    hi = a.astype(BF16)
    return hi, (a - hi.astype(F32)).astype(BF16)


def _dot3(a, b):
    ah, al = _split(a)
    bh, bl = _split(b)
    return _dot(jnp.concatenate([ah, ah, al], axis=1), jnp.concatenate([bh, bl, bh], axis=0))


def _softplus(x):
    return jnp.maximum(x, 0.0) + jnp.log1p(jnp.exp(-jnp.abs(x)))


def _sigmoid(x):
    return 1.0 / (1.0 + jnp.exp(-x))


def _inproj_body(x_ref, g_ref, w_ref, wab_ref, proj_ref, ab_ref):
    x = x_ref[...]
    ms = jnp.mean(x * x, axis=-1, keepdims=True)
    h = (x * lax.rsqrt(ms + EPS) * g_ref[...]).astype(BF16)
    proj_ref[...] = _dot(h, w_ref[...])
    ab_ref[...] = _dot(h, wab_ref[...])


def _inproj(x2, g, w_main, w_ab, tm):
    t, d = x2.shape
    n = w_main.shape[1]
    return pl.pallas_call(
        _inproj_body,
        grid=(t // tm,),
        in_specs=[
            pl.BlockSpec((tm, d), lambda i: (i, 0)),
            pl.BlockSpec((1, d), lambda i: (0, 0)),
            pl.BlockSpec((d, n), lambda i: (0, 0)),
            pl.BlockSpec((d, LANES), lambda i: (0, 0)),
        ],
        out_specs=[
            pl.BlockSpec((tm, n), lambda i: (i, 0)),
            pl.BlockSpec((tm, LANES), lambda i: (i, 0)),
        ],
        out_shape=[
            jax.ShapeDtypeStruct((t, n), F32),
            jax.ShapeDtypeStruct((t, LANES), F32),
        ],
        compiler_params=_cparams(("parallel",)),
        name="inproj",
    )(x2, g, w_main, w_ab)


SB_DEAD_LOG = -104.0


def _sb_body(q_ref, k_ref, v_ref, gq_ref, gk_ref, go_ref, m2_ref, o_ref, kn_scr, vb_scr, *, tq, tk):
    i = pl.program_id(2)
    lane = lax.broadcasted_iota(jnp.int32, (1, LANES), 1)
    is0 = lane < SB_HEAD_DIM

    def headnorm(x, g):
        x2 = x * x
        s0 = jnp.sum(jnp.where(is0, x2, 0.0), axis=-1, keepdims=True)
        s1 = jnp.sum(jnp.where(is0, 0.0, x2), axis=-1, keepdims=True)
        ms = jnp.where(is0, s0, s1) * (1.0 / SB_HEAD_DIM)
        return x * lax.rsqrt(ms + EPS) * g

    @pl.when(i == 0)
    def _():
        kn_scr[...] = headnorm(k_ref[...], gk_ref[...]).astype(BF16)
        vb_scr[...] = v_ref[...].astype(BF16)

    qn = headnorm(q_ref[...], gq_ref[...]) * (SB_HEAD_DIM ** -0.5)
    qh = (jnp.where(is0, qn, 0.0).astype(BF16), jnp.where(is0, 0.0, qn).astype(BF16))
    m2 = m2_ref[...]
    row = i * tq + lax.broadcasted_iota(jnp.int32, (tq, tk), 0)
    col0 = lax.broadcasted_iota(jnp.int32, (tq, tk), 1)
    nkb = (i + 1) * (tq // tk)

    def scores(j):
        kj = kn_scr[pl.ds(pl.multiple_of(j * tk, tk), tk), :]
        return [_dot_nt(qh[h], kj) for h in range(2)]

    def logs(j, zs):
        causal = (col0 + j * tk) < row
        zls, cats = [], []
        for z in zs:
            lk = jnp.where(causal, -(jnp.maximum(z, 0.0) + jnp.log(1.0 + jnp.exp(-jnp.abs(z)))), 0.0)
            hi = lk.astype(BF16)
            lo = (lk - hi.astype(F32)).astype(BF16)
            zls.append(jnp.where(causal, z + lk, -jnp.inf))
            cats.append(jnp.concatenate([hi, lo], axis=1))
        return zls, cats

    def sums(cats):
        return [_dot(c, m2) for c in cats]

    def body(state):
        n, _, carry, zls, rts = state
        j = nkb - 1 - n
        rest_next = [carry[2 * h + 1] + rts[h][:, tk:] for h in range(2)]
        alive = (jnp.max(jnp.maximum(rest_next[0], rest_next[1])) > SB_DEAD_LOG).astype(jnp.int32)
        jn = jnp.maximum(j - 1, 0)
        zs_next = scores(jn)
        vj = vb_scr[pl.ds(pl.multiple_of(j * tk, tk), tk), :]
        ws = [jnp.exp(zls[h] + (carry[2 * h + 1] + rts[h][:, :tk])).astype(BF16) for h in range(2)]
        pvs = [_dot(w, vj) for w in ws]
        zls_next, cats = logs(jn, zs_next)
        rts_next = sums(cats)
        out = []
        for h in range(2):
            out += [carry[2 * h] + pvs[h], rest_next[h]]
        return n + 1, alive, tuple(out), tuple(zls_next), tuple(rts_next)

    def cond(state):
        return jnp.logical_and(state[0] < nkb, state[1] > 0)

    zero = jnp.zeros((tq, LANES), F32)
    zls0, cats0 = logs(nkb - 1, scores(nkb - 1))
    init = (jnp.int32(0), jnp.int32(1), (zero, zero, zero, zero), tuple(zls0), tuple(sums(cats0)))
    res = lax.while_loop(cond, body, init)[2]
    o = jnp.where(is0, res[0], res[2])
    o_ref[...] = headnorm(o, go_ref[...])


def _sb_attention(proj3, gq, gk, go, m2, tq, tk):
    b, s, _ = proj3.shape
    hp = SB_HEADS // 2
    return pl.pallas_call(
        functools.partial(_sb_body, tq=tq, tk=tk),
        grid=(b, hp, s // tq),
        in_specs=[
            pl.BlockSpec((None, tq, LANES), lambda bi, h, i: (bi, i, h)),
            pl.BlockSpec((None, s, LANES), lambda bi, h, i: (bi, 0, hp + h)),
            pl.BlockSpec((None, s, LANES), lambda bi, h, i: (bi, 0, 2 * hp + h)),
            pl.BlockSpec((1, LANES), lambda bi, h, i: (0, 0)),
            pl.BlockSpec((1, LANES), lambda bi, h, i: (0, 0)),
            pl.BlockSpec((1, LANES), lambda bi, h, i: (0, 0)),
            pl.BlockSpec((2 * tk, 2 * tk), lambda bi, h, i: (0, 0)),
        ],
        out_specs=pl.BlockSpec((None, tq, LANES), lambda bi, h, i: (bi, i, h)),
        out_shape=jax.ShapeDtypeStruct((b, s, SB_WIDTH), F32),
        scratch_shapes=[pltpu.VMEM((s, LANES), BF16), pltpu.VMEM((s, LANES), BF16)],
        compiler_params=_cparams(("parallel", "parallel", "arbitrary")),
        name="sb_attention",
    )(proj3, proj3, proj3, gq, gk, go, m2)


GDN_BLOCK = 128


GDN_HALO = 8


def _gdn_body(x_ref, halo_ref, z_ref, ab_ref, cw_ref, alog_ref, dtb_ref, og_ref, o_ref,
              q_scr, k_scr, v_scr, g_scr, beta_scr, state_scr, *, ts):
    t = pl.program_id(1)
    c = GDN_BLOCK
    nh = GDN_HEADS
    lane = lax.broadcasted_iota(jnp.int32, (1, LANES), 1)

    @pl.when(t == 0)
    def _():
        state_scr[...] = jnp.zeros_like(state_scr)

    halo = jnp.where(t > 0, halo_ref[...], 0.0)
    xe = jnp.concatenate([halo, x_ref[...]], axis=0)
    cw = cw_ref[...]
    y = xe * cw[3:4, :]
    for d in (1, 2, 3):
        y = y + pltpu.roll(xe, d, 0) * cw[3 - d:4 - d, :]
    y = y[GDN_HALO:, :]
    y = y * _sigmoid(y)

    def l2n(x):
        return x * lax.rsqrt(jnp.sum(x * x, axis=-1, keepdims=True) + EPS)

    ab = ab_ref[...]
    for h in range(nh):
        sl = pl.ds(h * LANES, LANES)
        q_scr[:, sl] = l2n(y[:, h * LANES:(h + 1) * LANES]) * (GDN_HEAD_DIM ** -0.5)
        k_scr[:, sl] = l2n(y[:, GDN_WIDTH + h * LANES:GDN_WIDTH + (h + 1) * LANES])
        a_col = jnp.sum(jnp.where(lane == h, ab, 0.0), axis=-1, keepdims=True)
        b_col = jnp.sum(jnp.where(lane == h + nh, ab, 0.0), axis=-1, keepdims=True)
        g_scr[:, sl] = -jnp.exp(alog_ref[h]) * _softplus(a_col + dtb_ref[h])
        beta_scr[:, sl] = jnp.broadcast_to(_sigmoid(b_col), (ts, LANES))
    v_scr[...] = y[:, 2 * GDN_WIDTH:]

    ri = lax.broadcasted_iota(jnp.int32, (c, nh * c), 0)
    ci = lax.broadcasted_iota(jnp.int32, (c, nh * c), 1) % c
    lower_incl = ci <= ri
    lower_strict = ci < ri
    eye = (ci == ri).astype(F32)
    ltri = lower_incl[:, :c].astype(F32)
    ones = jnp.ones((c, c), F32)
    og = og_ref[...]
    heads = range(nh)
    hs = lambda m, h: m[:, h * c:(h + 1) * c]

    def blk(n, states):
        r0 = pl.multiple_of(n * c, c)
        q = q_scr[pl.ds(r0, c), :]
        k = k_scr[pl.ds(r0, c), :]
        v = v_scr[pl.ds(r0, c), :]
        g = g_scr[pl.ds(r0, c), :]
        beta = beta_scr[pl.ds(r0, c), :]
        gc = _dot_f32(ltri, g)
        gc_row = _dot_f32(ones, gc * eye)
        decay = jnp.exp(jnp.where(lower_incl, gc - gc_row, -jnp.inf))
        kb = k * beta
        kbf = k.astype(BF16)
        kbb = kb.astype(BF16)
        qbf = q.astype(BF16)
        kk = jnp.concatenate([_dot_nt(hs(kbb, h), hs(kbf, h)) for h in heads], axis=1)
        qk = jnp.concatenate([_dot_nt(hs(qbf, h), hs(kbf, h)) for h in heads], axis=1)
        a = jnp.where(lower_strict, kk * decay, 0.0)
        attn = jnp.where(lower_incl, qk * decay, 0.0).astype(BF16)
        nmat = [-hs(a, h) for h in heads]
        xp = [hs(a, h) for h in heads]
        for _ in range(6):
            xp = [_dot3(x, x) for x in xp]
            prod = [_dot3(nm, x) for nm, x in zip(nmat, xp)]
            nmat = [nm + x + p for nm, x, p in zip(nmat, xp, prod)]
        eg = jnp.exp(gc)
        vb = v * beta
        kbd = kb * eg
        gl = gc[c - 1:c, :]
        qd = (q * eg).astype(BF16)
        kd = (k * jnp.exp(gl - gc)).astype(BF16)
        dl = jnp.exp(gl)
        nb = [nm.astype(BF16) for nm in nmat]
        u = [hs(vb, h) + _dot(nb[h], hs(vb, h).astype(BF16)) for h in heads]
        w = [(hs(kbd, h) + _dot(nb[h], hs(kbd, h).astype(BF16))).astype(BF16) for h in heads]
        sb = [st.astype(BF16) for st in states]
        v_new = [u[h] - _dot(w[h], sb[h]) for h in heads]
        o_state = [_dot(hs(qd, h), sb[h]) for h in heads]
        vnb = [vn.astype(BF16) for vn in v_new]
        o = [o_state[h] + _dot(hs(attn, h), vnb[h]) for h in heads]
        new_states = tuple(states[h] * hs(dl, h) + _dot_tn(hs(kd, h), vnb[h]) for h in heads)
        on = jnp.concatenate(
            [x * lax.rsqrt(jnp.mean(x * x, axis=-1, keepdims=True) + EPS) * og for x in o], axis=1)
        z = z_ref[pl.ds(r0, c), :]
        o_ref[pl.ds(r0, c), :] = on * (z * _sigmoid(z))
        return new_states

    states = lax.fori_loop(0, ts // c, blk, tuple(state_scr[h] for h in heads))
    for h in heads:
        state_scr[h] = states[h]


def _gdn(proj3, ab3, conv_w, alog_b, dtb_b, og, ts):
    b, s, _ = proj3.shape
    w3 = 3 * GDN_WIDTH
    assert 3 * SB_WIDTH == w3 and 2 * w3 % GDN_WIDTH == 0
    per = ts // GDN_HALO
    par = pl.BlockSpec((GDN_HEADS, 1, LANES), lambda bi, t: (0, 0, 0))
    return pl.pallas_call(
        functools.partial(_gdn_body, ts=ts),
        grid=(b, s // ts),
        in_specs=[
            pl.BlockSpec((None, ts, w3), lambda bi, t: (bi, t, 1)),
            pl.BlockSpec((None, GDN_HALO, w3), lambda bi, t: (bi, jnp.maximum(t * per - 1, 0), 1)),
            pl.BlockSpec((None, ts, GDN_WIDTH), lambda bi, t: (bi, t, 2 * w3 // GDN_WIDTH)),
            pl.BlockSpec((None, ts, LANES), lambda bi, t: (bi, t, 0)),
            pl.BlockSpec((GDN_CONV, w3), lambda bi, t: (0, 0)),
            par, par,
            pl.BlockSpec((1, LANES), lambda bi, t: (0, 0)),
        ],
        out_specs=pl.BlockSpec((None, ts, GDN_WIDTH), lambda bi, t: (bi, t, 0)),
        out_shape=jax.ShapeDtypeStruct((b, s, GDN_WIDTH), F32),
        scratch_shapes=[pltpu.VMEM((ts, GDN_WIDTH), F32) for _ in range(5)]
        + [pltpu.VMEM((GDN_HEADS, GDN_HEAD_DIM, GDN_HEAD_DIM), F32)],
        compiler_params=_cparams(("parallel", "arbitrary")),
        name="gdn",
    )(proj3, proj3, proj3, ab3, conv_w, alog_b, dtb_b, og)


def _outproj_body(sb_ref, gd_ref, x_ref, wo1_ref, wo2_ref, g2_ref, wqt_ref, x1_ref, hnt_ref, pqt_ref):
    mix = _dot(sb_ref[...].astype(BF16), wo1_ref[...]) + _dot(gd_ref[...].astype(BF16), wo2_ref[...])
    x1 = x_ref[...] + mix
    x1_ref[...] = x1
    ms = jnp.mean(x1 * x1, axis=-1, keepdims=True)
    hn = x1 * lax.rsqrt(ms + EPS) * g2_ref[...]
    hnt = hn.T.astype(BF16)
    hnt_ref[...] = hnt
    pqt_ref[...] = _dot(wqt_ref[...], hnt)


def _outproj(sb_o, gd_o, x2, wo1, wo2, g2, wqt, tm):
    t, d = x2.shape
    nq = wqt.shape[0]
    return pl.pallas_call(
        _outproj_body,
        grid=(t // tm,),
        in_specs=[
            pl.BlockSpec((tm, SB_WIDTH), lambda i: (i, 0)),
            pl.BlockSpec((tm, GDN_WIDTH), lambda i: (i, 0)),
            pl.BlockSpec((tm, d), lambda i: (i, 0)),
            pl.BlockSpec((SB_WIDTH, d), lambda i: (0, 0)),
            pl.BlockSpec((GDN_WIDTH, d), lambda i: (0, 0)),
            pl.BlockSpec((1, d), lambda i: (0, 0)),
            pl.BlockSpec((nq, d), lambda i: (0, 0)),
        ],
        out_specs=[
            pl.BlockSpec((tm, d), lambda i: (i, 0)),
            pl.BlockSpec((d, tm), lambda i: (0, i)),
            pl.BlockSpec((nq, tm), lambda i: (0, i)),
        ],
        out_shape=[
            jax.ShapeDtypeStruct((t, d), F32),
            jax.ShapeDtypeStruct((d, t), BF16),
            jax.ShapeDtypeStruct((nq, t), F32),
        ],
        compiler_params=_cparams(("parallel",)),
        name="outproj",
    )(sb_o, gd_o, x2, wo1, wo2, g2, wqt)


def _extract_topk(s, ids, k):
    big = jnp.int32(2 ** 30)
    work = s
    vals = []
    rank = jnp.full(s.shape, float(k), F32)
    for r in range(k):
        m = jnp.max(work, axis=0, keepdims=True)
        first = jnp.min(jnp.where(work == m, ids, big), axis=0, keepdims=True)
        hit = ids == first
        rank = jnp.where(hit, float(r), rank)
        work = jnp.where(hit, -jnp.inf, work)
        vals.append(m)
    return vals, rank


def _route_body(pqt_ref, k1_ref, k2_ref, rank2_ref, cnt_ref, e1_ref, e2z_ref):
    kk = PEER_TOPK
    tn = pqt_ref.shape[1]
    key_ids = lax.broadcasted_iota(jnp.int32, (PEER_NKEYS, tn), 0)
    i8 = lax.broadcasted_iota(jnp.int32, (8, tn), 0)
    i16 = lax.broadcasted_iota(jnp.int32, (kk, tn), 0)
    cand_ids = jnp.concatenate(
        [i16] + [i8 + a * kk for a in (1, 2, 3)]
        + [jnp.where(i8 >= 4, i8 * kk + b, kk * kk + i8 * kk + b) for b in (0, 1, 2)] + [(i8 + 8) * kk], axis=0)
    for h in range(PEER_HEADS):
        q1 = pqt_ref[pl.ds(h * 2 * PEER_HALF, PEER_HALF), :].astype(BF16)
        q2 = pqt_ref[pl.ds(h * 2 * PEER_HALF + PEER_HALF, PEER_HALF), :].astype(BF16)
        s1 = _dot(k1_ref[h], q1)
        s2 = _dot(k2_ref[h], q2)
        v1, rank1 = _extract_topk(s1, key_ids, kk)
        v2, rank2 = _extract_topk(s2, key_ids, kk)
        v1a = jnp.concatenate(v1, axis=0)
        v2a = jnp.concatenate(v2, axis=0)
        cand = jnp.concatenate(
            [v1[0] + v2a] + [v1[a] + v2a[:8] for a in (1, 2, 3)]
            + [jnp.where(i8 >= 4, v1a[:8] + v2[b], -jnp.inf) for b in (0, 1, 2)] + [v1a[8:] + v2[0]], axis=0)
        top, crank = _extract_topk(cand, cand_ids, kk)
        zsum = jnp.ones_like(top[0])
        for r in range(1, kk):
            zsum = zsum + jnp.exp(top[r] - top[0])
        sel = (crank < float(kk)).astype(F32)
        low = [jnp.sum(sel[0:16], axis=0, keepdims=True)] + [
            jnp.sum(sel[8 + 8 * a:16 + 8 * a], axis=0, keepdims=True) for a in (1, 2, 3)]
        mid = sel[40:48] + sel[48:56] + sel[56:64]
        high = sel[64:72]
        cnt = jnp.zeros(s1.shape, F32)
        for a in range(kk):
            n_a = low[a] if a < 4 else (mid[a:a + 1] if a < 8 else high[a - 8:a - 7])
            cnt = jnp.where(rank1 == float(a), n_a, cnt)
        rank2_ref[h] = rank2.astype(BF16)
        cnt_ref[h] = cnt
        e1_ref[h] = jnp.exp(s1 - v1[0])
        e2z_ref[h] = (jnp.exp(s2 - v2[0]) / zsum).astype(BF16)


def _route(pqt, k1, k2, tn):
    nq, t = pqt.shape
    hk = (PEER_HEADS, PEER_NKEYS, PEER_HALF)
    out = lambda dt: jax.ShapeDtypeStruct((PEER_HEADS, PEER_NKEYS, t), dt)
    ospec = pl.BlockSpec((PEER_HEADS, PEER_NKEYS, tn), lambda i: (0, 0, i))
    return pl.pallas_call(
        _route_body,
        grid=(t // tn,),
        in_specs=[
            pl.BlockSpec((nq, tn), lambda i: (0, i)),
            pl.BlockSpec(hk, lambda i: (0, 0, 0)),
            pl.BlockSpec(hk, lambda i: (0, 0, 0)),
        ],
        out_specs=[ospec, ospec, ospec, ospec],
        out_shape=[out(BF16), out(F32), out(F32), out(BF16)],
        compiler_params=_cparams(("parallel",)),
        name="peer_route",
    )(pqt, k1, k2)


PEER_I1_PER_STEP = 8
PEER_I1_PER_CHUNK = 2


def _expert_body(hnt_ref, u_ref, vt_ref, rank2_ref, cnt_ref, e1_ref, e2z_ref, x1_ref, o_ref, acc_scr):
    j = pl.program_id(1)

    @pl.when(j == 0)
    def _():
        acc_scr[...] = jnp.zeros_like(acc_scr)

    hnt = hnt_ref[...]
    ce = PEER_I1_PER_CHUNK * PEER_NKEYS
    nchunk = PEER_I1_PER_STEP // PEER_I1_PER_CHUNK

    def pre_act(c):
        return _dot(u_ref[pl.ds(c * ce, ce), :], hnt)

    pres = [pre_act(0), pre_act(1)] + [None] * (nchunk - 2)
    total = None
    for c in range(nchunk):
        pre = pres[c]
        act = (0.5 * pre * (1.0 + lax.erf(pre * (2.0 ** -0.5)))).astype(BF16)
        if c + 2 < nchunk:
            pres[c + 2] = pre_act(c + 2)
        gates = []
        for l in range(c * PEER_I1_PER_CHUNK, (c + 1) * PEER_I1_PER_CHUNK):
            gate = None
            for h in range(PEER_HEADS):
                cnt = cnt_ref[h, l:l + 1, :].astype(BF16)
                e1 = e1_ref[h, l:l + 1, :].astype(BF16)
                term = jnp.where(rank2_ref[h] < cnt, e2z_ref[h] * e1, 0.0)
                gate = term if gate is None else gate + term
            gates.append(gate)
        p = jnp.concatenate(gates, axis=0) * act
        part = _dot(vt_ref[:, pl.ds(c * ce, ce)], p)
        total = part if total is None else total + part
    acc_scr[...] += total

    @pl.when(j == pl.num_programs(1) - 1)
    def _():
        o_ref[...] = x1_ref[...] + acc_scr[...].T


def _experts(hnt, u_b, vt_b, rank2, cnt, e1, e2z, x1, tn):
    d, t = hnt.shape
    ne = u_b.shape[0]
    eb = PEER_I1_PER_STEP * PEER_NKEYS
    full = pl.BlockSpec((PEER_HEADS, PEER_NKEYS, tn), lambda i, j: (0, 0, i))
    part = pl.BlockSpec((PEER_HEADS, PEER_I1_PER_STEP, tn), lambda i, j: (0, j, i))
    return pl.pallas_call(
        _expert_body,
        grid=(t // tn, ne // eb),
        in_specs=[
            pl.BlockSpec((d, tn), lambda i, j: (0, i)),
            pl.BlockSpec((eb, d), lambda i, j: (j, 0)),
            pl.BlockSpec((d, eb), lambda i, j: (0, j)),
            full, part, part, full,
            pl.BlockSpec((tn, d), lambda i, j: (i, 0)),
        ],
        out_specs=pl.BlockSpec((tn, d), lambda i, j: (i, 0)),
        out_shape=jax.ShapeDtypeStruct((t, d), F32),
        scratch_shapes=[pltpu.VMEM((d, tn), F32)],
        compiler_params=_cparams(("parallel", "arbitrary")),
        name="peer_experts",
    )(hnt, u_b, vt_b, rank2, cnt, e1, e2z, x1)


def _suffix_sum_matrix(tk):
    r = jnp.arange(2 * tk)[:, None] % tk
    c = jnp.arange(2 * tk)[None, :]
    return jnp.where(c < tk, r > c, True).astype(BF16)


def _layer(x, norm1_g, w_in, sb_q_g, sb_k_g, sb_o_g, conv_w, a_log, dt_bias, gdn_o_g, w_out, norm2_g,
           w_q, keys1, keys2, u_tab, v_tab):
    b, s, d = x.shape
    t = b * s
    n_main = 3 * SB_WIDTH + 4 * GDN_WIDTH
    x2 = x.reshape(t, d)
    w_main = w_in[:, :n_main].astype(BF16)
    w_ab = jnp.pad(w_in[:, n_main:], ((0, 0), (0, LANES - 2 * GDN_HEADS))).astype(BF16)
    tm = min(256, t)
    proj, ab = _inproj(x2, norm1_g.reshape(1, d), w_main, w_ab, tm)
    proj3 = proj.reshape(b, s, n_main)
    ab3 = ab.reshape(b, s, LANES)

    tq, tk = min(256, s), 128
    tile2 = lambda g: jnp.tile(g, 2).reshape(1, LANES)
    sb_o = _sb_attention(proj3, tile2(sb_q_g), tile2(sb_k_g), tile2(sb_o_g), _suffix_sum_matrix(tk), tq, tk)

    bcast = lambda p: jnp.broadcast_to(p[:, None, None], (GDN_HEADS, 1, LANES))
    gd_o = _gdn(proj3, ab3, conv_w, bcast(a_log), bcast(dt_bias), gdn_o_g.reshape(1, LANES), min(512, s))

    wo = w_out.astype(BF16)
    x1, hnt, pqt = _outproj(sb_o.reshape(t, SB_WIDTH), gd_o.reshape(t, GDN_WIDTH), x2, wo[:SB_WIDTH],
                            wo[SB_WIDTH:], norm2_g.reshape(1, d), w_q.T.astype(BF16), tm)

    rank2, cnt, e1, e2z = _route(pqt, keys1.astype(BF16), keys2.astype(BF16), min(256, t))
    y = _experts(hnt, u_tab.astype(BF16), v_tab.T.astype(BF16), rank2, cnt, e1, e2z, x1, min(512, t))
    return y.reshape(b, s, d)


def kernel(x, norm1_g, w_in, sb_q_norm_g, sb_k_norm_g, sb_out_norm_g, gdn_conv_w, gdn_a_log, gdn_dt_bias,
           gdn_out_norm_g, w_out, norm2_g, peer_w_q, peer_keys1, peer_keys2, peer_u, peer_v):
    for layer in range(norm1_g.shape[0]):
        x = _layer(x, norm1_g[layer], w_in[layer], sb_q_norm_g[layer], sb_k_norm_g[layer],
                   sb_out_norm_g[layer], gdn_conv_w[layer], gdn_a_log[layer], gdn_dt_bias[layer],
                   gdn_out_norm_g[layer], w_out[layer], norm2_g[layer], peer_w_q[layer],
                   peer_keys1[layer], peer_keys2[layer], peer_u[layer], peer_v[layer])
    return x
```

```python
import functools

import jax
import jax.numpy as jnp
from jax import lax
from jax.experimental import pallas as pl
from jax.experimental.pallas import tpu as pltpu

F32 = jnp.float32
BF16 = jnp.bfloat16
EPS = 1e-6

SB_HEADS = 8
SB_HEAD_DIM = 64
SB_WIDTH = SB_HEADS * SB_HEAD_DIM
GDN_HEADS = 4
GDN_HEAD_DIM = 128
GDN_WIDTH = GDN_HEADS * GDN_HEAD_DIM
GDN_CONV = 4
PEER_HEADS = 8
PEER_NKEYS = 128
PEER_HALF = 128
PEER_TOPK = 16
LANES = 128

VMEM_LIMIT = 56 * 1024 * 1024


def _cparams(sem):
    return pltpu.CompilerParams(dimension_semantics=sem, vmem_limit_bytes=VMEM_LIMIT)


def _dot(a, b):
    return jnp.dot(a, b, preferred_element_type=F32)


def _dot_nt(a, b):
    return lax.dot_general(a, b, (((1,), (1,)), ((), ())), preferred_element_type=F32)


def _dot_tn(a, b):
    return lax.dot_general(a, b, (((0,), (0,)), ((), ())), preferred_element_type=F32)


def _dot_f32(a, b):
    return jnp.dot(a, b, preferred_element_type=F32, precision=lax.Precision.HIGHEST)


def _split(a):
    hi = a.astype(BF16)
    return hi, (a - hi.astype(F32)).astype(BF16)


def _dot3(a, b):
    ah, al = _split(a)
    bh, bl = _split(b)
    return _dot(jnp.concatenate([ah, ah, al], axis=1), jnp.concatenate([bh, bl, bh], axis=0))


def _softplus(x):
    return jnp.maximum(x, 0.0) + jnp.log1p(jnp.exp(-jnp.abs(x)))


def _sigmoid(x):
    return 1.0 / (1.0 + jnp.exp(-x))


def _inproj_body(x_ref, g_ref, w_ref, wab_ref, proj_ref, ab_ref):
    x = x_ref[...]
    ms = jnp.mean(x * x, axis=-1, keepdims=True)
    h = (x * lax.rsqrt(ms + EPS) * g_ref[...]).astype(BF16)
    proj_ref[...] = _dot(h, w_ref[...])
    ab_ref[...] = _dot(h, wab_ref[...])


def _inproj(x2, g, w_main, w_ab, tm):
    t, d = x2.shape
    n = w_main.shape[1]
    return pl.pallas_call(
        _inproj_body,
        grid=(t // tm,),
        in_specs=[
            pl.BlockSpec((tm, d), lambda i: (i, 0)),
            pl.BlockSpec((1, d), lambda i: (0, 0)),
            pl.BlockSpec((d, n), lambda i: (0, 0)),
            pl.BlockSpec((d, LANES), lambda i: (0, 0)),
        ],
        out_specs=[
            pl.BlockSpec((tm, n), lambda i: (i, 0)),
            pl.BlockSpec((tm, LANES), lambda i: (i, 0)),
        ],
        out_shape=[
            jax.ShapeDtypeStruct((t, n), F32),
            jax.ShapeDtypeStruct((t, LANES), F32),
        ],
        compiler_params=_cparams(("parallel",)),
        name="inproj",
    )(x2, g, w_main, w_ab)


SB_DEAD_LOG = -104.0


def _sb_body(q_ref, k_ref, v_ref, gq_ref, gk_ref, go_ref, m2_ref, o_ref, kn_scr, vb_scr, *, tq, tk):
    i = pl.program_id(2)
    lane = lax.broadcasted_iota(jnp.int32, (1, LANES), 1)
    is0 = lane < SB_HEAD_DIM

    def headnorm(x, g):
        x2 = x * x
        s0 = jnp.sum(jnp.where(is0, x2, 0.0), axis=-1, keepdims=True)
        s1 = jnp.sum(jnp.where(is0, 0.0, x2), axis=-1, keepdims=True)
        ms = jnp.where(is0, s0, s1) * (1.0 / SB_HEAD_DIM)
        return x * lax.rsqrt(ms + EPS) * g

    @pl.when(i == 0)
    def _():
        kn_scr[...] = headnorm(k_ref[...], gk_ref[...]).astype(BF16)
        vb_scr[...] = v_ref[...].astype(BF16)

    qn = headnorm(q_ref[...], gq_ref[...]) * (SB_HEAD_DIM ** -0.5)
    qh = (jnp.where(is0, qn, 0.0).astype(BF16), jnp.where(is0, 0.0, qn).astype(BF16))
    m2 = m2_ref[...]
    row = i * tq + lax.broadcasted_iota(jnp.int32, (tq, tk), 0)
    col0 = lax.broadcasted_iota(jnp.int32, (tq, tk), 1)
    nkb = (i + 1) * (tq // tk)

    def scores(j):
        kj = kn_scr[pl.ds(pl.multiple_of(j * tk, tk), tk), :]
        return [_dot_nt(qh[h], kj) for h in range(2)]

    def logs(j, zs):
        causal = (col0 + j * tk) < row
        zls, cats = [], []
        for z in zs:
            lk = jnp.where(causal, -(jnp.maximum(z, 0.0) + jnp.log(1.0 + jnp.exp(-jnp.abs(z)))), 0.0)
            hi = lk.astype(BF16)
            lo = (lk - hi.astype(F32)).astype(BF16)
            zls.append(jnp.where(causal, z + lk, -jnp.inf))
            cats.append(jnp.concatenate([hi, lo], axis=1))
        return zls, cats

    def sums(cats):
        return [_dot(c, m2) for c in cats]

    def body(state):
        n, _, carry, zls, rts = state
        j = nkb - 1 - n
        rest_next = [carry[2 * h + 1] + rts[h][:, tk:] for h in range(2)]
        alive = (jnp.max(jnp.maximum(rest_next[0], rest_next[1])) > SB_DEAD_LOG).astype(jnp.int32)
        jn = jnp.maximum(j - 1, 0)
        zs_next = scores(jn)
        vj = vb_scr[pl.ds(pl.multiple_of(j * tk, tk), tk), :]
        ws = [jnp.exp(zls[h] + (carry[2 * h + 1] + rts[h][:, :tk])).astype(BF16) for h in range(2)]
        pvs = [_dot(w, vj) for w in ws]
        zls_next, cats = logs(jn, zs_next)
        rts_next = sums(cats)
        out = []
        for h in range(2):
            out += [carry[2 * h] + pvs[h], rest_next[h]]
        return n + 1, alive, tuple(out), tuple(zls_next), tuple(rts_next)

    def cond(state):
        return jnp.logical_and(state[0] < nkb, state[1] > 0)

    zero = jnp.zeros((tq, LANES), F32)
    zls0, cats0 = logs(nkb - 1, scores(nkb - 1))
    init = (jnp.int32(0), jnp.int32(1), (zero, zero, zero, zero), tuple(zls0), tuple(sums(cats0)))
    res = lax.while_loop(cond, body, init)[2]
    o = jnp.where(is0, res[0], res[2])
    o_ref[...] = headnorm(o, go_ref[...])


def _sb_attention(proj3, gq, gk, go, m2, tq, tk):
    b, s, _ = proj3.shape
    hp = SB_HEADS // 2
    return pl.pallas_call(
        functools.partial(_sb_body, tq=tq, tk=tk),
        grid=(b, hp, s // tq),
        in_specs=[
            pl.BlockSpec((None, tq, LANES), lambda bi, h, i: (bi, i, h)),
            pl.BlockSpec((None, s, LANES), lambda bi, h, i: (bi, 0, hp + h)),
            pl.BlockSpec((None, s, LANES), lambda bi, h, i: (bi, 0, 2 * hp + h)),
            pl.BlockSpec((1, LANES), lambda bi, h, i: (0, 0)),
            pl.BlockSpec((1, LANES), lambda bi, h, i: (0, 0)),
            pl.BlockSpec((1, LANES), lambda bi, h, i: (0, 0)),
            pl.BlockSpec((2 * tk, 2 * tk), lambda bi, h, i: (0, 0)),
        ],
        out_specs=pl.BlockSpec((None, tq, LANES), lambda bi, h, i: (bi, i, h)),
        out_shape=jax.ShapeDtypeStruct((b, s, SB_WIDTH), F32),
        scratch_shapes=[pltpu.VMEM((s, LANES), BF16), pltpu.VMEM((s, LANES), BF16)],
        compiler_params=_cparams(("parallel", "parallel", "arbitrary")),
        name="sb_attention",
    )(proj3, proj3, proj3, gq, gk, go, m2)


GDN_BLOCK = 128


GDN_HALO = 8


def _gdn_body(x_ref, halo_ref, z_ref, ab_ref, cw_ref, alog_ref, dtb_ref, og_ref, o_ref,
              q_scr, k_scr, v_scr, g_scr, beta_scr, state_scr, *, ts):
    t = pl.program_id(1)
    c = GDN_BLOCK
    nh = GDN_HEADS
    lane = lax.broadcasted_iota(jnp.int32, (1, LANES), 1)

    @pl.when(t == 0)
    def _():
        state_scr[...] = jnp.zeros_like(state_scr)

    halo = jnp.where(t > 0, halo_ref[...], 0.0)
    xe = jnp.concatenate([halo, x_ref[...]], axis=0)
    cw = cw_ref[...]
    y = xe * cw[3:4, :]
    for d in (1, 2, 3):
        y = y + pltpu.roll(xe, d, 0) * cw[3 - d:4 - d, :]
    y = y[GDN_HALO:, :]
    y = y * _sigmoid(y)

    def l2n(x):
        return x * lax.rsqrt(jnp.sum(x * x, axis=-1, keepdims=True) + EPS)

    ab = ab_ref[...]
    for h in range(nh):
        sl = pl.ds(h * LANES, LANES)
        q_scr[:, sl] = l2n(y[:, h * LANES:(h + 1) * LANES]) * (GDN_HEAD_DIM ** -0.5)
        k_scr[:, sl] = l2n(y[:, GDN_WIDTH + h * LANES:GDN_WIDTH + (h + 1) * LANES])
        a_col = jnp.sum(jnp.where(lane == h, ab, 0.0), axis=-1, keepdims=True)
        b_col = jnp.sum(jnp.where(lane == h + nh, ab, 0.0), axis=-1, keepdims=True)
        g_scr[:, sl] = -jnp.exp(alog_ref[h]) * _softplus(a_col + dtb_ref[h])
        beta_scr[:, sl] = jnp.broadcast_to(_sigmoid(b_col), (ts, LANES))
    v_scr[...] = y[:, 2 * GDN_WIDTH:]

    ri = lax.broadcasted_iota(jnp.int32, (c, nh * c), 0)
    ci = lax.broadcasted_iota(jnp.int32, (c, nh * c), 1) % c
    lower_incl = ci <= ri
    lower_strict = ci < ri
    eye = (ci == ri).astype(F32)
    ltri = lower_incl[:, :c].astype(F32)
    ones = jnp.ones((c, c), F32)
    og = og_ref[...]
    heads = range(nh)
    hs = lambda m, h: m[:, h * c:(h + 1) * c]

    def blk(n, states):
        r0 = pl.multiple_of(n * c, c)
        q = q_scr[pl.ds(r0, c), :]
        k = k_scr[pl.ds(r0, c), :]
        v = v_scr[pl.ds(r0, c), :]
        g = g_scr[pl.ds(r0, c), :]
        beta = beta_scr[pl.ds(r0, c), :]
        gc = _dot_f32(ltri, g)
        gc_row = _dot_f32(ones, gc * eye)
        decay = jnp.exp(jnp.where(lower_incl, gc - gc_row, -jnp.inf))
        kb = k * beta
        kbf = k.astype(BF16)
        kbb = kb.astype(BF16)
        qbf = q.astype(BF16)
        kk = jnp.concatenate([_dot_nt(hs(kbb, h), hs(kbf, h)) for h in heads], axis=1)
        qk = jnp.concatenate([_dot_nt(hs(qbf, h), hs(kbf, h)) for h in heads], axis=1)
        a = jnp.where(lower_strict, kk * decay, 0.0)
        attn = jnp.where(lower_incl, qk * decay, 0.0).astype(BF16)
        nmat = [-hs(a, h) for h in heads]
        xp = [hs(a, h) for h in heads]
        for _ in range(6):
            xp = [_dot3(x, x) for x in xp]
            prod = [_dot3(nm, x) for nm, x in zip(nmat, xp)]
            nmat = [nm + x + p for nm, x, p in zip(nmat, xp, prod)]
        eg = jnp.exp(gc)
        vb = v * beta
        kbd = kb * eg
        gl = gc[c - 1:c, :]
        qd = (q * eg).astype(BF16)
        kd = (k * jnp.exp(gl - gc)).astype(BF16)
        dl = jnp.exp(gl)
        nb = [nm.astype(BF16) for nm in nmat]
        u = [hs(vb, h) + _dot(nb[h], hs(vb, h).astype(BF16)) for h in heads]
        w = [(hs(kbd, h) + _dot(nb[h], hs(kbd, h).astype(BF16))).astype(BF16) for h in heads]
        sb = [st.astype(BF16) for st in states]
        v_new = [u[h] - _dot(w[h], sb[h]) for h in heads]
        o_state = [_dot(hs(qd, h), sb[h]) for h in heads]
        vnb = [vn.astype(BF16) for vn in v_new]
        o = [o_state[h] + _dot(hs(attn, h), vnb[h]) for h in heads]
        new_states = tuple(states[h] * hs(dl, h) + _dot_tn(hs(kd, h), vnb[h]) for h in heads)
        on = jnp.concatenate(
            [x * lax.rsqrt(jnp.mean(x * x, axis=-1, keepdims=True) + EPS) * og for x in o], axis=1)
        z = z_ref[pl.ds(r0, c), :]
        o_ref[pl.ds(r0, c), :] = on * (z * _sigmoid(z))
        return new_states

    states = lax.fori_loop(0, ts // c, blk, tuple(state_scr[h] for h in heads))
    for h in heads:
        state_scr[h] = states[h]


def _gdn(proj3, ab3, conv_w, alog_b, dtb_b, og, ts):
    b, s, _ = proj3.shape
    w3 = 3 * GDN_WIDTH
    assert 3 * SB_WIDTH == w3 and 2 * w3 % GDN_WIDTH == 0
    per = ts // GDN_HALO
    par = pl.BlockSpec((GDN_HEADS, 1, LANES), lambda bi, t: (0, 0, 0))
    return pl.pallas_call(
        functools.partial(_gdn_body, ts=ts),
        grid=(b, s // ts),
        in_specs=[
            pl.BlockSpec((None, ts, w3), lambda bi, t: (bi, t, 1)),
            pl.BlockSpec((None, GDN_HALO, w3), lambda bi, t: (bi, jnp.maximum(t * per - 1, 0), 1)),
            pl.BlockSpec((None, ts, GDN_WIDTH), lambda bi, t: (bi, t, 2 * w3 // GDN_WIDTH)),
            pl.BlockSpec((None, ts, LANES), lambda bi, t: (bi, t, 0)),
            pl.BlockSpec((GDN_CONV, w3), lambda bi, t: (0, 0)),
            par, par,
            pl.BlockSpec((1, LANES), lambda bi, t: (0, 0)),
        ],
        out_specs=pl.BlockSpec((None, ts, GDN_WIDTH), lambda bi, t: (bi, t, 0)),
        out_shape=jax.ShapeDtypeStruct((b, s, GDN_WIDTH), F32),
        scratch_shapes=[pltpu.VMEM((ts, GDN_WIDTH), F32) for _ in range(5)]
        + [pltpu.VMEM((GDN_HEADS, GDN_HEAD_DIM, GDN_HEAD_DIM), F32)],
        compiler_params=_cparams(("parallel", "arbitrary")),
        name="gdn",
    )(proj3, proj3, proj3, ab3, conv_w, alog_b, dtb_b, og)


def _outproj_body(sb_ref, gd_ref, x_ref, wo1_ref, wo2_ref, g2_ref, wqt_ref, x1_ref, hnt_ref, pqt_ref):
    mix = _dot(sb_ref[...].astype(BF16), wo1_ref[...]) + _dot(gd_ref[...].astype(BF16), wo2_ref[...])
    x1 = x_ref[...] + mix
    x1_ref[...] = x1
    ms = jnp.mean(x1 * x1, axis=-1, keepdims=True)
    hn = x1 * lax.rsqrt(ms + EPS) * g2_ref[...]
    hnt = hn.T.astype(BF16)
    hnt_ref[...] = hnt
    pqt_ref[...] = _dot(wqt_ref[...], hnt)


def _outproj(sb_o, gd_o, x2, wo1, wo2, g2, wqt, tm):
    t, d = x2.shape
    nq = wqt.shape[0]
    return pl.pallas_call(
        _outproj_body,
        grid=(t // tm,),
        in_specs=[
            pl.BlockSpec((tm, SB_WIDTH), lambda i: (i, 0)),
            pl.BlockSpec((tm, GDN_WIDTH), lambda i: (i, 0)),
            pl.BlockSpec((tm, d), lambda i: (i, 0)),
            pl.BlockSpec((SB_WIDTH, d), lambda i: (0, 0)),
            pl.BlockSpec((GDN_WIDTH, d), lambda i: (0, 0)),
            pl.BlockSpec((1, d), lambda i: (0, 0)),
            pl.BlockSpec((nq, d), lambda i: (0, 0)),
        ],
        out_specs=[
            pl.BlockSpec((tm, d), lambda i: (i, 0)),
            pl.BlockSpec((d, tm), lambda i: (0, i)),
            pl.BlockSpec((nq, tm), lambda i: (0, i)),
        ],
        out_shape=[
            jax.ShapeDtypeStruct((t, d), F32),
            jax.ShapeDtypeStruct((d, t), BF16),
            jax.ShapeDtypeStruct((nq, t), F32),
        ],
        compiler_params=_cparams(("parallel",)),
        name="outproj",
    )(sb_o, gd_o, x2, wo1, wo2, g2, wqt)


def _extract_topk(s, ids, k):
    big = jnp.int32(2 ** 30)
    work = s
    vals = []
    rank = jnp.full(s.shape, float(k), F32)
    for r in range(k):
        m = jnp.max(work, axis=0, keepdims=True)
        first = jnp.min(jnp.where(work == m, ids, big), axis=0, keepdims=True)
        hit = ids == first
        rank = jnp.where(hit, float(r), rank)
        work = jnp.where(hit, -jnp.inf, work)
        vals.append(m)
    return vals, rank, None


def _extract_topk_untied(s, ids, k):
    del ids
    work = s
    vals = []
    rank = jnp.full(s.shape, float(k), F32)
    for r in range(k):
        m = jnp.max(work, axis=0, keepdims=True)
        hit = work == m
        rank = jnp.where(hit, float(r), rank)
        work = jnp.where(hit, -jnp.inf, work)
        vals.append(m)
    taken = jnp.sum((rank < float(k)).astype(F32), axis=0, keepdims=True)
    return vals, rank, taken == float(k)


def _bf16_pair(x):
    bits = pltpu.bitcast(x.astype(BF16).astype(F32), jnp.uint32)
    return bits | (bits >> 16)


def _route_body(pqt_ref, k1_ref, k2_ref, rank2_ref, cnt_ref, e1_ref, e2z_ref):
    kk = PEER_TOPK
    tn = pqt_ref.shape[1]
    key_ids = lax.broadcasted_iota(jnp.int32, (PEER_NKEYS, tn), 0)
    i8 = lax.broadcasted_iota(jnp.int32, (8, tn), 0)
    i16 = lax.broadcasted_iota(jnp.int32, (kk, tn), 0)
    cand_ids = jnp.concatenate(
        [i16] + [i8 + a * kk for a in (1, 2, 3)]
        + [jnp.where(i8 >= 4, i8 * kk + b, kk * kk + i8 * kk + b) for b in (0, 1, 2)] + [(i8 + 8) * kk], axis=0)
    def route_head(h, s1, s2, extract):
        v1, rank1, ok1 = extract(s1, key_ids, kk)
        v2, rank2, ok2 = extract(s2, key_ids, kk)
        v1a = jnp.concatenate(v1, axis=0)
        v2a = jnp.concatenate(v2, axis=0)
        cand = jnp.concatenate(
            [v1[0] + v2a] + [v1[a] + v2a[:8] for a in (1, 2, 3)]
            + [jnp.where(i8 >= 4, v1a[:8] + v2[b], -jnp.inf) for b in (0, 1, 2)] + [v1a[8:] + v2[0]], axis=0)
        top, crank, ok3 = extract(cand, cand_ids, kk)
        zsum = jnp.ones_like(top[0])
        for r in range(1, kk):
            zsum = zsum + jnp.exp(top[r] - top[0])
        sel = (crank < float(kk)).astype(F32)
        low = [jnp.sum(sel[0:16], axis=0, keepdims=True)] + [
            jnp.sum(sel[8 + 8 * a:16 + 8 * a], axis=0, keepdims=True) for a in (1, 2, 3)]
        mid = sel[40:48] + sel[48:56] + sel[56:64]
        high = sel[64:72]
        cnt = jnp.zeros(s1.shape, F32)
        for a in range(kk):
            n_a = low[a] if a < 4 else (mid[a:a + 1] if a < 8 else high[a - 8:a - 7])
            cnt = jnp.where(rank1 == float(a), n_a, cnt)
        rank2_ref[h] = rank2.astype(BF16)
        cnt_ref[h] = _bf16_pair(cnt)
        e1_ref[h] = _bf16_pair(jnp.exp(s1 - v1[0]))
        e2z_ref[h] = (jnp.exp(s2 - v2[0]) / zsum).astype(BF16)
        return None if ok1 is None else jnp.logical_and(jnp.logical_and(ok1, ok2), ok3)

    for h in range(PEER_HEADS):
        q1 = pqt_ref[pl.ds(h * 2 * PEER_HALF, PEER_HALF), :].astype(BF16)
        q2 = pqt_ref[pl.ds(h * 2 * PEER_HALF + PEER_HALF, PEER_HALF), :].astype(BF16)
        s1 = _dot(k1_ref[h], q1)
        s2 = _dot(k2_ref[h], q2)
        ok = route_head(h, s1, s2, _extract_topk_untied)
        tied = jnp.max(jnp.where(ok, 0.0, 1.0)) > 0.0

        @pl.when(tied)
        def _():
            route_head(h, s1, s2, _extract_topk)


def _route(pqt, k1, k2, tn):
    nq, t = pqt.shape
    hk = (PEER_HEADS, PEER_NKEYS, PEER_HALF)
    out = lambda dt: jax.ShapeDtypeStruct((PEER_HEADS, PEER_NKEYS, t), dt)
    ospec = pl.BlockSpec((PEER_HEADS, PEER_NKEYS, tn), lambda i: (0, 0, i))
    return pl.pallas_call(
        _route_body,
        grid=(t // tn,),
        in_specs=[
            pl.BlockSpec((nq, tn), lambda i: (0, i)),
            pl.BlockSpec(hk, lambda i: (0, 0, 0)),
            pl.BlockSpec(hk, lambda i: (0, 0, 0)),
        ],
        out_specs=[ospec, ospec, ospec, ospec],
        out_shape=[out(BF16), out(jnp.uint32), out(jnp.uint32), out(BF16)],
        compiler_params=_cparams(("parallel",)),
        name="peer_route",
    )(pqt, k1, k2)


PEER_I1_PER_STEP = 16
PEER_I1_PER_CHUNK = 2
PEER_CHUNKS_PER_OUT = 2
BF16_ROWS = 16


def _expert_body(hnt_ref, u_ref, vt_ref, rank2_ref, cnt_ref, e1_ref, e2z_ref, x1_ref, o_ref, acc_scr):
    j = pl.program_id(1)

    @pl.when(j == 0)
    def _():
        acc_scr[...] = jnp.zeros_like(acc_scr)

    hnt = hnt_ref[...]
    tn = hnt.shape[1]
    ce = PEER_I1_PER_CHUNK * PEER_NKEYS
    nchunk = PEER_I1_PER_STEP // PEER_I1_PER_CHUNK
    tiles = PEER_NKEYS // BF16_ROWS

    def pre_act(c):
        return _dot(u_ref[pl.ds(c * ce, ce), :], hnt)

    def row(ref, h, l):
        word = jnp.broadcast_to(ref[h, l:l + 1, :], (BF16_ROWS // 2, tn))
        return pltpu.bitcast(word, BF16)[None]

    pres = [pre_act(0), pre_act(1)] + [None] * (nchunk - 2)
    total = None
    ps = []
    for c in range(nchunk):
        pre = pres[c]
        act = (0.5 * pre * (1.0 + lax.erf(pre * (2.0 ** -0.5)))).astype(BF16)
        if c + 2 < nchunk:
            pres[c + 2] = pre_act(c + 2)
        gates = []
        for l in range(c * PEER_I1_PER_CHUNK, (c + 1) * PEER_I1_PER_CHUNK):
            gate = None
            for h in range(PEER_HEADS):
                rank2 = rank2_ref[h].reshape(tiles, BF16_ROWS, tn)
                e2z = e2z_ref[h].reshape(tiles, BF16_ROWS, tn)
                term = jnp.where(rank2 < row(cnt_ref, h, l), e2z * row(e1_ref, h, l), 0.0)
                gate = term if gate is None else gate + term
            gates.append(gate.reshape(PEER_NKEYS, tn))
        ps.append(jnp.concatenate(gates, axis=0) * act)
        if len(ps) == PEER_CHUNKS_PER_OUT:
            c0 = c + 1 - PEER_CHUNKS_PER_OUT
            part = _dot(vt_ref[:, pl.ds(c0 * ce, PEER_CHUNKS_PER_OUT * ce)], jnp.concatenate(ps, axis=0))
            total = part if total is None else total + part
            ps = []
    acc_scr[...] += total

    @pl.when(j == pl.num_programs(1) - 1)
    def _():
        o_ref[...] = x1_ref[...] + acc_scr[...].T


def _experts(hnt, u_b, vt_b, rank2, cnt, e1, e2z, x1, tn):
    d, t = hnt.shape
    ne = u_b.shape[0]
    eb = PEER_I1_PER_STEP * PEER_NKEYS
    full = pl.BlockSpec((PEER_HEADS, PEER_NKEYS, tn), lambda i, j: (0, 0, i))
    part = pl.BlockSpec((PEER_HEADS, PEER_I1_PER_STEP, tn), lambda i, j: (0, j, i))
    return pl.pallas_call(
        _expert_body,
        grid=(t // tn, ne // eb),
        in_specs=[
            pl.BlockSpec((d, tn), lambda i, j: (0, i)),
            pl.BlockSpec((eb, d), lambda i, j: (j, 0)),
            pl.BlockSpec((d, eb), lambda i, j: (0, j)),
            full, part, part, full,
            pl.BlockSpec((tn, d), lambda i, j: (i, 0)),
        ],
        out_specs=pl.BlockSpec((tn, d), lambda i, j: (i, 0)),
        out_shape=jax.ShapeDtypeStruct((t, d), F32),
        scratch_shapes=[pltpu.VMEM((d, tn), F32)],
        compiler_params=_cparams(("parallel", "arbitrary")),
        name="peer_experts",
    )(hnt, u_b, vt_b, rank2, cnt, e1, e2z, x1)


def _suffix_sum_matrix(tk):
    r = jnp.arange(2 * tk)[:, None] % tk
    c = jnp.arange(2 * tk)[None, :]
    return jnp.where(c < tk, r > c, True).astype(BF16)


def _layer(x, norm1_g, w_in, sb_q_g, sb_k_g, sb_o_g, conv_w, a_log, dt_bias, gdn_o_g, w_out, norm2_g,
           w_q, keys1, keys2, u_tab, v_tab):
    b, s, d = x.shape
    t = b * s
    n_main = 3 * SB_WIDTH + 4 * GDN_WIDTH
    x2 = x.reshape(t, d)
    w_main = w_in[:, :n_main].astype(BF16)
    w_ab = jnp.pad(w_in[:, n_main:], ((0, 0), (0, LANES - 2 * GDN_HEADS))).astype(BF16)
    tm = min(256, t)
    proj, ab = _inproj(x2, norm1_g.reshape(1, d), w_main, w_ab, tm)
    proj3 = proj.reshape(b, s, n_main)
    ab3 = ab.reshape(b, s, LANES)

    tq, tk = min(256, s), 128
    tile2 = lambda g: jnp.tile(g, 2).reshape(1, LANES)
    sb_o = _sb_attention(proj3, tile2(sb_q_g), tile2(sb_k_g), tile2(sb_o_g), _suffix_sum_matrix(tk), tq, tk)

    bcast = lambda p: jnp.broadcast_to(p[:, None, None], (GDN_HEADS, 1, LANES))
    gd_o = _gdn(proj3, ab3, conv_w, bcast(a_log), bcast(dt_bias), gdn_o_g.reshape(1, LANES), min(512, s))

    wo = w_out.astype(BF16)
    x1, hnt, pqt = _outproj(sb_o.reshape(t, SB_WIDTH), gd_o.reshape(t, GDN_WIDTH), x2, wo[:SB_WIDTH],
                            wo[SB_WIDTH:], norm2_g.reshape(1, d), w_q.T.astype(BF16), tm)

    rank2, cnt, e1, e2z = _route(pqt, keys1.astype(BF16), keys2.astype(BF16), min(256, t))
    y = _experts(hnt, u_tab.astype(BF16), v_tab.T.astype(BF16), rank2, cnt, e1, e2z, x1, min(512, t))
    return y.reshape(b, s, d)


def kernel(x, norm1_g, w_in, sb_q_norm_g, sb_k_norm_g, sb_out_norm_g, gdn_conv_w, gdn_a_log, gdn_dt_bias,
           gdn_out_norm_g, w_out, norm2_g, peer_w_q, peer_keys1, peer_keys2, peer_u, peer_v):
    for layer in range(norm1_g.shape[0]):
        x = _layer(x, norm1_g[layer], w_in[layer], sb_q_norm_g[layer], sb_k_norm_g[layer],
                   sb_out_norm_g[layer], gdn_conv_w[layer], gdn_a_log[layer], gdn_dt_bias[layer],
                   gdn_out_norm_g[layer], w_out[layer], norm2_g[layer], peer_w_q[layer],
                   peer_keys1[layer], peer_keys2[layer], peer_u[layer], peer_v[layer])
    return x
```

```python
import functools

import jax
import jax.numpy as jnp
from jax import lax
from jax.experimental import pallas as pl
from jax.experimental.pallas import tpu as pltpu

F32 = jnp.float32
BF16 = jnp.bfloat16
EPS = 1e-6

SB_HEADS = 8
SB_HEAD_DIM = 64
SB_WIDTH = SB_HEADS * SB_HEAD_DIM
GDN_HEADS = 4
GDN_HEAD_DIM = 128
GDN_WIDTH = GDN_HEADS * GDN_HEAD_DIM
GDN_CONV = 4
PEER_HEADS = 8
PEER_NKEYS = 128
PEER_HALF = 128
PEER_TOPK = 16
LANES = 128

VMEM_LIMIT = 56 * 1024 * 1024


def _cparams(sem):
    return pltpu.CompilerParams(dimension_semantics=sem, vmem_limit_bytes=VMEM_LIMIT)


def _dot(a, b):
    return jnp.dot(a, b, preferred_element_type=F32)


def _dot_nt(a, b):
    return lax.dot_general(a, b, (((1,), (1,)), ((), ())), preferred_element_type=F32)


def _dot_tn(a, b):
    return lax.dot_general(a, b, (((0,), (0,)), ((), ())), preferred_element_type=F32)


def _dot_f32(a, b):
    return jnp.dot(a, b, preferred_element_type=F32, precision=lax.Precision.HIGHEST)


def _split(a):
    hi = a.astype(BF16)
    return hi, (a - hi.astype(F32)).astype(BF16)


def _dot3(a, b):
    ah, al = _split(a)
    bh, bl = _split(b)
    return _dot(jnp.concatenate([ah, ah, al], axis=1), jnp.concatenate([bh, bl, bh], axis=0))


def _softplus(x):
    return jnp.maximum(x, 0.0) + jnp.log1p(jnp.exp(-jnp.abs(x)))


def _sigmoid(x):
    return 1.0 / (1.0 + jnp.exp(-x))


def _inproj_body(x_ref, g_ref, w_ref, wab_ref, proj_ref, ab_ref):
    x = x_ref[...]
    ms = jnp.mean(x * x, axis=-1, keepdims=True)
    h = (x * lax.rsqrt(ms + EPS) * g_ref[...]).astype(BF16)
    proj_ref[...] = _dot(h, w_ref[...])
    ab_ref[...] = _dot(h, wab_ref[...])


def _inproj(x2, g, w_main, w_ab, tm):
    t, d = x2.shape
    n = w_main.shape[1]
    return pl.pallas_call(
        _inproj_body,
        grid=(t // tm,),
        in_specs=[
            pl.BlockSpec((tm, d), lambda i: (i, 0)),
            pl.BlockSpec((1, d), lambda i: (0, 0)),
            pl.BlockSpec((d, n), lambda i: (0, 0)),
            pl.BlockSpec((d, LANES), lambda i: (0, 0)),
        ],
        out_specs=[
            pl.BlockSpec((tm, n), lambda i: (i, 0)),
            pl.BlockSpec((tm, LANES), lambda i: (i, 0)),
        ],
        out_shape=[
            jax.ShapeDtypeStruct((t, n), F32),
            jax.ShapeDtypeStruct((t, LANES), F32),
        ],
        compiler_params=_cparams(("parallel",)),
        name="inproj",
    )(x2, g, w_main, w_ab)


SB_DEAD_LOG = -104.0


def _sb_body(q_ref, k_ref, v_ref, gq_ref, gk_ref, go_ref, m2_ref, o_ref, kn_scr, vb_scr, *, tq, tk):
    i = pl.program_id(2)
    lane = lax.broadcasted_iota(jnp.int32, (1, LANES), 1)
    is0 = lane < SB_HEAD_DIM

    def headnorm(x, g):
        x2 = x * x
        s0 = jnp.sum(jnp.where(is0, x2, 0.0), axis=-1, keepdims=True)
        s1 = jnp.sum(jnp.where(is0, 0.0, x2), axis=-1, keepdims=True)
        ms = jnp.where(is0, s0, s1) * (1.0 / SB_HEAD_DIM)
        return x * lax.rsqrt(ms + EPS) * g

    @pl.when(i == 0)
    def _():
        kn_scr[...] = headnorm(k_ref[...], gk_ref[...]).astype(BF16)
        vb_scr[...] = v_ref[...].astype(BF16)

    qn = headnorm(q_ref[...], gq_ref[...]) * (SB_HEAD_DIM ** -0.5)
    qh = (jnp.where(is0, qn, 0.0).astype(BF16), jnp.where(is0, 0.0, qn).astype(BF16))
    m2 = m2_ref[...]
    row = i * tq + lax.broadcasted_iota(jnp.int32, (tq, tk), 0)
    col0 = lax.broadcasted_iota(jnp.int32, (tq, tk), 1)
    nkb = (i + 1) * (tq // tk)

    def scores(j):
        kj = kn_scr[pl.ds(pl.multiple_of(j * tk, tk), tk), :]
        return [_dot_nt(qh[h], kj) for h in range(2)]

    def logs(j, zs):
        causal = (col0 + j * tk) < row
        zls, cats = [], []
        for z in zs:
            lk = jnp.where(causal, -(jnp.maximum(z, 0.0) + jnp.log(1.0 + jnp.exp(-jnp.abs(z)))), 0.0)
            hi = lk.astype(BF16)
            lo = (lk - hi.astype(F32)).astype(BF16)
            zls.append(jnp.where(causal, z + lk, -jnp.inf))
            cats.append(jnp.concatenate([hi, lo], axis=1))
        return zls, cats

    def sums(cats):
        return [_dot(c, m2) for c in cats]

    def body(state):
        n, _, carry, zls, rts = state
        j = nkb - 1 - n
        rest_next = [carry[2 * h + 1] + rts[h][:, tk:] for h in range(2)]
        alive = (jnp.max(jnp.maximum(rest_next[0], rest_next[1])) > SB_DEAD_LOG).astype(jnp.int32)
        jn = jnp.maximum(j - 1, 0)
        zs_next = scores(jn)
        vj = vb_scr[pl.ds(pl.multiple_of(j * tk, tk), tk), :]
        ws = [jnp.exp(zls[h] + (carry[2 * h + 1] + rts[h][:, :tk])).astype(BF16) for h in range(2)]
        pvs = [_dot(w, vj) for w in ws]
        zls_next, cats = logs(jn, zs_next)
        rts_next = sums(cats)
        out = []
        for h in range(2):
            out += [carry[2 * h] + pvs[h], rest_next[h]]
        return n + 1, alive, tuple(out), tuple(zls_next), tuple(rts_next)

    def cond(state):
        return jnp.logical_and(state[0] < nkb, state[1] > 0)

    zero = jnp.zeros((tq, LANES), F32)
    zls0, cats0 = logs(nkb - 1, scores(nkb - 1))
    init = (jnp.int32(0), jnp.int32(1), (zero, zero, zero, zero), tuple(zls0), tuple(sums(cats0)))
    res = lax.while_loop(cond, body, init)[2]
    o = jnp.where(is0, res[0], res[2])
    o_ref[...] = headnorm(o, go_ref[...])


def _sb_attention(proj3, gq, gk, go, m2, tq, tk):
    b, s, _ = proj3.shape
    hp = SB_HEADS // 2
    return pl.pallas_call(
        functools.partial(_sb_body, tq=tq, tk=tk),
        grid=(b, hp, s // tq),
        in_specs=[
            pl.BlockSpec((None, tq, LANES), lambda bi, h, i: (bi, i, h)),
            pl.BlockSpec((None, s, LANES), lambda bi, h, i: (bi, 0, hp + h)),
            pl.BlockSpec((None, s, LANES), lambda bi, h, i: (bi, 0, 2 * hp + h)),
            pl.BlockSpec((1, LANES), lambda bi, h, i: (0, 0)),
            pl.BlockSpec((1, LANES), lambda bi, h, i: (0, 0)),
            pl.BlockSpec((1, LANES), lambda bi, h, i: (0, 0)),
            pl.BlockSpec((2 * tk, 2 * tk), lambda bi, h, i: (0, 0)),
        ],
        out_specs=pl.BlockSpec((None, tq, LANES), lambda bi, h, i: (bi, i, h)),
        out_shape=jax.ShapeDtypeStruct((b, s, SB_WIDTH), F32),
        scratch_shapes=[pltpu.VMEM((s, LANES), BF16), pltpu.VMEM((s, LANES), BF16)],
        compiler_params=_cparams(("parallel", "parallel", "arbitrary")),
        name="sb_attention",
    )(proj3, proj3, proj3, gq, gk, go, m2)


GDN_BLOCK = 128


GDN_HALO = 8


def _gdn_body(x_ref, halo_ref, z_ref, ab_ref, cw_ref, alog_ref, dtb_ref, og_ref, o_ref,
              q_scr, k_scr, v_scr, g_scr, beta_scr, state_scr, *, ts):
    t = pl.program_id(1)
    c = GDN_BLOCK
    nh = GDN_HEADS
    lane = lax.broadcasted_iota(jnp.int32, (1, LANES), 1)

    @pl.when(t == 0)
    def _():
        state_scr[...] = jnp.zeros_like(state_scr)

    halo = jnp.where(t > 0, halo_ref[...], 0.0)
    xe = jnp.concatenate([halo, x_ref[...]], axis=0)
    cw = cw_ref[...]
    y = xe * cw[3:4, :]
    for d in (1, 2, 3):
        y = y + pltpu.roll(xe, d, 0) * cw[3 - d:4 - d, :]
    y = y[GDN_HALO:, :]
    y = y * _sigmoid(y)

    def l2n(x):
        return x * lax.rsqrt(jnp.sum(x * x, axis=-1, keepdims=True) + EPS)

    ab = ab_ref[...]
    for h in range(nh):
        sl = pl.ds(h * LANES, LANES)
        q_scr[:, sl] = l2n(y[:, h * LANES:(h + 1) * LANES]) * (GDN_HEAD_DIM ** -0.5)
        k_scr[:, sl] = l2n(y[:, GDN_WIDTH + h * LANES:GDN_WIDTH + (h + 1) * LANES])
        a_col = jnp.sum(jnp.where(lane == h, ab, 0.0), axis=-1, keepdims=True)
        b_col = jnp.sum(jnp.where(lane == h + nh, ab, 0.0), axis=-1, keepdims=True)
        g_scr[:, sl] = -jnp.exp(alog_ref[h]) * _softplus(a_col + dtb_ref[h])
        beta_scr[:, sl] = jnp.broadcast_to(_sigmoid(b_col), (ts, LANES))
    v_scr[...] = y[:, 2 * GDN_WIDTH:]

    ri = lax.broadcasted_iota(jnp.int32, (c, nh * c), 0)
    ci = lax.broadcasted_iota(jnp.int32, (c, nh * c), 1) % c
    lower_incl = ci <= ri
    lower_strict = ci < ri
    eye = (ci == ri).astype(F32)
    ltri = lower_incl[:, :c].astype(F32)
    ones = jnp.ones((c, c), F32)
    og = og_ref[...]
    heads = range(nh)
    hs = lambda m, h: m[:, h * c:(h + 1) * c]

    def blk(n, states):
        r0 = pl.multiple_of(n * c, c)
        q = q_scr[pl.ds(r0, c), :]
        k = k_scr[pl.ds(r0, c), :]
        v = v_scr[pl.ds(r0, c), :]
        g = g_scr[pl.ds(r0, c), :]
        beta = beta_scr[pl.ds(r0, c), :]
        gc = _dot_f32(ltri, g)
        gc_row = _dot_f32(ones, gc * eye)
        decay = jnp.exp(jnp.where(lower_incl, gc - gc_row, -jnp.inf))
        kb = k * beta
        kbf = k.astype(BF16)
        kbb = kb.astype(BF16)
        qbf = q.astype(BF16)
        kk = jnp.concatenate([_dot_nt(hs(kbb, h), hs(kbf, h)) for h in heads], axis=1)
        qk = jnp.concatenate([_dot_nt(hs(qbf, h), hs(kbf, h)) for h in heads], axis=1)
        a = jnp.where(lower_strict, kk * decay, 0.0)
        attn = jnp.where(lower_incl, qk * decay, 0.0).astype(BF16)
        nmat = [-hs(a, h) for h in heads]
        xp = [hs(a, h) for h in heads]
        for _ in range(6):
            xp = [_dot3(x, x) for x in xp]
            prod = [_dot3(nm, x) for nm, x in zip(nmat, xp)]
            nmat = [nm + x + p for nm, x, p in zip(nmat, xp, prod)]
        eg = jnp.exp(gc)
        vb = v * beta
        kbd = kb * eg
        gl = gc[c - 1:c, :]
        qd = (q * eg).astype(BF16)
        kd = (k * jnp.exp(gl - gc)).astype(BF16)
        dl = jnp.exp(gl)
        nb = [nm.astype(BF16) for nm in nmat]
        u = [hs(vb, h) + _dot(nb[h], hs(vb, h).astype(BF16)) for h in heads]
        w = [(hs(kbd, h) + _dot(nb[h], hs(kbd, h).astype(BF16))).astype(BF16) for h in heads]
        sb = [st.astype(BF16) for st in states]
        v_new = [u[h] - _dot(w[h], sb[h]) for h in heads]
        o_state = [_dot(hs(qd, h), sb[h]) for h in heads]
        vnb = [vn.astype(BF16) for vn in v_new]
        o = [o_state[h] + _dot(hs(attn, h), vnb[h]) for h in heads]
        new_states = tuple(states[h] * hs(dl, h) + _dot_tn(hs(kd, h), vnb[h]) for h in heads)
        on = jnp.concatenate(
            [x * lax.rsqrt(jnp.mean(x * x, axis=-1, keepdims=True) + EPS) * og for x in o], axis=1)
        z = z_ref[pl.ds(r0, c), :]
        o_ref[pl.ds(r0, c), :] = on * (z * _sigmoid(z))
        return new_states

    states = lax.fori_loop(0, ts // c, blk, tuple(state_scr[h] for h in heads))
    for h in heads:
        state_scr[h] = states[h]


def _gdn(proj3, ab3, conv_w, alog_b, dtb_b, og, ts):
    b, s, _ = proj3.shape
    w3 = 3 * GDN_WIDTH
    assert 3 * SB_WIDTH == w3 and 2 * w3 % GDN_WIDTH == 0
    per = ts // GDN_HALO
    par = pl.BlockSpec((GDN_HEADS, 1, LANES), lambda bi, t: (0, 0, 0))
    return pl.pallas_call(
        functools.partial(_gdn_body, ts=ts),
        grid=(b, s // ts),
        in_specs=[
            pl.BlockSpec((None, ts, w3), lambda bi, t: (bi, t, 1)),
            pl.BlockSpec((None, GDN_HALO, w3), lambda bi, t: (bi, jnp.maximum(t * per - 1, 0), 1)),
            pl.BlockSpec((None, ts, GDN_WIDTH), lambda bi, t: (bi, t, 2 * w3 // GDN_WIDTH)),
            pl.BlockSpec((None, ts, LANES), lambda bi, t: (bi, t, 0)),
            pl.BlockSpec((GDN_CONV, w3), lambda bi, t: (0, 0)),
            par, par,
            pl.BlockSpec((1, LANES), lambda bi, t: (0, 0)),
        ],
        out_specs=pl.BlockSpec((None, ts, GDN_WIDTH), lambda bi, t: (bi, t, 0)),
        out_shape=jax.ShapeDtypeStruct((b, s, GDN_WIDTH), F32),
        scratch_shapes=[pltpu.VMEM((ts, GDN_WIDTH), F32) for _ in range(5)]
        + [pltpu.VMEM((GDN_HEADS, GDN_HEAD_DIM, GDN_HEAD_DIM), F32)],
        compiler_params=_cparams(("parallel", "arbitrary")),
        name="gdn",
    )(proj3, proj3, proj3, ab3, conv_w, alog_b, dtb_b, og)


def _outproj_body(sb_ref, gd_ref, x_ref, wo1_ref, wo2_ref, g2_ref, wqt_ref, x1_ref, hnt_ref, pqt_ref):
    mix = _dot(sb_ref[...].astype(BF16), wo1_ref[...]) + _dot(gd_ref[...].astype(BF16), wo2_ref[...])
    x1 = x_ref[...] + mix
    x1_ref[...] = x1
    ms = jnp.mean(x1 * x1, axis=-1, keepdims=True)
    hn = x1 * lax.rsqrt(ms + EPS) * g2_ref[...]
    hnt = hn.T.astype(BF16)
    hnt_ref[...] = hnt
    pqt_ref[...] = _dot(wqt_ref[...], hnt)


def _outproj(sb_o, gd_o, x2, wo1, wo2, g2, wqt, tm):
    t, d = x2.shape
    nq = wqt.shape[0]
    return pl.pallas_call(
        _outproj_body,
        grid=(t // tm,),
        in_specs=[
            pl.BlockSpec((tm, SB_WIDTH), lambda i: (i, 0)),
            pl.BlockSpec((tm, GDN_WIDTH), lambda i: (i, 0)),
            pl.BlockSpec((tm, d), lambda i: (i, 0)),
            pl.BlockSpec((SB_WIDTH, d), lambda i: (0, 0)),
            pl.BlockSpec((GDN_WIDTH, d), lambda i: (0, 0)),
            pl.BlockSpec((1, d), lambda i: (0, 0)),
            pl.BlockSpec((nq, d), lambda i: (0, 0)),
        ],
        out_specs=[
            pl.BlockSpec((tm, d), lambda i: (i, 0)),
            pl.BlockSpec((d, tm), lambda i: (0, i)),
            pl.BlockSpec((nq, tm), lambda i: (0, i)),
        ],
        out_shape=[
            jax.ShapeDtypeStruct((t, d), F32),
            jax.ShapeDtypeStruct((d, t), BF16),
            jax.ShapeDtypeStruct((nq, t), F32),
        ],
        compiler_params=_cparams(("parallel",)),
        name="outproj",
    )(sb_o, gd_o, x2, wo1, wo2, g2, wqt)


def _extract_topk(s, ids, k):
    big = jnp.int32(2 ** 30)
    work = s
    vals = []
    rank = jnp.full(s.shape, float(k), F32)
    for r in range(k):
        m = jnp.max(work, axis=0, keepdims=True)
        first = jnp.min(jnp.where(work == m, ids, big), axis=0, keepdims=True)
        hit = ids == first
        rank = jnp.where(hit, float(r), rank)
        work = jnp.where(hit, -jnp.inf, work)
        vals.append(m)
    return vals, rank, None


def _extract_topk_untied(s, ids, k):
    del ids
    work = s
    vals = []
    rank = jnp.full(s.shape, float(k), F32)
    for r in range(k):
        m = jnp.max(work, axis=0, keepdims=True)
        hit = work == m
        rank = jnp.where(hit, float(r), rank)
        work = jnp.where(hit, -jnp.inf, work)
        vals.append(m)
    taken = jnp.sum((rank < float(k)).astype(F32), axis=0, keepdims=True)
    return vals, rank, taken == float(k)


def _route_body(pqt_ref, k1_ref, k2_ref, rank2_ref, cnt_ref, e1_ref, e2z_ref):
    kk = PEER_TOPK
    tn = pqt_ref.shape[1]
    key_ids = lax.broadcasted_iota(jnp.int32, (PEER_NKEYS, tn), 0)
    i8 = lax.broadcasted_iota(jnp.int32, (8, tn), 0)
    i16 = lax.broadcasted_iota(jnp.int32, (kk, tn), 0)
    cand_ids = jnp.concatenate(
        [i16] + [i8 + a * kk for a in (1, 2, 3)]
        + [jnp.where(i8 >= 4, i8 * kk + b, kk * kk + i8 * kk + b) for b in (0, 1, 2)] + [(i8 + 8) * kk], axis=0)
    def route_head(h, s1, s2, extract):
        v1, rank1, ok1 = extract(s1, key_ids, kk)
        v2, rank2, ok2 = extract(s2, key_ids, kk)
        v1a = jnp.concatenate(v1, axis=0)
        v2a = jnp.concatenate(v2, axis=0)
        cand = jnp.concatenate(
            [v1[0] + v2a] + [v1[a] + v2a[:8] for a in (1, 2, 3)]
            + [jnp.where(i8 >= 4, v1a[:8] + v2[b], -jnp.inf) for b in (0, 1, 2)] + [v1a[8:] + v2[0]], axis=0)
        top, crank, ok3 = extract(cand, cand_ids, kk)
        zsum = jnp.ones_like(top[0])
        for r in range(1, kk):
            zsum = zsum + jnp.exp(top[r] - top[0])
        sel = (crank < float(kk)).astype(F32)
        low = [jnp.sum(sel[0:16], axis=0, keepdims=True)] + [
            jnp.sum(sel[8 + 8 * a:16 + 8 * a], axis=0, keepdims=True) for a in (1, 2, 3)]
        mid = sel[40:48] + sel[48:56] + sel[56:64]
        high = sel[64:72]
        cnt = jnp.zeros(s1.shape, F32)
        for a in range(kk):
            n_a = low[a] if a < 4 else (mid[a:a + 1] if a < 8 else high[a - 8:a - 7])
            cnt = jnp.where(rank1 == float(a), n_a, cnt)
        rank2_ref[h] = rank2.astype(BF16)
        cnt_ref[h] = cnt
        e1_ref[h] = jnp.exp(s1 - v1[0])
        e2z_ref[h] = (jnp.exp(s2 - v2[0]) * (0.5 / zsum)).astype(BF16)
        return None if ok1 is None else jnp.logical_and(jnp.logical_and(ok1, ok2), ok3)

    for h in range(PEER_HEADS):
        q1 = pqt_ref[pl.ds(h * 2 * PEER_HALF, PEER_HALF), :].astype(BF16)
        q2 = pqt_ref[pl.ds(h * 2 * PEER_HALF + PEER_HALF, PEER_HALF), :].astype(BF16)
        s1 = _dot(k1_ref[h], q1)
        s2 = _dot(k2_ref[h], q2)
        ok = route_head(h, s1, s2, _extract_topk_untied)
        tied = jnp.max(jnp.where(ok, 0.0, 1.0)) > 0.0

        @pl.when(tied)
        def _():
            route_head(h, s1, s2, _extract_topk)


def _route(pqt, k1, k2, tn):
    nq, t = pqt.shape
    hk = (PEER_HEADS, PEER_NKEYS, PEER_HALF)
    out = lambda dt: jax.ShapeDtypeStruct((PEER_HEADS, PEER_NKEYS, t), dt)
    ospec = pl.BlockSpec((PEER_HEADS, PEER_NKEYS, tn), lambda i: (0, 0, i))
    return pl.pallas_call(
        _route_body,
        grid=(t // tn,),
        in_specs=[
            pl.BlockSpec((nq, tn), lambda i: (0, i)),
            pl.BlockSpec(hk, lambda i: (0, 0, 0)),
            pl.BlockSpec(hk, lambda i: (0, 0, 0)),
        ],
        out_specs=[ospec, ospec, ospec, ospec],
        out_shape=[out(BF16), out(F32), out(F32), out(BF16)],
        compiler_params=_cparams(("parallel",)),
        name="peer_route",
    )(pqt, k1, k2)


PEER_I1_PER_BLOCK = 8
PEER_I1_PER_CHUNK = 2
BF16_ROWS = 16


def _expert_body(hnt_ref, u_first_ref, u_b_ref, u_next_ref, vt_prev_ref, vt_a_ref, vt_last_ref, rank2_ref,
                 cnt_ref, e1_ref, e2z_ref, x1_ref, o_ref, acc_scr, act_a, act_b, p_a, p_b):
    s = pl.program_id(1)
    hnt = hnt_ref[...]
    tn = hnt.shape[1]
    ce = PEER_I1_PER_CHUNK * PEER_NKEYS
    nchunk = PEER_I1_PER_BLOCK // PEER_I1_PER_CHUNK
    tiles = PEER_NKEYS // BF16_ROWS

    def activation(u_ref, act_ref):
        for c in range(nchunk):
            pre = _dot(u_ref[pl.ds(c * ce, ce), :], hnt)
            act_ref[pl.ds(c * ce, ce), :] = (pre * (1.0 + lax.erf(pre * (2.0 ** -0.5)))).astype(BF16)

    def row(ref, h, l):
        return jnp.broadcast_to(ref[h, l:l + 1, :], (BF16_ROWS, tn)).astype(BF16)[None]

    def gates(l0, act_ref, p_ref):
        for li in range(PEER_I1_PER_BLOCK):
            gate = None
            for h in range(PEER_HEADS):
                rank2 = rank2_ref[h].reshape(tiles, BF16_ROWS, tn)
                e2z = e2z_ref[h].reshape(tiles, BF16_ROWS, tn)
                term = jnp.where(rank2 < row(cnt_ref, h, l0 + li), e2z * row(e1_ref, h, l0 + li), 0.0)
                gate = term if gate is None else gate + term
            rows = pl.ds(li * PEER_NKEYS, PEER_NKEYS)
            p_ref[rows, :] = gate.reshape(PEER_NKEYS, tn) * act_ref[rows, :]

    @pl.when(s == 0)
    def _():
        acc_scr[...] = jnp.zeros_like(acc_scr)
        p_b[...] = jnp.zeros_like(p_b)
        activation(u_first_ref, act_a)

    out_prev = _dot(vt_prev_ref[...], p_b[...])
    activation(u_b_ref, act_b)
    gates(0, act_a, p_a)
    out_a = _dot(vt_a_ref[...], p_a[...])
    activation(u_next_ref, act_a)
    gates(PEER_I1_PER_BLOCK, act_b, p_b)
    acc_scr[...] += out_prev + out_a

    @pl.when(s == pl.num_programs(1) - 1)
    def _():
        o_ref[...] = x1_ref[...] + (acc_scr[...] + _dot(vt_last_ref[...], p_b[...])).T


def _experts(hnt, u_b, vt_b, rank2, cnt, e1, e2z, x1, tn):
    d, t = hnt.shape
    ne = u_b.shape[0]
    eb = PEER_I1_PER_BLOCK * PEER_NKEYS
    nblk = ne // eb
    once = pl.Buffered(1)
    full = pl.BlockSpec((PEER_HEADS, PEER_NKEYS, tn), lambda i, s: (0, 0, i))
    part = pl.BlockSpec((PEER_HEADS, 2 * PEER_I1_PER_BLOCK, tn), lambda i, s: (0, s, i))
    return pl.pallas_call(
        _expert_body,
        grid=(t // tn, nblk // 2),
        in_specs=[
            pl.BlockSpec((d, tn), lambda i, s: (0, i)),
            pl.BlockSpec((eb, d), lambda i, s: (0, 0), pipeline_mode=once),
            pl.BlockSpec((eb, d), lambda i, s: (2 * s + 1, 0)),
            pl.BlockSpec((eb, d), lambda i, s: (jnp.minimum(2 * s + 2, nblk - 1), 0)),
            pl.BlockSpec((d, eb), lambda i, s: (0, jnp.maximum(2 * s - 1, 0))),
            pl.BlockSpec((d, eb), lambda i, s: (0, 2 * s)),
            pl.BlockSpec((d, eb), lambda i, s: (0, nblk - 1), pipeline_mode=once),
            full, part, part, full,
            pl.BlockSpec((tn, d), lambda i, s: (i, 0), pipeline_mode=once),
        ],
        out_specs=pl.BlockSpec((tn, d), lambda i, s: (i, 0)),
        out_shape=jax.ShapeDtypeStruct((t, d), F32),
        scratch_shapes=[pltpu.VMEM((d, tn), F32)] + [pltpu.VMEM((eb, tn), BF16) for _ in range(4)],
        compiler_params=_cparams(("parallel", "arbitrary")),
        name="peer_experts",
    )(hnt, u_b, u_b, u_b, vt_b, vt_b, vt_b, rank2, cnt, e1, e2z, x1)


def _suffix_sum_matrix(tk):
    r = jnp.arange(2 * tk)[:, None] % tk
    c = jnp.arange(2 * tk)[None, :]
    return jnp.where(c < tk, r > c, True).astype(BF16)


def _layer(x, norm1_g, w_in, sb_q_g, sb_k_g, sb_o_g, conv_w, a_log, dt_bias, gdn_o_g, w_out, norm2_g,
           w_q, keys1, keys2, u_tab, v_tab):
    b, s, d = x.shape
    t = b * s
    n_main = 3 * SB_WIDTH + 4 * GDN_WIDTH
    x2 = x.reshape(t, d)
    w_main = w_in[:, :n_main].astype(BF16)
    w_ab = jnp.pad(w_in[:, n_main:], ((0, 0), (0, LANES - 2 * GDN_HEADS))).astype(BF16)
    tm = min(256, t)
    proj, ab = _inproj(x2, norm1_g.reshape(1, d), w_main, w_ab, tm)
    proj3 = proj.reshape(b, s, n_main)
    ab3 = ab.reshape(b, s, LANES)

    tq, tk = min(256, s), 128
    tile2 = lambda g: jnp.tile(g, 2).reshape(1, LANES)
    sb_o = _sb_attention(proj3, tile2(sb_q_g), tile2(sb_k_g), tile2(sb_o_g), _suffix_sum_matrix(tk), tq, tk)

    bcast = lambda p: jnp.broadcast_to(p[:, None, None], (GDN_HEADS, 1, LANES))
    gd_o = _gdn(proj3, ab3, conv_w, bcast(a_log), bcast(dt_bias), gdn_o_g.reshape(1, LANES), min(512, s))

    wo = w_out.astype(BF16)
    x1, hnt, pqt = _outproj(sb_o.reshape(t, SB_WIDTH), gd_o.reshape(t, GDN_WIDTH), x2, wo[:SB_WIDTH],
                            wo[SB_WIDTH:], norm2_g.reshape(1, d), w_q.T.astype(BF16), tm)

    rank2, cnt, e1, e2z = _route(pqt, keys1.astype(BF16), keys2.astype(BF16), min(256, t))
    y = _experts(hnt, u_tab.astype(BF16), v_tab.T.astype(BF16), rank2, cnt, e1, e2z, x1, min(512, t))
    return y.reshape(b, s, d)


def kernel(x, norm1_g, w_in, sb_q_norm_g, sb_k_norm_g, sb_out_norm_g, gdn_conv_w, gdn_a_log, gdn_dt_bias,
           gdn_out_norm_g, w_out, norm2_g, peer_w_q, peer_keys1, peer_keys2, peer_u, peer_v):
    for layer in range(norm1_g.shape[0]):
        x = _layer(x, norm1_g[layer], w_in[layer], sb_q_norm_g[layer], sb_k_norm_g[layer],
                   sb_out_norm_g[layer], gdn_conv_w[layer], gdn_a_log[layer], gdn_dt_bias[layer],
                   gdn_out_norm_g[layer], w_out[layer], norm2_g[layer], peer_w_q[layer],
                   peer_keys1[layer], peer_keys2[layer], peer_u[layer], peer_v[layer])
    return x
```

```python
import functools

import jax
import jax.numpy as jnp
from jax import lax
from jax.experimental import pallas as pl
from jax.experimental.pallas import tpu as pltpu

F32 = jnp.float32
BF16 = jnp.bfloat16
EPS = 1e-6

SB_HEADS = 8
SB_HEAD_DIM = 64
SB_WIDTH = SB_HEADS * SB_HEAD_DIM
GDN_HEADS = 4
GDN_HEAD_DIM = 128
GDN_WIDTH = GDN_HEADS * GDN_HEAD_DIM
GDN_CONV = 4
PEER_HEADS = 8
PEER_NKEYS = 128
PEER_HALF = 128
PEER_TOPK = 16
LANES = 128

VMEM_LIMIT = 56 * 1024 * 1024


def _cparams(sem):
    return pltpu.CompilerParams(dimension_semantics=sem, vmem_limit_bytes=VMEM_LIMIT)


def _dot(a, b):
    return jnp.dot(a, b, preferred_element_type=F32)


def _dot_nt(a, b):
    return lax.dot_general(a, b, (((1,), (1,)), ((), ())), preferred_element_type=F32)


def _dot_tn(a, b):
    return lax.dot_general(a, b, (((0,), (0,)), ((), ())), preferred_element_type=F32)


def _dot_f32(a, b):
    return jnp.dot(a, b, preferred_element_type=F32, precision=lax.Precision.HIGHEST)


def _split(a):
    hi = a.astype(BF16)
    return hi, (a - hi.astype(F32)).astype(BF16)


def _dot3(a, b):
    ah, al = _split(a)
    bh, bl = _split(b)
    return _dot(jnp.concatenate([ah, ah, al], axis=1), jnp.concatenate([bh, bl, bh], axis=0))


def _softplus(x):
    return jnp.maximum(x, 0.0) + jnp.log1p(jnp.exp(-jnp.abs(x)))


def _sigmoid(x):
    return 1.0 / (1.0 + jnp.exp(-x))


def _inproj_body(x_ref, g_ref, w_ref, wab_ref, proj_ref, ab_ref):
    x = x_ref[...]
    ms = jnp.mean(x * x, axis=-1, keepdims=True)
    h = (x * lax.rsqrt(ms + EPS) * g_ref[...]).astype(BF16)
    proj_ref[...] = _dot(h, w_ref[...])
    ab_ref[...] = _dot(h, wab_ref[...])


def _inproj(x2, g, w_main, w_ab, tm):
    t, d = x2.shape
    n = w_main.shape[1]
    return pl.pallas_call(
        _inproj_body,
        grid=(t // tm,),
        in_specs=[
            pl.BlockSpec((tm, d), lambda i: (i, 0)),
            pl.BlockSpec((1, d), lambda i: (0, 0)),
            pl.BlockSpec((d, n), lambda i: (0, 0)),
            pl.BlockSpec((d, LANES), lambda i: (0, 0)),
        ],
        out_specs=[
            pl.BlockSpec((tm, n), lambda i: (i, 0)),
            pl.BlockSpec((tm, LANES), lambda i: (i, 0)),
        ],
        out_shape=[
            jax.ShapeDtypeStruct((t, n), F32),
            jax.ShapeDtypeStruct((t, LANES), F32),
        ],
        compiler_params=_cparams(("parallel",)),
        name="inproj",
    )(x2, g, w_main, w_ab)


SB_DEAD_LOG = -104.0


def _sb_body(q_ref, k_ref, v_ref, gq_ref, gk_ref, go_ref, m2_ref, o_ref, kn_scr, vb_scr, *, tq, tk):
    i = pl.program_id(2)
    lane = lax.broadcasted_iota(jnp.int32, (1, LANES), 1)
    is0 = lane < SB_HEAD_DIM

    def headnorm(x, g):
        x2 = x * x
        s0 = jnp.sum(jnp.where(is0, x2, 0.0), axis=-1, keepdims=True)
        s1 = jnp.sum(jnp.where(is0, 0.0, x2), axis=-1, keepdims=True)
        ms = jnp.where(is0, s0, s1) * (1.0 / SB_HEAD_DIM)
        return x * lax.rsqrt(ms + EPS) * g

    @pl.when(i == 0)
    def _():
        kn_scr[...] = headnorm(k_ref[...], gk_ref[...]).astype(BF16)
        vb_scr[...] = v_ref[...].astype(BF16)

    qn = headnorm(q_ref[...], gq_ref[...]) * (SB_HEAD_DIM ** -0.5)
    qh = (jnp.where(is0, qn, 0.0).astype(BF16), jnp.where(is0, 0.0, qn).astype(BF16))
    m2 = m2_ref[...]
    row = i * tq + lax.broadcasted_iota(jnp.int32, (tq, tk), 0)
    col0 = lax.broadcasted_iota(jnp.int32, (tq, tk), 1)
    nkb = (i + 1) * (tq // tk)

    def scores(j):
        kj = kn_scr[pl.ds(pl.multiple_of(j * tk, tk), tk), :]
        return [_dot_nt(qh[h], kj) for h in range(2)]

    def logs(j, zs):
        causal = (col0 + j * tk) < row
        zls, cats = [], []
        for z in zs:
            lk = jnp.where(causal, -(jnp.maximum(z, 0.0) + jnp.log(1.0 + jnp.exp(-jnp.abs(z)))), 0.0)
            hi = lk.astype(BF16)
            lo = (lk - hi.astype(F32)).astype(BF16)
            zls.append(jnp.where(causal, z + lk, -jnp.inf))
            cats.append(jnp.concatenate([hi, lo], axis=1))
        return zls, cats

    def sums(cats):
        return [_dot(c, m2) for c in cats]

    def body(state):
        n, _, carry = state
        j = nkb - 1 - 2 * n
        zs_a = scores(j)
        zs_b = scores(j - 1)
        zl_a, cats_a = logs(j, zs_a)
        rt_a = sums(cats_a)
        zl_b, cats_b = logs(j - 1, zs_b)
        rt_b = sums(cats_b)
        v2 = vb_scr[pl.ds(pl.multiple_of((j - 1) * tk, tk), 2 * tk), :]
        out = []
        for h in range(2):
            acc, rest = carry[2 * h], carry[2 * h + 1]
            rest_mid = rest + rt_a[h][:, tk:]
            w_a = jnp.exp(zl_a[h] + (rest + rt_a[h][:, :tk])).astype(BF16)
            w_b = jnp.exp(zl_b[h] + (rest_mid + rt_b[h][:, :tk])).astype(BF16)
            out += [acc + _dot(jnp.concatenate([w_b, w_a], axis=1), v2), rest_mid + rt_b[h][:, tk:]]
        alive = (jnp.max(jnp.maximum(out[1], out[3])) > SB_DEAD_LOG).astype(jnp.int32)
        return n + 1, alive, tuple(out)

    def cond(state):
        return jnp.logical_and(state[0] < nkb // 2, state[1] > 0)

    zero = jnp.zeros((tq, LANES), F32)
    res = lax.while_loop(cond, body, (jnp.int32(0), jnp.int32(1), (zero, zero, zero, zero)))[2]
    o = jnp.where(is0, res[0], res[2])
    o_ref[...] = headnorm(o, go_ref[...])


def _sb_attention(proj3, gq, gk, go, m2, tq, tk):
    b, s, _ = proj3.shape
    hp = SB_HEADS // 2
    return pl.pallas_call(
        functools.partial(_sb_body, tq=tq, tk=tk),
        grid=(b, hp, s // tq),
        in_specs=[
            pl.BlockSpec((None, tq, LANES), lambda bi, h, i: (bi, i, h)),
            pl.BlockSpec((None, s, LANES), lambda bi, h, i: (bi, 0, hp + h)),
            pl.BlockSpec((None, s, LANES), lambda bi, h, i: (bi, 0, 2 * hp + h)),
            pl.BlockSpec((1, LANES), lambda bi, h, i: (0, 0)),
            pl.BlockSpec((1, LANES), lambda bi, h, i: (0, 0)),
            pl.BlockSpec((1, LANES), lambda bi, h, i: (0, 0)),
            pl.BlockSpec((2 * tk, 2 * tk), lambda bi, h, i: (0, 0)),
        ],
        out_specs=pl.BlockSpec((None, tq, LANES), lambda bi, h, i: (bi, i, h)),
        out_shape=jax.ShapeDtypeStruct((b, s, SB_WIDTH), F32),
        scratch_shapes=[pltpu.VMEM((s, LANES), BF16), pltpu.VMEM((s, LANES), BF16)],
        compiler_params=_cparams(("parallel", "parallel", "arbitrary")),
        name="sb_attention",
    )(proj3, proj3, proj3, gq, gk, go, m2)


GDN_BLOCK = 128


GDN_HALO = 8


def _gdn_body(x_ref, halo_ref, z_ref, ab_ref, cw_ref, alog_ref, dtb_ref, og_ref, o_ref,
              q_scr, k_scr, v_scr, g_scr, beta_scr, state_scr, *, ts):
    t = pl.program_id(1)
    c = GDN_BLOCK
    nh = GDN_HEADS
    lane = lax.broadcasted_iota(jnp.int32, (1, LANES), 1)

    @pl.when(t == 0)
    def _():
        state_scr[...] = jnp.zeros_like(state_scr)

    halo = jnp.where(t > 0, halo_ref[...], 0.0)
    xe = jnp.concatenate([halo, x_ref[...]], axis=0)
    cw = cw_ref[...]
    y = xe * cw[3:4, :]
    for d in (1, 2, 3):
        y = y + pltpu.roll(xe, d, 0) * cw[3 - d:4 - d, :]
    y = y[GDN_HALO:, :]
    y = y * _sigmoid(y)

    def l2n(x):
        return x * lax.rsqrt(jnp.sum(x * x, axis=-1, keepdims=True) + EPS)

    ab = ab_ref[...]
    for h in range(nh):
        sl = pl.ds(h * LANES, LANES)
        q_scr[:, sl] = l2n(y[:, h * LANES:(h + 1) * LANES]) * (GDN_HEAD_DIM ** -0.5)
        k_scr[:, sl] = l2n(y[:, GDN_WIDTH + h * LANES:GDN_WIDTH + (h + 1) * LANES])
        a_col = jnp.sum(jnp.where(lane == h, ab, 0.0), axis=-1, keepdims=True)
        b_col = jnp.sum(jnp.where(lane == h + nh, ab, 0.0), axis=-1, keepdims=True)
        g_scr[:, sl] = -jnp.exp(alog_ref[h]) * _softplus(a_col + dtb_ref[h])
        beta_scr[:, sl] = jnp.broadcast_to(_sigmoid(b_col), (ts, LANES))
    v_scr[...] = y[:, 2 * GDN_WIDTH:]

    ri = lax.broadcasted_iota(jnp.int32, (c, nh * c), 0)
    ci = lax.broadcasted_iota(jnp.int32, (c, nh * c), 1) % c
    lower_incl = ci <= ri
    lower_strict = ci < ri
    eye = (ci == ri).astype(F32)
    ltri = lower_incl[:, :c].astype(F32)
    ones = jnp.ones((c, c), F32)
    og = og_ref[...]
    heads = range(nh)
    hs = lambda m, h: m[:, h * c:(h + 1) * c]

    def blk(n, states):
        r0 = pl.multiple_of(n * c, c)
        q = q_scr[pl.ds(r0, c), :]
        k = k_scr[pl.ds(r0, c), :]
        v = v_scr[pl.ds(r0, c), :]
        g = g_scr[pl.ds(r0, c), :]
        beta = beta_scr[pl.ds(r0, c), :]
        gc = _dot_f32(ltri, g)
        gc_row = _dot_f32(ones, gc * eye)
        decay = jnp.exp(jnp.where(lower_incl, gc - gc_row, -jnp.inf))
        kb = k * beta
        kbf = k.astype(BF16)
        kbb = kb.astype(BF16)
        qbf = q.astype(BF16)
        kk = jnp.concatenate([_dot_nt(hs(kbb, h), hs(kbf, h)) for h in heads], axis=1)
        qk = jnp.concatenate([_dot_nt(hs(qbf, h), hs(kbf, h)) for h in heads], axis=1)
        a = jnp.where(lower_strict, kk * decay, 0.0)
        attn = jnp.where(lower_incl, qk * decay, 0.0).astype(BF16)
        nmat = [-hs(a, h) for h in heads]
        xp = [hs(a, h) for h in heads]
        for _ in range(6):
            xp = [_dot3(x, x) for x in xp]
            prod = [_dot3(nm, x) for nm, x in zip(nmat, xp)]
            nmat = [nm + x + p for nm, x, p in zip(nmat, xp, prod)]
        eg = jnp.exp(gc)
        vb = v * beta
        kbd = kb * eg
        gl = gc[c - 1:c, :]
        qd = (q * eg).astype(BF16)
        kd = (k * jnp.exp(gl - gc)).astype(BF16)
        dl = jnp.exp(gl)
        nb = [nm.astype(BF16) for nm in nmat]
        u = [hs(vb, h) + _dot(nb[h], hs(vb, h).astype(BF16)) for h in heads]
        w = [(hs(kbd, h) + _dot(nb[h], hs(kbd, h).astype(BF16))).astype(BF16) for h in heads]
        sb = [st.astype(BF16) for st in states]
        v_new = [u[h] - _dot(w[h], sb[h]) for h in heads]
        o_state = [_dot(hs(qd, h), sb[h]) for h in heads]
        vnb = [vn.astype(BF16) for vn in v_new]
        o = [o_state[h] + _dot(hs(attn, h), vnb[h]) for h in heads]
        new_states = tuple(states[h] * hs(dl, h) + _dot_tn(hs(kd, h), vnb[h]) for h in heads)
        on = jnp.concatenate(
            [x * lax.rsqrt(jnp.mean(x * x, axis=-1, keepdims=True) + EPS) * og for x in o], axis=1)
        z = z_ref[pl.ds(r0, c), :]
        o_ref[pl.ds(r0, c), :] = on * (z * _sigmoid(z))
        return new_states

    states = lax.fori_loop(0, ts // c, blk, tuple(state_scr[h] for h in heads))
    for h in heads:
        state_scr[h] = states[h]


def _gdn(proj3, ab3, conv_w, alog_b, dtb_b, og, ts):
    b, s, _ = proj3.shape
    w3 = 3 * GDN_WIDTH
    assert 3 * SB_WIDTH == w3 and 2 * w3 % GDN_WIDTH == 0
    per = ts // GDN_HALO
    par = pl.BlockSpec((GDN_HEADS, 1, LANES), lambda bi, t: (0, 0, 0))
    return pl.pallas_call(
        functools.partial(_gdn_body, ts=ts),
        grid=(b, s // ts),
        in_specs=[
            pl.BlockSpec((None, ts, w3), lambda bi, t: (bi, t, 1)),
            pl.BlockSpec((None, GDN_HALO, w3), lambda bi, t: (bi, jnp.maximum(t * per - 1, 0), 1)),
            pl.BlockSpec((None, ts, GDN_WIDTH), lambda bi, t: (bi, t, 2 * w3 // GDN_WIDTH)),
            pl.BlockSpec((None, ts, LANES), lambda bi, t: (bi, t, 0)),
            pl.BlockSpec((GDN_CONV, w3), lambda bi, t: (0, 0)),
            par, par,
            pl.BlockSpec((1, LANES), lambda bi, t: (0, 0)),
        ],
        out_specs=pl.BlockSpec((None, ts, GDN_WIDTH), lambda bi, t: (bi, t, 0)),
        out_shape=jax.ShapeDtypeStruct((b, s, GDN_WIDTH), F32),
        scratch_shapes=[pltpu.VMEM((ts, GDN_WIDTH), F32) for _ in range(5)]
        + [pltpu.VMEM((GDN_HEADS, GDN_HEAD_DIM, GDN_HEAD_DIM), F32)],
        compiler_params=_cparams(("parallel", "arbitrary")),
        name="gdn",
    )(proj3, proj3, proj3, ab3, conv_w, alog_b, dtb_b, og)


def _outproj_body(sb_ref, gd_ref, x_ref, wo1_ref, wo2_ref, g2_ref, wqt_ref, x1_ref, hnt_ref, pqt_ref):
    mix = _dot(sb_ref[...].astype(BF16), wo1_ref[...]) + _dot(gd_ref[...].astype(BF16), wo2_ref[...])
    x1 = x_ref[...] + mix
    x1_ref[...] = x1
    ms = jnp.mean(x1 * x1, axis=-1, keepdims=True)
    hn = x1 * lax.rsqrt(ms + EPS) * g2_ref[...]
    hnt = hn.T.astype(BF16)
    hnt_ref[...] = hnt
    pqt_ref[...] = _dot(wqt_ref[...], hnt)


def _outproj(sb_o, gd_o, x2, wo1, wo2, g2, wqt, tm):
    t, d = x2.shape
    nq = wqt.shape[0]
    return pl.pallas_call(
        _outproj_body,
        grid=(t // tm,),
        in_specs=[
            pl.BlockSpec((tm, SB_WIDTH), lambda i: (i, 0)),
            pl.BlockSpec((tm, GDN_WIDTH), lambda i: (i, 0)),
            pl.BlockSpec((tm, d), lambda i: (i, 0)),
            pl.BlockSpec((SB_WIDTH, d), lambda i: (0, 0)),
            pl.BlockSpec((GDN_WIDTH, d), lambda i: (0, 0)),
            pl.BlockSpec((1, d), lambda i: (0, 0)),
            pl.BlockSpec((nq, d), lambda i: (0, 0)),
        ],
        out_specs=[
            pl.BlockSpec((tm, d), lambda i: (i, 0)),
            pl.BlockSpec((d, tm), lambda i: (0, i)),
            pl.BlockSpec((nq, tm), lambda i: (0, i)),
        ],
        out_shape=[
            jax.ShapeDtypeStruct((t, d), F32),
            jax.ShapeDtypeStruct((d, t), BF16),
            jax.ShapeDtypeStruct((nq, t), F32),
        ],
        compiler_params=_cparams(("parallel",)),
        name="outproj",
    )(sb_o, gd_o, x2, wo1, wo2, g2, wqt)


def _extract_topk(s, ids, k):
    big = jnp.int32(2 ** 30)
    work = s
    vals = []
    rank = jnp.full(s.shape, float(k), F32)
    for r in range(k):
        m = jnp.max(work, axis=0, keepdims=True)
        first = jnp.min(jnp.where(work == m, ids, big), axis=0, keepdims=True)
        hit = ids == first
        rank = jnp.where(hit, float(r), rank)
        work = jnp.where(hit, -jnp.inf, work)
        vals.append(m)
    return vals, rank, None


def _extract_topk_untied(s, ids, k):
    del ids
    work = s
    vals = []
    rank = jnp.full(s.shape, float(k), F32)
    for r in range(k):
        m = jnp.max(work, axis=0, keepdims=True)
        hit = work == m
        rank = jnp.where(hit, float(r), rank)
        work = jnp.where(hit, -jnp.inf, work)
        vals.append(m)
    taken = jnp.sum((rank < float(k)).astype(F32), axis=0, keepdims=True)
    return vals, rank, taken == float(k)


def _route_body(pqt_ref, k1_ref, k2_ref, rank2_ref, cnt_ref, e1_ref, e2z_ref):
    kk = PEER_TOPK
    tn = pqt_ref.shape[1]
    key_ids = lax.broadcasted_iota(jnp.int32, (PEER_NKEYS, tn), 0)
    i8 = lax.broadcasted_iota(jnp.int32, (8, tn), 0)
    i16 = lax.broadcasted_iota(jnp.int32, (kk, tn), 0)
    cand_ids = jnp.concatenate(
        [i16] + [i8 + a * kk for a in (1, 2, 3)]
        + [jnp.where(i8 >= 4, i8 * kk + b, kk * kk + i8 * kk + b) for b in (0, 1, 2)] + [(i8 + 8) * kk], axis=0)
    def route_head(h, s1, s2, extract):
        v1, rank1, ok1 = extract(s1, key_ids, kk)
        v2, rank2, ok2 = extract(s2, key_ids, kk)
        v1a = jnp.concatenate(v1, axis=0)
        v2a = jnp.concatenate(v2, axis=0)
        cand = jnp.concatenate(
            [v1[0] + v2a] + [v1[a] + v2a[:8] for a in (1, 2, 3)]
            + [jnp.where(i8 >= 4, v1a[:8] + v2[b], -jnp.inf) for b in (0, 1, 2)] + [v1a[8:] + v2[0]], axis=0)
        top, crank, ok3 = extract(cand, cand_ids, kk)
        zsum = jnp.ones_like(top[0])
        for r in range(1, kk):
            zsum = zsum + jnp.exp(top[r] - top[0])
        sel = (crank < float(kk)).astype(F32)
        low = [jnp.sum(sel[0:16], axis=0, keepdims=True)] + [
            jnp.sum(sel[8 + 8 * a:16 + 8 * a], axis=0, keepdims=True) for a in (1, 2, 3)]
        mid = sel[40:48] + sel[48:56] + sel[56:64]
        high = sel[64:72]
        cnt = jnp.zeros(s1.shape, F32)
        for a in range(kk):
            n_a = low[a] if a < 4 else (mid[a:a + 1] if a < 8 else high[a - 8:a - 7])
            cnt = jnp.where(rank1 == float(a), n_a, cnt)
        rank2_ref[h] = rank2.astype(BF16)
        cnt_ref[h] = cnt
        e1_ref[h] = jnp.exp(s1 - v1[0])
        e2z_ref[h] = (jnp.exp(s2 - v2[0]) * (0.5 / zsum)).astype(BF16)
        return None if ok1 is None else jnp.logical_and(jnp.logical_and(ok1, ok2), ok3)

    for h in range(PEER_HEADS):
        q1 = pqt_ref[pl.ds(h * 2 * PEER_HALF, PEER_HALF), :].astype(BF16)
        q2 = pqt_ref[pl.ds(h * 2 * PEER_HALF + PEER_HALF, PEER_HALF), :].astype(BF16)
        s1 = _dot(k1_ref[h], q1)
        s2 = _dot(k2_ref[h], q2)
        ok = route_head(h, s1, s2, _extract_topk_untied)
        tied = jnp.max(jnp.where(ok, 0.0, 1.0)) > 0.0

        @pl.when(tied)
        def _():
            route_head(h, s1, s2, _extract_topk)


def _route(pqt, k1, k2, tn):
    nq, t = pqt.shape
    hk = (PEER_HEADS, PEER_NKEYS, PEER_HALF)
    out = lambda dt: jax.ShapeDtypeStruct((PEER_HEADS, PEER_NKEYS, t), dt)
    ospec = pl.BlockSpec((PEER_HEADS, PEER_NKEYS, tn), lambda i: (0, 0, i))
    return pl.pallas_call(
        _route_body,
        grid=(t // tn,),
        in_specs=[
            pl.BlockSpec((nq, tn), lambda i: (0, i)),
            pl.BlockSpec(hk, lambda i: (0, 0, 0)),
            pl.BlockSpec(hk, lambda i: (0, 0, 0)),
        ],
        out_specs=[ospec, ospec, ospec, ospec],
        out_shape=[out(BF16), out(F32), out(F32), out(BF16)],
        compiler_params=_cparams(("parallel",)),
        name="peer_route",
    )(pqt, k1, k2)


PEER_I1_PER_BLOCK = 8
PEER_I1_PER_CHUNK = 2
BF16_ROWS = 16


def _expert_body(hnt_ref, u_first_ref, u_b_ref, u_next_ref, vt_prev_ref, vt_a_ref, vt_last_ref, rank2_ref,
                 cnt_ref, e1_ref, e2z_ref, x1_ref, o_ref, acc_scr, act_a, act_b, p_a, p_b):
    s = pl.program_id(1)
    hnt = hnt_ref[...]
    tn = hnt.shape[1]
    ce = PEER_I1_PER_CHUNK * PEER_NKEYS
    nchunk = PEER_I1_PER_BLOCK // PEER_I1_PER_CHUNK
    tiles = PEER_NKEYS // BF16_ROWS

    def activation(u_ref, act_ref):
        for c in range(nchunk):
            pre = _dot(u_ref[pl.ds(c * ce, ce), :], hnt)
            act_ref[pl.ds(c * ce, ce), :] = (pre * (1.0 + lax.erf(pre * (2.0 ** -0.5)))).astype(BF16)

    def row(ref, h, l):
        return jnp.broadcast_to(ref[h, l:l + 1, :], (BF16_ROWS, tn)).astype(BF16)[None]

    def gates(l0, act_ref, p_ref):
        for li in range(PEER_I1_PER_BLOCK):
            gate = None
            for h in range(PEER_HEADS):
                rank2 = rank2_ref[h].reshape(tiles, BF16_ROWS, tn)
                e2z = e2z_ref[h].reshape(tiles, BF16_ROWS, tn)
                term = jnp.where(rank2 < row(cnt_ref, h, l0 + li), e2z * row(e1_ref, h, l0 + li), 0.0)
                gate = term if gate is None else gate + term
            rows = pl.ds(li * PEER_NKEYS, PEER_NKEYS)
            p_ref[rows, :] = gate.reshape(PEER_NKEYS, tn) * act_ref[rows, :]

    @pl.when(s == 0)
    def _():
        acc_scr[...] = jnp.zeros_like(acc_scr)
        p_b[...] = jnp.zeros_like(p_b)
        activation(u_first_ref, act_a)

    out_prev = _dot(vt_prev_ref[...], p_b[...])
    activation(u_b_ref, act_b)
    gates(0, act_a, p_a)
    out_a = _dot(vt_a_ref[...], p_a[...])
    activation(u_next_ref, act_a)
    gates(PEER_I1_PER_BLOCK, act_b, p_b)
    acc_scr[...] += out_prev + out_a

    @pl.when(s == pl.num_programs(1) - 1)
    def _():
        o_ref[...] = x1_ref[...] + (acc_scr[...] + _dot(vt_last_ref[...], p_b[...])).T


def _experts(hnt, u_b, vt_b, rank2, cnt, e1, e2z, x1, tn):
    d, t = hnt.shape
    ne = u_b.shape[0]
    eb = PEER_I1_PER_BLOCK * PEER_NKEYS
    nblk = ne // eb
    once = pl.Buffered(1)
    full = pl.BlockSpec((PEER_HEADS, PEER_NKEYS, tn), lambda i, s: (0, 0, i))
    part = pl.BlockSpec((PEER_HEADS, 2 * PEER_I1_PER_BLOCK, tn), lambda i, s: (0, s, i))
    return pl.pallas_call(
        _expert_body,
        grid=(t // tn, nblk // 2),
        in_specs=[
            pl.BlockSpec((d, tn), lambda i, s: (0, i)),
            pl.BlockSpec((eb, d), lambda i, s: (0, 0), pipeline_mode=once),
            pl.BlockSpec((eb, d), lambda i, s: (2 * s + 1, 0)),
            pl.BlockSpec((eb, d), lambda i, s: (jnp.minimum(2 * s + 2, nblk - 1), 0)),
            pl.BlockSpec((d, eb), lambda i, s: (0, jnp.maximum(2 * s - 1, 0))),
            pl.BlockSpec((d, eb), lambda i, s: (0, 2 * s)),
            pl.BlockSpec((d, eb), lambda i, s: (0, nblk - 1), pipeline_mode=once),
            full, part, part, full,
            pl.BlockSpec((tn, d), lambda i, s: (i, 0), pipeline_mode=once),
        ],
        out_specs=pl.BlockSpec((tn, d), lambda i, s: (i, 0)),
        out_shape=jax.ShapeDtypeStruct((t, d), F32),
        scratch_shapes=[pltpu.VMEM((d, tn), F32)] + [pltpu.VMEM((eb, tn), BF16) for _ in range(4)],
        compiler_params=_cparams(("parallel", "arbitrary")),
        name="peer_experts",
    )(hnt, u_b, u_b, u_b, vt_b, vt_b, vt_b, rank2, cnt, e1, e2z, x1)


def _suffix_sum_matrix(tk):
    r = jnp.arange(2 * tk)[:, None] % tk
    c = jnp.arange(2 * tk)[None, :]
    return jnp.where(c < tk, r > c, True).astype(BF16)


def _layer(x, norm1_g, w_in, sb_q_g, sb_k_g, sb_o_g, conv_w, a_log, dt_bias, gdn_o_g, w_out, norm2_g,
           w_q, keys1, keys2, u_tab, v_tab):
    b, s, d = x.shape
    t = b * s
    n_main = 3 * SB_WIDTH + 4 * GDN_WIDTH
    x2 = x.reshape(t, d)
    w_main = w_in[:, :n_main].astype(BF16)
    w_ab = jnp.pad(w_in[:, n_main:], ((0, 0), (0, LANES - 2 * GDN_HEADS))).astype(BF16)
    tm = min(256, t)
    proj, ab = _inproj(x2, norm1_g.reshape(1, d), w_main, w_ab, tm)
    proj3 = proj.reshape(b, s, n_main)
    ab3 = ab.reshape(b, s, LANES)

    tq, tk = min(256, s), 128
    tile2 = lambda g: jnp.tile(g, 2).reshape(1, LANES)
    sb_o = _sb_attention(proj3, tile2(sb_q_g), tile2(sb_k_g), tile2(sb_o_g), _suffix_sum_matrix(tk), tq, tk)

    bcast = lambda p: jnp.broadcast_to(p[:, None, None], (GDN_HEADS, 1, LANES))
    gd_o = _gdn(proj3, ab3, conv_w, bcast(a_log), bcast(dt_bias), gdn_o_g.reshape(1, LANES), min(512, s))

    wo = w_out.astype(BF16)
    x1, hnt, pqt = _outproj(sb_o.reshape(t, SB_WIDTH), gd_o.reshape(t, GDN_WIDTH), x2, wo[:SB_WIDTH],
                            wo[SB_WIDTH:], norm2_g.reshape(1, d), w_q.T.astype(BF16), tm)

    rank2, cnt, e1, e2z = _route(pqt, keys1.astype(BF16), keys2.astype(BF16), min(128, t))
    y = _experts(hnt, u_tab.astype(BF16), v_tab.T.astype(BF16), rank2, cnt, e1, e2z, x1, min(512, t))
    return y.reshape(b, s, d)


def kernel(x, norm1_g, w_in, sb_q_norm_g, sb_k_norm_g, sb_out_norm_g, gdn_conv_w, gdn_a_log, gdn_dt_bias,
           gdn_out_norm_g, w_out, norm2_g, peer_w_q, peer_keys1, peer_keys2, peer_u, peer_v):
    for layer in range(norm1_g.shape[0]):
        x = _layer(x, norm1_g[layer], w_in[layer], sb_q_norm_g[layer], sb_k_norm_g[layer],
                   sb_out_norm_g[layer], gdn_conv_w[layer], gdn_a_log[layer], gdn_dt_bias[layer],
                   gdn_out_norm_g[layer], w_out[layer], norm2_g[layer], peer_w_q[layer],
                   peer_keys1[layer], peer_keys2[layer], peer_u[layer], peer_v[layer])
    return x
```

```python
import functools

import jax
import jax.numpy as jnp
from jax import lax
from jax.experimental import pallas as pl
from jax.experimental.pallas import tpu as pltpu

F32 = jnp.float32
BF16 = jnp.bfloat16
EPS = 1e-6

SB_HEADS = 8
SB_HEAD_DIM = 64
SB_WIDTH = SB_HEADS * SB_HEAD_DIM
GDN_HEADS = 4
GDN_HEAD_DIM = 128
GDN_WIDTH = GDN_HEADS * GDN_HEAD_DIM
GDN_CONV = 4
PEER_HEADS = 8
PEER_NKEYS = 128
PEER_HALF = 128
PEER_TOPK = 16
LANES = 128

VMEM_LIMIT = 56 * 1024 * 1024


def _cparams(sem):
    return pltpu.CompilerParams(dimension_semantics=sem, vmem_limit_bytes=VMEM_LIMIT)


def _dot(a, b):
    return jnp.dot(a, b, preferred_element_type=F32)


def _dot_nt(a, b):
    return lax.dot_general(a, b, (((1,), (1,)), ((), ())), preferred_element_type=F32)


def _dot_tn(a, b):
    return lax.dot_general(a, b, (((0,), (0,)), ((), ())), preferred_element_type=F32)


def _dot_f32(a, b):
    return jnp.dot(a, b, preferred_element_type=F32, precision=lax.Precision.HIGHEST)


def _split(a):
    hi = a.astype(BF16)
    return hi, (a - hi.astype(F32)).astype(BF16)


def _dot3(a, b):
    ah, al = _split(a)
    bh, bl = _split(b)
    return _dot(jnp.concatenate([ah, ah, al], axis=1), jnp.concatenate([bh, bl, bh], axis=0))


def _softplus(x):
    return jnp.maximum(x, 0.0) + jnp.log1p(jnp.exp(-jnp.abs(x)))


def _sigmoid(x):
    return 1.0 / (1.0 + jnp.exp(-x))


def _inproj_body(x_ref, g_ref, w_ref, wab_ref, proj_ref, ab_ref):
    x = x_ref[...]
    ms = jnp.mean(x * x, axis=-1, keepdims=True)
    h = (x * lax.rsqrt(ms + EPS) * g_ref[...]).astype(BF16)
    proj_ref[...] = _dot(h, w_ref[...])
    ab_ref[...] = _dot(h, wab_ref[...])


def _inproj(x2, g, w_main, w_ab, tm):
    t, d = x2.shape
    n = w_main.shape[1]
    return pl.pallas_call(
        _inproj_body,
        grid=(t // tm,),
        in_specs=[
            pl.BlockSpec((tm, d), lambda i: (i, 0)),
            pl.BlockSpec((1, d), lambda i: (0, 0)),
            pl.BlockSpec((d, n), lambda i: (0, 0)),
            pl.BlockSpec((d, LANES), lambda i: (0, 0)),
        ],
        out_specs=[
            pl.BlockSpec((tm, n), lambda i: (i, 0)),
            pl.BlockSpec((tm, LANES), lambda i: (i, 0)),
        ],
        out_shape=[
            jax.ShapeDtypeStruct((t, n), F32),
            jax.ShapeDtypeStruct((t, LANES), F32),
        ],
        compiler_params=_cparams(("parallel",)),
        name="inproj",
    )(x2, g, w_main, w_ab)


SB_DEAD_LOG = -104.0


def _sb_body(q_ref, k_ref, v_ref, gq_ref, gk_ref, go_ref, m2_ref, o_ref, kn_scr, vb_scr, *, tq, tk):
    i = pl.program_id(2)
    lane = lax.broadcasted_iota(jnp.int32, (1, LANES), 1)
    is0 = lane < SB_HEAD_DIM

    def headnorm(x, g):
        x2 = x * x
        s0 = jnp.sum(jnp.where(is0, x2, 0.0), axis=-1, keepdims=True)
        s1 = jnp.sum(jnp.where(is0, 0.0, x2), axis=-1, keepdims=True)
        ms = jnp.where(is0, s0, s1) * (1.0 / SB_HEAD_DIM)
        return x * lax.rsqrt(ms + EPS) * g

    @pl.when(i == 0)
    def _():
        kn_scr[...] = headnorm(k_ref[...], gk_ref[...]).astype(BF16)
        vb_scr[...] = v_ref[...].astype(BF16)

    qn = headnorm(q_ref[...], gq_ref[...]) * (SB_HEAD_DIM ** -0.5)
    qh = (jnp.where(is0, qn, 0.0).astype(BF16), jnp.where(is0, 0.0, qn).astype(BF16))
    m2 = m2_ref[...]
    row = i * tq + lax.broadcasted_iota(jnp.int32, (tq, tk), 0)
    col0 = lax.broadcasted_iota(jnp.int32, (tq, tk), 1)
    nkb = (i + 1) * (tq // tk)

    def scores(j):
        kj = kn_scr[pl.ds(pl.multiple_of(j * tk, tk), tk), :]
        return [_dot_nt(qh[h], kj) for h in range(2)]

    def logs(j, zs):
        causal = (col0 + j * tk) < row
        zls, cats = [], []
        for z in zs:
            lk = jnp.where(causal, -(jnp.maximum(z, 0.0) + jnp.log(1.0 + jnp.exp(-jnp.abs(z)))), 0.0)
            hi = lk.astype(BF16)
            lo = (lk - hi.astype(F32)).astype(BF16)
            zls.append(jnp.where(causal, z + lk, -jnp.inf))
            cats.append(jnp.concatenate([hi, lo], axis=1))
        return zls, cats

    def sums(cats):
        return [_dot(c, m2) for c in cats]

    def body(state):
        n, _, carry = state
        j = nkb - 1 - 2 * n
        zs_a = scores(j)
        zs_b = scores(j - 1)
        zl_a, cats_a = logs(j, zs_a)
        rt_a = sums(cats_a)
        zl_b, cats_b = logs(j - 1, zs_b)
        rt_b = sums(cats_b)
        v2 = vb_scr[pl.ds(pl.multiple_of((j - 1) * tk, tk), 2 * tk), :]
        out = []
        for h in range(2):
            acc, rest = carry[2 * h], carry[2 * h + 1]
            rest_mid = rest + rt_a[h][:, tk:]
            w_a = jnp.exp(zl_a[h] + (rest + rt_a[h][:, :tk])).astype(BF16)
            w_b = jnp.exp(zl_b[h] + (rest_mid + rt_b[h][:, :tk])).astype(BF16)
            out += [acc + _dot(jnp.concatenate([w_b, w_a], axis=1), v2), rest_mid + rt_b[h][:, tk:]]
        alive = (jnp.max(jnp.maximum(out[1], out[3])) > SB_DEAD_LOG).astype(jnp.int32)
        return n + 1, alive, tuple(out)

    def cond(state):
        return jnp.logical_and(state[0] < nkb // 2, state[1] > 0)

    zero = jnp.zeros((tq, LANES), F32)
    res = lax.while_loop(cond, body, (jnp.int32(0), jnp.int32(1), (zero, zero, zero, zero)))[2]
    o = jnp.where(is0, res[0], res[2])
    o_ref[...] = headnorm(o, go_ref[...])


def _sb_attention(proj3, gq, gk, go, m2, tq, tk):
    b, s, _ = proj3.shape
    hp = SB_HEADS // 2
    return pl.pallas_call(
        functools.partial(_sb_body, tq=tq, tk=tk),
        grid=(b, hp, s // tq),
        in_specs=[
            pl.BlockSpec((None, tq, LANES), lambda bi, h, i: (bi, i, h)),
            pl.BlockSpec((None, s, LANES), lambda bi, h, i: (bi, 0, hp + h)),
            pl.BlockSpec((None, s, LANES), lambda bi, h, i: (bi, 0, 2 * hp + h)),
            pl.BlockSpec((1, LANES), lambda bi, h, i: (0, 0)),
            pl.BlockSpec((1, LANES), lambda bi, h, i: (0, 0)),
            pl.BlockSpec((1, LANES), lambda bi, h, i: (0, 0)),
            pl.BlockSpec((2 * tk, 2 * tk), lambda bi, h, i: (0, 0)),
        ],
        out_specs=pl.BlockSpec((None, tq, LANES), lambda bi, h, i: (bi, i, h)),
        out_shape=jax.ShapeDtypeStruct((b, s, SB_WIDTH), F32),
        scratch_shapes=[pltpu.VMEM((s, LANES), BF16), pltpu.VMEM((s, LANES), BF16)],
        compiler_params=_cparams(("parallel", "parallel", "arbitrary")),
        name="sb_attention",
    )(proj3, proj3, proj3, gq, gk, go, m2)


GDN_BLOCK = 128


GDN_HALO = 8

def _gdn_body(x_ref, halo_ref, z_ref, ab_ref, cw_ref, alog_ref, dtb_ref, og_ref, o_ref,
              q_scr, k_scr, v_scr, g_scr, beta_scr, state_scr, *, ts):
    t = pl.program_id(1)
    c = GDN_BLOCK
    nh = GDN_HEADS
    lane = lax.broadcasted_iota(jnp.int32, (1, LANES), 1)

    @pl.when(t == 0)
    def _():
        state_scr[...] = jnp.zeros_like(state_scr)

    halo = jnp.where(t > 0, halo_ref[...], 0.0)
    xe = jnp.concatenate([halo, x_ref[...]], axis=0)
    cw = cw_ref[...]
    y = xe * cw[3:4, :]
    for d in (1, 2, 3):
        y = y + pltpu.roll(xe, d, 0) * cw[3 - d:4 - d, :]
    y = y[GDN_HALO:, :]
    y = y * _sigmoid(y)

    def l2n(x):
        return x * lax.rsqrt(jnp.sum(x * x, axis=-1, keepdims=True) + EPS)

    ab = ab_ref[...]
    for h in range(nh):
        sl = pl.ds(h * LANES, LANES)
        q_scr[:, sl] = l2n(y[:, h * LANES:(h + 1) * LANES]) * (GDN_HEAD_DIM ** -0.5)
        k_scr[:, sl] = l2n(y[:, GDN_WIDTH + h * LANES:GDN_WIDTH + (h + 1) * LANES])
        a_col = jnp.sum(jnp.where(lane == h, ab, 0.0), axis=-1, keepdims=True)
        b_col = jnp.sum(jnp.where(lane == h + nh, ab, 0.0), axis=-1, keepdims=True)
        g_scr[:, sl] = -jnp.exp(alog_ref[h]) * _softplus(a_col + dtb_ref[h])
        beta_scr[:, sl] = jnp.broadcast_to(_sigmoid(b_col), (ts, LANES))
    v_scr[...] = y[:, 2 * GDN_WIDTH:]

    ri = lax.broadcasted_iota(jnp.int32, (c, nh * c), 0)
    ci = lax.broadcasted_iota(jnp.int32, (c, nh * c), 1) % c
    lower_incl = ci <= ri
    lower_strict = ci < ri
    eye = (ci == ri).astype(F32)
    ltri = lower_incl[:, :c].astype(F32)
    ones = jnp.ones((c, c), F32)
    og = og_ref[...]
    heads = range(nh)
    hs = lambda m, h: m[:, h * c:(h + 1) * c]

    def blk(n, states):
        r0 = pl.multiple_of(n * c, c)
        q = q_scr[pl.ds(r0, c), :]
        k = k_scr[pl.ds(r0, c), :]
        v = v_scr[pl.ds(r0, c), :]
        g = g_scr[pl.ds(r0, c), :]
        beta = beta_scr[pl.ds(r0, c), :]
        gc = _dot_f32(ltri, g)
        gc_row = _dot_f32(ones, gc * eye)
        decay = jnp.exp(jnp.where(lower_incl, gc - gc_row, -jnp.inf))
        kb = k * beta
        kbf = k.astype(BF16)
        kbb = kb.astype(BF16)
        qbf = q.astype(BF16)
        kk = jnp.concatenate([_dot_nt(hs(kbb, h), hs(kbf, h)) for h in heads], axis=1)
        qk = jnp.concatenate([_dot_nt(hs(qbf, h), hs(kbf, h)) for h in heads], axis=1)
        a = jnp.where(lower_strict, kk * decay, 0.0)
        attn = jnp.where(lower_incl, qk * decay, 0.0).astype(BF16)
        nmat = [-hs(a, h) for h in heads]
        xp = [hs(a, h) for h in heads]
        for _ in range(6):
            xp = [_dot3(x, x) for x in xp]
            prod = [_dot3(nm, x) for nm, x in zip(nmat, xp)]
            nmat = [nm + x + p for nm, x, p in zip(nmat, xp, prod)]
        eg = jnp.exp(gc)
        vb = v * beta
        kbd = kb * eg
        gl = gc[c - 1:c, :]
        qd = (q * eg).astype(BF16)
        kd = (k * jnp.exp(gl - gc)).astype(BF16)
        dl = jnp.exp(gl)
        nb = [nm.astype(BF16) for nm in nmat]
        u = [hs(vb, h) + _dot(nb[h], hs(vb, h).astype(BF16)) for h in heads]
        w = [(hs(kbd, h) + _dot(nb[h], hs(kbd, h).astype(BF16))).astype(BF16) for h in heads]
        sb = [st.astype(BF16) for st in states]
        v_new = [u[h] - _dot(w[h], sb[h]) for h in heads]
        o_state = [_dot(hs(qd, h), sb[h]) for h in heads]
        vnb = [vn.astype(BF16) for vn in v_new]
        o = [o_state[h] + _dot(hs(attn, h), vnb[h]) for h in heads]
        new_states = tuple(states[h] * hs(dl, h) + _dot_tn(hs(kd, h), vnb[h]) for h in heads)
        on = jnp.concatenate(
            [x * lax.rsqrt(jnp.mean(x * x, axis=-1, keepdims=True) + EPS) * og for x in o], axis=1)
        z = z_ref[pl.ds(r0, c), :]
        o_ref[pl.ds(r0, c), :] = on * (z * _sigmoid(z))
        return new_states

    states = lax.fori_loop(0, ts // c, blk, tuple(state_scr[h] for h in heads))
    for h in heads:
        state_scr[h] = states[h]


def _gdn(proj3, ab3, conv_w, alog_b, dtb_b, og, ts):
    b, s, _ = proj3.shape
    w3 = 3 * GDN_WIDTH
    assert 3 * SB_WIDTH == w3 and 2 * w3 % GDN_WIDTH == 0
    per = ts // GDN_HALO
    par = pl.BlockSpec((GDN_HEADS, 1, LANES), lambda bi, t: (0, 0, 0))
    return pl.pallas_call(
        functools.partial(_gdn_body, ts=ts),
        grid=(b, s // ts),
        in_specs=[
            pl.BlockSpec((None, ts, w3), lambda bi, t: (bi, t, 1)),
            pl.BlockSpec((None, GDN_HALO, w3), lambda bi, t: (bi, jnp.maximum(t * per - 1, 0), 1)),
            pl.BlockSpec((None, ts, GDN_WIDTH), lambda bi, t: (bi, t, 2 * w3 // GDN_WIDTH)),
            pl.BlockSpec((None, ts, LANES), lambda bi, t: (bi, t, 0)),
            pl.BlockSpec((GDN_CONV, w3), lambda bi, t: (0, 0)),
            par, par,
            pl.BlockSpec((1, LANES), lambda bi, t: (0, 0)),
        ],
        out_specs=pl.BlockSpec((None, ts, GDN_WIDTH), lambda bi, t: (bi, t, 0)),
        out_shape=jax.ShapeDtypeStruct((b, s, GDN_WIDTH), F32),
        scratch_shapes=[pltpu.VMEM((ts, GDN_WIDTH), F32) for _ in range(5)]
        + [pltpu.VMEM((GDN_HEADS, GDN_HEAD_DIM, GDN_HEAD_DIM), F32)],
        compiler_params=_cparams(("parallel", "arbitrary")),
        name="gdn",
    )(proj3, proj3, proj3, ab3, conv_w, alog_b, dtb_b, og)


def _outproj_body(sb_ref, gd_ref, x_ref, wo1_ref, wo2_ref, g2_ref, wqt_ref, x1_ref, hnt_ref, pqt_ref):
    mix = _dot(sb_ref[...].astype(BF16), wo1_ref[...]) + _dot(gd_ref[...].astype(BF16), wo2_ref[...])
    x1 = x_ref[...] + mix
    x1_ref[...] = x1
    ms = jnp.mean(x1 * x1, axis=-1, keepdims=True)
    hn = x1 * lax.rsqrt(ms + EPS) * g2_ref[...]
    hnt = hn.T.astype(BF16)
    hnt_ref[...] = hnt
    pqt_ref[...] = _dot(wqt_ref[...], hnt)


def _outproj(sb_o, gd_o, x2, wo1, wo2, g2, wqt, tm):
    t, d = x2.shape
    nq = wqt.shape[0]
    return pl.pallas_call(
        _outproj_body,
        grid=(t // tm,),
        in_specs=[
            pl.BlockSpec((tm, SB_WIDTH), lambda i: (i, 0)),
            pl.BlockSpec((tm, GDN_WIDTH), lambda i: (i, 0)),
            pl.BlockSpec((tm, d), lambda i: (i, 0)),
            pl.BlockSpec((SB_WIDTH, d), lambda i: (0, 0)),
            pl.BlockSpec((GDN_WIDTH, d), lambda i: (0, 0)),
            pl.BlockSpec((1, d), lambda i: (0, 0)),
            pl.BlockSpec((nq, d), lambda i: (0, 0)),
        ],
        out_specs=[
            pl.BlockSpec((tm, d), lambda i: (i, 0)),
            pl.BlockSpec((d, tm), lambda i: (0, i)),
            pl.BlockSpec((nq, tm), lambda i: (0, i)),
        ],
        out_shape=[
            jax.ShapeDtypeStruct((t, d), F32),
            jax.ShapeDtypeStruct((d, t), BF16),
            jax.ShapeDtypeStruct((nq, t), F32),
        ],
        compiler_params=_cparams(("parallel",)),
        name="outproj",
    )(sb_o, gd_o, x2, wo1, wo2, g2, wqt)


def _extract_topk(s, ids, k):
    big = jnp.int32(2 ** 30)
    work = s
    vals = []
    rank = jnp.full(s.shape, float(k), F32)
    for r in range(k):
        m = jnp.max(work, axis=0, keepdims=True)
        first = jnp.min(jnp.where(work == m, ids, big), axis=0, keepdims=True)
        hit = ids == first
        rank = jnp.where(hit, float(r), rank)
        work = jnp.where(hit, -jnp.inf, work)
        vals.append(m)
    return vals, rank, None


def _extract_topk_untied(s, ids, k):
    del ids
    work = s
    vals = []
    rank = jnp.full(s.shape, float(k), F32)
    for r in range(k):
        m = jnp.max(work, axis=0, keepdims=True)
        hit = work == m
        rank = jnp.where(hit, float(r), rank)
        work = jnp.where(hit, -jnp.inf, work)
        vals.append(m)
    taken = jnp.sum((rank < float(k)).astype(F32), axis=0, keepdims=True)
    return vals, rank, taken == float(k)


def _route_body(pqt_ref, k1_ref, k2_ref, rank2_ref, cnt_ref, e1_ref, e2z_ref):
    kk = PEER_TOPK
    tn = pqt_ref.shape[1]
    key_ids = lax.broadcasted_iota(jnp.int32, (PEER_NKEYS, tn), 0)
    i8 = lax.broadcasted_iota(jnp.int32, (8, tn), 0)
    i16 = lax.broadcasted_iota(jnp.int32, (kk, tn), 0)
    cand_ids = jnp.concatenate(
        [i16] + [i8 + a * kk for a in (1, 2, 3)]
        + [jnp.where(i8 >= 4, i8 * kk + b, kk * kk + i8 * kk + b) for b in (0, 1, 2)] + [(i8 + 8) * kk], axis=0)
    def route_head(h, s1, s2, extract):
        v1, rank1, ok1 = extract(s1, key_ids, kk)
        v2, rank2, ok2 = extract(s2, key_ids, kk)
        v1a = jnp.concatenate(v1, axis=0)
        v2a = jnp.concatenate(v2, axis=0)
        cand = jnp.concatenate(
            [v1[0] + v2a] + [v1[a] + v2a[:8] for a in (1, 2, 3)]
            + [jnp.where(i8 >= 4, v1a[:8] + v2[b], -jnp.inf) for b in (0, 1, 2)] + [v1a[8:] + v2[0]], axis=0)
        top, crank, ok3 = extract(cand, cand_ids, kk)
        zsum = jnp.ones_like(top[0])
        for r in range(1, kk):
            zsum = zsum + jnp.exp(top[r] - top[0])
        sel = (crank < float(kk)).astype(F32)
        low = [jnp.sum(sel[0:16], axis=0, keepdims=True)] + [
            jnp.sum(sel[8 + 8 * a:16 + 8 * a], axis=0, keepdims=True) for a in (1, 2, 3)]
        mid = sel[40:48] + sel[48:56] + sel[56:64]
        high = sel[64:72]
        cnt = jnp.zeros(s1.shape, F32)
        for a in range(kk):
            n_a = low[a] if a < 4 else (mid[a:a + 1] if a < 8 else high[a - 8:a - 7])
            cnt = jnp.where(rank1 == float(a), n_a, cnt)
        rank2_ref[h] = rank2.astype(BF16)
        cnt_ref[h] = cnt
        e1_ref[h] = jnp.exp(s1 - v1[0])
        e2z_ref[h] = (jnp.exp(s2 - v2[0]) * (0.5 / zsum)).astype(BF16)
        return None if ok1 is None else jnp.logical_and(jnp.logical_and(ok1, ok2), ok3)

    for h in range(PEER_HEADS):
        q1 = pqt_ref[pl.ds(h * 2 * PEER_HALF, PEER_HALF), :].astype(BF16)
        q2 = pqt_ref[pl.ds(h * 2 * PEER_HALF + PEER_HALF, PEER_HALF), :].astype(BF16)
        s1 = _dot(k1_ref[h], q1)
        s2 = _dot(k2_ref[h], q2)
        ok = route_head(h, s1, s2, _extract_topk_untied)
        tied = jnp.max(jnp.where(ok, 0.0, 1.0)) > 0.0

        @pl.when(tied)
        def _():
            route_head(h, s1, s2, _extract_topk)


def _route(pqt, k1, k2, tn):
    nq, t = pqt.shape
    hk = (PEER_HEADS, PEER_NKEYS, PEER_HALF)
    out = lambda dt: jax.ShapeDtypeStruct((PEER_HEADS, PEER_NKEYS, t), dt)
    ospec = pl.BlockSpec((PEER_HEADS, PEER_NKEYS, tn), lambda i: (0, 0, i))
    return pl.pallas_call(
        _route_body,
        grid=(t // tn,),
        in_specs=[
            pl.BlockSpec((nq, tn), lambda i: (0, i)),
            pl.BlockSpec(hk, lambda i: (0, 0, 0)),
            pl.BlockSpec(hk, lambda i: (0, 0, 0)),
        ],
        out_specs=[ospec, ospec, ospec, ospec],
        out_shape=[out(BF16), out(F32), out(F32), out(BF16)],
        compiler_params=_cparams(("parallel",)),
        name="peer_route",
    )(pqt, k1, k2)


PEER_I1_PER_BLOCK = 8
PEER_I1_PER_CHUNK = 2
BF16_ROWS = 16


def _expert_body(hnt_ref, u_first_ref, u_b_ref, u_next_ref, vt_prev_ref, vt_a_ref, vt_last_ref, rank2_ref,
                 cnt_ref, e1_ref, e2z_ref, x1_ref, o_ref, acc_scr, act_a, act_b, p_a, p_b):
    s = pl.program_id(1)
    hnt = hnt_ref[...]
    tn = hnt.shape[1]
    ce = PEER_I1_PER_CHUNK * PEER_NKEYS
    nchunk = PEER_I1_PER_BLOCK // PEER_I1_PER_CHUNK
    tiles = PEER_NKEYS // BF16_ROWS

    def activation(u_ref, act_ref, c):
        pre = _dot(u_ref[pl.ds(c * ce, ce), :], hnt)
        act_ref[pl.ds(c * ce, ce), :] = (pre * (1.0 + lax.erf(pre * (2.0 ** -0.5)))).astype(BF16)

    def row(ref, h, l):
        return jnp.broadcast_to(ref[h, l:l + 1, :], (BF16_ROWS, tn)).astype(BF16)[None]

    def gates(l, li, act_ref, p_ref):
        gate = None
        for h in range(PEER_HEADS):
            rank2 = rank2_ref[h].reshape(tiles, BF16_ROWS, tn)
            e2z = e2z_ref[h].reshape(tiles, BF16_ROWS, tn)
            term = jnp.where(rank2 < row(cnt_ref, h, l), e2z * row(e1_ref, h, l), 0.0)
            gate = term if gate is None else gate + term
        rows = pl.ds(li * PEER_NKEYS, PEER_NKEYS)
        p_ref[rows, :] = gate.reshape(PEER_NKEYS, tn) * act_ref[rows, :]

    def half(vt_ref, p_in, u_ref, act_out, l0, act_in, p_out):
        rc = vt_ref.shape[0] // nchunk
        p_done = p_in[...]
        for c in range(nchunk):
            rows = pl.ds(c * rc, rc)
            acc_scr[rows, :] += _dot(vt_ref[rows, :], p_done)
            activation(u_ref, act_out, c)
            for li in range(c * PEER_I1_PER_CHUNK, (c + 1) * PEER_I1_PER_CHUNK):
                gates(l0 + li, li, act_in, p_out)

    @pl.when(s == 0)
    def _():
        acc_scr[...] = jnp.zeros_like(acc_scr)
        p_b[...] = jnp.zeros_like(p_b)
        for c in range(nchunk):
            activation(u_first_ref, act_a, c)

    half(vt_prev_ref, p_b, u_b_ref, act_b, 0, act_a, p_a)
    half(vt_a_ref, p_a, u_next_ref, act_a, PEER_I1_PER_BLOCK, act_b, p_b)

    @pl.when(s == pl.num_programs(1) - 1)
    def _():
        o_ref[...] = x1_ref[...] + (acc_scr[...] + _dot(vt_last_ref[...], p_b[...])).T


def _experts(hnt, u_b, vt_b, rank2, cnt, e1, e2z, x1, tn):
    d, t = hnt.shape
    ne = u_b.shape[0]
    eb = PEER_I1_PER_BLOCK * PEER_NKEYS
    nblk = ne // eb
    once = pl.Buffered(1)
    full = pl.BlockSpec((PEER_HEADS, PEER_NKEYS, tn), lambda i, s: (0, 0, i))
    part = pl.BlockSpec((PEER_HEADS, 2 * PEER_I1_PER_BLOCK, tn), lambda i, s: (0, s, i))
    return pl.pallas_call(
        _expert_body,
        grid=(t // tn, nblk // 2),
        in_specs=[
            pl.BlockSpec((d, tn), lambda i, s: (0, i)),
            pl.BlockSpec((eb, d), lambda i, s: (0, 0), pipeline_mode=once),
            pl.BlockSpec((eb, d), lambda i, s: (2 * s + 1, 0)),
            pl.BlockSpec((eb, d), lambda i, s: (jnp.minimum(2 * s + 2, nblk - 1), 0)),
            pl.BlockSpec((d, eb), lambda i, s: (0, jnp.maximum(2 * s - 1, 0))),
            pl.BlockSpec((d, eb), lambda i, s: (0, 2 * s)),
            pl.BlockSpec((d, eb), lambda i, s: (0, nblk - 1), pipeline_mode=once),
            full, part, part, full,
            pl.BlockSpec((tn, d), lambda i, s: (i, 0), pipeline_mode=once),
        ],
        out_specs=pl.BlockSpec((tn, d), lambda i, s: (i, 0)),
        out_shape=jax.ShapeDtypeStruct((t, d), F32),
        scratch_shapes=[pltpu.VMEM((d, tn), F32)] + [pltpu.VMEM((eb, tn), BF16) for _ in range(4)],
        compiler_params=_cparams(("parallel", "arbitrary")),
        name="peer_experts",
    )(hnt, u_b, u_b, u_b, vt_b, vt_b, vt_b, rank2, cnt, e1, e2z, x1)


def _suffix_sum_matrix(tk):
    r = jnp.arange(2 * tk)[:, None] % tk
    c = jnp.arange(2 * tk)[None, :]
    return jnp.where(c < tk, r > c, True).astype(BF16)


def _layer(x, norm1_g, w_in, sb_q_g, sb_k_g, sb_o_g, conv_w, a_log, dt_bias, gdn_o_g, w_out, norm2_g,
           w_q, keys1, keys2, u_tab, v_tab):
    b, s, d = x.shape
    t = b * s
    n_main = 3 * SB_WIDTH + 4 * GDN_WIDTH
    x2 = x.reshape(t, d)
    w_main = w_in[:, :n_main].astype(BF16)
    w_ab = jnp.pad(w_in[:, n_main:], ((0, 0), (0, LANES - 2 * GDN_HEADS))).astype(BF16)
    tm = min(256, t)
    proj, ab = _inproj(x2, norm1_g.reshape(1, d), w_main, w_ab, tm)
    proj3 = proj.reshape(b, s, n_main)
    ab3 = ab.reshape(b, s, LANES)

    tq, tk = min(256, s), 128
    tile2 = lambda g: jnp.tile(g, 2).reshape(1, LANES)
    sb_o = _sb_attention(proj3, tile2(sb_q_g), tile2(sb_k_g), tile2(sb_o_g), _suffix_sum_matrix(tk), tq, tk)

    bcast = lambda p: jnp.broadcast_to(p[:, None, None], (GDN_HEADS, 1, LANES))
    gd_o = _gdn(proj3, ab3, conv_w, bcast(a_log), bcast(dt_bias), gdn_o_g.reshape(1, LANES), min(512, s))

    wo = w_out.astype(BF16)
    x1, hnt, pqt = _outproj(sb_o.reshape(t, SB_WIDTH), gd_o.reshape(t, GDN_WIDTH), x2, wo[:SB_WIDTH],
                            wo[SB_WIDTH:], norm2_g.reshape(1, d), w_q.T.astype(BF16), tm)

    rank2, cnt, e1, e2z = _route(pqt, keys1.astype(BF16), keys2.astype(BF16), min(256, t))
    y = _experts(hnt, u_tab.astype(BF16), v_tab.T.astype(BF16), rank2, cnt, e1, e2z, x1, min(512, t))
    return y.reshape(b, s, d)


def kernel(x, norm1_g, w_in, sb_q_norm_g, sb_k_norm_g, sb_out_norm_g, gdn_conv_w, gdn_a_log, gdn_dt_bias,
           gdn_out_norm_g, w_out, norm2_g, peer_w_q, peer_keys1, peer_keys2, peer_u, peer_v):
    for layer in range(norm1_g.shape[0]):
        x = _layer(x, norm1_g[layer], w_in[layer], sb_q_norm_g[layer], sb_k_norm_g[layer],
                   sb_out_norm_g[layer], gdn_conv_w[layer], gdn_a_log[layer], gdn_dt_bias[layer],
                   gdn_out_norm_g[layer], w_out[layer], norm2_g[layer], peer_w_q[layer],
                   peer_keys1[layer], peer_keys2[layer], peer_u[layer], peer_v[layer])
    return x
```

```python
import functools

import jax
import jax.numpy as jnp
from jax import lax
from jax.experimental import pallas as pl
from jax.experimental.pallas import tpu as pltpu

F32 = jnp.float32
BF16 = jnp.bfloat16
EPS = 1e-6

SB_HEADS = 8
SB_HEAD_DIM = 64
SB_WIDTH = SB_HEADS * SB_HEAD_DIM
GDN_HEADS = 4
GDN_HEAD_DIM = 128
GDN_WIDTH = GDN_HEADS * GDN_HEAD_DIM
GDN_CONV = 4
PEER_HEADS = 8
PEER_NKEYS = 128
PEER_HALF = 128
PEER_TOPK = 16
LANES = 128

VMEM_LIMIT = 56 * 1024 * 1024


def _cparams(sem):
    return pltpu.CompilerParams(dimension_semantics=sem, vmem_limit_bytes=VMEM_LIMIT)


def _dot(a, b):
    return jnp.dot(a, b, preferred_element_type=F32)


def _dot_nt(a, b):
    return lax.dot_general(a, b, (((1,), (1,)), ((), ())), preferred_element_type=F32)


def _dot_tn(a, b):
    return lax.dot_general(a, b, (((0,), (0,)), ((), ())), preferred_element_type=F32)


def _dot_f32(a, b):
    return jnp.dot(a, b, preferred_element_type=F32, precision=lax.Precision.HIGHEST)


def _split(a):
    hi = a.astype(BF16)
    return hi, (a - hi.astype(F32)).astype(BF16)


def _dot3(a, b):
    ah, al = _split(a)
    bh, bl = _split(b)
    return _dot(jnp.concatenate([ah, ah, al], axis=1), jnp.concatenate([bh, bl, bh], axis=0))


def _softplus(x):
    return jnp.maximum(x, 0.0) + jnp.log1p(jnp.exp(-jnp.abs(x)))


def _sigmoid(x):
    return 1.0 / (1.0 + jnp.exp(-x))


def _inproj_body(x_ref, g_ref, w_ref, wab_ref, proj_ref, ab_ref):
    x = x_ref[...]
    ms = jnp.mean(x * x, axis=-1, keepdims=True)
    h = (x * lax.rsqrt(ms + EPS) * g_ref[...]).astype(BF16)
    proj_ref[...] = _dot(h, w_ref[...])
    ab_ref[...] = _dot(h, wab_ref[...])


def _inproj(x2, g, w_main, w_ab, tm):
    t, d = x2.shape
    n = w_main.shape[1]
    return pl.pallas_call(
        _inproj_body,
        grid=(t // tm,),
        in_specs=[
            pl.BlockSpec((tm, d), lambda i: (i, 0)),
            pl.BlockSpec((1, d), lambda i: (0, 0)),
            pl.BlockSpec((d, n), lambda i: (0, 0)),
            pl.BlockSpec((d, LANES), lambda i: (0, 0)),
        ],
        out_specs=[
            pl.BlockSpec((tm, n), lambda i: (i, 0)),
            pl.BlockSpec((tm, LANES), lambda i: (i, 0)),
        ],
        out_shape=[
            jax.ShapeDtypeStruct((t, n), F32),
            jax.ShapeDtypeStruct((t, LANES), F32),
        ],
        compiler_params=_cparams(("parallel",)),
        name="inproj",
    )(x2, g, w_main, w_ab)


SB_DEAD_LOG = -104.0


def _sb_body(q_ref, k_ref, v_ref, gq_ref, gk_ref, go_ref, m2_ref, o_ref, kn_scr, vb_scr, *, tq, tk):
    i = pl.program_id(2)
    lane = lax.broadcasted_iota(jnp.int32, (1, LANES), 1)
    is0 = lane < SB_HEAD_DIM

    def headnorm(x, g):
        x2 = x * x
        s0 = jnp.sum(jnp.where(is0, x2, 0.0), axis=-1, keepdims=True)
        s1 = jnp.sum(jnp.where(is0, 0.0, x2), axis=-1, keepdims=True)
        ms = jnp.where(is0, s0, s1) * (1.0 / SB_HEAD_DIM)
        return x * lax.rsqrt(ms + EPS) * g

    @pl.when(i == 0)
    def _():
        kn_scr[...] = headnorm(k_ref[...], gk_ref[...]).astype(BF16)
        vb_scr[...] = v_ref[...].astype(BF16)

    qn = headnorm(q_ref[...], gq_ref[...]) * (SB_HEAD_DIM ** -0.5)
    qh = (jnp.where(is0, qn, 0.0).astype(BF16), jnp.where(is0, 0.0, qn).astype(BF16))
    m2 = m2_ref[...]
    row = i * tq + lax.broadcasted_iota(jnp.int32, (tq, tk), 0)
    col0 = lax.broadcasted_iota(jnp.int32, (tq, tk), 1)
    nkb = (i + 1) * (tq // tk)

    def scores(j):
        kj = kn_scr[pl.ds(pl.multiple_of(j * tk, tk), tk), :]
        return [_dot_nt(qh[h], kj) for h in range(2)]

    def logs(j, zs):
        causal = (col0 + j * tk) < row
        zls, cats = [], []
        for z in zs:
            lk = jnp.where(causal, -(jnp.maximum(z, 0.0) + jnp.log(1.0 + jnp.exp(-jnp.abs(z)))), 0.0)
            hi = lk.astype(BF16)
            lo = (lk - hi.astype(F32)).astype(BF16)
            zls.append(jnp.where(causal, z + lk, -jnp.inf))
            cats.append(jnp.concatenate([hi, lo], axis=1))
        return zls, cats

    def sums(cats):
        return [_dot(c, m2) for c in cats]

    def body(state):
        n, _, carry = state
        j = nkb - 1 - 2 * n
        zs_a = scores(j)
        zs_b = scores(j - 1)
        zl_a, cats_a = logs(j, zs_a)
        rt_a = sums(cats_a)
        zl_b, cats_b = logs(j - 1, zs_b)
        rt_b = sums(cats_b)
        v2 = vb_scr[pl.ds(pl.multiple_of((j - 1) * tk, tk), 2 * tk), :]
        out = []
        for h in range(2):
            acc, rest = carry[2 * h], carry[2 * h + 1]
            rest_mid = rest + rt_a[h][:, tk:]
            w_a = jnp.exp(zl_a[h] + (rest + rt_a[h][:, :tk])).astype(BF16)
            w_b = jnp.exp(zl_b[h] + (rest_mid + rt_b[h][:, :tk])).astype(BF16)
            out += [acc + _dot(jnp.concatenate([w_b, w_a], axis=1), v2), rest_mid + rt_b[h][:, tk:]]
        alive = (jnp.max(jnp.maximum(out[1], out[3])) > SB_DEAD_LOG).astype(jnp.int32)
        return n + 1, alive, tuple(out)

    def cond(state):
        return jnp.logical_and(state[0] < nkb // 2, state[1] > 0)

    zero = jnp.zeros((tq, LANES), F32)
    res = lax.while_loop(cond, body, (jnp.int32(0), jnp.int32(1), (zero, zero, zero, zero)))[2]
    o = jnp.where(is0, res[0], res[2])
    o_ref[...] = headnorm(o, go_ref[...])


def _sb_attention(proj3, gq, gk, go, m2, tq, tk):
    b, s, _ = proj3.shape
    hp = SB_HEADS // 2
    return pl.pallas_call(
        functools.partial(_sb_body, tq=tq, tk=tk),
        grid=(b, hp, s // tq),
        in_specs=[
            pl.BlockSpec((None, tq, LANES), lambda bi, h, i: (bi, i, h)),
            pl.BlockSpec((None, s, LANES), lambda bi, h, i: (bi, 0, hp + h)),
            pl.BlockSpec((None, s, LANES), lambda bi, h, i: (bi, 0, 2 * hp + h)),
            pl.BlockSpec((1, LANES), lambda bi, h, i: (0, 0)),
            pl.BlockSpec((1, LANES), lambda bi, h, i: (0, 0)),
            pl.BlockSpec((1, LANES), lambda bi, h, i: (0, 0)),
            pl.BlockSpec((2 * tk, 2 * tk), lambda bi, h, i: (0, 0)),
        ],
        out_specs=pl.BlockSpec((None, tq, LANES), lambda bi, h, i: (bi, i, h)),
        out_shape=jax.ShapeDtypeStruct((b, s, SB_WIDTH), F32),
        scratch_shapes=[pltpu.VMEM((s, LANES), BF16), pltpu.VMEM((s, LANES), BF16)],
        compiler_params=_cparams(("parallel", "parallel", "arbitrary")),
        name="sb_attention",
    )(proj3, proj3, proj3, gq, gk, go, m2)


GDN_BLOCK = 128


GDN_HALO = 8

def _gdn_body(x_ref, halo_ref, z_ref, ab_ref, cw_ref, alog_ref, dtb_ref, og_ref, o_ref,
              q_scr, k_scr, v_scr, g_scr, beta_scr, state_scr, *, ts):
    t = pl.program_id(1)
    c = GDN_BLOCK
    nh = GDN_HEADS
    lane = lax.broadcasted_iota(jnp.int32, (1, LANES), 1)

    @pl.when(t == 0)
    def _():
        state_scr[...] = jnp.zeros_like(state_scr)

    halo = jnp.where(t > 0, halo_ref[...], 0.0)
    xe = jnp.concatenate([halo, x_ref[...]], axis=0)
    cw = cw_ref[...]
    y = xe * cw[3:4, :]
    for d in (1, 2, 3):
        y = y + pltpu.roll(xe, d, 0) * cw[3 - d:4 - d, :]
    y = y[GDN_HALO:, :]
    y = y * _sigmoid(y)

    def l2n(x):
        return x * lax.rsqrt(jnp.sum(x * x, axis=-1, keepdims=True) + EPS)

    ab = ab_ref[...]
    for h in range(nh):
        sl = pl.ds(h * LANES, LANES)
        q_scr[:, sl] = l2n(y[:, h * LANES:(h + 1) * LANES]) * (GDN_HEAD_DIM ** -0.5)
        k_scr[:, sl] = l2n(y[:, GDN_WIDTH + h * LANES:GDN_WIDTH + (h + 1) * LANES])
        a_col = jnp.sum(jnp.where(lane == h, ab, 0.0), axis=-1, keepdims=True)
        b_col = jnp.sum(jnp.where(lane == h + nh, ab, 0.0), axis=-1, keepdims=True)
        g_scr[:, sl] = -jnp.exp(alog_ref[h]) * _softplus(a_col + dtb_ref[h])
        beta_scr[:, sl] = jnp.broadcast_to(_sigmoid(b_col), (ts, LANES))
    v_scr[...] = y[:, 2 * GDN_WIDTH:]

    ri = lax.broadcasted_iota(jnp.int32, (c, nh * c), 0)
    ci = lax.broadcasted_iota(jnp.int32, (c, nh * c), 1) % c
    lower_incl = ci <= ri
    lower_strict = ci < ri
    eye = (ci == ri).astype(F32)
    ltri = lower_incl[:, :c].astype(F32)
    ones = jnp.ones((c, c), F32)
    og = og_ref[...]
    heads = range(nh)
    hs = lambda m, h: m[:, h * c:(h + 1) * c]

    def blk(n, states):
        r0 = pl.multiple_of(n * c, c)
        q = q_scr[pl.ds(r0, c), :]
        k = k_scr[pl.ds(r0, c), :]
        v = v_scr[pl.ds(r0, c), :]
        g = g_scr[pl.ds(r0, c), :]
        beta = beta_scr[pl.ds(r0, c), :]
        gc = _dot_f32(ltri, g)
        gc_row = _dot_f32(ones, gc * eye)
        decay = jnp.exp(jnp.where(lower_incl, gc - gc_row, -jnp.inf))
        kb = k * beta
        kbf = k.astype(BF16)
        kbb = kb.astype(BF16)
        qbf = q.astype(BF16)
        kk = jnp.concatenate([_dot_nt(hs(kbb, h), hs(kbf, h)) for h in heads], axis=1)
        qk = jnp.concatenate([_dot_nt(hs(qbf, h), hs(kbf, h)) for h in heads], axis=1)
        a = jnp.where(lower_strict, kk * decay, 0.0)
        attn = jnp.where(lower_incl, qk * decay, 0.0).astype(BF16)
        nmat = [-hs(a, h) for h in heads]
        xp = [hs(a, h) for h in heads]
        for _ in range(6):
            xp = [_dot3(x, x) for x in xp]
            prod = [_dot3(nm, x) for nm, x in zip(nmat, xp)]
            nmat = [nm + x + p for nm, x, p in zip(nmat, xp, prod)]
        eg = jnp.exp(gc)
        vb = v * beta
        kbd = kb * eg
        gl = gc[c - 1:c, :]
        qd = (q * eg).astype(BF16)
        kd = (k * jnp.exp(gl - gc)).astype(BF16)
        dl = jnp.exp(gl)
        nb = [nm.astype(BF16) for nm in nmat]
        u = [hs(vb, h) + _dot(nb[h], hs(vb, h).astype(BF16)) for h in heads]
        w = [(hs(kbd, h) + _dot(nb[h], hs(kbd, h).astype(BF16))).astype(BF16) for h in heads]
        sb = [st.astype(BF16) for st in states]
        v_new = [u[h] - _dot(w[h], sb[h]) for h in heads]
        o_state = [_dot(hs(qd, h), sb[h]) for h in heads]
        vnb = [vn.astype(BF16) for vn in v_new]
        o = [o_state[h] + _dot(hs(attn, h), vnb[h]) for h in heads]
        new_states = tuple(states[h] * hs(dl, h) + _dot_tn(hs(kd, h), vnb[h]) for h in heads)
        on = jnp.concatenate(
            [x * lax.rsqrt(jnp.mean(x * x, axis=-1, keepdims=True) + EPS) * og for x in o], axis=1)
        z = z_ref[pl.ds(r0, c), :]
        o_ref[pl.ds(r0, c), :] = on * (z * _sigmoid(z))
        return new_states

    states = lax.fori_loop(0, ts // c, blk, tuple(state_scr[h] for h in heads))
    for h in heads:
        state_scr[h] = states[h]


def _gdn(proj3, ab3, conv_w, alog_b, dtb_b, og, ts):
    b, s, _ = proj3.shape
    w3 = 3 * GDN_WIDTH
    assert 3 * SB_WIDTH == w3 and 2 * w3 % GDN_WIDTH == 0
    per = ts // GDN_HALO
    par = pl.BlockSpec((GDN_HEADS, 1, LANES), lambda bi, t: (0, 0, 0))
    return pl.pallas_call(
        functools.partial(_gdn_body, ts=ts),
        grid=(b, s // ts),
        in_specs=[
            pl.BlockSpec((None, ts, w3), lambda bi, t: (bi, t, 1)),
            pl.BlockSpec((None, GDN_HALO, w3), lambda bi, t: (bi, jnp.maximum(t * per - 1, 0), 1)),
            pl.BlockSpec((None, ts, GDN_WIDTH), lambda bi, t: (bi, t, 2 * w3 // GDN_WIDTH)),
            pl.BlockSpec((None, ts, LANES), lambda bi, t: (bi, t, 0)),
            pl.BlockSpec((GDN_CONV, w3), lambda bi, t: (0, 0)),
            par, par,
            pl.BlockSpec((1, LANES), lambda bi, t: (0, 0)),
        ],
        out_specs=pl.BlockSpec((None, ts, GDN_WIDTH), lambda bi, t: (bi, t, 0)),
        out_shape=jax.ShapeDtypeStruct((b, s, GDN_WIDTH), F32),
        scratch_shapes=[pltpu.VMEM((ts, GDN_WIDTH), F32) for _ in range(5)]
        + [pltpu.VMEM((GDN_HEADS, GDN_HEAD_DIM, GDN_HEAD_DIM), F32)],
        compiler_params=_cparams(("parallel", "arbitrary")),
        name="gdn",
    )(proj3, proj3, proj3, ab3, conv_w, alog_b, dtb_b, og)


def _outproj_body(sb_ref, gd_ref, x_ref, wo1_ref, wo2_ref, g2_ref, wqt_ref, x1_ref, hnt_ref, pqt_ref):
    mix = _dot(sb_ref[...].astype(BF16), wo1_ref[...]) + _dot(gd_ref[...].astype(BF16), wo2_ref[...])
    x1 = x_ref[...] + mix
    x1_ref[...] = x1
    ms = jnp.mean(x1 * x1, axis=-1, keepdims=True)
    hn = x1 * lax.rsqrt(ms + EPS) * g2_ref[...]
    hnt = hn.T.astype(BF16)
    hnt_ref[...] = hnt
    pqt_ref[...] = _dot(wqt_ref[...], hnt)


def _outproj(sb_o, gd_o, x2, wo1, wo2, g2, wqt, tm):
    t, d = x2.shape
    nq = wqt.shape[0]
    return pl.pallas_call(
        _outproj_body,
        grid=(t // tm,),
        in_specs=[
            pl.BlockSpec((tm, SB_WIDTH), lambda i: (i, 0)),
            pl.BlockSpec((tm, GDN_WIDTH), lambda i: (i, 0)),
            pl.BlockSpec((tm, d), lambda i: (i, 0)),
            pl.BlockSpec((SB_WIDTH, d), lambda i: (0, 0)),
            pl.BlockSpec((GDN_WIDTH, d), lambda i: (0, 0)),
            pl.BlockSpec((1, d), lambda i: (0, 0)),
            pl.BlockSpec((nq, d), lambda i: (0, 0)),
        ],
        out_specs=[
            pl.BlockSpec((tm, d), lambda i: (i, 0)),
            pl.BlockSpec((d, tm), lambda i: (0, i)),
            pl.BlockSpec((nq, tm), lambda i: (0, i)),
        ],
        out_shape=[
            jax.ShapeDtypeStruct((t, d), F32),
            jax.ShapeDtypeStruct((d, t), BF16),
            jax.ShapeDtypeStruct((nq, t), F32),
        ],
        compiler_params=_cparams(("parallel",)),
        name="outproj",
    )(sb_o, gd_o, x2, wo1, wo2, g2, wqt)


def _extract_topk(s, ids, k):
    big = jnp.int32(2 ** 30)
    work = s
    vals = []
    rank = jnp.full(s.shape, float(k), F32)
    for r in range(k):
        m = jnp.max(work, axis=0, keepdims=True)
        first = jnp.min(jnp.where(work == m, ids, big), axis=0, keepdims=True)
        hit = ids == first
        rank = jnp.where(hit, float(r), rank)
        work = jnp.where(hit, -jnp.inf, work)
        vals.append(m)
    return vals, rank, None


def _extract_topk_untied(s, ids, k):
    del ids
    work = s
    vals = []
    rank = jnp.full(s.shape, float(k), F32)
    for r in range(k):
        m = jnp.max(work, axis=0, keepdims=True)
        hit = work == m
        rank = jnp.where(hit, float(r), rank)
        work = jnp.where(hit, -jnp.inf, work)
        vals.append(m)
    taken = jnp.sum((rank < float(k)).astype(F32), axis=0, keepdims=True)
    return vals, rank, taken == float(k)


def _route_body(pqt_ref, k1_ref, k2_ref, rank2_ref, cnt_ref, e1_ref, e2z_ref):
    kk = PEER_TOPK
    tn = pqt_ref.shape[1]
    key_ids = lax.broadcasted_iota(jnp.int32, (PEER_NKEYS, tn), 0)
    i8 = lax.broadcasted_iota(jnp.int32, (8, tn), 0)
    i16 = lax.broadcasted_iota(jnp.int32, (kk, tn), 0)
    cand_ids = jnp.concatenate(
        [i16] + [i8 + a * kk for a in (1, 2, 3)]
        + [jnp.where(i8 >= 4, i8 * kk + b, kk * kk + i8 * kk + b) for b in (0, 1, 2)] + [(i8 + 8) * kk], axis=0)
    def route_head(h, s1, s2, extract):
        v1, rank1, ok1 = extract(s1, key_ids, kk)
        v2, rank2, ok2 = extract(s2, key_ids, kk)
        v1a = jnp.concatenate(v1, axis=0)
        v2a = jnp.concatenate(v2, axis=0)
        cand = jnp.concatenate(
            [v1[0] + v2a] + [v1[a] + v2a[:8] for a in (1, 2, 3)]
            + [jnp.where(i8 >= 4, v1a[:8] + v2[b], -jnp.inf) for b in (0, 1, 2)] + [v1a[8:] + v2[0]], axis=0)
        top, crank, ok3 = extract(cand, cand_ids, kk)
        zsum = jnp.ones_like(top[0])
        for r in range(1, kk):
            zsum = zsum + jnp.exp(top[r] - top[0])
        sel = (crank < float(kk)).astype(F32)
        low = [jnp.sum(sel[0:16], axis=0, keepdims=True)] + [
            jnp.sum(sel[8 + 8 * a:16 + 8 * a], axis=0, keepdims=True) for a in (1, 2, 3)]
        mid = sel[40:48] + sel[48:56] + sel[56:64]
        high = sel[64:72]
        cnt = jnp.zeros(s1.shape, F32)
        for a in range(kk):
            n_a = low[a] if a < 4 else (mid[a:a + 1] if a < 8 else high[a - 8:a - 7])
            cnt = jnp.where(rank1 == float(a), n_a, cnt)
        rank2_ref[h] = rank2.astype(BF16)
        cnt_ref[h] = cnt
        e1_ref[h] = jnp.exp(s1 - v1[0])
        e2z_ref[h] = (jnp.exp(s2 - v2[0]) * (0.5 / zsum)).astype(BF16)
        return None if ok1 is None else jnp.logical_and(jnp.logical_and(ok1, ok2), ok3)

    for h in range(PEER_HEADS):
        q1 = pqt_ref[pl.ds(h * 2 * PEER_HALF, PEER_HALF), :].astype(BF16)
        q2 = pqt_ref[pl.ds(h * 2 * PEER_HALF + PEER_HALF, PEER_HALF), :].astype(BF16)
        s1 = _dot(k1_ref[h], q1)
        s2 = _dot(k2_ref[h], q2)
        ok = route_head(h, s1, s2, _extract_topk_untied)
        tied = jnp.max(jnp.where(ok, 0.0, 1.0)) > 0.0

        @pl.when(tied)
        def _():
            route_head(h, s1, s2, _extract_topk)


def _route(pqt, k1, k2, tn):
    nq, t = pqt.shape
    hk = (PEER_HEADS, PEER_NKEYS, PEER_HALF)
    out = lambda dt: jax.ShapeDtypeStruct((PEER_HEADS, PEER_NKEYS, t), dt)
    ospec = pl.BlockSpec((PEER_HEADS, PEER_NKEYS, tn), lambda i: (0, 0, i))
    return pl.pallas_call(
        _route_body,
        grid=(t // tn,),
        in_specs=[
            pl.BlockSpec((nq, tn), lambda i: (0, i)),
            pl.BlockSpec(hk, lambda i: (0, 0, 0)),
            pl.BlockSpec(hk, lambda i: (0, 0, 0)),
        ],
        out_specs=[ospec, ospec, ospec, ospec],
        out_shape=[out(BF16), out(F32), out(F32), out(BF16)],
        compiler_params=_cparams(("parallel",)),
        name="peer_route",
    )(pqt, k1, k2)


PEER_I1_PER_BLOCK = 8
PEER_I1_PER_CHUNK = 2
BF16_ROWS = 16


def _expert_body(hnt_ref, u_first_ref, u_b_ref, u_next_ref, vt_prev_ref, vt_a_ref, vt_last_ref, rank2_ref,
                 cnt_ref, e1_ref, e2z_ref, x1_ref, o_ref, acc_scr, act_next, p_prev):
    s = pl.program_id(1)
    hnt = hnt_ref[...]
    tn = hnt.shape[1]
    ce = PEER_I1_PER_CHUNK * PEER_NKEYS
    nchunk = PEER_I1_PER_BLOCK // PEER_I1_PER_CHUNK
    tiles = PEER_NKEYS // BF16_ROWS

    def activation(u_ref):
        for c in range(nchunk):
            pre = _dot(u_ref[pl.ds(c * ce, ce), :], hnt)
            yield (pre * (1.0 + lax.erf(pre * (2.0 ** -0.5)))).astype(BF16)

    def row(ref, h, l):
        return jnp.broadcast_to(ref[h, l:l + 1, :], (BF16_ROWS, tn)).astype(BF16)[None]

    def gates(l0, act_rows):
        for li in range(PEER_I1_PER_BLOCK):
            gate = None
            for h in range(PEER_HEADS):
                rank2 = rank2_ref[h].reshape(tiles, BF16_ROWS, tn)
                e2z = e2z_ref[h].reshape(tiles, BF16_ROWS, tn)
                term = jnp.where(rank2 < row(cnt_ref, h, l0 + li), e2z * row(e1_ref, h, l0 + li), 0.0)
                gate = term if gate is None else gate + term
            yield gate.reshape(PEER_NKEYS, tn) * act_rows(li)

    @pl.when(s == 0)
    def _():
        acc_scr[...] = jnp.zeros_like(acc_scr)
        p_prev[...] = jnp.zeros_like(p_prev)
        for c, act in enumerate(activation(u_first_ref)):
            act_next[pl.ds(c * ce, ce), :] = act

    out_prev = _dot(vt_prev_ref[...], p_prev[...])
    act_b = jnp.concatenate(list(activation(u_b_ref)), axis=0)
    p_a = jnp.concatenate(
        list(gates(0, lambda li: act_next[pl.ds(li * PEER_NKEYS, PEER_NKEYS), :])), axis=0)
    out_a = _dot(vt_a_ref[...], p_a)
    for c, act in enumerate(activation(u_next_ref)):
        act_next[pl.ds(c * ce, ce), :] = act
    for li, p in enumerate(gates(PEER_I1_PER_BLOCK, lambda li: act_b[li * PEER_NKEYS:(li + 1) * PEER_NKEYS, :])):
        p_prev[pl.ds(li * PEER_NKEYS, PEER_NKEYS), :] = p
    acc_scr[...] += out_prev + out_a

    @pl.when(s == pl.num_programs(1) - 1)
    def _():
        o_ref[...] = x1_ref[...] + (acc_scr[...] + _dot(vt_last_ref[...], p_prev[...])).T


def _experts(hnt, u_b, vt_b, rank2, cnt, e1, e2z, x1, tn):
    d, t = hnt.shape
    ne = u_b.shape[0]
    eb = PEER_I1_PER_BLOCK * PEER_NKEYS
    nblk = ne // eb
    once = pl.Buffered(1)
    full = pl.BlockSpec((PEER_HEADS, PEER_NKEYS, tn), lambda i, s: (0, 0, i))
    part = pl.BlockSpec((PEER_HEADS, 2 * PEER_I1_PER_BLOCK, tn), lambda i, s: (0, s, i))
    return pl.pallas_call(
        _expert_body,
        grid=(t // tn, nblk // 2),
        in_specs=[
            pl.BlockSpec((d, tn), lambda i, s: (0, i)),
            pl.BlockSpec((eb, d), lambda i, s: (0, 0), pipeline_mode=once),
            pl.BlockSpec((eb, d), lambda i, s: (2 * s + 1, 0)),
            pl.BlockSpec((eb, d), lambda i, s: (jnp.minimum(2 * s + 2, nblk - 1), 0)),
            pl.BlockSpec((d, eb), lambda i, s: (0, jnp.maximum(2 * s - 1, 0))),
            pl.BlockSpec((d, eb), lambda i, s: (0, 2 * s)),
            pl.BlockSpec((d, eb), lambda i, s: (0, nblk - 1), pipeline_mode=once),
            full, part, part, full,
            pl.BlockSpec((tn, d), lambda i, s: (i, 0), pipeline_mode=once),
        ],
        out_specs=pl.BlockSpec((tn, d), lambda i, s: (i, 0)),
        out_shape=jax.ShapeDtypeStruct((t, d), F32),
        scratch_shapes=[pltpu.VMEM((d, tn), F32), pltpu.VMEM((eb, tn), BF16), pltpu.VMEM((eb, tn), BF16)],
        compiler_params=_cparams(("parallel", "arbitrary")),
        name="peer_experts",
    )(hnt, u_b, u_b, u_b, vt_b, vt_b, vt_b, rank2, cnt, e1, e2z, x1)


def _suffix_sum_matrix(tk):
    r = jnp.arange(2 * tk)[:, None] % tk
    c = jnp.arange(2 * tk)[None, :]
    return jnp.where(c < tk, r > c, True).astype(BF16)


def _layer(x, norm1_g, w_in, sb_q_g, sb_k_g, sb_o_g, conv_w, a_log, dt_bias, gdn_o_g, w_out, norm2_g,
           w_q, keys1, keys2, u_tab, v_tab):
    b, s, d = x.shape
    t = b * s
    n_main = 3 * SB_WIDTH + 4 * GDN_WIDTH
    x2 = x.reshape(t, d)
    w_main = w_in[:, :n_main].astype(BF16)
    w_ab = jnp.pad(w_in[:, n_main:], ((0, 0), (0, LANES - 2 * GDN_HEADS))).astype(BF16)
    tm = min(256, t)
    proj, ab = _inproj(x2, norm1_g.reshape(1, d), w_main, w_ab, tm)
    proj3 = proj.reshape(b, s, n_main)
    ab3 = ab.reshape(b, s, LANES)

    tq, tk = min(256, s), 128
    tile2 = lambda g: jnp.tile(g, 2).reshape(1, LANES)
    sb_o = _sb_attention(proj3, tile2(sb_q_g), tile2(sb_k_g), tile2(sb_o_g), _suffix_sum_matrix(tk), tq, tk)

    bcast = lambda p: jnp.broadcast_to(p[:, None, None], (GDN_HEADS, 1, LANES))
    gd_o = _gdn(proj3, ab3, conv_w, bcast(a_log), bcast(dt_bias), gdn_o_g.reshape(1, LANES), min(512, s))

    wo = w_out.astype(BF16)
    x1, hnt, pqt = _outproj(sb_o.reshape(t, SB_WIDTH), gd_o.reshape(t, GDN_WIDTH), x2, wo[:SB_WIDTH],
                            wo[SB_WIDTH:], norm2_g.reshape(1, d), w_q.T.astype(BF16), tm)

    rank2, cnt, e1, e2z = _route(pqt, keys1.astype(BF16), keys2.astype(BF16), min(256, t))
    y = _experts(hnt, u_tab.astype(BF16), v_tab.T.astype(BF16), rank2, cnt, e1, e2z, x1, min(512, t))
    return y.reshape(b, s, d)


def kernel(x, norm1_g, w_in, sb_q_norm_g, sb_k_norm_g, sb_out_norm_g, gdn_conv_w, gdn_a_log, gdn_dt_bias,
           gdn_out_norm_g, w_out, norm2_g, peer_w_q, peer_keys1, peer_keys2, peer_u, peer_v):
    for layer in range(norm1_g.shape[0]):
        x = _layer(x, norm1_g[layer], w_in[layer], sb_q_norm_g[layer], sb_k_norm_g[layer],
                   sb_out_norm_g[layer], gdn_conv_w[layer], gdn_a_log[layer], gdn_dt_bias[layer],
                   gdn_out_norm_g[layer], w_out[layer], norm2_g[layer], peer_w_q[layer],
                   peer_keys1[layer], peer_keys2[layer], peer_u[layer], peer_v[layer])
    return x
```

```python
import functools

import jax
import jax.numpy as jnp
from jax import lax
from jax.experimental import pallas as pl
from jax.experimental.pallas import tpu as pltpu

F32 = jnp.float32
BF16 = jnp.bfloat16
EPS = 1e-6

SB_HEADS = 8
SB_HEAD_DIM = 64
SB_WIDTH = SB_HEADS * SB_HEAD_DIM
GDN_HEADS = 4
GDN_HEAD_DIM = 128
GDN_WIDTH = GDN_HEADS * GDN_HEAD_DIM
GDN_CONV = 4
PEER_HEADS = 8
PEER_NKEYS = 128
PEER_HALF = 128
PEER_TOPK = 16
LANES = 128

VMEM_LIMIT = 56 * 1024 * 1024


def _cparams(sem):
    return pltpu.CompilerParams(dimension_semantics=sem, vmem_limit_bytes=VMEM_LIMIT)


def _dot(a, b):
    return jnp.dot(a, b, preferred_element_type=F32)


def _dot_nt(a, b):
    return lax.dot_general(a, b, (((1,), (1,)), ((), ())), preferred_element_type=F32)


def _dot_tn(a, b):
    return lax.dot_general(a, b, (((0,), (0,)), ((), ())), preferred_element_type=F32)


def _dot_f32(a, b):
    return jnp.dot(a, b, preferred_element_type=F32, precision=lax.Precision.HIGHEST)


def _split(a):
    hi = a.astype(BF16)
    return hi, (a - hi.astype(F32)).astype(BF16)


def _dot3(a, b):
    ah, al = _split(a)
    bh, bl = _split(b)
    return _dot(jnp.concatenate([ah, ah, al], axis=1), jnp.concatenate([bh, bl, bh], axis=0))


def _softplus(x):
    return jnp.maximum(x, 0.0) + jnp.log1p(jnp.exp(-jnp.abs(x)))


def _sigmoid(x):
    return 1.0 / (1.0 + jnp.exp(-x))


def _inproj_body(x_ref, g_ref, w_ref, wab_ref, proj_ref, ab_ref):
    x = x_ref[...]
    ms = jnp.mean(x * x, axis=-1, keepdims=True)
    h = (x * lax.rsqrt(ms + EPS) * g_ref[...]).astype(BF16)
    proj_ref[...] = _dot(h, w_ref[...])
    ab_ref[...] = _dot(h, wab_ref[...])


def _inproj(x2, g, w_main, w_ab, tm):
    t, d = x2.shape
    n = w_main.shape[1]
    return pl.pallas_call(
        _inproj_body,
        grid=(t // tm,),
        in_specs=[
            pl.BlockSpec((tm, d), lambda i: (i, 0)),
            pl.BlockSpec((1, d), lambda i: (0, 0)),
            pl.BlockSpec((d, n), lambda i: (0, 0)),
            pl.BlockSpec((d, LANES), lambda i: (0, 0)),
        ],
        out_specs=[
            pl.BlockSpec((tm, n), lambda i: (i, 0)),
            pl.BlockSpec((tm, LANES), lambda i: (i, 0)),
        ],
        out_shape=[
            jax.ShapeDtypeStruct((t, n), F32),
            jax.ShapeDtypeStruct((t, LANES), F32),
        ],
        compiler_params=_cparams(("parallel",)),
        name="inproj",
    )(x2, g, w_main, w_ab)


SB_DEAD_LOG = -104.0


def _sb_body(q_ref, k_ref, v_ref, gq_ref, gk_ref, go_ref, m2_ref, o_ref, kn_scr, vb_scr, *, tq, tk):
    i = pl.program_id(2)
    lane = lax.broadcasted_iota(jnp.int32, (1, LANES), 1)
    is0 = lane < SB_HEAD_DIM

    def headnorm(x, g):
        x2 = x * x
        s0 = jnp.sum(jnp.where(is0, x2, 0.0), axis=-1, keepdims=True)
        s1 = jnp.sum(jnp.where(is0, 0.0, x2), axis=-1, keepdims=True)
        ms = jnp.where(is0, s0, s1) * (1.0 / SB_HEAD_DIM)
        return x * lax.rsqrt(ms + EPS) * g

    @pl.when(i == 0)
    def _():
        kn_scr[...] = headnorm(k_ref[...], gk_ref[...]).astype(BF16)
        vb_scr[...] = v_ref[...].astype(BF16)

    qn = headnorm(q_ref[...], gq_ref[...]) * (SB_HEAD_DIM ** -0.5)
    qh = (jnp.where(is0, qn, 0.0).astype(BF16), jnp.where(is0, 0.0, qn).astype(BF16))
    m2 = m2_ref[...]
    row = i * tq + lax.broadcasted_iota(jnp.int32, (tq, tk), 0)
    col0 = lax.broadcasted_iota(jnp.int32, (tq, tk), 1)
    nkb = (i + 1) * (tq // tk)

    def scores(j):
        kj = kn_scr[pl.ds(pl.multiple_of(j * tk, tk), tk), :]
        return [_dot_nt(qh[h], kj) for h in range(2)]

    def logs(j, zs):
        causal = (col0 + j * tk) < row
        zls, cats = [], []
        for z in zs:
            lk = jnp.where(causal, -(jnp.maximum(z, 0.0) + jnp.log(1.0 + jnp.exp(-jnp.abs(z)))), 0.0)
            hi = lk.astype(BF16)
            lo = (lk - hi.astype(F32)).astype(BF16)
            zls.append(jnp.where(causal, z + lk, -jnp.inf))
            cats.append(jnp.concatenate([hi, lo], axis=1))
        return zls, cats

    def sums(cats):
        return [_dot(c, m2) for c in cats]

    def body(state):
        n, _, carry = state
        j = nkb - 1 - 2 * n
        zs_a = scores(j)
        zs_b = scores(j - 1)
        zl_a, cats_a = logs(j, zs_a)
        rt_a = sums(cats_a)
        zl_b, cats_b = logs(j - 1, zs_b)
        rt_b = sums(cats_b)
        v2 = vb_scr[pl.ds(pl.multiple_of((j - 1) * tk, tk), 2 * tk), :]
        out = []
        for h in range(2):
            acc, rest = carry[2 * h], carry[2 * h + 1]
            rest_mid = rest + rt_a[h][:, tk:]
            w_a = jnp.exp(zl_a[h] + (rest + rt_a[h][:, :tk])).astype(BF16)
            w_b = jnp.exp(zl_b[h] + (rest_mid + rt_b[h][:, :tk])).astype(BF16)
            out += [acc + _dot(jnp.concatenate([w_b, w_a], axis=1), v2), rest_mid + rt_b[h][:, tk:]]
        alive = (jnp.max(jnp.maximum(out[1], out[3])) > SB_DEAD_LOG).astype(jnp.int32)
        return n + 1, alive, tuple(out)

    def cond(state):
        return jnp.logical_and(state[0] < nkb // 2, state[1] > 0)

    zero = jnp.zeros((tq, LANES), F32)
    res = lax.while_loop(cond, body, (jnp.int32(0), jnp.int32(1), (zero, zero, zero, zero)))[2]
    o = jnp.where(is0, res[0], res[2])
    o_ref[...] = headnorm(o, go_ref[...])


def _sb_attention(proj3, gq, gk, go, m2, tq, tk):
    b, s, _ = proj3.shape
    hp = SB_HEADS // 2
    return pl.pallas_call(
        functools.partial(_sb_body, tq=tq, tk=tk),
        grid=(b, hp, s // tq),
        in_specs=[
            pl.BlockSpec((None, tq, LANES), lambda bi, h, i: (bi, i, h)),
            pl.BlockSpec((None, s, LANES), lambda bi, h, i: (bi, 0, hp + h)),
            pl.BlockSpec((None, s, LANES), lambda bi, h, i: (bi, 0, 2 * hp + h)),
            pl.BlockSpec((1, LANES), lambda bi, h, i: (0, 0)),
            pl.BlockSpec((1, LANES), lambda bi, h, i: (0, 0)),
            pl.BlockSpec((1, LANES), lambda bi, h, i: (0, 0)),
            pl.BlockSpec((2 * tk, 2 * tk), lambda bi, h, i: (0, 0)),
        ],
        out_specs=pl.BlockSpec((None, tq, LANES), lambda bi, h, i: (bi, i, h)),
        out_shape=jax.ShapeDtypeStruct((b, s, SB_WIDTH), F32),
        scratch_shapes=[pltpu.VMEM((s, LANES), BF16), pltpu.VMEM((s, LANES), BF16)],
        compiler_params=_cparams(("parallel", "parallel", "arbitrary")),
        name="sb_attention",
    )(proj3, proj3, proj3, gq, gk, go, m2)


GDN_BLOCK = 128


GDN_HALO = 8

def _gdn_body(x_ref, halo_ref, z_ref, ab_ref, cw_ref, alog_ref, dtb_ref, og_ref, o_ref,
              q_scr, k_scr, v_scr, g_scr, beta_scr, state_scr, *, ts):
    t = pl.program_id(1)
    c = GDN_BLOCK
    nh = GDN_HEADS
    lane = lax.broadcasted_iota(jnp.int32, (1, LANES), 1)

    @pl.when(t == 0)
    def _():
        state_scr[...] = jnp.zeros_like(state_scr)

    halo = jnp.where(t > 0, halo_ref[...], 0.0)
    xe = jnp.concatenate([halo, x_ref[...]], axis=0)
    cw = cw_ref[...]
    y = xe * cw[3:4, :]
    for d in (1, 2, 3):
        y = y + pltpu.roll(xe, d, 0) * cw[3 - d:4 - d, :]
    y = y[GDN_HALO:, :]
    y = y * _sigmoid(y)

    def l2n(x):
        return x * lax.rsqrt(jnp.sum(x * x, axis=-1, keepdims=True) + EPS)

    ab = ab_ref[...]
    for h in range(nh):
        sl = pl.ds(h * LANES, LANES)
        q_scr[:, sl] = l2n(y[:, h * LANES:(h + 1) * LANES]) * (GDN_HEAD_DIM ** -0.5)
        k_scr[:, sl] = l2n(y[:, GDN_WIDTH + h * LANES:GDN_WIDTH + (h + 1) * LANES])
        a_col = jnp.sum(jnp.where(lane == h, ab, 0.0), axis=-1, keepdims=True)
        b_col = jnp.sum(jnp.where(lane == h + nh, ab, 0.0), axis=-1, keepdims=True)
        g_scr[:, sl] = -jnp.exp(alog_ref[h]) * _softplus(a_col + dtb_ref[h])
        beta_scr[:, sl] = jnp.broadcast_to(_sigmoid(b_col), (ts, LANES))
    v_scr[...] = y[:, 2 * GDN_WIDTH:]

    ri = lax.broadcasted_iota(jnp.int32, (c, nh * c), 0)
    ci = lax.broadcasted_iota(jnp.int32, (c, nh * c), 1) % c
    lower_incl = ci <= ri
    lower_strict = ci < ri
    eye = (ci == ri).astype(F32)
    ltri = lower_incl[:, :c].astype(F32)
    ones = jnp.ones((c, c), F32)
    og = og_ref[...]
    heads = range(nh)
    hs = lambda m, h: m[:, h * c:(h + 1) * c]

    def blk(n, states):
        r0 = pl.multiple_of(n * c, c)
        q = q_scr[pl.ds(r0, c), :]
        k = k_scr[pl.ds(r0, c), :]
        v = v_scr[pl.ds(r0, c), :]
        g = g_scr[pl.ds(r0, c), :]
        beta = beta_scr[pl.ds(r0, c), :]
        gc = _dot_f32(ltri, g)
        gc_row = _dot_f32(ones, gc * eye)
        decay = jnp.exp(jnp.where(lower_incl, gc - gc_row, -jnp.inf))
        kb = k * beta
        kbf = k.astype(BF16)
        kbb = kb.astype(BF16)
        qbf = q.astype(BF16)
        kk = jnp.concatenate([_dot_nt(hs(kbb, h), hs(kbf, h)) for h in heads], axis=1)
        qk = jnp.concatenate([_dot_nt(hs(qbf, h), hs(kbf, h)) for h in heads], axis=1)
        a = jnp.where(lower_strict, kk * decay, 0.0)
        attn = jnp.where(lower_incl, qk * decay, 0.0).astype(BF16)
        nmat = [-hs(a, h) for h in heads]
        xp = [hs(a, h) for h in heads]
        for _ in range(6):
            xp = [_dot3(x, x) for x in xp]
            prod = [_dot3(nm, x) for nm, x in zip(nmat, xp)]
            nmat = [nm + x + p for nm, x, p in zip(nmat, xp, prod)]
        eg = jnp.exp(gc)
        vb = v * beta
        kbd = kb * eg
        gl = gc[c - 1:c, :]
        qd = (q * eg).astype(BF16)
        kd = (k * jnp.exp(gl - gc)).astype(BF16)
        dl = jnp.exp(gl)
        nb = [nm.astype(BF16) for nm in nmat]
        u = [hs(vb, h) + _dot(nb[h], hs(vb, h).astype(BF16)) for h in heads]
        w = [(hs(kbd, h) + _dot(nb[h], hs(kbd, h).astype(BF16))).astype(BF16) for h in heads]
        sb = [st.astype(BF16) for st in states]
        v_new = [u[h] - _dot(w[h], sb[h]) for h in heads]
        o_state = [_dot(hs(qd, h), sb[h]) for h in heads]
        vnb = [vn.astype(BF16) for vn in v_new]
        o = [o_state[h] + _dot(hs(attn, h), vnb[h]) for h in heads]
        new_states = tuple(states[h] * hs(dl, h) + _dot_tn(hs(kd, h), vnb[h]) for h in heads)
        on = jnp.concatenate(
            [x * lax.rsqrt(jnp.mean(x * x, axis=-1, keepdims=True) + EPS) * og for x in o], axis=1)
        z = z_ref[pl.ds(r0, c), :]
        o_ref[pl.ds(r0, c), :] = on * (z * _sigmoid(z))
        return new_states

    states = lax.fori_loop(0, ts // c, blk, tuple(state_scr[h] for h in heads))
    for h in heads:
        state_scr[h] = states[h]


def _gdn(proj3, ab3, conv_w, alog_b, dtb_b, og, ts):
    b, s, _ = proj3.shape
    w3 = 3 * GDN_WIDTH
    assert 3 * SB_WIDTH == w3 and 2 * w3 % GDN_WIDTH == 0
    per = ts // GDN_HALO
    par = pl.BlockSpec((GDN_HEADS, 1, LANES), lambda bi, t: (0, 0, 0))
    return pl.pallas_call(
        functools.partial(_gdn_body, ts=ts),
        grid=(b, s // ts),
        in_specs=[
            pl.BlockSpec((None, ts, w3), lambda bi, t: (bi, t, 1)),
            pl.BlockSpec((None, GDN_HALO, w3), lambda bi, t: (bi, jnp.maximum(t * per - 1, 0), 1)),
            pl.BlockSpec((None, ts, GDN_WIDTH), lambda bi, t: (bi, t, 2 * w3 // GDN_WIDTH)),
            pl.BlockSpec((None, ts, LANES), lambda bi, t: (bi, t, 0)),
            pl.BlockSpec((GDN_CONV, w3), lambda bi, t: (0, 0)),
            par, par,
            pl.BlockSpec((1, LANES), lambda bi, t: (0, 0)),
        ],
        out_specs=pl.BlockSpec((None, ts, GDN_WIDTH), lambda bi, t: (bi, t, 0)),
        out_shape=jax.ShapeDtypeStruct((b, s, GDN_WIDTH), F32),
        scratch_shapes=[pltpu.VMEM((ts, GDN_WIDTH), F32) for _ in range(5)]
        + [pltpu.VMEM((GDN_HEADS, GDN_HEAD_DIM, GDN_HEAD_DIM), F32)],
        compiler_params=_cparams(("parallel", "arbitrary")),
        name="gdn",
    )(proj3, proj3, proj3, ab3, conv_w, alog_b, dtb_b, og)


def _outproj_body(sb_ref, gd_ref, x_ref, wo1_ref, wo2_ref, g2_ref, wqt_ref, x1_ref, hnt_ref, pqt_ref):
    mix = _dot(sb_ref[...].astype(BF16), wo1_ref[...]) + _dot(gd_ref[...].astype(BF16), wo2_ref[...])
    x1 = x_ref[...] + mix
    x1_ref[...] = x1
    ms = jnp.mean(x1 * x1, axis=-1, keepdims=True)
    hn = x1 * lax.rsqrt(ms + EPS) * g2_ref[...]
    hnt = hn.T.astype(BF16)
    hnt_ref[...] = hnt
    pqt_ref[...] = _dot(wqt_ref[...], hnt)


def _outproj(sb_o, gd_o, x2, wo1, wo2, g2, wqt, tm):
    t, d = x2.shape
    nq = wqt.shape[0]
    return pl.pallas_call(
        _outproj_body,
        grid=(t // tm,),
        in_specs=[
            pl.BlockSpec((tm, SB_WIDTH), lambda i: (i, 0)),
            pl.BlockSpec((tm, GDN_WIDTH), lambda i: (i, 0)),
            pl.BlockSpec((tm, d), lambda i: (i, 0)),
            pl.BlockSpec((SB_WIDTH, d), lambda i: (0, 0)),
            pl.BlockSpec((GDN_WIDTH, d), lambda i: (0, 0)),
            pl.BlockSpec((1, d), lambda i: (0, 0)),
            pl.BlockSpec((nq, d), lambda i: (0, 0)),
        ],
        out_specs=[
            pl.BlockSpec((tm, d), lambda i: (i, 0)),
            pl.BlockSpec((d, tm), lambda i: (0, i)),
            pl.BlockSpec((nq, tm), lambda i: (0, i)),
        ],
        out_shape=[
            jax.ShapeDtypeStruct((t, d), F32),
            jax.ShapeDtypeStruct((d, t), BF16),
            jax.ShapeDtypeStruct((nq, t), F32),
        ],
        compiler_params=_cparams(("parallel",)),
        name="outproj",
    )(sb_o, gd_o, x2, wo1, wo2, g2, wqt)


def _extract_topk(s, ids, k, want_rank):
    del want_rank
    big = jnp.int32(2 ** 30)
    work = s
    vals = []
    rank = jnp.full(s.shape, float(k), F32)
    for r in range(k):
        m = jnp.max(work, axis=0, keepdims=True)
        first = jnp.min(jnp.where(work == m, ids, big), axis=0, keepdims=True)
        hit = ids == first
        rank = jnp.where(hit, float(r), rank)
        work = jnp.where(hit, -jnp.inf, work)
        vals.append(m)
    return vals, rank, rank < float(k), None


def _extract_topk_untied(s, ids, k, want_rank):
    del ids
    work = s
    vals = []
    rank = jnp.full(s.shape, float(k), F32) if want_rank else None
    for r in range(k):
        m = jnp.max(work, axis=0, keepdims=True)
        hit = work == m
        if want_rank:
            rank = jnp.where(hit, float(r), rank)
        work = jnp.where(hit, -jnp.inf, work)
        vals.append(m)
    taken = rank < float(k) if want_rank else jnp.logical_and(work == -jnp.inf, s > -jnp.inf)
    count = jnp.sum(taken.astype(F32), axis=0, keepdims=True)
    return vals, rank, taken, count == float(k)


def _route_body(pqt_ref, k1_ref, k2_ref, rank2_ref, cnt_ref, e1_ref, e2z_ref):
    kk = PEER_TOPK
    tn = pqt_ref.shape[1]
    key_ids = lax.broadcasted_iota(jnp.int32, (PEER_NKEYS, tn), 0)
    i8 = lax.broadcasted_iota(jnp.int32, (8, tn), 0)
    i16 = lax.broadcasted_iota(jnp.int32, (kk, tn), 0)
    cand_ids = jnp.concatenate(
        [i16] + [i8 + a * kk for a in (1, 2, 3)]
        + [jnp.where(i8 >= 4, i8 * kk + b, kk * kk + i8 * kk + b) for b in (0, 1, 2)] + [(i8 + 8) * kk], axis=0)
    def route_head(h, s1, s2, extract):
        v1, rank1, _, ok1 = extract(s1, key_ids, kk, True)
        v2, rank2, _, ok2 = extract(s2, key_ids, kk, True)
        v1a = jnp.concatenate(v1, axis=0)
        v2a = jnp.concatenate(v2, axis=0)
        cand = jnp.concatenate(
            [v1[0] + v2a] + [v1[a] + v2a[:8] for a in (1, 2, 3)]
            + [jnp.where(i8 >= 4, v1a[:8] + v2[b], -jnp.inf) for b in (0, 1, 2)] + [v1a[8:] + v2[0]], axis=0)
        top, _, taken, ok3 = extract(cand, cand_ids, kk, False)
        zsum = jnp.ones_like(top[0])
        for r in range(1, kk):
            zsum = zsum + jnp.exp(top[r] - top[0])
        sel = taken.astype(F32)
        low = [jnp.sum(sel[0:16], axis=0, keepdims=True)] + [
            jnp.sum(sel[8 + 8 * a:16 + 8 * a], axis=0, keepdims=True) for a in (1, 2, 3)]
        mid = sel[40:48] + sel[48:56] + sel[56:64]
        high = sel[64:72]
        cnt = jnp.zeros(s1.shape, F32)
        for a in range(kk):
            n_a = low[a] if a < 4 else (mid[a:a + 1] if a < 8 else high[a - 8:a - 7])
            cnt = jnp.where(rank1 == float(a), n_a, cnt)
        rank2_ref[h] = rank2.astype(BF16)
        cnt_ref[h] = cnt
        e1_ref[h] = jnp.exp(s1 - v1[0])
        e2z_ref[h] = (jnp.exp(s2 - v2[0]) * (0.5 / zsum)).astype(BF16)
        return None if ok1 is None else jnp.logical_and(jnp.logical_and(ok1, ok2), ok3)

    for h in range(PEER_HEADS):
        q1 = pqt_ref[pl.ds(h * 2 * PEER_HALF, PEER_HALF), :].astype(BF16)
        q2 = pqt_ref[pl.ds(h * 2 * PEER_HALF + PEER_HALF, PEER_HALF), :].astype(BF16)
        s1 = _dot(k1_ref[h], q1)
        s2 = _dot(k2_ref[h], q2)
        ok = route_head(h, s1, s2, _extract_topk_untied)
        tied = jnp.max(jnp.where(ok, 0.0, 1.0)) > 0.0

        @pl.when(tied)
        def _():
            route_head(h, s1, s2, _extract_topk)


def _route(pqt, k1, k2, tn):
    nq, t = pqt.shape
    hk = (PEER_HEADS, PEER_NKEYS, PEER_HALF)
    out = lambda dt: jax.ShapeDtypeStruct((PEER_HEADS, PEER_NKEYS, t), dt)
    ospec = pl.BlockSpec((PEER_HEADS, PEER_NKEYS, tn), lambda i: (0, 0, i))
    return pl.pallas_call(
        _route_body,
        grid=(t // tn,),
        in_specs=[
            pl.BlockSpec((nq, tn), lambda i: (0, i)),
            pl.BlockSpec(hk, lambda i: (0, 0, 0)),
            pl.BlockSpec(hk, lambda i: (0, 0, 0)),
        ],
        out_specs=[ospec, ospec, ospec, ospec],
        out_shape=[out(BF16), out(F32), out(F32), out(BF16)],
        compiler_params=_cparams(("parallel",)),
        name="peer_route",
    )(pqt, k1, k2)


PEER_I1_PER_BLOCK = 8
PEER_I1_PER_CHUNK = 2
BF16_ROWS = 16


def _expert_body(hnt_ref, u_first_ref, u_b_ref, u_next_ref, vt_prev_ref, vt_a_ref, vt_last_ref, rank2_ref,
                 cnt_ref, e1_ref, e2z_ref, x1_ref, o_ref, acc_scr, act_next, p_prev):
    s = pl.program_id(1)
    hnt = hnt_ref[...]
    tn = hnt.shape[1]
    ce = PEER_I1_PER_CHUNK * PEER_NKEYS
    nchunk = PEER_I1_PER_BLOCK // PEER_I1_PER_CHUNK
    tiles = PEER_NKEYS // BF16_ROWS

    def activation(u_ref):
        for c in range(nchunk):
            pre = _dot(u_ref[pl.ds(c * ce, ce), :], hnt)
            yield (pre * (1.0 + lax.erf(pre * (2.0 ** -0.5)))).astype(BF16)

    def row(ref, h, l):
        return jnp.broadcast_to(ref[h, l:l + 1, :], (BF16_ROWS, tn)).astype(BF16)[None]

    def gates(l0, act_rows):
        for li in range(PEER_I1_PER_BLOCK):
            gate = None
            for h in range(PEER_HEADS):
                rank2 = rank2_ref[h].reshape(tiles, BF16_ROWS, tn)
                e2z = e2z_ref[h].reshape(tiles, BF16_ROWS, tn)
                term = jnp.where(rank2 < row(cnt_ref, h, l0 + li), e2z * row(e1_ref, h, l0 + li), 0.0)
                gate = term if gate is None else gate + term
            yield gate.reshape(PEER_NKEYS, tn) * act_rows(li)

    @pl.when(s == 0)
    def _():
        acc_scr[...] = jnp.zeros_like(acc_scr)
        p_prev[...] = jnp.zeros_like(p_prev)
        for c, act in enumerate(activation(u_first_ref)):
            act_next[pl.ds(c * ce, ce), :] = act

    def half(vt_ref, p_done, u_ref, gate_rows):
        hd = vt_ref.shape[0] // 2
        outs, acts, ps = [], [], []
        act_chunks = activation(u_ref)
        outs.append(_dot(vt_ref[pl.ds(0, hd), :], p_done))
        ps += [next(gate_rows), next(gate_rows)]
        outs.append(_dot(vt_ref[pl.ds(hd, hd), :], p_done))
        for c in range(nchunk):
            for _ in range(2 if c < 2 else 1):
                ps.append(next(gate_rows))
            acts.append(next(act_chunks))
        assert len(ps) == PEER_I1_PER_BLOCK and nchunk == 4
        return jnp.concatenate(outs, axis=0), acts, ps

    out_prev, acts_b, ps_a = half(
        vt_prev_ref, p_prev[...], u_b_ref,
        gates(0, lambda li: act_next[pl.ds(li * PEER_NKEYS, PEER_NKEYS), :]))
    act_b = jnp.concatenate(acts_b, axis=0)
    out_a, acts_next, ps_b = half(
        vt_a_ref, jnp.concatenate(ps_a, axis=0), u_next_ref,
        gates(PEER_I1_PER_BLOCK, lambda li: act_b[li * PEER_NKEYS:(li + 1) * PEER_NKEYS, :]))
    for c, act in enumerate(acts_next):
        act_next[pl.ds(c * ce, ce), :] = act
    for li, p in enumerate(ps_b):
        p_prev[pl.ds(li * PEER_NKEYS, PEER_NKEYS), :] = p
    acc_scr[...] += out_prev + out_a

    @pl.when(s == pl.num_programs(1) - 1)
    def _():
        o_ref[...] = x1_ref[...] + (acc_scr[...] + _dot(vt_last_ref[...], p_prev[...])).T


def _experts(hnt, u_b, vt_b, rank2, cnt, e1, e2z, x1, tn):
    d, t = hnt.shape
    ne = u_b.shape[0]
    eb = PEER_I1_PER_BLOCK * PEER_NKEYS
    nblk = ne // eb
    once = pl.Buffered(1)
    full = pl.BlockSpec((PEER_HEADS, PEER_NKEYS, tn), lambda i, s: (0, 0, i))
    part = pl.BlockSpec((PEER_HEADS, 2 * PEER_I1_PER_BLOCK, tn), lambda i, s: (0, s, i))
    return pl.pallas_call(
        _expert_body,
        grid=(t // tn, nblk // 2),
        in_specs=[
            pl.BlockSpec((d, tn), lambda i, s: (0, i)),
            pl.BlockSpec((eb, d), lambda i, s: (0, 0), pipeline_mode=once),
            pl.BlockSpec((eb, d), lambda i, s: (2 * s + 1, 0)),
            pl.BlockSpec((eb, d), lambda i, s: (jnp.minimum(2 * s + 2, nblk - 1), 0)),
            pl.BlockSpec((d, eb), lambda i, s: (0, jnp.maximum(2 * s - 1, 0))),
            pl.BlockSpec((d, eb), lambda i, s: (0, 2 * s)),
            pl.BlockSpec((d, eb), lambda i, s: (0, nblk - 1), pipeline_mode=once),
            full, part, part, full,
            pl.BlockSpec((tn, d), lambda i, s: (i, 0), pipeline_mode=once),
        ],
        out_specs=pl.BlockSpec((tn, d), lambda i, s: (i, 0)),
        out_shape=jax.ShapeDtypeStruct((t, d), F32),
        scratch_shapes=[pltpu.VMEM((d, tn), F32), pltpu.VMEM((eb, tn), BF16), pltpu.VMEM((eb, tn), BF16)],
        compiler_params=_cparams(("parallel", "arbitrary")),
        name="peer_experts",
    )(hnt, u_b, u_b, u_b, vt_b, vt_b, vt_b, rank2, cnt, e1, e2z, x1)


def _suffix_sum_matrix(tk):
    r = jnp.arange(2 * tk)[:, None] % tk
    c = jnp.arange(2 * tk)[None, :]
    return jnp.where(c < tk, r > c, True).astype(BF16)


def _layer(x, norm1_g, w_in, sb_q_g, sb_k_g, sb_o_g, conv_w, a_log, dt_bias, gdn_o_g, w_out, norm2_g,
           w_q, keys1, keys2, u_tab, v_tab):
    b, s, d = x.shape
    t = b * s
    n_main = 3 * SB_WIDTH + 4 * GDN_WIDTH
    x2 = x.reshape(t, d)
    w_main = w_in[:, :n_main].astype(BF16)
    w_ab = jnp.pad(w_in[:, n_main:], ((0, 0), (0, LANES - 2 * GDN_HEADS))).astype(BF16)
    tm = min(512, t)
    proj, ab = _inproj(x2, norm1_g.reshape(1, d), w_main, w_ab, tm)
    proj3 = proj.reshape(b, s, n_main)
    ab3 = ab.reshape(b, s, LANES)

    tq, tk = min(256, s), 128
    tile2 = lambda g: jnp.tile(g, 2).reshape(1, LANES)
    sb_o = _sb_attention(proj3, tile2(sb_q_g), tile2(sb_k_g), tile2(sb_o_g), _suffix_sum_matrix(tk), tq, tk)

    bcast = lambda p: jnp.broadcast_to(p[:, None, None], (GDN_HEADS, 1, LANES))
    gd_o = _gdn(proj3, ab3, conv_w, bcast(a_log), bcast(dt_bias), gdn_o_g.reshape(1, LANES), min(512, s))

    wo = w_out.astype(BF16)
    x1, hnt, pqt = _outproj(sb_o.reshape(t, SB_WIDTH), gd_o.reshape(t, GDN_WIDTH), x2, wo[:SB_WIDTH],
                            wo[SB_WIDTH:], norm2_g.reshape(1, d), w_q.T.astype(BF16), tm)

    rank2, cnt, e1, e2z = _route(pqt, keys1.astype(BF16), keys2.astype(BF16), min(256, t))
    y = _experts(hnt, u_tab.astype(BF16), v_tab.T.astype(BF16), rank2, cnt, e1, e2z, x1, min(512, t))
    return y.reshape(b, s, d)


def kernel(x, norm1_g, w_in, sb_q_norm_g, sb_k_norm_g, sb_out_norm_g, gdn_conv_w, gdn_a_log, gdn_dt_bias,
           gdn_out_norm_g, w_out, norm2_g, peer_w_q, peer_keys1, peer_keys2, peer_u, peer_v):
    for layer in range(norm1_g.shape[0]):
        x = _layer(x, norm1_g[layer], w_in[layer], sb_q_norm_g[layer], sb_k_norm_g[layer],
                   sb_out_norm_g[layer], gdn_conv_w[layer], gdn_a_log[layer], gdn_dt_bias[layer],
                   gdn_out_norm_g[layer], w_out[layer], norm2_g[layer], peer_w_q[layer],
                   peer_keys1[layer], peer_keys2[layer], peer_u[layer], peer_v[layer])
    return x
```

```python
import functools

import jax
import jax.numpy as jnp
from jax import lax
from jax.experimental import pallas as pl
from jax.experimental.pallas import tpu as pltpu

F32 = jnp.float32
BF16 = jnp.bfloat16
EPS = 1e-6

SB_HEADS = 8
SB_HEAD_DIM = 64
SB_WIDTH = SB_HEADS * SB_HEAD_DIM
GDN_HEADS = 4
GDN_HEAD_DIM = 128
GDN_WIDTH = GDN_HEADS * GDN_HEAD_DIM
GDN_CONV = 4
PEER_HEADS = 8
PEER_NKEYS = 128
PEER_HALF = 128
PEER_TOPK = 16
LANES = 128

VMEM_LIMIT = 56 * 1024 * 1024


def _cparams(sem):
    return pltpu.CompilerParams(dimension_semantics=sem, vmem_limit_bytes=VMEM_LIMIT)


def _dot(a, b):
    return jnp.dot(a, b, preferred_element_type=F32)


def _dot_nt(a, b):
    return lax.dot_general(a, b, (((1,), (1,)), ((), ())), preferred_element_type=F32)


def _dot_tn(a, b):
    return lax.dot_general(a, b, (((0,), (0,)), ((), ())), preferred_element_type=F32)


def _dot_f32(a, b):
    return jnp.dot(a, b, preferred_element_type=F32, precision=lax.Precision.HIGHEST)


def _split(a):
    hi = a.astype(BF16)
    return hi, (a - hi.astype(F32)).astype(BF16)


def _dot3(a, b):
    ah, al = _split(a)
    bh, bl = _split(b)
    return _dot(jnp.concatenate([ah, ah, al], axis=1), jnp.concatenate([bh, bl, bh], axis=0))


def _softplus(x):
    return jnp.maximum(x, 0.0) + jnp.log1p(jnp.exp(-jnp.abs(x)))


def _sigmoid(x):
    return 1.0 / (1.0 + jnp.exp(-x))


def _inproj_body(x_ref, g_ref, w_ref, wab_ref, proj_ref, ab_ref):
    x = x_ref[...]
    ms = jnp.mean(x * x, axis=-1, keepdims=True)
    h = (x * lax.rsqrt(ms + EPS) * g_ref[...]).astype(BF16)
    proj_ref[...] = _dot(h, w_ref[...])
    ab_ref[...] = _dot(h, wab_ref[...])


def _inproj(x2, g, w_main, w_ab, tm):
    t, d = x2.shape
    n = w_main.shape[1]
    return pl.pallas_call(
        _inproj_body,
        grid=(t // tm,),
        in_specs=[
            pl.BlockSpec((tm, d), lambda i: (i, 0)),
            pl.BlockSpec((1, d), lambda i: (0, 0)),
            pl.BlockSpec((d, n), lambda i: (0, 0)),
            pl.BlockSpec((d, LANES), lambda i: (0, 0)),
        ],
        out_specs=[
            pl.BlockSpec((tm, n), lambda i: (i, 0)),
            pl.BlockSpec((tm, LANES), lambda i: (i, 0)),
        ],
        out_shape=[
            jax.ShapeDtypeStruct((t, n), F32),
            jax.ShapeDtypeStruct((t, LANES), F32),
        ],
        compiler_params=_cparams(("parallel",)),
        name="inproj",
    )(x2, g, w_main, w_ab)


SB_DEAD_LOG = -104.0


def _sb_body(q_ref, k_ref, v_ref, gq_ref, gk_ref, go_ref, m2_ref, o_ref, kn_scr, vb_scr, *, tq, tk):
    i = pl.program_id(2)
    lane = lax.broadcasted_iota(jnp.int32, (1, LANES), 1)
    is0 = lane < SB_HEAD_DIM

    def headnorm(x, g):
        x2 = x * x
        s0 = jnp.sum(jnp.where(is0, x2, 0.0), axis=-1, keepdims=True)
        s1 = jnp.sum(jnp.where(is0, 0.0, x2), axis=-1, keepdims=True)
        ms = jnp.where(is0, s0, s1) * (1.0 / SB_HEAD_DIM)
        return x * lax.rsqrt(ms + EPS) * g

    @pl.when(i == 0)
    def _():
        kn_scr[...] = headnorm(k_ref[...], gk_ref[...]).astype(BF16)
        vb_scr[...] = v_ref[...].astype(BF16)

    qn = headnorm(q_ref[...], gq_ref[...]) * (SB_HEAD_DIM ** -0.5)
    qh = (jnp.where(is0, qn, 0.0).astype(BF16), jnp.where(is0, 0.0, qn).astype(BF16))
    m2 = m2_ref[...]
    row = i * tq + lax.broadcasted_iota(jnp.int32, (tq, tk), 0)
    col0 = lax.broadcasted_iota(jnp.int32, (tq, tk), 1)
    nkb = (i + 1) * (tq // tk)

    def scores(j):
        kj = kn_scr[pl.ds(pl.multiple_of(j * tk, tk), tk), :]
        return [_dot_nt(qh[h], kj) for h in range(2)]

    def logs(j, zs):
        causal = (col0 + j * tk) < row
        zls, cats = [], []
        for z in zs:
            lk = jnp.where(causal, -(jnp.maximum(z, 0.0) + jnp.log(1.0 + jnp.exp(-jnp.abs(z)))), 0.0)
            hi = lk.astype(BF16)
            lo = (lk - hi.astype(F32)).astype(BF16)
            zls.append(jnp.where(causal, z + lk, -jnp.inf))
            cats.append(jnp.concatenate([hi, lo], axis=1))
        return zls, cats

    def sums(cats):
        return [_dot(c, m2) for c in cats]

    def body(state):
        n, _, carry = state
        j = nkb - 1 - 2 * n
        zs_a = scores(j)
        zs_b = scores(j - 1)
        zl_a, cats_a = logs(j, zs_a)
        rt_a = sums(cats_a)
        zl_b, cats_b = logs(j - 1, zs_b)
        rt_b = sums(cats_b)
        v2 = vb_scr[pl.ds(pl.multiple_of((j - 1) * tk, tk), 2 * tk), :]
        out = []
        for h in range(2):
            acc, rest = carry[2 * h], carry[2 * h + 1]
            rest_mid = rest + rt_a[h][:, tk:]
            w_a = jnp.exp(zl_a[h] + (rest + rt_a[h][:, :tk])).astype(BF16)
            w_b = jnp.exp(zl_b[h] + (rest_mid + rt_b[h][:, :tk])).astype(BF16)
            out += [acc + _dot(jnp.concatenate([w_b, w_a], axis=1), v2), rest_mid + rt_b[h][:, tk:]]
        alive = (jnp.max(jnp.maximum(out[1], out[3])) > SB_DEAD_LOG).astype(jnp.int32)
        return n + 1, alive, tuple(out)

    def cond(state):
        return jnp.logical_and(state[0] < nkb // 2, state[1] > 0)

    zero = jnp.zeros((tq, LANES), F32)
    res = lax.while_loop(cond, body, (jnp.int32(0), jnp.int32(1), (zero, zero, zero, zero)))[2]
    o = jnp.where(is0, res[0], res[2])
    o_ref[...] = headnorm(o, go_ref[...])


def _sb_attention(proj3, gq, gk, go, m2, tq, tk):
    b, s, _ = proj3.shape
    hp = SB_HEADS // 2
    return pl.pallas_call(
        functools.partial(_sb_body, tq=tq, tk=tk),
        grid=(b, hp, s // tq),
        in_specs=[
            pl.BlockSpec((None, tq, LANES), lambda bi, h, i: (bi, i, h)),
            pl.BlockSpec((None, s, LANES), lambda bi, h, i: (bi, 0, hp + h)),
            pl.BlockSpec((None, s, LANES), lambda bi, h, i: (bi, 0, 2 * hp + h)),
            pl.BlockSpec((1, LANES), lambda bi, h, i: (0, 0)),
            pl.BlockSpec((1, LANES), lambda bi, h, i: (0, 0)),
            pl.BlockSpec((1, LANES), lambda bi, h, i: (0, 0)),
            pl.BlockSpec((2 * tk, 2 * tk), lambda bi, h, i: (0, 0)),
        ],
        out_specs=pl.BlockSpec((None, tq, LANES), lambda bi, h, i: (bi, i, h)),
        out_shape=jax.ShapeDtypeStruct((b, s, SB_WIDTH), F32),
        scratch_shapes=[pltpu.VMEM((s, LANES), BF16), pltpu.VMEM((s, LANES), BF16)],
        compiler_params=_cparams(("parallel", "parallel", "arbitrary")),
        name="sb_attention",
    )(proj3, proj3, proj3, gq, gk, go, m2)


GDN_BLOCK = 128


GDN_HALO = 8

def _gdn_body(x_ref, halo_ref, z_ref, ab_ref, cw_ref, alog_ref, dtb_ref, og_ref, o_ref,
              q_scr, k_scr, v_scr, g_scr, beta_scr, state_scr, *, ts):
    t = pl.program_id(1)
    c = GDN_BLOCK
    nh = GDN_HEADS
    lane = lax.broadcasted_iota(jnp.int32, (1, LANES), 1)

    @pl.when(t == 0)
    def _():
        state_scr[...] = jnp.zeros_like(state_scr)

    halo = jnp.where(t > 0, halo_ref[...], 0.0)
    xe = jnp.concatenate([halo, x_ref[...]], axis=0)
    cw = cw_ref[...]
    y = xe * cw[3:4, :]
    for d in (1, 2, 3):
        y = y + pltpu.roll(xe, d, 0) * cw[3 - d:4 - d, :]
    y = y[GDN_HALO:, :]
    y = y * _sigmoid(y)

    def l2n(x):
        return x * lax.rsqrt(jnp.sum(x * x, axis=-1, keepdims=True) + EPS)

    ab = ab_ref[...]
    for h in range(nh):
        sl = pl.ds(h * LANES, LANES)
        q_scr[:, sl] = l2n(y[:, h * LANES:(h + 1) * LANES]) * (GDN_HEAD_DIM ** -0.5)
        k_scr[:, sl] = l2n(y[:, GDN_WIDTH + h * LANES:GDN_WIDTH + (h + 1) * LANES])
        a_col = jnp.sum(jnp.where(lane == h, ab, 0.0), axis=-1, keepdims=True)
        b_col = jnp.sum(jnp.where(lane == h + nh, ab, 0.0), axis=-1, keepdims=True)
        g_scr[:, sl] = -jnp.exp(alog_ref[h]) * _softplus(a_col + dtb_ref[h])
        beta_scr[:, sl] = jnp.broadcast_to(_sigmoid(b_col), (ts, LANES))
    v_scr[...] = y[:, 2 * GDN_WIDTH:]

    ri = lax.broadcasted_iota(jnp.int32, (c, nh * c), 0)
    ci = lax.broadcasted_iota(jnp.int32, (c, nh * c), 1) % c
    lower_incl = ci <= ri
    lower_strict = ci < ri
    eye = (ci == ri).astype(F32)
    ltri = lower_incl[:, :c].astype(F32)
    ones = jnp.ones((c, c), F32)
    og = og_ref[...]
    heads = range(nh)
    hs = lambda m, h: m[:, h * c:(h + 1) * c]

    def blk(n, states):
        r0 = pl.multiple_of(n * c, c)
        q = q_scr[pl.ds(r0, c), :]
        k = k_scr[pl.ds(r0, c), :]
        v = v_scr[pl.ds(r0, c), :]
        g = g_scr[pl.ds(r0, c), :]
        beta = beta_scr[pl.ds(r0, c), :]
        gc = _dot_f32(ltri, g)
        gc_row = _dot_f32(ones, gc * eye)
        decay = jnp.exp(jnp.where(lower_incl, gc - gc_row, -jnp.inf))
        kb = k * beta
        kbf = k.astype(BF16)
        kbb = kb.astype(BF16)
        qbf = q.astype(BF16)
        kk = jnp.concatenate([_dot_nt(hs(kbb, h), hs(kbf, h)) for h in heads], axis=1)
        qk = jnp.concatenate([_dot_nt(hs(qbf, h), hs(kbf, h)) for h in heads], axis=1)
        a = jnp.where(lower_strict, kk * decay, 0.0)
        attn = jnp.where(lower_incl, qk * decay, 0.0).astype(BF16)
        nmat = [-hs(a, h) for h in heads]
        xp = [hs(a, h) for h in heads]
        for _ in range(6):
            xp = [_dot3(x, x) for x in xp]
            prod = [_dot3(nm, x) for nm, x in zip(nmat, xp)]
            nmat = [nm + x + p for nm, x, p in zip(nmat, xp, prod)]
        eg = jnp.exp(gc)
        vb = v * beta
        kbd = kb * eg
        gl = gc[c - 1:c, :]
        qd = (q * eg).astype(BF16)
        kd = (k * jnp.exp(gl - gc)).astype(BF16)
        dl = jnp.exp(gl)
        nb = [nm.astype(BF16) for nm in nmat]
        u = [hs(vb, h) + _dot(nb[h], hs(vb, h).astype(BF16)) for h in heads]
        w = [(hs(kbd, h) + _dot(nb[h], hs(kbd, h).astype(BF16))).astype(BF16) for h in heads]
        sb = [st.astype(BF16) for st in states]
        v_new = [u[h] - _dot(w[h], sb[h]) for h in heads]
        o_state = [_dot(hs(qd, h), sb[h]) for h in heads]
        vnb = [vn.astype(BF16) for vn in v_new]
        o = [o_state[h] + _dot(hs(attn, h), vnb[h]) for h in heads]
        new_states = tuple(states[h] * hs(dl, h) + _dot_tn(hs(kd, h), vnb[h]) for h in heads)
        on = jnp.concatenate(
            [x * lax.rsqrt(jnp.mean(x * x, axis=-1, keepdims=True) + EPS) * og for x in o], axis=1)
        z = z_ref[pl.ds(r0, c), :]
        o_ref[pl.ds(r0, c), :] = on * (z * _sigmoid(z))
        return new_states

    states = lax.fori_loop(0, ts // c, blk, tuple(state_scr[h] for h in heads))
    for h in heads:
        state_scr[h] = states[h]


def _gdn(proj3, ab3, conv_w, alog_b, dtb_b, og, ts):
    b, s, _ = proj3.shape
    w3 = 3 * GDN_WIDTH
    assert 3 * SB_WIDTH == w3 and 2 * w3 % GDN_WIDTH == 0
    per = ts // GDN_HALO
    par = pl.BlockSpec((GDN_HEADS, 1, LANES), lambda bi, t: (0, 0, 0))
    return pl.pallas_call(
        functools.partial(_gdn_body, ts=ts),
        grid=(b, s // ts),
        in_specs=[
            pl.BlockSpec((None, ts, w3), lambda bi, t: (bi, t, 1)),
            pl.BlockSpec((None, GDN_HALO, w3), lambda bi, t: (bi, jnp.maximum(t * per - 1, 0), 1)),
            pl.BlockSpec((None, ts, GDN_WIDTH), lambda bi, t: (bi, t, 2 * w3 // GDN_WIDTH)),
            pl.BlockSpec((None, ts, LANES), lambda bi, t: (bi, t, 0)),
            pl.BlockSpec((GDN_CONV, w3), lambda bi, t: (0, 0)),
            par, par,
            pl.BlockSpec((1, LANES), lambda bi, t: (0, 0)),
        ],
        out_specs=pl.BlockSpec((None, ts, GDN_WIDTH), lambda bi, t: (bi, t, 0)),
        out_shape=jax.ShapeDtypeStruct((b, s, GDN_WIDTH), F32),
        scratch_shapes=[pltpu.VMEM((ts, GDN_WIDTH), F32) for _ in range(5)]
        + [pltpu.VMEM((GDN_HEADS, GDN_HEAD_DIM, GDN_HEAD_DIM), F32)],
        compiler_params=_cparams(("parallel", "arbitrary")),
        name="gdn",
    )(proj3, proj3, proj3, ab3, conv_w, alog_b, dtb_b, og)


def _outproj_body(sb_ref, gd_ref, x_ref, wo1_ref, wo2_ref, g2_ref, wqt_ref, x1_ref, hnt_ref, pqt_ref):
    mix = _dot(sb_ref[...].astype(BF16), wo1_ref[...]) + _dot(gd_ref[...].astype(BF16), wo2_ref[...])
    x1 = x_ref[...] + mix
    x1_ref[...] = x1
    ms = jnp.mean(x1 * x1, axis=-1, keepdims=True)
    hn = x1 * lax.rsqrt(ms + EPS) * g2_ref[...]
    hnt = hn.T.astype(BF16)
    hnt_ref[...] = hnt
    pqt_ref[...] = _dot(wqt_ref[...], hnt)


def _outproj(sb_o, gd_o, x2, wo1, wo2, g2, wqt, tm):
    t, d = x2.shape
    nq = wqt.shape[0]
    return pl.pallas_call(
        _outproj_body,
        grid=(t // tm,),
        in_specs=[
            pl.BlockSpec((tm, SB_WIDTH), lambda i: (i, 0)),
            pl.BlockSpec((tm, GDN_WIDTH), lambda i: (i, 0)),
            pl.BlockSpec((tm, d), lambda i: (i, 0)),
            pl.BlockSpec((SB_WIDTH, d), lambda i: (0, 0)),
            pl.BlockSpec((GDN_WIDTH, d), lambda i: (0, 0)),
            pl.BlockSpec((1, d), lambda i: (0, 0)),
            pl.BlockSpec((nq, d), lambda i: (0, 0)),
        ],
        out_specs=[
            pl.BlockSpec((tm, d), lambda i: (i, 0)),
            pl.BlockSpec((d, tm), lambda i: (0, i)),
            pl.BlockSpec((nq, tm), lambda i: (0, i)),
        ],
        out_shape=[
            jax.ShapeDtypeStruct((t, d), F32),
            jax.ShapeDtypeStruct((d, t), BF16),
            jax.ShapeDtypeStruct((nq, t), F32),
        ],
        compiler_params=_cparams(("parallel",)),
        name="outproj",
    )(sb_o, gd_o, x2, wo1, wo2, g2, wqt)


def _extract_topk(s, ids, k, want_rank):
    del want_rank
    big = jnp.int32(2 ** 30)
    work = s
    vals = []
    rank = jnp.full(s.shape, float(k), F32)
    for r in range(k):
        m = jnp.max(work, axis=0, keepdims=True)
        first = jnp.min(jnp.where(work == m, ids, big), axis=0, keepdims=True)
        hit = ids == first
        rank = jnp.where(hit, float(r), rank)
        work = jnp.where(hit, -jnp.inf, work)
        vals.append(m)
    return vals, rank, rank < float(k), None


def _extract_topk_untied(s, ids, k, want_rank):
    del ids
    work = s
    vals = []
    rank = jnp.full(s.shape, float(k), F32) if want_rank else None
    for r in range(k):
        m = jnp.max(work, axis=0, keepdims=True)
        hit = work == m
        if want_rank:
            rank = jnp.where(hit, float(r), rank)
        work = jnp.where(hit, -jnp.inf, work)
        vals.append(m)
    taken = rank < float(k) if want_rank else jnp.logical_and(work == -jnp.inf, s > -jnp.inf)
    count = jnp.sum(taken.astype(F32), axis=0, keepdims=True)
    return vals, rank, taken, count == float(k)


def _route_body(pqt_ref, k1_ref, k2_ref, rank2_ref, cnt_ref, e1_ref, e2z_ref):
    kk = PEER_TOPK
    tn = pqt_ref.shape[1]
    key_ids = lax.broadcasted_iota(jnp.int32, (PEER_NKEYS, tn), 0)
    i8 = lax.broadcasted_iota(jnp.int32, (8, tn), 0)
    i16 = lax.broadcasted_iota(jnp.int32, (kk, tn), 0)
    cand_ids = jnp.concatenate(
        [i16] + [i8 + a * kk for a in (1, 2, 3)]
        + [jnp.where(i8 >= 4, i8 * kk + b, kk * kk + i8 * kk + b) for b in (0, 1, 2)] + [(i8 + 8) * kk], axis=0)
    def route_head(h, s1, s2, extract):
        v1, rank1, _, ok1 = extract(s1, key_ids, kk, True)
        v2, rank2, _, ok2 = extract(s2, key_ids, kk, True)
        v1a = jnp.concatenate(v1, axis=0)
        v2a = jnp.concatenate(v2, axis=0)
        cand = jnp.concatenate(
            [v1[0] + v2a] + [v1[a] + v2a[:8] for a in (1, 2, 3)]
            + [jnp.where(i8 >= 4, v1a[:8] + v2[b], -jnp.inf) for b in (0, 1, 2)] + [v1a[8:] + v2[0]], axis=0)
        top, _, taken, ok3 = extract(cand, cand_ids, kk, False)
        zsum = jnp.ones_like(top[0])
        for r in range(1, kk):
            zsum = zsum + jnp.exp(top[r] - top[0])
        sel = taken.astype(F32)
        low = [jnp.sum(sel[0:16], axis=0, keepdims=True)] + [
            jnp.sum(sel[8 + 8 * a:16 + 8 * a], axis=0, keepdims=True) for a in (1, 2, 3)]
        mid = sel[40:48] + sel[48:56] + sel[56:64]
        high = sel[64:72]
        cnt = jnp.zeros(s1.shape, F32)
        for a in range(kk):
            n_a = low[a] if a < 4 else (mid[a:a + 1] if a < 8 else high[a - 8:a - 7])
            cnt = jnp.where(rank1 == float(a), n_a, cnt)
        rank2_ref[h] = rank2.astype(BF16)
        cnt_ref[h] = cnt
        e1_ref[h] = jnp.exp(s1 - v1[0])
        e2z_ref[h] = (jnp.exp(s2 - v2[0]) * (0.5 / zsum)).astype(BF16)
        return None if ok1 is None else jnp.logical_and(jnp.logical_and(ok1, ok2), ok3)

    for h in range(PEER_HEADS):
        q1 = pqt_ref[pl.ds(h * 2 * PEER_HALF, PEER_HALF), :].astype(BF16)
        q2 = pqt_ref[pl.ds(h * 2 * PEER_HALF + PEER_HALF, PEER_HALF), :].astype(BF16)
        s1 = _dot(k1_ref[h], q1)
        s2 = _dot(k2_ref[h], q2)
        ok = route_head(h, s1, s2, _extract_topk_untied)
        tied = jnp.max(jnp.where(ok, 0.0, 1.0)) > 0.0

        @pl.when(tied)
        def _():
            route_head(h, s1, s2, _extract_topk)


def _route(pqt, k1, k2, tn):
    nq, t = pqt.shape
    hk = (PEER_HEADS, PEER_NKEYS, PEER_HALF)
    out = lambda dt: jax.ShapeDtypeStruct((PEER_HEADS, PEER_NKEYS, t), dt)
    ospec = pl.BlockSpec((PEER_HEADS, PEER_NKEYS, tn), lambda i: (0, 0, i))
    return pl.pallas_call(
        _route_body,
        grid=(t // tn,),
        in_specs=[
            pl.BlockSpec((nq, tn), lambda i: (0, i)),
            pl.BlockSpec(hk, lambda i: (0, 0, 0)),
            pl.BlockSpec(hk, lambda i: (0, 0, 0)),
        ],
        out_specs=[ospec, ospec, ospec, ospec],
        out_shape=[out(BF16), out(F32), out(F32), out(BF16)],
        compiler_params=_cparams(("parallel",)),
        name="peer_route",
    )(pqt, k1, k2)


PEER_I1_PER_BLOCK = 8
PEER_I1_PER_CHUNK = 8
BF16_ROWS = 16


def _expert_body(hnt_ref, u_first_ref, u_b_ref, u_next_ref, vt_prev_ref, vt_a_ref, vt_last_ref, rank2_ref,
                 cnt_ref, e1_ref, e2z_ref, x1_ref, o_ref, acc_scr, act_next, p_prev):
    s = pl.program_id(1)
    hnt = hnt_ref[...]
    tn = hnt.shape[1]
    ce = PEER_I1_PER_CHUNK * PEER_NKEYS
    nchunk = PEER_I1_PER_BLOCK // PEER_I1_PER_CHUNK
    tiles = PEER_NKEYS // BF16_ROWS

    def activation(u_ref):
        for c in range(nchunk):
            pre = _dot(u_ref[pl.ds(c * ce, ce), :], hnt)
            yield (pre * (1.0 + lax.erf(pre * (2.0 ** -0.5)))).astype(BF16)

    def row(ref, h, l):
        return jnp.broadcast_to(ref[h, l:l + 1, :], (BF16_ROWS, tn)).astype(BF16)[None]

    def gates(l0, act_rows):
        for li in range(PEER_I1_PER_BLOCK):
            gate = None
            for h in range(PEER_HEADS):
                rank2 = rank2_ref[h].reshape(tiles, BF16_ROWS, tn)
                e2z = e2z_ref[h].reshape(tiles, BF16_ROWS, tn)
                term = jnp.where(rank2 < row(cnt_ref, h, l0 + li), e2z * row(e1_ref, h, l0 + li), 0.0)
                gate = term if gate is None else gate + term
            yield gate.reshape(PEER_NKEYS, tn) * act_rows(li)

    @pl.when(s == 0)
    def _():
        acc_scr[...] = jnp.zeros_like(acc_scr)
        p_prev[...] = jnp.zeros_like(p_prev)
        for c, act in enumerate(activation(u_first_ref)):
            act_next[pl.ds(c * ce, ce), :] = act

    out_prev = _dot(vt_prev_ref[...], p_prev[...])
    act_b = jnp.concatenate(list(activation(u_b_ref)), axis=0)
    p_a = jnp.concatenate(
        list(gates(0, lambda li: act_next[pl.ds(li * PEER_NKEYS, PEER_NKEYS), :])), axis=0)
    out_a = _dot(vt_a_ref[...], p_a)
    for c, act in enumerate(activation(u_next_ref)):
        act_next[pl.ds(c * ce, ce), :] = act
    for li, p in enumerate(gates(PEER_I1_PER_BLOCK, lambda li: act_b[li * PEER_NKEYS:(li + 1) * PEER_NKEYS, :])):
        p_prev[pl.ds(li * PEER_NKEYS, PEER_NKEYS), :] = p
    acc_scr[...] += out_prev + out_a

    @pl.when(s == pl.num_programs(1) - 1)
    def _():
        o_ref[...] = x1_ref[...] + (acc_scr[...] + _dot(vt_last_ref[...], p_prev[...])).T


def _experts(hnt, u_b, vt_b, rank2, cnt, e1, e2z, x1, tn):
    d, t = hnt.shape
    ne = u_b.shape[0]
    eb = PEER_I1_PER_BLOCK * PEER_NKEYS
    nblk = ne // eb
    once = pl.Buffered(1)
    full = pl.BlockSpec((PEER_HEADS, PEER_NKEYS, tn), lambda i, s: (0, 0, i))
    part = pl.BlockSpec((PEER_HEADS, 2 * PEER_I1_PER_BLOCK, tn), lambda i, s: (0, s, i))
    return pl.pallas_call(
        _expert_body,
        grid=(t // tn, nblk // 2),
        in_specs=[
            pl.BlockSpec((d, tn), lambda i, s: (0, i)),
            pl.BlockSpec((eb, d), lambda i, s: (0, 0), pipeline_mode=once),
            pl.BlockSpec((eb, d), lambda i, s: (2 * s + 1, 0)),
            pl.BlockSpec((eb, d), lambda i, s: (jnp.minimum(2 * s + 2, nblk - 1), 0)),
            pl.BlockSpec((d, eb), lambda i, s: (0, jnp.maximum(2 * s - 1, 0))),
            pl.BlockSpec((d, eb), lambda i, s: (0, 2 * s)),
            pl.BlockSpec((d, eb), lambda i, s: (0, nblk - 1), pipeline_mode=once),
            full, part, part, full,
            pl.BlockSpec((tn, d), lambda i, s: (i, 0), pipeline_mode=once),
        ],
        out_specs=pl.BlockSpec((tn, d), lambda i, s: (i, 0)),
        out_shape=jax.ShapeDtypeStruct((t, d), F32),
        scratch_shapes=[pltpu.VMEM((d, tn), F32), pltpu.VMEM((eb, tn), BF16), pltpu.VMEM((eb, tn), BF16)],
        compiler_params=_cparams(("parallel", "arbitrary")),
        name="peer_experts",
    )(hnt, u_b, u_b, u_b, vt_b, vt_b, vt_b, rank2, cnt, e1, e2z, x1)


def _suffix_sum_matrix(tk):
    r = jnp.arange(2 * tk)[:, None] % tk
    c = jnp.arange(2 * tk)[None, :]
    return jnp.where(c < tk, r > c, True).astype(BF16)


def _layer(x, norm1_g, w_in, sb_q_g, sb_k_g, sb_o_g, conv_w, a_log, dt_bias, gdn_o_g, w_out, norm2_g,
           w_q, keys1, keys2, u_tab, v_tab):
    b, s, d = x.shape
    t = b * s
    n_main = 3 * SB_WIDTH + 4 * GDN_WIDTH
    x2 = x.reshape(t, d)
    w_main = w_in[:, :n_main].astype(BF16)
    w_ab = jnp.pad(w_in[:, n_main:], ((0, 0), (0, LANES - 2 * GDN_HEADS))).astype(BF16)
    tm = min(512, t)
    proj, ab = _inproj(x2, norm1_g.reshape(1, d), w_main, w_ab, tm)
    proj3 = proj.reshape(b, s, n_main)
    ab3 = ab.reshape(b, s, LANES)

    tq, tk = min(256, s), 128
    tile2 = lambda g: jnp.tile(g, 2).reshape(1, LANES)
    sb_o = _sb_attention(proj3, tile2(sb_q_g), tile2(sb_k_g), tile2(sb_o_g), _suffix_sum_matrix(tk), tq, tk)

    bcast = lambda p: jnp.broadcast_to(p[:, None, None], (GDN_HEADS, 1, LANES))
    gd_o = _gdn(proj3, ab3, conv_w, bcast(a_log), bcast(dt_bias), gdn_o_g.reshape(1, LANES), min(512, s))

    wo = w_out.astype(BF16)
    x1, hnt, pqt = _outproj(sb_o.reshape(t, SB_WIDTH), gd_o.reshape(t, GDN_WIDTH), x2, wo[:SB_WIDTH],
                            wo[SB_WIDTH:], norm2_g.reshape(1, d), w_q.T.astype(BF16), tm)

    rank2, cnt, e1, e2z = _route(pqt, keys1.astype(BF16), keys2.astype(BF16), min(256, t))
    y = _experts(hnt, u_tab.astype(BF16), v_tab.T.astype(BF16), rank2, cnt, e1, e2z, x1, min(512, t))
    return y.reshape(b, s, d)


def kernel(x, norm1_g, w_in, sb_q_norm_g, sb_k_norm_g, sb_out_norm_g, gdn_conv_w, gdn_a_log, gdn_dt_bias,
           gdn_out_norm_g, w_out, norm2_g, peer_w_q, peer_keys1, peer_keys2, peer_u, peer_v):
    for layer in range(norm1_g.shape[0]):
        x = _layer(x, norm1_g[layer], w_in[layer], sb_q_norm_g[layer], sb_k_norm_g[layer],
                   sb_out_norm_g[layer], gdn_conv_w[layer], gdn_a_log[layer], gdn_dt_bias[layer],
                   gdn_out_norm_g[layer], w_out[layer], norm2_g[layer], peer_w_q[layer],
                   peer_keys1[layer], peer_keys2[layer], peer_u[layer], peer_v[layer])
    return x
```

```python
import functools
from typing import NamedTuple

import jax
import jax.numpy as jnp
from jax import lax
from jax.experimental import pallas as pl
from jax.experimental.pallas import tpu as pltpu

F32 = jnp.float32
BF16 = jnp.bfloat16
EPS = 1e-6

SB_HEADS = 8
SB_HEAD_DIM = 64
SB_WIDTH = SB_HEADS * SB_HEAD_DIM
GDN_HEADS = 4
GDN_HEAD_DIM = 128
GDN_WIDTH = GDN_HEADS * GDN_HEAD_DIM
GDN_CONV = 4
PEER_HEADS = 8
PEER_NKEYS = 128
PEER_HALF = 128
PEER_TOPK = 16
LANES = 128

VMEM_LIMIT = 56 * 1024 * 1024


def _cparams(sem):
    return pltpu.CompilerParams(dimension_semantics=sem, vmem_limit_bytes=VMEM_LIMIT)


def _dot(a, b):
    return jnp.dot(a, b, preferred_element_type=F32)


def _dot_nt(a, b):
    return lax.dot_general(a, b, (((1,), (1,)), ((), ())), preferred_element_type=F32)


def _dot_tn(a, b):
    return lax.dot_general(a, b, (((0,), (0,)), ((), ())), preferred_element_type=F32)


def _dot_f32(a, b):
    return jnp.dot(a, b, preferred_element_type=F32, precision=lax.Precision.HIGHEST)


def _split(a):
    hi = a.astype(BF16)
    return hi, (a - hi.astype(F32)).astype(BF16)


def _dot3(a, b):
    ah, al = _split(a)
    bh, bl = _split(b)
    return _dot(jnp.concatenate([ah, ah, al], axis=1), jnp.concatenate([bh, bl, bh], axis=0))


def _softplus(x):
    return jnp.maximum(x, 0.0) + jnp.log1p(jnp.exp(-jnp.abs(x)))


def _sigmoid(x):
    return 1.0 / (1.0 + jnp.exp(-x))


def _inproj_body(x_ref, g_ref, w_ref, wab_ref, proj_ref, ab_ref):
    x = x_ref[...]
    ms = jnp.mean(x * x, axis=-1, keepdims=True)
    h = (x * lax.rsqrt(ms + EPS) * g_ref[...]).astype(BF16)
    proj_ref[...] = _dot(h, w_ref[...])
    ab_ref[...] = _dot(h, wab_ref[...])


def _inproj(x2, g, w_main, w_ab, tm):
    t, d = x2.shape
    n = w_main.shape[1]
    return pl.pallas_call(
        _inproj_body,
        grid=(t // tm,),
        in_specs=[
            pl.BlockSpec((tm, d), lambda i: (i, 0)),
            pl.BlockSpec((1, d), lambda i: (0, 0)),
            pl.BlockSpec((d, n), lambda i: (0, 0)),
            pl.BlockSpec((d, LANES), lambda i: (0, 0)),
        ],
        out_specs=[
            pl.BlockSpec((tm, n), lambda i: (i, 0)),
            pl.BlockSpec((tm, LANES), lambda i: (i, 0)),
        ],
        out_shape=[
            jax.ShapeDtypeStruct((t, n), F32),
            jax.ShapeDtypeStruct((t, LANES), F32),
        ],
        compiler_params=_cparams(("parallel",)),
        name="inproj",
    )(x2, g, w_main, w_ab)


SB_DEAD_LOG = -104.0


def _sb_body(q_ref, k_ref, v_ref, gq_ref, gk_ref, go_ref, m2_ref, o_ref, kn_scr, vb_scr, *, tq, tk):
    i = pl.program_id(2)
    lane = lax.broadcasted_iota(jnp.int32, (1, LANES), 1)
    is0 = lane < SB_HEAD_DIM

    def headnorm(x, g):
        x2 = x * x
        s0 = jnp.sum(jnp.where(is0, x2, 0.0), axis=-1, keepdims=True)
        s1 = jnp.sum(jnp.where(is0, 0.0, x2), axis=-1, keepdims=True)
        ms = jnp.where(is0, s0, s1) * (1.0 / SB_HEAD_DIM)
        return x * lax.rsqrt(ms + EPS) * g

    @pl.when(i == 0)
    def _():
        kn_scr[...] = headnorm(k_ref[...], gk_ref[...]).astype(BF16)
        vb_scr[...] = v_ref[...].astype(BF16)

    qn = headnorm(q_ref[...], gq_ref[...]) * (SB_HEAD_DIM ** -0.5)
    qh = (jnp.where(is0, qn, 0.0).astype(BF16), jnp.where(is0, 0.0, qn).astype(BF16))
    m2 = m2_ref[...]
    row = i * tq + lax.broadcasted_iota(jnp.int32, (tq, tk), 0)
    col0 = lax.broadcasted_iota(jnp.int32, (tq, tk), 1)
    nkb = (i + 1) * (tq // tk)

    def scores(j):
        kj = kn_scr[pl.ds(pl.multiple_of(j * tk, tk), tk), :]
        return [_dot_nt(qh[h], kj) for h in range(2)]

    def logs(j, zs, on_diagonal):
        causal = (col0 + j * tk) < row
        zls, cats = [], []
        for z in zs:
            lk = -(jnp.maximum(z, 0.0) + jnp.log(1.0 + jnp.exp(-jnp.abs(z))))
            zl = z + lk
            if on_diagonal:
                lk = jnp.where(causal, lk, 0.0)
                zl = jnp.where(causal, zl, -jnp.inf)
            hi = lk.astype(BF16)
            lo = (lk - hi.astype(F32)).astype(BF16)
            zls.append(zl)
            cats.append(jnp.concatenate([hi, lo], axis=1))
        return zls, cats

    def sums(cats):
        return [_dot(c, m2) for c in cats]

    def pair(n, carry, on_diagonal):
        j = nkb - 1 - 2 * n
        zs_a = scores(j)
        zs_b = scores(j - 1)
        zl_a, cats_a = logs(j, zs_a, on_diagonal)
        rt_a = sums(cats_a)
        zl_b, cats_b = logs(j - 1, zs_b, on_diagonal)
        rt_b = sums(cats_b)
        v2 = vb_scr[pl.ds(pl.multiple_of((j - 1) * tk, tk), 2 * tk), :]
        out = []
        for h in range(2):
            acc, rest = carry[2 * h], carry[2 * h + 1]
            rest_mid = rest + rt_a[h][:, tk:]
            w_a = jnp.exp(zl_a[h] + (rest + rt_a[h][:, :tk])).astype(BF16)
            w_b = jnp.exp(zl_b[h] + (rest_mid + rt_b[h][:, :tk])).astype(BF16)
            out += [acc + _dot(jnp.concatenate([w_b, w_a], axis=1), v2), rest_mid + rt_b[h][:, tk:]]
        alive = (jnp.max(jnp.maximum(out[1], out[3])) > SB_DEAD_LOG).astype(jnp.int32)
        return n + 1, alive, tuple(out)

    def cond(state):
        return jnp.logical_and(state[0] < nkb // 2, state[1] > 0)

    assert tq == 2 * tk
    zero = jnp.zeros((tq, LANES), F32)
    first = pair(0, (zero, zero, zero, zero), True)
    res = lax.while_loop(cond, lambda state: pair(state[0], state[2], False), first)[2]
    o = jnp.where(is0, res[0], res[2])
    o_ref[...] = headnorm(o, go_ref[...])


def _sb_attention(proj3, gq, gk, go, m2, tq, tk):
    b, s, _ = proj3.shape
    hp = SB_HEADS // 2
    return pl.pallas_call(
        functools.partial(_sb_body, tq=tq, tk=tk),
        grid=(b, hp, s // tq),
        in_specs=[
            pl.BlockSpec((None, tq, LANES), lambda bi, h, i: (bi, i, h)),
            pl.BlockSpec((None, s, LANES), lambda bi, h, i: (bi, 0, hp + h)),
            pl.BlockSpec((None, s, LANES), lambda bi, h, i: (bi, 0, 2 * hp + h)),
            pl.BlockSpec((1, LANES), lambda bi, h, i: (0, 0)),
            pl.BlockSpec((1, LANES), lambda bi, h, i: (0, 0)),
            pl.BlockSpec((1, LANES), lambda bi, h, i: (0, 0)),
            pl.BlockSpec((2 * tk, 2 * tk), lambda bi, h, i: (0, 0)),
        ],
        out_specs=pl.BlockSpec((None, tq, LANES), lambda bi, h, i: (bi, i, h)),
        out_shape=jax.ShapeDtypeStruct((b, s, SB_WIDTH), F32),
        scratch_shapes=[pltpu.VMEM((s, LANES), BF16), pltpu.VMEM((s, LANES), BF16)],
        compiler_params=_cparams(("parallel", "parallel", "arbitrary")),
        name="sb_attention",
    )(proj3, proj3, proj3, gq, gk, go, m2)


GDN_BLOCK = 128
GDN_HALO = 8


def _gdn_body(x_ref, halo_ref, z_ref, ab_ref, cw_ref, alog_ref, dtb_ref, og_ref, o_ref,
              q_scr, k_scr, v_scr, g_scr, beta_scr, state_scr, *, ts):
    t = pl.program_id(1)
    c = GDN_BLOCK
    nh = GDN_HEADS
    lane = lax.broadcasted_iota(jnp.int32, (1, LANES), 1)

    @pl.when(t == 0)
    def _():
        state_scr[...] = jnp.zeros_like(state_scr)

    halo = jnp.where(t > 0, halo_ref[...], 0.0)
    xe = jnp.concatenate([halo, x_ref[...]], axis=0)
    cw = cw_ref[...]
    y = xe * cw[3:4, :]
    for d in (1, 2, 3):
        y = y + pltpu.roll(xe, d, 0) * cw[3 - d:4 - d, :]
    y = y[GDN_HALO:, :]
    y = y * _sigmoid(y)

    def l2n(x):
        return x * lax.rsqrt(jnp.sum(x * x, axis=-1, keepdims=True) + EPS)

    ab = ab_ref[...]
    for h in range(nh):
        sl = pl.ds(h * LANES, LANES)
        q_scr[:, sl] = l2n(y[:, h * LANES:(h + 1) * LANES]) * (GDN_HEAD_DIM ** -0.5)
        k_scr[:, sl] = l2n(y[:, GDN_WIDTH + h * LANES:GDN_WIDTH + (h + 1) * LANES])
        a_col = jnp.sum(jnp.where(lane == h, ab, 0.0), axis=-1, keepdims=True)
        b_col = jnp.sum(jnp.where(lane == h + nh, ab, 0.0), axis=-1, keepdims=True)
        g_scr[:, sl] = -jnp.exp(alog_ref[h]) * _softplus(a_col + dtb_ref[h])
        beta_scr[:, sl] = jnp.broadcast_to(_sigmoid(b_col), (ts, LANES))
    v_scr[...] = y[:, 2 * GDN_WIDTH:]

    ri = lax.broadcasted_iota(jnp.int32, (c, nh * c), 0)
    ci = lax.broadcasted_iota(jnp.int32, (c, nh * c), 1) % c
    lower_incl = ci <= ri
    lower_strict = ci < ri
    eye = (ci == ri).astype(F32)
    ltri = lower_incl[:, :c].astype(F32)
    ones = jnp.ones((c, c), F32)
    og = og_ref[...]
    heads = range(nh)
    hs = lambda m, h: m[:, h * c:(h + 1) * c]

    def blk(n, states):
        r0 = pl.multiple_of(n * c, c)
        q = q_scr[pl.ds(r0, c), :]
        k = k_scr[pl.ds(r0, c), :]
        v = v_scr[pl.ds(r0, c), :]
        g = g_scr[pl.ds(r0, c), :]
        beta = beta_scr[pl.ds(r0, c), :]
        gc = _dot_f32(ltri, g)
        gc_row = _dot_f32(ones, gc * eye)
        decay = jnp.exp(jnp.where(lower_incl, gc - gc_row, -jnp.inf))
        kb = k * beta
        kbf = k.astype(BF16)
        kbb = kb.astype(BF16)
        qbf = q.astype(BF16)
        kk = jnp.concatenate([_dot_nt(hs(kbb, h), hs(kbf, h)) for h in heads], axis=1)
        qk = jnp.concatenate([_dot_nt(hs(qbf, h), hs(kbf, h)) for h in heads], axis=1)
        a = jnp.where(lower_strict, kk * decay, 0.0)
        attn = jnp.where(lower_incl, qk * decay, 0.0).astype(BF16)
        nmat = [-hs(a, h) for h in heads]
        xp = [hs(a, h) for h in heads]
        for _ in range(6):
            xp = [_dot3(x, x) for x in xp]
            prod = [_dot3(nm, x) for nm, x in zip(nmat, xp)]
            nmat = [nm + x + p for nm, x, p in zip(nmat, xp, prod)]
        eg = jnp.exp(gc)
        vb = v * beta
        kbd = kb * eg
        gl = gc[c - 1:c, :]
        qd = (q * eg).astype(BF16)
        kd = (k * jnp.exp(gl - gc)).astype(BF16)
        dl = jnp.exp(gl)
        nb = [nm.astype(BF16) for nm in nmat]
        u = [hs(vb, h) + _dot(nb[h], hs(vb, h).astype(BF16)) for h in heads]
        w = [(hs(kbd, h) + _dot(nb[h], hs(kbd, h).astype(BF16))).astype(BF16) for h in heads]
        sb = [st.astype(BF16) for st in states]
        v_new = [u[h] - _dot(w[h], sb[h]) for h in heads]
        o_state = [_dot(hs(qd, h), sb[h]) for h in heads]
        vnb = [vn.astype(BF16) for vn in v_new]
        o = [o_state[h] + _dot(hs(attn, h), vnb[h]) for h in heads]
        new_states = tuple(states[h] * hs(dl, h) + _dot_tn(hs(kd, h), vnb[h]) for h in heads)
        on = jnp.concatenate(
            [x * lax.rsqrt(jnp.mean(x * x, axis=-1, keepdims=True) + EPS) * og for x in o], axis=1)
        z = z_ref[pl.ds(r0, c), :]
        o_ref[pl.ds(r0, c), :] = on * (z * _sigmoid(z))
        return new_states

    states = lax.fori_loop(0, ts // c, blk, tuple(state_scr[h] for h in heads))
    for h in heads:
        state_scr[h] = states[h]


def _gdn(proj3, ab3, conv_w, alog_b, dtb_b, og, ts):
    b, s, _ = proj3.shape
    w3 = 3 * GDN_WIDTH
    assert 3 * SB_WIDTH == w3 and 2 * w3 % GDN_WIDTH == 0
    per = ts // GDN_HALO
    par = pl.BlockSpec((GDN_HEADS, 1, LANES), lambda bi, t: (0, 0, 0))
    return pl.pallas_call(
        functools.partial(_gdn_body, ts=ts),
        grid=(b, s // ts),
        in_specs=[
            pl.BlockSpec((None, ts, w3), lambda bi, t: (bi, t, 1)),
            pl.BlockSpec((None, GDN_HALO, w3), lambda bi, t: (bi, jnp.maximum(t * per - 1, 0), 1)),
            pl.BlockSpec((None, ts, GDN_WIDTH), lambda bi, t: (bi, t, 2 * w3 // GDN_WIDTH)),
            pl.BlockSpec((None, ts, LANES), lambda bi, t: (bi, t, 0)),
            pl.BlockSpec((GDN_CONV, w3), lambda bi, t: (0, 0)),
            par, par,
            pl.BlockSpec((1, LANES), lambda bi, t: (0, 0)),
        ],
        out_specs=pl.BlockSpec((None, ts, GDN_WIDTH), lambda bi, t: (bi, t, 0)),
        out_shape=jax.ShapeDtypeStruct((b, s, GDN_WIDTH), F32),
        scratch_shapes=[pltpu.VMEM((ts, GDN_WIDTH), F32) for _ in range(5)]
        + [pltpu.VMEM((GDN_HEADS, GDN_HEAD_DIM, GDN_HEAD_DIM), F32)],
        compiler_params=_cparams(("parallel", "arbitrary")),
        name="gdn",
    )(proj3, proj3, proj3, ab3, conv_w, alog_b, dtb_b, og)


def _outproj_body(sb_ref, gd_ref, x_ref, wo1_ref, wo2_ref, g2_ref, wqt_ref, x1_ref, hnt_ref, pqt_ref):
    mix = _dot(sb_ref[...].astype(BF16), wo1_ref[...]) + _dot(gd_ref[...].astype(BF16), wo2_ref[...])
    x1 = x_ref[...] + mix
    x1_ref[...] = x1
    ms = jnp.mean(x1 * x1, axis=-1, keepdims=True)
    hn = x1 * lax.rsqrt(ms + EPS) * g2_ref[...]
    hnt = hn.T.astype(BF16)
    hnt_ref[...] = hnt
    pqt_ref[...] = _dot(wqt_ref[...], hnt)


def _outproj(sb_o, gd_o, x2, wo1, wo2, g2, wqt, tm):
    t, d = x2.shape
    nq = wqt.shape[0]
    return pl.pallas_call(
        _outproj_body,
        grid=(t // tm,),
        in_specs=[
            pl.BlockSpec((tm, SB_WIDTH), lambda i: (i, 0)),
            pl.BlockSpec((tm, GDN_WIDTH), lambda i: (i, 0)),
            pl.BlockSpec((tm, d), lambda i: (i, 0)),
            pl.BlockSpec((SB_WIDTH, d), lambda i: (0, 0)),
            pl.BlockSpec((GDN_WIDTH, d), lambda i: (0, 0)),
            pl.BlockSpec((1, d), lambda i: (0, 0)),
            pl.BlockSpec((nq, d), lambda i: (0, 0)),
        ],
        out_specs=[
            pl.BlockSpec((tm, d), lambda i: (i, 0)),
            pl.BlockSpec((d, tm), lambda i: (0, i)),
            pl.BlockSpec((nq, tm), lambda i: (0, i)),
        ],
        out_shape=[
            jax.ShapeDtypeStruct((t, d), F32),
            jax.ShapeDtypeStruct((d, t), BF16),
            jax.ShapeDtypeStruct((nq, t), F32),
        ],
        compiler_params=_cparams(("parallel",)),
        name="outproj",
    )(sb_o, gd_o, x2, wo1, wo2, g2, wqt)


def _extract_topk(s, ids, k, want_rank):
    del want_rank
    big = jnp.int32(2 ** 30)
    work = s
    vals = []
    rank = jnp.full(s.shape, float(k), F32)
    for r in range(k):
        m = jnp.max(work, axis=0, keepdims=True)
        first = jnp.min(jnp.where(work == m, ids, big), axis=0, keepdims=True)
        hit = ids == first
        rank = jnp.where(hit, float(r), rank)
        work = jnp.where(hit, -jnp.inf, work)
        vals.append(m)
    return vals, rank, rank < float(k), None


def _extract_topk_untied(s, ids, k, want_rank):
    del ids
    work = s
    vals = []
    rank = jnp.full(s.shape, float(k), F32) if want_rank else None
    for r in range(k):
        m = jnp.max(work, axis=0, keepdims=True)
        hit = work == m
        if want_rank:
            rank = jnp.where(hit, float(r), rank)
        work = jnp.where(hit, -jnp.inf, work)
        vals.append(m)
    taken = rank < float(k) if want_rank else jnp.logical_and(work == -jnp.inf, s > -jnp.inf)
    count = jnp.sum(taken.astype(F32), axis=0, keepdims=True)
    return vals, rank, taken, count == float(k)


def _route_body(pqt_ref, k1_ref, k2_ref, rank2_ref, cnt_ref, e1_ref, e2z_ref):
    kk = PEER_TOPK
    tn = pqt_ref.shape[1]
    key_ids = lax.broadcasted_iota(jnp.int32, (PEER_NKEYS, tn), 0)
    i8 = lax.broadcasted_iota(jnp.int32, (8, tn), 0)
    i16 = lax.broadcasted_iota(jnp.int32, (kk, tn), 0)
    cand_ids = jnp.concatenate(
        [i16] + [i8 + a * kk for a in (1, 2, 3)]
        + [jnp.where(i8 >= 4, i8 * kk + b, kk * kk + i8 * kk + b) for b in (0, 1, 2)] + [(i8 + 8) * kk], axis=0)
    def route_head(h, s1, s2, extract):
        v1, rank1, _, ok1 = extract(s1, key_ids, kk, True)
        v2, rank2, _, ok2 = extract(s2, key_ids, kk, True)
        v1a = jnp.concatenate(v1, axis=0)
        v2a = jnp.concatenate(v2, axis=0)
        cand = jnp.concatenate(
            [v1[0] + v2a] + [v1[a] + v2a[:8] for a in (1, 2, 3)]
            + [jnp.where(i8 >= 4, v1a[:8] + v2[b], -jnp.inf) for b in (0, 1, 2)] + [v1a[8:] + v2[0]], axis=0)
        top, _, taken, ok3 = extract(cand, cand_ids, kk, False)
        zsum = jnp.ones_like(top[0])
        for r in range(1, kk):
            zsum = zsum + jnp.exp(top[r] - top[0])
        sel = taken.astype(F32)
        low = [jnp.sum(sel[0:16], axis=0, keepdims=True)] + [
            jnp.sum(sel[8 + 8 * a:16 + 8 * a], axis=0, keepdims=True) for a in (1, 2, 3)]
        mid = sel[40:48] + sel[48:56] + sel[56:64]
        high = sel[64:72]
        cnt = jnp.zeros(s1.shape, F32)
        for a in range(kk):
            n_a = low[a] if a < 4 else (mid[a:a + 1] if a < 8 else high[a - 8:a - 7])
            cnt = jnp.where(rank1 == float(a), n_a, cnt)
        rank2_ref[h] = rank2.astype(BF16)
        cnt_ref[h] = cnt
        e1_ref[h] = jnp.exp(s1 - v1[0])
        e2z_ref[h] = (jnp.exp(s2 - v2[0]) * (0.5 / zsum)).astype(BF16)
        return None if ok1 is None else jnp.logical_and(jnp.logical_and(ok1, ok2), ok3)

    for h in range(PEER_HEADS):
        q1 = pqt_ref[pl.ds(h * 2 * PEER_HALF, PEER_HALF), :].astype(BF16)
        q2 = pqt_ref[pl.ds(h * 2 * PEER_HALF + PEER_HALF, PEER_HALF), :].astype(BF16)
        s1 = _dot(k1_ref[h], q1)
        s2 = _dot(k2_ref[h], q2)
        ok = route_head(h, s1, s2, _extract_topk_untied)
        tied = jnp.max(jnp.where(ok, 0.0, 1.0)) > 0.0

        @pl.when(tied)
        def _():
            route_head(h, s1, s2, _extract_topk)


def _route(pqt, k1, k2, tn):
    nq, t = pqt.shape
    hk = (PEER_HEADS, PEER_NKEYS, PEER_HALF)
    out = lambda dt: jax.ShapeDtypeStruct((PEER_HEADS, PEER_NKEYS, t), dt)
    ospec = pl.BlockSpec((PEER_HEADS, PEER_NKEYS, tn), lambda i: (0, 0, i))
    return pl.pallas_call(
        _route_body,
        grid=(t // tn,),
        in_specs=[
            pl.BlockSpec((nq, tn), lambda i: (0, i)),
            pl.BlockSpec(hk, lambda i: (0, 0, 0)),
            pl.BlockSpec(hk, lambda i: (0, 0, 0)),
        ],
        out_specs=[ospec, ospec, ospec, ospec],
        out_shape=[out(BF16), out(F32), out(F32), out(BF16)],
        compiler_params=_cparams(("parallel",)),
        name="peer_route",
    )(pqt, k1, k2)


PEER_I1_PER_BLOCK = 8
BF16_ROWS = 16


def _expert_body(hnt_ref, u_first_ref, u_b_ref, u_next_ref, vt_prev_ref, vt_a_ref, vt_last_ref, rank2_ref,
                 cnt_ref, e1_ref, e2z_ref, x1_ref, o_ref, acc_scr, act_next, p_prev):
    s = pl.program_id(1)
    hnt = hnt_ref[...]
    tn = hnt.shape[1]
    tiles = PEER_NKEYS // BF16_ROWS

    def activation(u_ref):
        pre = _dot(u_ref[...], hnt)
        return (pre * (1.0 + lax.erf(pre * (2.0 ** -0.5)))).astype(BF16)

    def row(ref, h, l):
        return jnp.broadcast_to(ref[h, l:l + 1, :], (BF16_ROWS, tn)).astype(BF16)[None]

    def gates(l0, act_rows):
        ps = []
        for li in range(PEER_I1_PER_BLOCK):
            gate = None
            for h in range(PEER_HEADS):
                rank2 = rank2_ref[h].reshape(tiles, BF16_ROWS, tn)
                e2z = e2z_ref[h].reshape(tiles, BF16_ROWS, tn)
                term = jnp.where(rank2 < row(cnt_ref, h, l0 + li), e2z * row(e1_ref, h, l0 + li), 0.0)
                gate = term if gate is None else gate + term
            ps.append(gate.reshape(PEER_NKEYS, tn) * act_rows(li))
        return jnp.concatenate(ps, axis=0)

    @pl.when(s == 0)
    def _():
        acc_scr[...] = jnp.zeros_like(acc_scr)
        p_prev[...] = jnp.zeros_like(p_prev)
        act_next[...] = activation(u_first_ref)

    out_prev = _dot(vt_prev_ref[...], p_prev[...])
    act_b = activation(u_b_ref)
    p_a = gates(0, lambda li: act_next[pl.ds(li * PEER_NKEYS, PEER_NKEYS), :])
    out_a = _dot(vt_a_ref[...], p_a)
    act_next[...] = activation(u_next_ref)
    p_prev[...] = gates(PEER_I1_PER_BLOCK, lambda li: act_b[li * PEER_NKEYS:(li + 1) * PEER_NKEYS, :])
    acc_scr[...] += out_prev + out_a

    @pl.when(s == pl.num_programs(1) - 1)
    def _():
        o_ref[...] = x1_ref[...] + (acc_scr[...] + _dot(vt_last_ref[...], p_prev[...])).T


def _experts(hnt, u_b, vt_b, rank2, cnt, e1, e2z, x1, tn):
    d, t = hnt.shape
    ne = u_b.shape[0]
    eb = PEER_I1_PER_BLOCK * PEER_NKEYS
    nblk = ne // eb
    once = pl.Buffered(1)
    full = pl.BlockSpec((PEER_HEADS, PEER_NKEYS, tn), lambda i, s: (0, 0, i))
    part = pl.BlockSpec((PEER_HEADS, 2 * PEER_I1_PER_BLOCK, tn), lambda i, s: (0, s, i))
    return pl.pallas_call(
        _expert_body,
        grid=(t // tn, nblk // 2),
        in_specs=[
            pl.BlockSpec((d, tn), lambda i, s: (0, i)),
            pl.BlockSpec((eb, d), lambda i, s: (0, 0), pipeline_mode=once),
            pl.BlockSpec((eb, d), lambda i, s: (2 * s + 1, 0)),
            pl.BlockSpec((eb, d), lambda i, s: (jnp.minimum(2 * s + 2, nblk - 1), 0)),
            pl.BlockSpec((d, eb), lambda i, s: (0, jnp.maximum(2 * s - 1, 0))),
            pl.BlockSpec((d, eb), lambda i, s: (0, 2 * s)),
            pl.BlockSpec((d, eb), lambda i, s: (0, nblk - 1), pipeline_mode=once),
            full, part, part, full,
            pl.BlockSpec((tn, d), lambda i, s: (i, 0), pipeline_mode=once),
        ],
        out_specs=pl.BlockSpec((tn, d), lambda i, s: (i, 0)),
        out_shape=jax.ShapeDtypeStruct((t, d), F32),
        scratch_shapes=[pltpu.VMEM((d, tn), F32), pltpu.VMEM((eb, tn), BF16), pltpu.VMEM((eb, tn), BF16)],
        compiler_params=_cparams(("parallel", "arbitrary")),
        name="peer_experts",
    )(hnt, u_b, u_b, u_b, vt_b, vt_b, vt_b, rank2, cnt, e1, e2z, x1)


def _suffix_sum_matrix(tk):
    r = jnp.arange(2 * tk)[:, None] % tk
    c = jnp.arange(2 * tk)[None, :]
    return jnp.where(c < tk, r > c, True).astype(BF16)


class _Tiles(NamedTuple):
    proj_tokens: int
    sb_q: int
    sb_k: int
    gdn_seq: int
    route_tokens: int
    expert_tokens: int


def _tiles(t, s):
    sb_k = 128
    return _Tiles(min(512, t), min(2 * sb_k, s), sb_k, min(512, s), min(256, t), min(512, t))


def _layer(x, norm1_g, w_in, sb_q_g, sb_k_g, sb_o_g, conv_w, a_log, dt_bias, gdn_o_g, w_out, norm2_g,
           w_q, keys1, keys2, u_tab, v_tab):
    b, s, d = x.shape
    t = b * s
    tiles = _tiles(t, s)
    n_main = 3 * SB_WIDTH + 4 * GDN_WIDTH
    x2 = x.reshape(t, d)
    w_main = w_in[:, :n_main].astype(BF16)
    w_ab = jnp.pad(w_in[:, n_main:], ((0, 0), (0, LANES - 2 * GDN_HEADS))).astype(BF16)
    proj, ab = _inproj(x2, norm1_g.reshape(1, d), w_main, w_ab, tiles.proj_tokens)
    proj3 = proj.reshape(b, s, n_main)
    ab3 = ab.reshape(b, s, LANES)

    tile2 = lambda g: jnp.tile(g, 2).reshape(1, LANES)
    sb_o = _sb_attention(proj3, tile2(sb_q_g), tile2(sb_k_g), tile2(sb_o_g), _suffix_sum_matrix(tiles.sb_k),
                         tiles.sb_q, tiles.sb_k)

    bcast = lambda p: jnp.broadcast_to(p[:, None, None], (GDN_HEADS, 1, LANES))
    gd_o = _gdn(proj3, ab3, conv_w, bcast(a_log), bcast(dt_bias), gdn_o_g.reshape(1, LANES), tiles.gdn_seq)

    wo = w_out.astype(BF16)
    x1, hnt, pqt = _outproj(sb_o.reshape(t, SB_WIDTH), gd_o.reshape(t, GDN_WIDTH), x2, wo[:SB_WIDTH],
                            wo[SB_WIDTH:], norm2_g.reshape(1, d), w_q.T.astype(BF16), tiles.proj_tokens)

    rank2, cnt, e1, e2z = _route(pqt, keys1.astype(BF16), keys2.astype(BF16), tiles.route_tokens)
    y = _experts(hnt, u_tab.astype(BF16), v_tab.T.astype(BF16), rank2, cnt, e1, e2z, x1, tiles.expert_tokens)
    return y.reshape(b, s, d)


def kernel(x, norm1_g, w_in, sb_q_norm_g, sb_k_norm_g, sb_out_norm_g, gdn_conv_w, gdn_a_log, gdn_dt_bias,
           gdn_out_norm_g, w_out, norm2_g, peer_w_q, peer_keys1, peer_keys2, peer_u, peer_v):
    for layer in range(norm1_g.shape[0]):
        x = _layer(x, norm1_g[layer], w_in[layer], sb_q_norm_g[layer], sb_k_norm_g[layer],
                   sb_out_norm_g[layer], gdn_conv_w[layer], gdn_a_log[layer], gdn_dt_bias[layer],
                   gdn_out_norm_g[layer], w_out[layer], norm2_g[layer], peer_w_q[layer],
                   peer_keys1[layer], peer_keys2[layer], peer_u[layer], peer_v[layer])
    return x
```

```python
import functools
from typing import NamedTuple

import jax
import jax.numpy as jnp
from jax import lax
from jax.experimental import pallas as pl
from jax.experimental.pallas import tpu as pltpu

F32 = jnp.float32
BF16 = jnp.bfloat16
EPS = 1e-6

SB_HEADS = 8
SB_HEAD_DIM = 64
SB_WIDTH = SB_HEADS * SB_HEAD_DIM
GDN_HEADS = 4
GDN_HEAD_DIM = 128
GDN_WIDTH = GDN_HEADS * GDN_HEAD_DIM
GDN_CONV = 4
PEER_HEADS = 8
PEER_NKEYS = 128
PEER_HALF = 128
PEER_TOPK = 16
LANES = 128

VMEM_LIMIT = 56 * 1024 * 1024


def _cparams(sem):
    return pltpu.CompilerParams(dimension_semantics=sem, vmem_limit_bytes=VMEM_LIMIT)


def _dot(a, b):
    return jnp.dot(a, b, preferred_element_type=F32)


def _dot_nt(a, b):
    return lax.dot_general(a, b, (((1,), (1,)), ((), ())), preferred_element_type=F32)


def _dot_tn(a, b):
    return lax.dot_general(a, b, (((0,), (0,)), ((), ())), preferred_element_type=F32)


def _dot_f32(a, b):
    return jnp.dot(a, b, preferred_element_type=F32, precision=lax.Precision.HIGHEST)


def _split(a):
    hi = a.astype(BF16)
    return hi, (a - hi.astype(F32)).astype(BF16)


def _dot3(a, b):
    ah, al = _split(a)
    bh, bl = _split(b)
    return _dot(jnp.concatenate([ah, ah, al], axis=1), jnp.concatenate([bh, bl, bh], axis=0))


def _softplus(x):
    return jnp.maximum(x, 0.0) + jnp.log1p(jnp.exp(-jnp.abs(x)))


def _sigmoid(x):
    return 1.0 / (1.0 + jnp.exp(-x))


def _inproj_body(x_ref, g_ref, w_ref, wab_ref, proj_ref, ab_ref):
    x = x_ref[...]
    ms = jnp.mean(x * x, axis=-1, keepdims=True)
    h = (x * lax.rsqrt(ms + EPS) * g_ref[...]).astype(BF16)
    proj_ref[...] = _dot(h, w_ref[...])
    ab_ref[...] = _dot(h, wab_ref[...])


def _inproj(x2, g, w_main, w_ab, tm):
    t, d = x2.shape
    n = w_main.shape[1]
    return pl.pallas_call(
        _inproj_body,
        grid=(t // tm,),
        in_specs=[
            pl.BlockSpec((tm, d), lambda i: (i, 0)),
            pl.BlockSpec((1, d), lambda i: (0, 0)),
            pl.BlockSpec((d, n), lambda i: (0, 0)),
            pl.BlockSpec((d, LANES), lambda i: (0, 0)),
        ],
        out_specs=[
            pl.BlockSpec((tm, n), lambda i: (i, 0)),
            pl.BlockSpec((tm, LANES), lambda i: (i, 0)),
        ],
        out_shape=[
            jax.ShapeDtypeStruct((t, n), F32),
            jax.ShapeDtypeStruct((t, LANES), F32),
        ],
        compiler_params=_cparams(("parallel",)),
        name="inproj",
    )(x2, g, w_main, w_ab)


SB_DEAD_LOG = -104.0


def _sb_body(q_ref, k_ref, v_ref, gq_ref, gk_ref, go_ref, m2_ref, o_ref, kn_scr, vb_scr, *, tq, tk):
    i = pl.program_id(2)
    lane = lax.broadcasted_iota(jnp.int32, (1, LANES), 1)
    is0 = lane < SB_HEAD_DIM

    def headnorm(x, g):
        x2 = x * x
        s0 = jnp.sum(jnp.where(is0, x2, 0.0), axis=-1, keepdims=True)
        s1 = jnp.sum(jnp.where(is0, 0.0, x2), axis=-1, keepdims=True)
        ms = jnp.where(is0, s0, s1) * (1.0 / SB_HEAD_DIM)
        return x * lax.rsqrt(ms + EPS) * g

    @pl.when(i == 0)
    def _():
        kn_scr[...] = headnorm(k_ref[...], gk_ref[...]).astype(BF16)
        vb_scr[...] = v_ref[...].astype(BF16)

    qn = headnorm(q_ref[...], gq_ref[...]) * (SB_HEAD_DIM ** -0.5)
    qh = (jnp.where(is0, qn, 0.0).astype(BF16), jnp.where(is0, 0.0, qn).astype(BF16))
    m2 = m2_ref[...]
    row = i * tq + lax.broadcasted_iota(jnp.int32, (tq, tk), 0)
    col0 = lax.broadcasted_iota(jnp.int32, (tq, tk), 1)
    nkb = (i + 1) * (tq // tk)

    def scores(j):
        kj = kn_scr[pl.ds(pl.multiple_of(j * tk, tk), tk), :]
        return [_dot_nt(qh[h], kj) for h in range(2)]

    def logs(j, zs, on_diagonal):
        causal = (col0 + j * tk) < row
        zls, cats = [], []
        for z in zs:
            lk = -(jnp.maximum(z, 0.0) + jnp.log(1.0 + jnp.exp(-jnp.abs(z))))
            zl = z + lk
            if on_diagonal:
                lk = jnp.where(causal, lk, 0.0)
                zl = jnp.where(causal, zl, -jnp.inf)
            hi = lk.astype(BF16)
            lo = (lk - hi.astype(F32)).astype(BF16)
            zls.append(zl)
            cats.append(jnp.concatenate([hi, lo], axis=1))
        return zls, cats

    def sums(cats):
        return [_dot(c, m2) for c in cats]

    def pair(n, carry, on_diagonal):
        j = nkb - 1 - 2 * n
        zs_a = scores(j)
        zs_b = scores(j - 1)
        zl_a, cats_a = logs(j, zs_a, on_diagonal)
        rt_a = sums(cats_a)
        zl_b, cats_b = logs(j - 1, zs_b, on_diagonal)
        rt_b = sums(cats_b)
        v2 = vb_scr[pl.ds(pl.multiple_of((j - 1) * tk, tk), 2 * tk), :]
        out = []
        for h in range(2):
            acc, rest = carry[2 * h], carry[2 * h + 1]
            rest_mid = rest + rt_a[h][:, tk:]
            w_a = jnp.exp(zl_a[h] + (rest + rt_a[h][:, :tk])).astype(BF16)
            w_b = jnp.exp(zl_b[h] + (rest_mid + rt_b[h][:, :tk])).astype(BF16)
            out += [acc + _dot(jnp.concatenate([w_b, w_a], axis=1), v2), rest_mid + rt_b[h][:, tk:]]
        alive = (jnp.max(jnp.maximum(out[1], out[3])) > SB_DEAD_LOG).astype(jnp.int32)
        return n + 1, alive, tuple(out)

    def cond(state):
        return jnp.logical_and(state[0] < nkb // 2, state[1] > 0)

    assert tq == 2 * tk
    zero = jnp.zeros((tq, LANES), F32)
    first = pair(0, (zero, zero, zero, zero), True)
    res = lax.while_loop(cond, lambda state: pair(state[0], state[2], False), first)[2]
    o = jnp.where(is0, res[0], res[2])
    o_ref[...] = headnorm(o, go_ref[...])


def _sb_attention(proj3, gq, gk, go, m2, tq, tk):
    b, s, _ = proj3.shape
    hp = SB_HEADS // 2
    return pl.pallas_call(
        functools.partial(_sb_body, tq=tq, tk=tk),
        grid=(b, hp, s // tq),
        in_specs=[
            pl.BlockSpec((None, tq, LANES), lambda bi, h, i: (bi, i, h)),
            pl.BlockSpec((None, s, LANES), lambda bi, h, i: (bi, 0, hp + h)),
            pl.BlockSpec((None, s, LANES), lambda bi, h, i: (bi, 0, 2 * hp + h)),
            pl.BlockSpec((1, LANES), lambda bi, h, i: (0, 0)),
            pl.BlockSpec((1, LANES), lambda bi, h, i: (0, 0)),
            pl.BlockSpec((1, LANES), lambda bi, h, i: (0, 0)),
            pl.BlockSpec((2 * tk, 2 * tk), lambda bi, h, i: (0, 0)),
        ],
        out_specs=pl.BlockSpec((None, tq, LANES), lambda bi, h, i: (bi, i, h)),
        out_shape=jax.ShapeDtypeStruct((b, s, SB_WIDTH), F32),
        scratch_shapes=[pltpu.VMEM((s, LANES), BF16), pltpu.VMEM((s, LANES), BF16)],
        compiler_params=_cparams(("parallel", "parallel", "arbitrary")),
        name="sb_attention",
    )(proj3, proj3, proj3, gq, gk, go, m2)


GDN_BLOCK = 128
GDN_HALO = 8


def _gdn_body(x_ref, halo_ref, z_ref, ab_ref, cw_ref, alog_ref, dtb_ref, og_ref, o_ref,
              q_scr, k_scr, v_scr, g_scr, beta_scr, state_scr, *, ts):
    t = pl.program_id(1)
    c = GDN_BLOCK
    nh = GDN_HEADS
    lane = lax.broadcasted_iota(jnp.int32, (1, LANES), 1)

    @pl.when(t == 0)
    def _():
        state_scr[...] = jnp.zeros_like(state_scr)

    halo = jnp.where(t > 0, halo_ref[...], 0.0)
    xe = jnp.concatenate([halo, x_ref[...]], axis=0)
    cw = cw_ref[...]
    y = xe * cw[3:4, :]
    for d in (1, 2, 3):
        y = y + pltpu.roll(xe, d, 0) * cw[3 - d:4 - d, :]
    y = y[GDN_HALO:, :]
    y = y * _sigmoid(y)

    def l2n(x):
        return x * lax.rsqrt(jnp.sum(x * x, axis=-1, keepdims=True) + EPS)

    ab = ab_ref[...]
    g_all = -jnp.exp(alog_ref[...]) * _softplus(ab + dtb_ref[...])
    beta_all = _sigmoid(ab)
    for h in range(nh):
        sl = pl.ds(h * LANES, LANES)
        q_scr[:, sl] = l2n(y[:, h * LANES:(h + 1) * LANES]) * (GDN_HEAD_DIM ** -0.5)
        k_scr[:, sl] = l2n(y[:, GDN_WIDTH + h * LANES:GDN_WIDTH + (h + 1) * LANES])
        g_col = jnp.sum(jnp.where(lane == h, g_all, 0.0), axis=-1, keepdims=True)
        beta_col = jnp.sum(jnp.where(lane == h + nh, beta_all, 0.0), axis=-1, keepdims=True)
        g_scr[:, sl] = jnp.broadcast_to(g_col, (ts, LANES))
        beta_scr[:, sl] = jnp.broadcast_to(beta_col, (ts, LANES))
    v_scr[...] = y[:, 2 * GDN_WIDTH:]

    ri = lax.broadcasted_iota(jnp.int32, (c, nh * c), 0)
    ci = lax.broadcasted_iota(jnp.int32, (c, nh * c), 1) % c
    lower_incl = ci <= ri
    lower_strict = ci < ri
    eye = (ci == ri).astype(F32)
    ltri = lower_incl[:, :c].astype(F32)
    ones = jnp.ones((c, c), F32)
    og = og_ref[...]
    heads = range(nh)
    hs = lambda m, h: m[:, h * c:(h + 1) * c]

    def blk(n, states):
        r0 = pl.multiple_of(n * c, c)
        q = q_scr[pl.ds(r0, c), :]
        k = k_scr[pl.ds(r0, c), :]
        v = v_scr[pl.ds(r0, c), :]
        g = g_scr[pl.ds(r0, c), :]
        beta = beta_scr[pl.ds(r0, c), :]
        gc = _dot_f32(ltri, g)
        gc_row = _dot_f32(ones, gc * eye)
        decay = jnp.exp(jnp.where(lower_incl, gc - gc_row, -jnp.inf))
        kb = k * beta
        kbf = k.astype(BF16)
        kbb = kb.astype(BF16)
        qbf = q.astype(BF16)
        kk = jnp.concatenate([_dot_nt(hs(kbb, h), hs(kbf, h)) for h in heads], axis=1)
        qk = jnp.concatenate([_dot_nt(hs(qbf, h), hs(kbf, h)) for h in heads], axis=1)
        a = jnp.where(lower_strict, kk * decay, 0.0)
        attn = jnp.where(lower_incl, qk * decay, 0.0).astype(BF16)
        nmat = [-hs(a, h) for h in heads]
        xp = [hs(a, h) for h in heads]
        for _ in range(6):
            xp = [_dot3(x, x) for x in xp]
            prod = [_dot3(nm, x) for nm, x in zip(nmat, xp)]
            nmat = [nm + x + p for nm, x, p in zip(nmat, xp, prod)]
        eg = jnp.exp(gc)
        vb = v * beta
        kbd = kb * eg
        gl = gc[c - 1:c, :]
        qd = (q * eg).astype(BF16)
        kd = (k * jnp.exp(gl - gc)).astype(BF16)
        dl = jnp.exp(gl)
        nb = [nm.astype(BF16) for nm in nmat]
        u = [hs(vb, h) + _dot(nb[h], hs(vb, h).astype(BF16)) for h in heads]
        w = [(hs(kbd, h) + _dot(nb[h], hs(kbd, h).astype(BF16))).astype(BF16) for h in heads]
        sb = [st.astype(BF16) for st in states]
        v_new = [u[h] - _dot(w[h], sb[h]) for h in heads]
        o_state = [_dot(hs(qd, h), sb[h]) for h in heads]
        vnb = [vn.astype(BF16) for vn in v_new]
        o = [o_state[h] + _dot(hs(attn, h), vnb[h]) for h in heads]
        new_states = tuple(states[h] * hs(dl, h) + _dot_tn(hs(kd, h), vnb[h]) for h in heads)
        on = jnp.concatenate(
            [x * lax.rsqrt(jnp.mean(x * x, axis=-1, keepdims=True) + EPS) * og for x in o], axis=1)
        z = z_ref[pl.ds(r0, c), :]
        o_ref[pl.ds(r0, c), :] = on * (z * _sigmoid(z))
        return new_states

    states = lax.fori_loop(0, ts // c, blk, tuple(state_scr[h] for h in heads))
    for h in heads:
        state_scr[h] = states[h]


def _gdn(proj3, ab3, conv_w, alog_b, dtb_b, og, ts):
    b, s, _ = proj3.shape
    w3 = 3 * GDN_WIDTH
    assert 3 * SB_WIDTH == w3 and 2 * w3 % GDN_WIDTH == 0
    per = ts // GDN_HALO
    par = pl.BlockSpec((1, LANES), lambda bi, t: (0, 0))
    return pl.pallas_call(
        functools.partial(_gdn_body, ts=ts),
        grid=(b, s // ts),
        in_specs=[
            pl.BlockSpec((None, ts, w3), lambda bi, t: (bi, t, 1)),
            pl.BlockSpec((None, GDN_HALO, w3), lambda bi, t: (bi, jnp.maximum(t * per - 1, 0), 1)),
            pl.BlockSpec((None, ts, GDN_WIDTH), lambda bi, t: (bi, t, 2 * w3 // GDN_WIDTH)),
            pl.BlockSpec((None, ts, LANES), lambda bi, t: (bi, t, 0)),
            pl.BlockSpec((GDN_CONV, w3), lambda bi, t: (0, 0)),
            par, par,
            pl.BlockSpec((1, LANES), lambda bi, t: (0, 0)),
        ],
        out_specs=pl.BlockSpec((None, ts, GDN_WIDTH), lambda bi, t: (bi, t, 0)),
        out_shape=jax.ShapeDtypeStruct((b, s, GDN_WIDTH), F32),
        scratch_shapes=[pltpu.VMEM((ts, GDN_WIDTH), F32) for _ in range(5)]
        + [pltpu.VMEM((GDN_HEADS, GDN_HEAD_DIM, GDN_HEAD_DIM), F32)],
        compiler_params=_cparams(("parallel", "arbitrary")),
        name="gdn",
    )(proj3, proj3, proj3, ab3, conv_w, alog_b, dtb_b, og)


def _outproj_body(sb_ref, gd_ref, x_ref, wo1_ref, wo2_ref, g2_ref, wqt_ref, x1_ref, hnt_ref, pqt_ref):
    mix = _dot(sb_ref[...].astype(BF16), wo1_ref[...]) + _dot(gd_ref[...].astype(BF16), wo2_ref[...])
    x1 = x_ref[...] + mix
    x1_ref[...] = x1
    ms = jnp.mean(x1 * x1, axis=-1, keepdims=True)
    hn = x1 * lax.rsqrt(ms + EPS) * g2_ref[...]
    hnt = hn.T.astype(BF16)
    hnt_ref[...] = hnt
    pqt_ref[...] = _dot(wqt_ref[...], hnt)


def _outproj(sb_o, gd_o, x2, wo1, wo2, g2, wqt, tm):
    t, d = x2.shape
    nq = wqt.shape[0]
    return pl.pallas_call(
        _outproj_body,
        grid=(t // tm,),
        in_specs=[
            pl.BlockSpec((tm, SB_WIDTH), lambda i: (i, 0)),
            pl.BlockSpec((tm, GDN_WIDTH), lambda i: (i, 0)),
            pl.BlockSpec((tm, d), lambda i: (i, 0)),
            pl.BlockSpec((SB_WIDTH, d), lambda i: (0, 0)),
            pl.BlockSpec((GDN_WIDTH, d), lambda i: (0, 0)),
            pl.BlockSpec((1, d), lambda i: (0, 0)),
            pl.BlockSpec((nq, d), lambda i: (0, 0)),
        ],
        out_specs=[
            pl.BlockSpec((tm, d), lambda i: (i, 0)),
            pl.BlockSpec((d, tm), lambda i: (0, i)),
            pl.BlockSpec((nq, tm), lambda i: (0, i)),
        ],
        out_shape=[
            jax.ShapeDtypeStruct((t, d), F32),
            jax.ShapeDtypeStruct((d, t), BF16),
            jax.ShapeDtypeStruct((nq, t), F32),
        ],
        compiler_params=_cparams(("parallel",)),
        name="outproj",
    )(sb_o, gd_o, x2, wo1, wo2, g2, wqt)


def _extract_topk(s, ids, k, want_rank):
    del want_rank
    big = jnp.int32(2 ** 30)
    work = s
    vals = []
    rank = jnp.full(s.shape, float(k), F32)
    for r in range(k):
        m = jnp.max(work, axis=0, keepdims=True)
        first = jnp.min(jnp.where(work == m, ids, big), axis=0, keepdims=True)
        hit = ids == first
        rank = jnp.where(hit, float(r), rank)
        work = jnp.where(hit, -jnp.inf, work)
        vals.append(m)
    return vals, rank, rank < float(k), None


def _extract_topk_untied(s, ids, k, want_rank):
    del ids
    work = s
    vals = []
    rank = jnp.full(s.shape, float(k), F32) if want_rank else None
    for r in range(k):
        m = jnp.max(work, axis=0, keepdims=True)
        hit = work == m
        if want_rank:
            rank = jnp.where(hit, float(r), rank)
        work = jnp.where(hit, -jnp.inf, work)
        vals.append(m)
    taken = rank < float(k) if want_rank else jnp.logical_and(work == -jnp.inf, s > -jnp.inf)
    count = jnp.sum(taken.astype(F32), axis=0, keepdims=True)
    return vals, rank, taken, count == float(k)


def _route_body(pqt_ref, k1_ref, k2_ref, rank2_ref, cnt_ref, e1_ref, e2z_ref):
    kk = PEER_TOPK
    tn = pqt_ref.shape[1]
    key_ids = lax.broadcasted_iota(jnp.int32, (PEER_NKEYS, tn), 0)
    i8 = lax.broadcasted_iota(jnp.int32, (8, tn), 0)
    i16 = lax.broadcasted_iota(jnp.int32, (kk, tn), 0)
    cand_ids = jnp.concatenate(
        [i16] + [i8 + a * kk for a in (1, 2, 3)]
        + [jnp.where(i8 >= 4, i8 * kk + b, kk * kk + i8 * kk + b) for b in (0, 1, 2)] + [(i8 + 8) * kk], axis=0)
    def route_head(h, s1, s2, extract, tie_rule):
        v1, rank1, _, ok1 = extract(s1, key_ids, kk, tie_rule)
        v2, rank2, _, ok2 = extract(s2, key_ids, kk, True)
        v1a = jnp.concatenate(v1, axis=0)
        v2a = jnp.concatenate(v2, axis=0)
        cand = jnp.concatenate(
            [v1[0] + v2a] + [v1[a] + v2a[:8] for a in (1, 2, 3)]
            + [jnp.where(i8 >= 4, v1a[:8] + v2[b], -jnp.inf) for b in (0, 1, 2)] + [v1a[8:] + v2[0]], axis=0)
        top, _, taken, ok3 = extract(cand, cand_ids, kk, False)
        zsum = jnp.ones_like(top[0])
        for r in range(1, kk):
            zsum = zsum + jnp.exp(top[r] - top[0])
        sel = taken.astype(F32)
        low = [jnp.sum(sel[0:16], axis=0, keepdims=True)] + [
            jnp.sum(sel[8 + 8 * a:16 + 8 * a], axis=0, keepdims=True) for a in (1, 2, 3)]
        mid = sel[40:48] + sel[48:56] + sel[56:64]
        high = sel[64:72]
        cnt = jnp.zeros(s1.shape, F32)
        for a in range(kk):
            n_a = low[a] if a < 4 else (mid[a:a + 1] if a < 8 else high[a - 8:a - 7])
            cnt = jnp.where(rank1 == float(a) if tie_rule else s1 == v1[a], n_a, cnt)
        rank2_ref[h] = rank2.astype(BF16)
        cnt_ref[h] = cnt
        e1_ref[h] = jnp.exp(s1 - v1[0])
        e2z_ref[h] = (jnp.exp(s2 - v2[0]) * (0.5 / zsum)).astype(BF16)
        return None if ok1 is None else jnp.logical_and(jnp.logical_and(ok1, ok2), ok3)

    for h in range(PEER_HEADS):
        q1 = pqt_ref[pl.ds(h * 2 * PEER_HALF, PEER_HALF), :].astype(BF16)
        q2 = pqt_ref[pl.ds(h * 2 * PEER_HALF + PEER_HALF, PEER_HALF), :].astype(BF16)
        s1 = _dot(k1_ref[h], q1)
        s2 = _dot(k2_ref[h], q2)
        ok = route_head(h, s1, s2, _extract_topk_untied, False)
        tied = jnp.max(jnp.where(ok, 0.0, 1.0)) > 0.0

        @pl.when(tied)
        def _():
            route_head(h, s1, s2, _extract_topk, True)


def _route(pqt, k1, k2, tn):
    nq, t = pqt.shape
    hk = (PEER_HEADS, PEER_NKEYS, PEER_HALF)
    out = lambda dt: jax.ShapeDtypeStruct((PEER_HEADS, PEER_NKEYS, t), dt)
    ospec = pl.BlockSpec((PEER_HEADS, PEER_NKEYS, tn), lambda i: (0, 0, i))
    return pl.pallas_call(
        _route_body,
        grid=(t // tn,),
        in_specs=[
            pl.BlockSpec((nq, tn), lambda i: (0, i)),
            pl.BlockSpec(hk, lambda i: (0, 0, 0)),
            pl.BlockSpec(hk, lambda i: (0, 0, 0)),
        ],
        out_specs=[ospec, ospec, ospec, ospec],
        out_shape=[out(BF16), out(F32), out(F32), out(BF16)],
        compiler_params=_cparams(("parallel",)),
        name="peer_route",
    )(pqt, k1, k2)


PEER_I1_PER_BLOCK = 8
BF16_ROWS = 16


def _expert_body(hnt_ref, u_first_ref, u_b_ref, u_next_ref, vt_prev_ref, vt_a_ref, vt_last_ref, rank2_ref,
                 cnt_ref, e1_ref, e2z_ref, x1_ref, o_ref, acc_scr, act_next, p_prev):
    s = pl.program_id(1)
    hnt = hnt_ref[...]
    tn = hnt.shape[1]
    tiles = PEER_NKEYS // BF16_ROWS

    def activation(u_ref):
        pre = _dot(u_ref[...], hnt)
        return (pre * (1.0 + lax.erf(pre * (2.0 ** -0.5)))).astype(BF16)

    def row(ref, h, l):
        return jnp.broadcast_to(ref[h, l:l + 1, :], (BF16_ROWS, tn)).astype(BF16)[None]

    def gates(l0, act_rows):
        ps = []
        for li in range(PEER_I1_PER_BLOCK):
            gate = None
            for h in range(PEER_HEADS):
                rank2 = rank2_ref[h].reshape(tiles, BF16_ROWS, tn)
                e2z = e2z_ref[h].reshape(tiles, BF16_ROWS, tn)
                term = jnp.where(rank2 < row(cnt_ref, h, l0 + li), e2z * row(e1_ref, h, l0 + li), 0.0)
                gate = term if gate is None else gate + term
            ps.append(gate.reshape(PEER_NKEYS, tn) * act_rows(li))
        return jnp.concatenate(ps, axis=0)

    @pl.when(s == 0)
    def _():
        acc_scr[...] = jnp.zeros_like(acc_scr)
        p_prev[...] = jnp.zeros_like(p_prev)
        act_next[...] = activation(u_first_ref)

    out_prev = _dot(vt_prev_ref[...], p_prev[...])
    act_b = activation(u_b_ref)
    p_a = gates(0, lambda li: act_next[pl.ds(li * PEER_NKEYS, PEER_NKEYS), :])
    out_a = _dot(vt_a_ref[...], p_a)
    act_next[...] = activation(u_next_ref)
    p_prev[...] = gates(PEER_I1_PER_BLOCK, lambda li: act_b[li * PEER_NKEYS:(li + 1) * PEER_NKEYS, :])
    acc_scr[...] += out_prev + out_a

    @pl.when(s == pl.num_programs(1) - 1)
    def _():
        o_ref[...] = x1_ref[...] + (acc_scr[...] + _dot(vt_last_ref[...], p_prev[...])).T


def _experts(hnt, u_b, vt_b, rank2, cnt, e1, e2z, x1, tn):
    d, t = hnt.shape
    ne = u_b.shape[0]
    eb = PEER_I1_PER_BLOCK * PEER_NKEYS
    nblk = ne // eb
    once = pl.Buffered(1)
    full = pl.BlockSpec((PEER_HEADS, PEER_NKEYS, tn), lambda i, s: (0, 0, i))
    part = pl.BlockSpec((PEER_HEADS, 2 * PEER_I1_PER_BLOCK, tn), lambda i, s: (0, s, i))
    return pl.pallas_call(
        _expert_body,
        grid=(t // tn, nblk // 2),
        in_specs=[
            pl.BlockSpec((d, tn), lambda i, s: (0, i)),
            pl.BlockSpec((eb, d), lambda i, s: (0, 0), pipeline_mode=once),
            pl.BlockSpec((eb, d), lambda i, s: (2 * s + 1, 0)),
            pl.BlockSpec((eb, d), lambda i, s: (jnp.minimum(2 * s + 2, nblk - 1), 0)),
            pl.BlockSpec((d, eb), lambda i, s: (0, jnp.maximum(2 * s - 1, 0))),
            pl.BlockSpec((d, eb), lambda i, s: (0, 2 * s)),
            pl.BlockSpec((d, eb), lambda i, s: (0, nblk - 1), pipeline_mode=once),
            full, part, part, full,
            pl.BlockSpec((tn, d), lambda i, s: (i, 0), pipeline_mode=once),
        ],
        out_specs=pl.BlockSpec((tn, d), lambda i, s: (i, 0)),
        out_shape=jax.ShapeDtypeStruct((t, d), F32),
        scratch_shapes=[pltpu.VMEM((d, tn), F32), pltpu.VMEM((eb, tn), BF16), pltpu.VMEM((eb, tn), BF16)],
        compiler_params=_cparams(("parallel", "arbitrary")),
        name="peer_experts",
    )(hnt, u_b, u_b, u_b, vt_b, vt_b, vt_b, rank2, cnt, e1, e2z, x1)


def _suffix_sum_matrix(tk):
    r = jnp.arange(2 * tk)[:, None] % tk
    c = jnp.arange(2 * tk)[None, :]
    return jnp.where(c < tk, r > c, True).astype(BF16)


class _Tiles(NamedTuple):
    proj_tokens: int
    sb_q: int
    sb_k: int
    gdn_seq: int
    route_tokens: int
    expert_tokens: int


def _tiles(t, s):
    sb_k = 128
    return _Tiles(min(512, t), min(2 * sb_k, s), sb_k, min(512, s), min(256, t), min(512, t))


def _layer(x, norm1_g, w_in, sb_q_g, sb_k_g, sb_o_g, conv_w, a_log, dt_bias, gdn_o_g, w_out, norm2_g,
           w_q, keys1, keys2, u_tab, v_tab):
    b, s, d = x.shape
    t = b * s
    tiles = _tiles(t, s)
    n_main = 3 * SB_WIDTH + 4 * GDN_WIDTH
    x2 = x.reshape(t, d)
    w_main = w_in[:, :n_main].astype(BF16)
    w_ab = jnp.pad(w_in[:, n_main:], ((0, 0), (0, LANES - 2 * GDN_HEADS))).astype(BF16)
    proj, ab = _inproj(x2, norm1_g.reshape(1, d), w_main, w_ab, tiles.proj_tokens)
    proj3 = proj.reshape(b, s, n_main)
    ab3 = ab.reshape(b, s, LANES)

    tile2 = lambda g: jnp.tile(g, 2).reshape(1, LANES)
    sb_o = _sb_attention(proj3, tile2(sb_q_g), tile2(sb_k_g), tile2(sb_o_g), _suffix_sum_matrix(tiles.sb_k),
                         tiles.sb_q, tiles.sb_k)

    on_lanes = lambda p: jnp.pad(p, (0, LANES - GDN_HEADS)).reshape(1, LANES)
    gd_o = _gdn(proj3, ab3, conv_w, on_lanes(a_log), on_lanes(dt_bias), gdn_o_g.reshape(1, LANES), tiles.gdn_seq)

    wo = w_out.astype(BF16)
    x1, hnt, pqt = _outproj(sb_o.reshape(t, SB_WIDTH), gd_o.reshape(t, GDN_WIDTH), x2, wo[:SB_WIDTH],
                            wo[SB_WIDTH:], norm2_g.reshape(1, d), w_q.T.astype(BF16), tiles.proj_tokens)

    rank2, cnt, e1, e2z = _route(pqt, keys1.astype(BF16), keys2.astype(BF16), tiles.route_tokens)
    y = _experts(hnt, u_tab.astype(BF16), v_tab.T.astype(BF16), rank2, cnt, e1, e2z, x1, tiles.expert_tokens)
    return y.reshape(b, s, d)


def kernel(x, norm1_g, w_in, sb_q_norm_g, sb_k_norm_g, sb_out_norm_g, gdn_conv_w, gdn_a_log, gdn_dt_bias,
           gdn_out_norm_g, w_out, norm2_g, peer_w_q, peer_keys1, peer_keys2, peer_u, peer_v):
    for layer in range(norm1_g.shape[0]):
        x = _layer(x, norm1_g[layer], w_in[layer], sb_q_norm_g[layer], sb_k_norm_g[layer],
                   sb_out_norm_g[layer], gdn_conv_w[layer], gdn_a_log[layer], gdn_dt_bias[layer],
                   gdn_out_norm_g[layer], w_out[layer], norm2_g[layer], peer_w_q[layer],
                   peer_keys1[layer], peer_keys2[layer], peer_u[layer], peer_v[layer])
    return x
```

```python
import functools
from typing import NamedTuple

import jax
import jax.numpy as jnp
from jax import lax
from jax.experimental import pallas as pl
from jax.experimental.pallas import tpu as pltpu

F32 = jnp.float32
BF16 = jnp.bfloat16
EPS = 1e-6

SB_HEADS = 8
SB_HEAD_DIM = 64
SB_WIDTH = SB_HEADS * SB_HEAD_DIM
GDN_HEADS = 4
GDN_HEAD_DIM = 128
GDN_WIDTH = GDN_HEADS * GDN_HEAD_DIM
GDN_CONV = 4
PEER_HEADS = 8
PEER_NKEYS = 128
PEER_HALF = 128
PEER_TOPK = 16
LANES = 128

VMEM_LIMIT = 56 * 1024 * 1024


def _cparams(sem):
    return pltpu.CompilerParams(dimension_semantics=sem, vmem_limit_bytes=VMEM_LIMIT)


def _dot(a, b):
    return jnp.dot(a, b, preferred_element_type=F32)


def _dot_nt(a, b):
    return lax.dot_general(a, b, (((1,), (1,)), ((), ())), preferred_element_type=F32)


def _dot_tn(a, b):
    return lax.dot_general(a, b, (((0,), (0,)), ((), ())), preferred_element_type=F32)


def _dot_f32(a, b):
    return jnp.dot(a, b, preferred_element_type=F32, precision=lax.Precision.HIGHEST)


def _split(a):
    hi = a.astype(BF16)
    return hi, (a - hi.astype(F32)).astype(BF16)


def _dot3(a, b):
    ah, al = _split(a)
    bh, bl = _split(b)
    return _dot(jnp.concatenate([ah, ah, al], axis=1), jnp.concatenate([bh, bl, bh], axis=0))


def _softplus(x):
    return jnp.maximum(x, 0.0) + jnp.log1p(jnp.exp(-jnp.abs(x)))


def _sigmoid(x):
    return 1.0 / (1.0 + jnp.exp(-x))


def _inproj_body(x_ref, g_ref, w_ref, wab_ref, proj_ref, ab_ref):
    x = x_ref[...]
    ms = jnp.mean(x * x, axis=-1, keepdims=True)
    h = (x * lax.rsqrt(ms + EPS) * g_ref[...]).astype(BF16)
    proj_ref[...] = _dot(h, w_ref[...])
    ab_ref[...] = _dot(h, wab_ref[...])


def _inproj(x2, g, w_main, w_ab, tm):
    t, d = x2.shape
    n = w_main.shape[1]
    return pl.pallas_call(
        _inproj_body,
        grid=(t // tm,),
        in_specs=[
            pl.BlockSpec((tm, d), lambda i: (i, 0)),
            pl.BlockSpec((1, d), lambda i: (0, 0)),
            pl.BlockSpec((d, n), lambda i: (0, 0)),
            pl.BlockSpec((d, LANES), lambda i: (0, 0)),
        ],
        out_specs=[
            pl.BlockSpec((tm, n), lambda i: (i, 0)),
            pl.BlockSpec((tm, LANES), lambda i: (i, 0)),
        ],
        out_shape=[
            jax.ShapeDtypeStruct((t, n), F32),
            jax.ShapeDtypeStruct((t, LANES), F32),
        ],
        compiler_params=_cparams(("parallel",)),
        name="inproj",
    )(x2, g, w_main, w_ab)


SB_DEAD_LOG = -104.0


def _sb_body(q_ref, k_ref, v_ref, gq_ref, gk_ref, go_ref, m2_ref, o_ref, kn_scr, vb_scr, *, tq, tk):
    i = pl.program_id(2)
    lane = lax.broadcasted_iota(jnp.int32, (1, LANES), 1)
    is0 = lane < SB_HEAD_DIM

    def headnorm(x, g):
        x2 = x * x
        s0 = jnp.sum(jnp.where(is0, x2, 0.0), axis=-1, keepdims=True)
        s1 = jnp.sum(jnp.where(is0, 0.0, x2), axis=-1, keepdims=True)
        ms = jnp.where(is0, s0, s1) * (1.0 / SB_HEAD_DIM)
        return x * lax.rsqrt(ms + EPS) * g

    @pl.when(i == 0)
    def _():
        kn_scr[...] = headnorm(k_ref[...], gk_ref[...]).astype(BF16)
        vb_scr[...] = v_ref[...].astype(BF16)

    qn = headnorm(q_ref[...], gq_ref[...]) * (SB_HEAD_DIM ** -0.5)
    q2h = jnp.concatenate([jnp.where(is0, qn, 0.0), jnp.where(is0, 0.0, qn)], axis=0).astype(BF16)
    halves = lambda x: [x[:tq], x[tq:]]
    m2 = m2_ref[...]
    row = i * tq + lax.broadcasted_iota(jnp.int32, (tq, tk), 0)
    col0 = lax.broadcasted_iota(jnp.int32, (tq, tk), 1)
    nkb = (i + 1) * (tq // tk)

    def scores(j):
        kj = kn_scr[pl.ds(pl.multiple_of(j * tk, tk), tk), :]
        return halves(_dot_nt(q2h, kj))

    def logs(j, zs, on_diagonal):
        causal = (col0 + j * tk) < row
        zls, cats = [], []
        for z in zs:
            lk = -(jnp.maximum(z, 0.0) + jnp.log(1.0 + jnp.exp(-jnp.abs(z))))
            zl = z + lk
            if on_diagonal:
                lk = jnp.where(causal, lk, 0.0)
                zl = jnp.where(causal, zl, -jnp.inf)
            hi = lk.astype(BF16)
            lo = (lk - hi.astype(F32)).astype(BF16)
            zls.append(zl)
            cats.append(jnp.concatenate([hi, lo], axis=1))
        return zls, cats

    def sums(cats):
        return halves(_dot(jnp.concatenate(cats, axis=0), m2))

    def pair(n, carry, on_diagonal):
        j = nkb - 1 - 2 * n
        zs_a = scores(j)
        zs_b = scores(j - 1)
        zl_a, cats_a = logs(j, zs_a, on_diagonal)
        rt_a = sums(cats_a)
        zl_b, cats_b = logs(j - 1, zs_b, on_diagonal)
        rt_b = sums(cats_b)
        v2 = vb_scr[pl.ds(pl.multiple_of((j - 1) * tk, tk), 2 * tk), :]
        ws, rests = [], []
        for h in range(2):
            rest = carry[2 * h + 1]
            rest_mid = rest + rt_a[h][:, tk:]
            w_a = jnp.exp(zl_a[h] + (rest + rt_a[h][:, :tk])).astype(BF16)
            w_b = jnp.exp(zl_b[h] + (rest_mid + rt_b[h][:, :tk])).astype(BF16)
            ws.append(jnp.concatenate([w_b, w_a], axis=1))
            rests.append(rest_mid + rt_b[h][:, tk:])
        pv = halves(_dot(jnp.concatenate(ws, axis=0), v2))
        out = [carry[0] + pv[0], rests[0], carry[2] + pv[1], rests[1]]
        alive = (jnp.max(jnp.maximum(out[1], out[3])) > SB_DEAD_LOG).astype(jnp.int32)
        return n + 1, alive, tuple(out)

    def cond(state):
        return jnp.logical_and(state[0] < nkb // 2, state[1] > 0)

    assert tq == 2 * tk
    zero = jnp.zeros((tq, LANES), F32)
    first = pair(0, (zero, zero, zero, zero), True)
    res = lax.while_loop(cond, lambda state: pair(state[0], state[2], False), first)[2]
    o = jnp.where(is0, res[0], res[2])
    o_ref[...] = headnorm(o, go_ref[...])


def _sb_attention(proj3, gq, gk, go, m2, tq, tk):
    b, s, _ = proj3.shape
    hp = SB_HEADS // 2
    return pl.pallas_call(
        functools.partial(_sb_body, tq=tq, tk=tk),
        grid=(b, hp, s // tq),
        in_specs=[
            pl.BlockSpec((None, tq, LANES), lambda bi, h, i: (bi, i, h)),
            pl.BlockSpec((None, s, LANES), lambda bi, h, i: (bi, 0, hp + h)),
            pl.BlockSpec((None, s, LANES), lambda bi, h, i: (bi, 0, 2 * hp + h)),
            pl.BlockSpec((1, LANES), lambda bi, h, i: (0, 0)),
            pl.BlockSpec((1, LANES), lambda bi, h, i: (0, 0)),
            pl.BlockSpec((1, LANES), lambda bi, h, i: (0, 0)),
            pl.BlockSpec((2 * tk, 2 * tk), lambda bi, h, i: (0, 0)),
        ],
        out_specs=pl.BlockSpec((None, tq, LANES), lambda bi, h, i: (bi, i, h)),
        out_shape=jax.ShapeDtypeStruct((b, s, SB_WIDTH), F32),
        scratch_shapes=[pltpu.VMEM((s, LANES), BF16), pltpu.VMEM((s, LANES), BF16)],
        compiler_params=_cparams(("parallel", "parallel", "arbitrary")),
        name="sb_attention",
    )(proj3, proj3, proj3, gq, gk, go, m2)


GDN_BLOCK = 128
GDN_HALO = 8


def _gdn_body(x_ref, halo_ref, z_ref, ab_ref, cw_ref, alog_ref, dtb_ref, og_ref, o_ref,
              q_scr, k_scr, v_scr, g_scr, beta_scr, state_scr, *, ts):
    t = pl.program_id(1)
    c = GDN_BLOCK
    nh = GDN_HEADS
    lane = lax.broadcasted_iota(jnp.int32, (1, LANES), 1)

    @pl.when(t == 0)
    def _():
        state_scr[...] = jnp.zeros_like(state_scr)

    halo = jnp.where(t > 0, halo_ref[...], 0.0)
    xe = jnp.concatenate([halo, x_ref[...]], axis=0)
    cw = cw_ref[...]
    y = xe * cw[3:4, :]
    for d in (1, 2, 3):
        y = y + pltpu.roll(xe, d, 0) * cw[3 - d:4 - d, :]
    y = y[GDN_HALO:, :]
    y = y * _sigmoid(y)

    def l2n(x):
        return x * lax.rsqrt(jnp.sum(x * x, axis=-1, keepdims=True) + EPS)

    ab = ab_ref[...]
    g_all = -jnp.exp(alog_ref[...]) * _softplus(ab + dtb_ref[...])
    beta_all = _sigmoid(ab)
    for h in range(nh):
        sl = pl.ds(h * LANES, LANES)
        q_scr[:, sl] = l2n(y[:, h * LANES:(h + 1) * LANES]) * (GDN_HEAD_DIM ** -0.5)
        k_scr[:, sl] = l2n(y[:, GDN_WIDTH + h * LANES:GDN_WIDTH + (h + 1) * LANES])
        g_col = jnp.sum(jnp.where(lane == h, g_all, 0.0), axis=-1, keepdims=True)
        beta_col = jnp.sum(jnp.where(lane == h + nh, beta_all, 0.0), axis=-1, keepdims=True)
        g_scr[:, sl] = jnp.broadcast_to(g_col, (ts, LANES))
        beta_scr[:, sl] = jnp.broadcast_to(beta_col, (ts, LANES))
    v_scr[...] = y[:, 2 * GDN_WIDTH:]

    ri = lax.broadcasted_iota(jnp.int32, (c, nh * c), 0)
    ci = lax.broadcasted_iota(jnp.int32, (c, nh * c), 1) % c
    lower_incl = ci <= ri
    lower_strict = ci < ri
    eye = (ci == ri).astype(F32)
    ltri = lower_incl[:, :c].astype(F32)
    ones = jnp.ones((c, c), F32)
    og = og_ref[...]
    heads = range(nh)
    hs = lambda m, h: m[:, h * c:(h + 1) * c]

    def blk(n, states):
        r0 = pl.multiple_of(n * c, c)
        q = q_scr[pl.ds(r0, c), :]
        k = k_scr[pl.ds(r0, c), :]
        v = v_scr[pl.ds(r0, c), :]
        g = g_scr[pl.ds(r0, c), :]
        beta = beta_scr[pl.ds(r0, c), :]
        gc = _dot_f32(ltri, g)
        gc_row = _dot_f32(ones, gc * eye)
        decay = jnp.exp(jnp.where(lower_incl, gc - gc_row, -jnp.inf))
        kb = k * beta
        kbf = k.astype(BF16)
        kbb = kb.astype(BF16)
        qbf = q.astype(BF16)
        kq = [_dot_nt(jnp.concatenate([hs(kbb, h), hs(qbf, h)], axis=0), hs(kbf, h)) for h in heads]
        kk = jnp.concatenate([x[:c] for x in kq], axis=1)
        qk = jnp.concatenate([x[c:] for x in kq], axis=1)
        a = jnp.where(lower_strict, kk * decay, 0.0)
        attn = jnp.where(lower_incl, qk * decay, 0.0).astype(BF16)
        nmat = [-hs(a, h) for h in heads]
        xp = [_dot3(hs(a, h), hs(a, h)) for h in heads]
        for _ in range(5):
            both = [_dot3(jnp.concatenate([nm, x], axis=0), x) for nm, x in zip(nmat, xp)]
            nmat = [nm + x + nx[:c] for nm, x, nx in zip(nmat, xp, both)]
            xp = [nx[c:] for nx in both]
        prod = [_dot3(nm, x) for nm, x in zip(nmat, xp)]
        nmat = [nm + x + p for nm, x, p in zip(nmat, xp, prod)]
        eg = jnp.exp(gc)
        vb = v * beta
        kbd = kb * eg
        gl = gc[c - 1:c, :]
        qd = (q * eg).astype(BF16)
        kd = (k * jnp.exp(gl - gc)).astype(BF16)
        dl = jnp.exp(gl)
        uw_rhs = [jnp.concatenate([hs(vb, h), hs(kbd, h)], axis=1) for h in heads]
        uw = [r + _dot(nm.astype(BF16), r.astype(BF16)) for nm, r in zip(nmat, uw_rhs)]
        u = [x[:, :c] for x in uw]
        w = [x[:, c:].astype(BF16) for x in uw]
        sb = [st.astype(BF16) for st in states]
        ws = [_dot(jnp.concatenate([w[h], hs(qd, h)], axis=0), sb[h]) for h in heads]
        v_new = [u[h] - ws[h][:c] for h in heads]
        vnb = [vn.astype(BF16) for vn in v_new]
        o = [ws[h][c:] + _dot(hs(attn, h), vnb[h]) for h in heads]
        new_states = tuple(states[h] * hs(dl, h) + _dot_tn(hs(kd, h), vnb[h]) for h in heads)
        on = jnp.concatenate(
            [x * lax.rsqrt(jnp.mean(x * x, axis=-1, keepdims=True) + EPS) * og for x in o], axis=1)
        z = z_ref[pl.ds(r0, c), :]
        o_ref[pl.ds(r0, c), :] = on * (z * _sigmoid(z))
        return new_states

    states = lax.fori_loop(0, ts // c, blk, tuple(state_scr[h] for h in heads))
    for h in heads:
        state_scr[h] = states[h]


def _gdn(proj3, ab3, conv_w, alog_b, dtb_b, og, ts):
    b, s, _ = proj3.shape
    w3 = 3 * GDN_WIDTH
    assert 3 * SB_WIDTH == w3 and 2 * w3 % GDN_WIDTH == 0
    per = ts // GDN_HALO
    par = pl.BlockSpec((1, LANES), lambda bi, t: (0, 0))
    return pl.pallas_call(
        functools.partial(_gdn_body, ts=ts),
        grid=(b, s // ts),
        in_specs=[
            pl.BlockSpec((None, ts, w3), lambda bi, t: (bi, t, 1)),
            pl.BlockSpec((None, GDN_HALO, w3), lambda bi, t: (bi, jnp.maximum(t * per - 1, 0), 1)),
            pl.BlockSpec((None, ts, GDN_WIDTH), lambda bi, t: (bi, t, 2 * w3 // GDN_WIDTH)),
            pl.BlockSpec((None, ts, LANES), lambda bi, t: (bi, t, 0)),
            pl.BlockSpec((GDN_CONV, w3), lambda bi, t: (0, 0)),
            par, par,
            pl.BlockSpec((1, LANES), lambda bi, t: (0, 0)),
        ],
        out_specs=pl.BlockSpec((None, ts, GDN_WIDTH), lambda bi, t: (bi, t, 0)),
        out_shape=jax.ShapeDtypeStruct((b, s, GDN_WIDTH), F32),
        scratch_shapes=[pltpu.VMEM((ts, GDN_WIDTH), F32) for _ in range(5)]
        + [pltpu.VMEM((GDN_HEADS, GDN_HEAD_DIM, GDN_HEAD_DIM), F32)],
        compiler_params=_cparams(("parallel", "arbitrary")),
        name="gdn",
    )(proj3, proj3, proj3, ab3, conv_w, alog_b, dtb_b, og)


def _outproj_body(sb_ref, gd_ref, x_ref, wo1_ref, wo2_ref, g2_ref, wqt_ref, x1_ref, hnt_ref, pqt_ref):
    mix = _dot(sb_ref[...].astype(BF16), wo1_ref[...]) + _dot(gd_ref[...].astype(BF16), wo2_ref[...])
    x1 = x_ref[...] + mix
    x1_ref[...] = x1
    ms = jnp.mean(x1 * x1, axis=-1, keepdims=True)
    hn = x1 * lax.rsqrt(ms + EPS) * g2_ref[...]
    hnt = hn.T.astype(BF16)
    hnt_ref[...] = hnt
    pqt_ref[...] = _dot(wqt_ref[...], hnt)


def _outproj(sb_o, gd_o, x2, wo1, wo2, g2, wqt, tm):
    t, d = x2.shape
    nq = wqt.shape[0]
    return pl.pallas_call(
        _outproj_body,
        grid=(t // tm,),
        in_specs=[
            pl.BlockSpec((tm, SB_WIDTH), lambda i: (i, 0)),
            pl.BlockSpec((tm, GDN_WIDTH), lambda i: (i, 0)),
            pl.BlockSpec((tm, d), lambda i: (i, 0)),
            pl.BlockSpec((SB_WIDTH, d), lambda i: (0, 0)),
            pl.BlockSpec((GDN_WIDTH, d), lambda i: (0, 0)),
            pl.BlockSpec((1, d), lambda i: (0, 0)),
            pl.BlockSpec((nq, d), lambda i: (0, 0)),
        ],
        out_specs=[
            pl.BlockSpec((tm, d), lambda i: (i, 0)),
            pl.BlockSpec((d, tm), lambda i: (0, i)),
            pl.BlockSpec((nq, tm), lambda i: (0, i)),
        ],
        out_shape=[
            jax.ShapeDtypeStruct((t, d), F32),
            jax.ShapeDtypeStruct((d, t), BF16),
            jax.ShapeDtypeStruct((nq, t), F32),
        ],
        compiler_params=_cparams(("parallel",)),
        name="outproj",
    )(sb_o, gd_o, x2, wo1, wo2, g2, wqt)


def _extract_topk(s, ids, k, want_rank):
    del want_rank
    big = jnp.int32(2 ** 30)
    work = s
    vals = []
    rank = jnp.full(s.shape, float(k), F32)
    for r in range(k):
        m = jnp.max(work, axis=0, keepdims=True)
        first = jnp.min(jnp.where(work == m, ids, big), axis=0, keepdims=True)
        hit = ids == first
        rank = jnp.where(hit, float(r), rank)
        work = jnp.where(hit, -jnp.inf, work)
        vals.append(m)
    return vals, rank, rank < float(k), None


def _extract_topk_untied(s, ids, k, want_rank):
    del ids
    work = s
    vals = []
    rank = jnp.full(s.shape, float(k), F32) if want_rank else None
    for r in range(k):
        m = jnp.max(work, axis=0, keepdims=True)
        hit = work == m
        if want_rank:
            rank = jnp.where(hit, float(r), rank)
        work = jnp.where(hit, -jnp.inf, work)
        vals.append(m)
    taken = rank < float(k) if want_rank else jnp.logical_and(work == -jnp.inf, s > -jnp.inf)
    count = jnp.sum(taken.astype(F32), axis=0, keepdims=True)
    return vals, rank, taken, count == float(k)


def _route_body(pqt_ref, k1_ref, k2_ref, rank2_ref, cnt_ref, e1_ref, e2z_ref):
    kk = PEER_TOPK
    tn = pqt_ref.shape[1]
    key_ids = lax.broadcasted_iota(jnp.int32, (PEER_NKEYS, tn), 0)
    i8 = lax.broadcasted_iota(jnp.int32, (8, tn), 0)
    i16 = lax.broadcasted_iota(jnp.int32, (kk, tn), 0)
    cand_ids = jnp.concatenate(
        [i16] + [i8 + a * kk for a in (1, 2, 3)]
        + [jnp.where(i8 >= 4, i8 * kk + b, kk * kk + i8 * kk + b) for b in (0, 1, 2)] + [(i8 + 8) * kk], axis=0)
    def route_head(h, s1, s2, extract, tie_rule):
        v1, rank1, _, ok1 = extract(s1, key_ids, kk, tie_rule)
        v2, rank2, _, ok2 = extract(s2, key_ids, kk, True)
        v1a = jnp.concatenate(v1, axis=0)
        v2a = jnp.concatenate(v2, axis=0)
        cand = jnp.concatenate(
            [v1[0] + v2a] + [v1[a] + v2a[:8] for a in (1, 2, 3)]
            + [jnp.where(i8 >= 4, v1a[:8] + v2[b], -jnp.inf) for b in (0, 1, 2)] + [v1a[8:] + v2[0]], axis=0)
        top, _, taken, ok3 = extract(cand, cand_ids, kk, False)
        zsum = jnp.ones_like(top[0])
        for r in range(1, kk):
            zsum = zsum + jnp.exp(top[r] - top[0])
        sel = taken.astype(F32)
        low = [jnp.sum(sel[0:16], axis=0, keepdims=True)] + [
            jnp.sum(sel[8 + 8 * a:16 + 8 * a], axis=0, keepdims=True) for a in (1, 2, 3)]
        mid = sel[40:48] + sel[48:56] + sel[56:64]
        high = sel[64:72]
        cnt = jnp.zeros(s1.shape, F32)
        for a in range(kk):
            n_a = low[a] if a < 4 else (mid[a:a + 1] if a < 8 else high[a - 8:a - 7])
            cnt = jnp.where(rank1 == float(a) if tie_rule else s1 == v1[a], n_a, cnt)
        rank2_ref[h] = rank2.astype(BF16)
        cnt_ref[h] = cnt
        e1_ref[h] = jnp.exp(s1 - v1[0])
        e2z_ref[h] = (jnp.exp(s2 - v2[0]) * (0.5 / zsum)).astype(BF16)
        return None if ok1 is None else jnp.logical_and(jnp.logical_and(ok1, ok2), ok3)

    for h in range(PEER_HEADS):
        q1 = pqt_ref[pl.ds(h * 2 * PEER_HALF, PEER_HALF), :].astype(BF16)
        q2 = pqt_ref[pl.ds(h * 2 * PEER_HALF + PEER_HALF, PEER_HALF), :].astype(BF16)
        s1 = _dot(k1_ref[h], q1)
        s2 = _dot(k2_ref[h], q2)
        ok = route_head(h, s1, s2, _extract_topk_untied, False)
        tied = jnp.max(jnp.where(ok, 0.0, 1.0)) > 0.0

        @pl.when(tied)
        def _():
            route_head(h, s1, s2, _extract_topk, True)


def _route(pqt, k1, k2, tn):
    nq, t = pqt.shape
    hk = (PEER_HEADS, PEER_NKEYS, PEER_HALF)
    out = lambda dt: jax.ShapeDtypeStruct((PEER_HEADS, PEER_NKEYS, t), dt)
    ospec = pl.BlockSpec((PEER_HEADS, PEER_NKEYS, tn), lambda i: (0, 0, i))
    return pl.pallas_call(
        _route_body,
        grid=(t // tn,),
        in_specs=[
            pl.BlockSpec((nq, tn), lambda i: (0, i)),
            pl.BlockSpec(hk, lambda i: (0, 0, 0)),
            pl.BlockSpec(hk, lambda i: (0, 0, 0)),
        ],
        out_specs=[ospec, ospec, ospec, ospec],
        out_shape=[out(BF16), out(F32), out(F32), out(BF16)],
        compiler_params=_cparams(("parallel",)),
        name="peer_route",
    )(pqt, k1, k2)


PEER_I1_PER_BLOCK = 8
BF16_ROWS = 16


def _expert_body(hnt_ref, u_first_ref, u_b_ref, u_next_ref, vt_prev_ref, vt_a_ref, vt_last_ref, rank2_ref,
                 cnt_ref, e1_ref, e2z_ref, x1_ref, o_ref, acc_scr, act_next, p_prev):
    s = pl.program_id(1)
    hnt = hnt_ref[...]
    tn = hnt.shape[1]
    tiles = PEER_NKEYS // BF16_ROWS

    def activation(u_ref):
        pre = _dot(u_ref[...], hnt)
        return (pre * (1.0 + lax.erf(pre * (2.0 ** -0.5)))).astype(BF16)

    def row(ref, h, l):
        return jnp.broadcast_to(ref[h, l:l + 1, :], (BF16_ROWS, tn)).astype(BF16)[None]

    def gates(l0, act_rows):
        ps = []
        for li in range(PEER_I1_PER_BLOCK):
            gate = None
            for h in range(PEER_HEADS):
                rank2 = rank2_ref[h].reshape(tiles, BF16_ROWS, tn)
                e2z = e2z_ref[h].reshape(tiles, BF16_ROWS, tn)
                term = jnp.where(rank2 < row(cnt_ref, h, l0 + li), e2z * row(e1_ref, h, l0 + li), 0.0)
                gate = term if gate is None else gate + term
            ps.append(gate.reshape(PEER_NKEYS, tn) * act_rows(li))
        return jnp.concatenate(ps, axis=0)

    @pl.when(s == 0)
    def _():
        acc_scr[...] = jnp.zeros_like(acc_scr)
        p_prev[...] = jnp.zeros_like(p_prev)
        act_next[...] = activation(u_first_ref)

    out_prev = _dot(vt_prev_ref[...], p_prev[...])
    act_b = activation(u_b_ref)
    p_a = gates(0, lambda li: act_next[pl.ds(li * PEER_NKEYS, PEER_NKEYS), :])
    out_a = _dot(vt_a_ref[...], p_a)
    act_next[...] = activation(u_next_ref)
    p_prev[...] = gates(PEER_I1_PER_BLOCK, lambda li: act_b[li * PEER_NKEYS:(li + 1) * PEER_NKEYS, :])
    acc_scr[...] += out_prev + out_a

    @pl.when(s == pl.num_programs(1) - 1)
    def _():
        o_ref[...] = x1_ref[...] + (acc_scr[...] + _dot(vt_last_ref[...], p_prev[...])).T


def _experts(hnt, u_b, vt_b, rank2, cnt, e1, e2z, x1, tn):
    d, t = hnt.shape
    ne = u_b.shape[0]
    eb = PEER_I1_PER_BLOCK * PEER_NKEYS
    nblk = ne // eb
    once = pl.Buffered(1)
    full = pl.BlockSpec((PEER_HEADS, PEER_NKEYS, tn), lambda i, s: (0, 0, i))
    part = pl.BlockSpec((PEER_HEADS, 2 * PEER_I1_PER_BLOCK, tn), lambda i, s: (0, s, i))
    return pl.pallas_call(
        _expert_body,
        grid=(t // tn, nblk // 2),
        in_specs=[
            pl.BlockSpec((d, tn), lambda i, s: (0, i)),
            pl.BlockSpec((eb, d), lambda i, s: (0, 0), pipeline_mode=once),
            pl.BlockSpec((eb, d), lambda i, s: (2 * s + 1, 0)),
            pl.BlockSpec((eb, d), lambda i, s: (jnp.minimum(2 * s + 2, nblk - 1), 0)),
            pl.BlockSpec((d, eb), lambda i, s: (0, jnp.maximum(2 * s - 1, 0))),
            pl.BlockSpec((d, eb), lambda i, s: (0, 2 * s)),
            pl.BlockSpec((d, eb), lambda i, s: (0, nblk - 1), pipeline_mode=once),
            full, part, part, full,
            pl.BlockSpec((tn, d), lambda i, s: (i, 0), pipeline_mode=once),
        ],
        out_specs=pl.BlockSpec((tn, d), lambda i, s: (i, 0)),
        out_shape=jax.ShapeDtypeStruct((t, d), F32),
        scratch_shapes=[pltpu.VMEM((d, tn), F32), pltpu.VMEM((eb, tn), BF16), pltpu.VMEM((eb, tn), BF16)],
        compiler_params=_cparams(("parallel", "arbitrary")),
        name="peer_experts",
    )(hnt, u_b, u_b, u_b, vt_b, vt_b, vt_b, rank2, cnt, e1, e2z, x1)


def _suffix_sum_matrix(tk):
    r = jnp.arange(2 * tk)[:, None] % tk
    c = jnp.arange(2 * tk)[None, :]
    return jnp.where(c < tk, r > c, True).astype(BF16)


class _Tiles(NamedTuple):
    proj_tokens: int
    sb_q: int
    sb_k: int
    gdn_seq: int
    route_tokens: int
    expert_tokens: int


def _tiles(t, s):
    sb_k = 128
    return _Tiles(min(512, t), min(2 * sb_k, s), sb_k, min(512, s), min(256, t), min(512, t))


def _layer(x, norm1_g, w_in, sb_q_g, sb_k_g, sb_o_g, conv_w, a_log, dt_bias, gdn_o_g, w_out, norm2_g,
           w_q, keys1, keys2, u_tab, v_tab):
    b, s, d = x.shape
    t = b * s
    tiles = _tiles(t, s)
    n_main = 3 * SB_WIDTH + 4 * GDN_WIDTH
    x2 = x.reshape(t, d)
    w_main = w_in[:, :n_main].astype(BF16)
    w_ab = jnp.pad(w_in[:, n_main:], ((0, 0), (0, LANES - 2 * GDN_HEADS))).astype(BF16)
    proj, ab = _inproj(x2, norm1_g.reshape(1, d), w_main, w_ab, tiles.proj_tokens)
    proj3 = proj.reshape(b, s, n_main)
    ab3 = ab.reshape(b, s, LANES)

    tile2 = lambda g: jnp.tile(g, 2).reshape(1, LANES)
    sb_o = _sb_attention(proj3, tile2(sb_q_g), tile2(sb_k_g), tile2(sb_o_g), _suffix_sum_matrix(tiles.sb_k),
                         tiles.sb_q, tiles.sb_k)

    on_lanes = lambda p: jnp.pad(p, (0, LANES - GDN_HEADS)).reshape(1, LANES)
    gd_o = _gdn(proj3, ab3, conv_w, on_lanes(a_log), on_lanes(dt_bias), gdn_o_g.reshape(1, LANES), tiles.gdn_seq)

    wo = w_out.astype(BF16)
    x1, hnt, pqt = _outproj(sb_o.reshape(t, SB_WIDTH), gd_o.reshape(t, GDN_WIDTH), x2, wo[:SB_WIDTH],
                            wo[SB_WIDTH:], norm2_g.reshape(1, d), w_q.T.astype(BF16), tiles.proj_tokens)

    rank2, cnt, e1, e2z = _route(pqt, keys1.astype(BF16), keys2.astype(BF16), tiles.route_tokens)
    y = _experts(hnt, u_tab.astype(BF16), v_tab.T.astype(BF16), rank2, cnt, e1, e2z, x1, tiles.expert_tokens)
    return y.reshape(b, s, d)


def kernel(x, norm1_g, w_in, sb_q_norm_g, sb_k_norm_g, sb_out_norm_g, gdn_conv_w, gdn_a_log, gdn_dt_bias,
           gdn_out_norm_g, w_out, norm2_g, peer_w_q, peer_keys1, peer_keys2, peer_u, peer_v):
    for layer in range(norm1_g.shape[0]):
        x = _layer(x, norm1_g[layer], w_in[layer], sb_q_norm_g[layer], sb_k_norm_g[layer],
                   sb_out_norm_g[layer], gdn_conv_w[layer], gdn_a_log[layer], gdn_dt_bias[layer],
                   gdn_out_norm_g[layer], w_out[layer], norm2_g[layer], peer_w_q[layer],
                   peer_keys1[layer], peer_keys2[layer], peer_u[layer], peer_v[layer])
    return x
```

```python
import functools
from typing import NamedTuple

import jax
import jax.numpy as jnp
from jax import lax
from jax.experimental import pallas as pl
from jax.experimental.pallas import tpu as pltpu

F32 = jnp.float32
BF16 = jnp.bfloat16
EPS = 1e-6

SB_HEADS = 8
SB_HEAD_DIM = 64
SB_WIDTH = SB_HEADS * SB_HEAD_DIM
GDN_HEADS = 4
GDN_HEAD_DIM = 128
GDN_WIDTH = GDN_HEADS * GDN_HEAD_DIM
GDN_CONV = 4
PEER_HEADS = 8
PEER_NKEYS = 128
PEER_HALF = 128
PEER_TOPK = 16
LANES = 128

VMEM_LIMIT = 56 * 1024 * 1024


def _cparams(sem):
    return pltpu.CompilerParams(dimension_semantics=sem, vmem_limit_bytes=VMEM_LIMIT)


def _dot(a, b):
    return jnp.dot(a, b, preferred_element_type=F32)


def _dot_nt(a, b):
    return lax.dot_general(a, b, (((1,), (1,)), ((), ())), preferred_element_type=F32)


def _dot_tn(a, b):
    return lax.dot_general(a, b, (((0,), (0,)), ((), ())), preferred_element_type=F32)


def _split(a):
    hi = a.astype(BF16)
    return hi, (a - hi.astype(F32)).astype(BF16)


def _dot_exact_lhs(a, b):
    ab = a.astype(BF16)
    hi = b.astype(BF16)
    rest = b - hi.astype(F32)
    mid = rest.astype(BF16)
    lo = (rest - mid.astype(F32)).astype(BF16)
    return _dot(jnp.concatenate([ab, ab, ab], axis=1), jnp.concatenate([hi, mid, lo], axis=0))


def _dot3(a, b):
    ah, al = _split(a)
    bh, bl = _split(b)
    return _dot(jnp.concatenate([ah, ah, al], axis=1), jnp.concatenate([bh, bl, bh], axis=0))


def _softplus(x):
    return jnp.maximum(x, 0.0) + jnp.log1p(jnp.exp(-jnp.abs(x)))


def _sigmoid(x):
    return 1.0 / (1.0 + jnp.exp(-x))


def _inproj_body(x_ref, g_ref, w_ref, wab_ref, proj_ref, ab_ref):
    x = x_ref[...]
    ms = jnp.mean(x * x, axis=-1, keepdims=True)
    h = (x * lax.rsqrt(ms + EPS) * g_ref[...]).astype(BF16)
    proj_ref[...] = _dot(h, w_ref[...])
    ab_ref[...] = _dot(h, wab_ref[...])


def _inproj(x2, g, w_main, w_ab, tm):
    t, d = x2.shape
    n = w_main.shape[1]
    return pl.pallas_call(
        _inproj_body,
        grid=(t // tm,),
        in_specs=[
            pl.BlockSpec((tm, d), lambda i: (i, 0)),
            pl.BlockSpec((1, d), lambda i: (0, 0)),
            pl.BlockSpec((d, n), lambda i: (0, 0)),
            pl.BlockSpec((d, LANES), lambda i: (0, 0)),
        ],
        out_specs=[
            pl.BlockSpec((tm, n), lambda i: (i, 0)),
            pl.BlockSpec((tm, LANES), lambda i: (i, 0)),
        ],
        out_shape=[
            jax.ShapeDtypeStruct((t, n), F32),
            jax.ShapeDtypeStruct((t, LANES), F32),
        ],
        compiler_params=_cparams(("parallel",)),
        name="inproj",
    )(x2, g, w_main, w_ab)


SB_DEAD_LOG = -104.0


def _sb_body(q_ref, k_ref, v_ref, gq_ref, gk_ref, go_ref, m2_ref, hm_ref, o_ref, kn_scr, vb_scr, *, tq, tk):
    i = pl.program_id(2)
    lane = lax.broadcasted_iota(jnp.int32, (1, LANES), 1)
    is0 = lane < SB_HEAD_DIM

    def headnorm(x, g):
        hi, lo = _split(x * x)
        ms = _dot(jnp.concatenate([hi, lo], axis=1), hm_ref[...])
        return x * lax.rsqrt(ms + EPS) * g

    @pl.when(i == 0)
    def _():
        kn_scr[...] = headnorm(k_ref[...], gk_ref[...]).astype(BF16)
        vb_scr[...] = v_ref[...].astype(BF16)

    qn = headnorm(q_ref[...], gq_ref[...]) * (SB_HEAD_DIM ** -0.5)
    q2h = jnp.concatenate([jnp.where(is0, qn, 0.0), jnp.where(is0, 0.0, qn)], axis=0).astype(BF16)
    halves = lambda x: [x[:tq], x[tq:]]
    m2 = m2_ref[...]
    row = i * tq + lax.broadcasted_iota(jnp.int32, (tq, tk), 0)
    col0 = lax.broadcasted_iota(jnp.int32, (tq, tk), 1)
    nkb = (i + 1) * (tq // tk)

    def scores(j):
        kj = kn_scr[pl.ds(pl.multiple_of(j * tk, tk), tk), :]
        return halves(_dot_nt(q2h, kj))

    def logs(j, zs, on_diagonal):
        causal = (col0 + j * tk) < row
        zls, cats = [], []
        for z in zs:
            lk = -(jnp.maximum(z, 0.0) + jnp.log(1.0 + jnp.exp(-jnp.abs(z))))
            zl = z + lk
            if on_diagonal:
                lk = jnp.where(causal, lk, 0.0)
                zl = jnp.where(causal, zl, -jnp.inf)
            hi = lk.astype(BF16)
            lo = (lk - hi.astype(F32)).astype(BF16)
            zls.append(zl)
            cats.append(jnp.concatenate([hi, lo], axis=1))
        return zls, cats

    def sums(cats):
        return halves(_dot(jnp.concatenate(cats, axis=0), m2))

    def pair(n, carry, on_diagonal):
        j = nkb - 1 - 2 * n
        zs_a = scores(j)
        zs_b = scores(j - 1)
        zl_a, cats_a = logs(j, zs_a, on_diagonal)
        rt_a = sums(cats_a)
        zl_b, cats_b = logs(j - 1, zs_b, on_diagonal)
        rt_b = sums(cats_b)
        v2 = vb_scr[pl.ds(pl.multiple_of((j - 1) * tk, tk), 2 * tk), :]
        ws, rests = [], []
        for h in range(2):
            rest = carry[2 * h + 1]
            rest_mid = rest + rt_a[h][:, tk:]
            w_a = jnp.exp(zl_a[h] + (rest + rt_a[h][:, :tk])).astype(BF16)
            w_b = jnp.exp(zl_b[h] + (rest_mid + rt_b[h][:, :tk])).astype(BF16)
            ws.append(jnp.concatenate([w_b, w_a], axis=1))
            rests.append(rest_mid + rt_b[h][:, tk:])
        pv = halves(_dot(jnp.concatenate(ws, axis=0), v2))
        out = [carry[0] + pv[0], rests[0], carry[2] + pv[1], rests[1]]
        alive = (jnp.max(jnp.maximum(out[1], out[3])) > SB_DEAD_LOG).astype(jnp.int32)
        return n + 1, alive, tuple(out)

    def cond(state):
        return jnp.logical_and(state[0] < nkb // 2, state[1] > 0)

    assert tq == 2 * tk
    zero = jnp.zeros((tq, LANES), F32)
    first = pair(0, (zero, zero, zero, zero), True)
    res = lax.while_loop(cond, lambda state: pair(state[0], state[2], False), first)[2]
    o = jnp.where(is0, res[0], res[2])
    o_ref[...] = headnorm(o, go_ref[...])


def _sb_attention(proj3, gq, gk, go, m2, hm, tq, tk):
    b, s, _ = proj3.shape
    hp = SB_HEADS // 2
    return pl.pallas_call(
        functools.partial(_sb_body, tq=tq, tk=tk),
        grid=(b, hp, s // tq),
        in_specs=[
            pl.BlockSpec((None, tq, LANES), lambda bi, h, i: (bi, i, h)),
            pl.BlockSpec((None, s, LANES), lambda bi, h, i: (bi, 0, hp + h)),
            pl.BlockSpec((None, s, LANES), lambda bi, h, i: (bi, 0, 2 * hp + h)),
            pl.BlockSpec((1, LANES), lambda bi, h, i: (0, 0)),
            pl.BlockSpec((1, LANES), lambda bi, h, i: (0, 0)),
            pl.BlockSpec((1, LANES), lambda bi, h, i: (0, 0)),
            pl.BlockSpec((2 * tk, 2 * tk), lambda bi, h, i: (0, 0)),
            pl.BlockSpec((2 * LANES, LANES), lambda bi, h, i: (0, 0)),
        ],
        out_specs=pl.BlockSpec((None, tq, LANES), lambda bi, h, i: (bi, i, h)),
        out_shape=jax.ShapeDtypeStruct((b, s, SB_WIDTH), F32),
        scratch_shapes=[pltpu.VMEM((s, LANES), BF16), pltpu.VMEM((s, LANES), BF16)],
        compiler_params=_cparams(("parallel", "parallel", "arbitrary")),
        name="sb_attention",
    )(proj3, proj3, proj3, gq, gk, go, m2, hm)


GDN_BLOCK = 128
GDN_HALO = 8


def _gdn_body(x_ref, halo_ref, z_ref, ab_ref, cw_ref, alog_ref, dtb_ref, og_ref, o_ref,
              q_scr, k_scr, v_scr, g_scr, beta_scr, state_scr, *, ts):
    t = pl.program_id(1)
    c = GDN_BLOCK
    nh = GDN_HEADS
    lane = lax.broadcasted_iota(jnp.int32, (1, LANES), 1)

    @pl.when(t == 0)
    def _():
        state_scr[...] = jnp.zeros_like(state_scr)

    halo = jnp.where(t > 0, halo_ref[...], 0.0)
    xe = jnp.concatenate([halo, x_ref[...]], axis=0)
    cw = cw_ref[...]
    y = xe * cw[3:4, :]
    for d in (1, 2, 3):
        y = y + pltpu.roll(xe, d, 0) * cw[3 - d:4 - d, :]
    y = y[GDN_HALO:, :]
    y = y * _sigmoid(y)

    def l2n(x):
        return x * lax.rsqrt(jnp.sum(x * x, axis=-1, keepdims=True) + EPS)

    ab = ab_ref[...]
    g_all = -jnp.exp(alog_ref[...]) * _softplus(ab + dtb_ref[...])
    beta_all = _sigmoid(ab)
    for h in range(nh):
        sl = pl.ds(h * LANES, LANES)
        q_scr[:, sl] = l2n(y[:, h * LANES:(h + 1) * LANES]) * (GDN_HEAD_DIM ** -0.5)
        k_scr[:, sl] = l2n(y[:, GDN_WIDTH + h * LANES:GDN_WIDTH + (h + 1) * LANES])
        g_col = jnp.sum(jnp.where(lane == h, g_all, 0.0), axis=-1, keepdims=True)
        beta_col = jnp.sum(jnp.where(lane == h + nh, beta_all, 0.0), axis=-1, keepdims=True)
        g_scr[:, sl] = jnp.broadcast_to(g_col, (ts, LANES))
        beta_scr[:, sl] = jnp.broadcast_to(beta_col, (ts, LANES))
    v_scr[...] = y[:, 2 * GDN_WIDTH:]

    ri = lax.broadcasted_iota(jnp.int32, (c, nh * c), 0)
    ci = lax.broadcasted_iota(jnp.int32, (c, nh * c), 1) % c
    lower_incl = ci <= ri
    lower_strict = ci < ri
    eye = (ci == ri).astype(F32)
    ltri = lower_incl[:, :c].astype(F32)
    ones = jnp.ones((c, c), F32)
    og = og_ref[...]
    heads = range(nh)
    hs = lambda m, h: m[:, h * c:(h + 1) * c]

    def blk(n, states):
        r0 = pl.multiple_of(n * c, c)
        q = q_scr[pl.ds(r0, c), :]
        k = k_scr[pl.ds(r0, c), :]
        v = v_scr[pl.ds(r0, c), :]
        g = g_scr[pl.ds(r0, c), :]
        beta = beta_scr[pl.ds(r0, c), :]
        gc = _dot_exact_lhs(ltri, g)
        gc_row = _dot_exact_lhs(ones, gc * eye)
        decay = jnp.exp(jnp.where(lower_incl, gc - gc_row, -jnp.inf))
        kb = k * beta
        kbf = k.astype(BF16)
        kbb = kb.astype(BF16)
        qbf = q.astype(BF16)
        kq = [_dot_nt(jnp.concatenate([hs(kbb, h), hs(qbf, h)], axis=0), hs(kbf, h)) for h in heads]
        kk = jnp.concatenate([x[:c] for x in kq], axis=1)
        qk = jnp.concatenate([x[c:] for x in kq], axis=1)
        a = jnp.where(lower_strict, kk * decay, 0.0)
        attn = jnp.where(lower_incl, qk * decay, 0.0).astype(BF16)
        nmat = [-hs(a, h) for h in heads]
        xp = [_dot3(hs(a, h), hs(a, h)) for h in heads]
        for _ in range(5):
            both = [_dot3(jnp.concatenate([nm, x], axis=0), x) for nm, x in zip(nmat, xp)]
            nmat = [nm + x + nx[:c] for nm, x, nx in zip(nmat, xp, both)]
            xp = [nx[c:] for nx in both]
        prod = [_dot3(nm, x) for nm, x in zip(nmat, xp)]
        nmat = [nm + x + p for nm, x, p in zip(nmat, xp, prod)]
        eg = jnp.exp(gc)
        vb = v * beta
        kbd = kb * eg
        gl = gc[c - 1:c, :]
        qd = (q * eg).astype(BF16)
        kd = (k * jnp.exp(gl - gc)).astype(BF16)
        dl = jnp.exp(gl)
        uw_rhs = [jnp.concatenate([hs(vb, h), hs(kbd, h)], axis=1) for h in heads]
        uw = [r + _dot(nm.astype(BF16), r.astype(BF16)) for nm, r in zip(nmat, uw_rhs)]
        u = [x[:, :c] for x in uw]
        w = [x[:, c:].astype(BF16) for x in uw]
        sb = [st.astype(BF16) for st in states]
        ws = [_dot(jnp.concatenate([w[h], hs(qd, h)], axis=0), sb[h]) for h in heads]
        v_new = [u[h] - ws[h][:c] for h in heads]
        vnb = [vn.astype(BF16) for vn in v_new]
        o = [ws[h][c:] + _dot(hs(attn, h), vnb[h]) for h in heads]
        new_states = tuple(states[h] * hs(dl, h) + _dot_tn(hs(kd, h), vnb[h]) for h in heads)
        on = jnp.concatenate(
            [x * lax.rsqrt(jnp.mean(x * x, axis=-1, keepdims=True) + EPS) * og for x in o], axis=1)
        z = z_ref[pl.ds(r0, c), :]
        o_ref[pl.ds(r0, c), :] = on * (z * _sigmoid(z))
        return new_states

    states = lax.fori_loop(0, ts // c, blk, tuple(state_scr[h] for h in heads))
    for h in heads:
        state_scr[h] = states[h]


def _gdn(proj3, ab3, conv_w, alog_b, dtb_b, og, ts):
    b, s, _ = proj3.shape
    w3 = 3 * GDN_WIDTH
    assert 3 * SB_WIDTH == w3 and 2 * w3 % GDN_WIDTH == 0
    per = ts // GDN_HALO
    par = pl.BlockSpec((1, LANES), lambda bi, t: (0, 0))
    return pl.pallas_call(
        functools.partial(_gdn_body, ts=ts),
        grid=(b, s // ts),
        in_specs=[
            pl.BlockSpec((None, ts, w3), lambda bi, t: (bi, t, 1)),
            pl.BlockSpec((None, GDN_HALO, w3), lambda bi, t: (bi, jnp.maximum(t * per - 1, 0), 1)),
            pl.BlockSpec((None, ts, GDN_WIDTH), lambda bi, t: (bi, t, 2 * w3 // GDN_WIDTH)),
            pl.BlockSpec((None, ts, LANES), lambda bi, t: (bi, t, 0)),
            pl.BlockSpec((GDN_CONV, w3), lambda bi, t: (0, 0)),
            par, par,
            pl.BlockSpec((1, LANES), lambda bi, t: (0, 0)),
        ],
        out_specs=pl.BlockSpec((None, ts, GDN_WIDTH), lambda bi, t: (bi, t, 0)),
        out_shape=jax.ShapeDtypeStruct((b, s, GDN_WIDTH), F32),
        scratch_shapes=[pltpu.VMEM((ts, GDN_WIDTH), F32) for _ in range(5)]
        + [pltpu.VMEM((GDN_HEADS, GDN_HEAD_DIM, GDN_HEAD_DIM), F32)],
        compiler_params=_cparams(("parallel", "arbitrary")),
        name="gdn",
    )(proj3, proj3, proj3, ab3, conv_w, alog_b, dtb_b, og)


def _outproj_body(sb_ref, gd_ref, x_ref, wo1_ref, wo2_ref, g2_ref, wqt_ref, x1_ref, hnt_ref, pqt_ref):
    mix = _dot(sb_ref[...].astype(BF16), wo1_ref[...]) + _dot(gd_ref[...].astype(BF16), wo2_ref[...])
    x1 = x_ref[...] + mix
    x1_ref[...] = x1
    ms = jnp.mean(x1 * x1, axis=-1, keepdims=True)
    hn = x1 * lax.rsqrt(ms + EPS) * g2_ref[...]
    hnt = hn.T.astype(BF16)
    hnt_ref[...] = hnt
    pqt_ref[...] = _dot(wqt_ref[...], hnt)


def _outproj(sb_o, gd_o, x2, wo1, wo2, g2, wqt, tm):
    t, d = x2.shape
    nq = wqt.shape[0]
    return pl.pallas_call(
        _outproj_body,
        grid=(t // tm,),
        in_specs=[
            pl.BlockSpec((tm, SB_WIDTH), lambda i: (i, 0)),
            pl.BlockSpec((tm, GDN_WIDTH), lambda i: (i, 0)),
            pl.BlockSpec((tm, d), lambda i: (i, 0)),
            pl.BlockSpec((SB_WIDTH, d), lambda i: (0, 0)),
            pl.BlockSpec((GDN_WIDTH, d), lambda i: (0, 0)),
            pl.BlockSpec((1, d), lambda i: (0, 0)),
            pl.BlockSpec((nq, d), lambda i: (0, 0)),
        ],
        out_specs=[
            pl.BlockSpec((tm, d), lambda i: (i, 0)),
            pl.BlockSpec((d, tm), lambda i: (0, i)),
            pl.BlockSpec((nq, tm), lambda i: (0, i)),
        ],
        out_shape=[
            jax.ShapeDtypeStruct((t, d), F32),
            jax.ShapeDtypeStruct((d, t), BF16),
            jax.ShapeDtypeStruct((nq, t), F32),
        ],
        compiler_params=_cparams(("parallel",)),
        name="outproj",
    )(sb_o, gd_o, x2, wo1, wo2, g2, wqt)


def _extract_topk(s, ids, k, want_rank):
    del want_rank
    big = jnp.int32(2 ** 30)
    work = s
    vals = []
    rank = jnp.full(s.shape, float(k), F32)
    for r in range(k):
        m = jnp.max(work, axis=0, keepdims=True)
        first = jnp.min(jnp.where(work == m, ids, big), axis=0, keepdims=True)
        hit = ids == first
        rank = jnp.where(hit, float(r), rank)
        work = jnp.where(hit, -jnp.inf, work)
        vals.append(m)
    return vals, rank, rank < float(k), None


def _extract_topk_untied(s, ids, k, want_rank):
    del ids
    work = s
    vals = []
    rank = jnp.full(s.shape, float(k), F32) if want_rank else None
    for r in range(k):
        m = jnp.max(work, axis=0, keepdims=True)
        hit = work == m
        if want_rank:
            rank = jnp.where(hit, float(r), rank)
        work = jnp.where(hit, -jnp.inf, work)
        vals.append(m)
    taken = rank < float(k) if want_rank else jnp.logical_and(work == -jnp.inf, s > -jnp.inf)
    count = jnp.sum(taken.astype(F32), axis=0, keepdims=True)
    return vals, rank, taken, count == float(k)


def _route_body(pqt_ref, k1_ref, k2_ref, rank2_ref, cnt_ref, e1_ref, e2z_ref):
    kk = PEER_TOPK
    tn = pqt_ref.shape[1]
    key_ids = lax.broadcasted_iota(jnp.int32, (PEER_NKEYS, tn), 0)
    i8 = lax.broadcasted_iota(jnp.int32, (8, tn), 0)
    i16 = lax.broadcasted_iota(jnp.int32, (kk, tn), 0)
    cand_ids = jnp.concatenate(
        [i16] + [i8 + a * kk for a in (1, 2, 3)]
        + [jnp.where(i8 >= 4, i8 * kk + b, kk * kk + i8 * kk + b) for b in (0, 1, 2)] + [(i8 + 8) * kk], axis=0)
    def route_head(h, s1, s2, extract, tie_rule):
        v1, rank1, _, ok1 = extract(s1, key_ids, kk, tie_rule)
        v2, rank2, _, ok2 = extract(s2, key_ids, kk, True)
        v1a = jnp.concatenate(v1, axis=0)
        v2a = jnp.concatenate(v2, axis=0)
        cand = jnp.concatenate(
            [v1[0] + v2a] + [v1[a] + v2a[:8] for a in (1, 2, 3)]
            + [jnp.where(i8 >= 4, v1a[:8] + v2[b], -jnp.inf) for b in (0, 1, 2)] + [v1a[8:] + v2[0]], axis=0)
        top, _, taken, ok3 = extract(cand, cand_ids, kk, False)
        zsum = jnp.ones_like(top[0])
        for r in range(1, kk):
            zsum = zsum + jnp.exp(top[r] - top[0])
        sel = taken.astype(F32)
        low = [jnp.sum(sel[0:16], axis=0, keepdims=True)] + [
            jnp.sum(sel[8 + 8 * a:16 + 8 * a], axis=0, keepdims=True) for a in (1, 2, 3)]
        mid = sel[40:48] + sel[48:56] + sel[56:64]
        high = sel[64:72]
        cnt = jnp.zeros(s1.shape, F32)
        for a in range(kk):
            n_a = low[a] if a < 4 else (mid[a:a + 1] if a < 8 else high[a - 8:a - 7])
            cnt = jnp.where(rank1 == float(a) if tie_rule else s1 == v1[a], n_a, cnt)
        rank2_ref[h] = rank2.astype(BF16)
        cnt_ref[h] = cnt
        e1_ref[h] = jnp.exp(s1 - v1[0])
        e2z_ref[h] = (jnp.exp(s2 - v2[0]) * (0.5 / zsum)).astype(BF16)
        return None if ok1 is None else jnp.logical_and(jnp.logical_and(ok1, ok2), ok3)

    for h in range(PEER_HEADS):
        q1 = pqt_ref[pl.ds(h * 2 * PEER_HALF, PEER_HALF), :].astype(BF16)
        q2 = pqt_ref[pl.ds(h * 2 * PEER_HALF + PEER_HALF, PEER_HALF), :].astype(BF16)
        s1 = _dot(k1_ref[h], q1)
        s2 = _dot(k2_ref[h], q2)
        ok = route_head(h, s1, s2, _extract_topk_untied, False)
        tied = jnp.max(jnp.where(ok, 0.0, 1.0)) > 0.0

        @pl.when(tied)
        def _():
            route_head(h, s1, s2, _extract_topk, True)


def _route(pqt, k1, k2, tn):
    nq, t = pqt.shape
    hk = (PEER_HEADS, PEER_NKEYS, PEER_HALF)
    out = lambda dt: jax.ShapeDtypeStruct((PEER_HEADS, PEER_NKEYS, t), dt)
    ospec = pl.BlockSpec((PEER_HEADS, PEER_NKEYS, tn), lambda i: (0, 0, i))
    return pl.pallas_call(
        _route_body,
        grid=(t // tn,),
        in_specs=[
            pl.BlockSpec((nq, tn), lambda i: (0, i)),
            pl.BlockSpec(hk, lambda i: (0, 0, 0)),
            pl.BlockSpec(hk, lambda i: (0, 0, 0)),
        ],
        out_specs=[ospec, ospec, ospec, ospec],
        out_shape=[out(BF16), out(F32), out(F32), out(BF16)],
        compiler_params=_cparams(("parallel",)),
        name="peer_route",
    )(pqt, k1, k2)


PEER_I1_PER_BLOCK = 8
BF16_ROWS = 16


def _expert_body(hnt_ref, u_first_ref, u_b_ref, u_next_ref, vt_prev_ref, vt_a_ref, vt_last_ref, rank2_ref,
                 cnt_ref, e1_ref, e2z_ref, x1_ref, o_ref, acc_scr, act_next, p_prev):
    s = pl.program_id(1)
    hnt = hnt_ref[...]
    tn = hnt.shape[1]
    tiles = PEER_NKEYS // BF16_ROWS

    def activation(u_ref):
        pre = _dot(u_ref[...], hnt)
        return (pre * (1.0 + lax.erf(pre * (2.0 ** -0.5)))).astype(BF16)

    def row(ref, h, l):
        return jnp.broadcast_to(ref[h, l:l + 1, :], (BF16_ROWS, tn)).astype(BF16)[None]

    def gates(l0, act_rows):
        ps = []
        for li in range(PEER_I1_PER_BLOCK):
            gate = None
            for h in range(PEER_HEADS):
                rank2 = rank2_ref[h].reshape(tiles, BF16_ROWS, tn)
                e2z = e2z_ref[h].reshape(tiles, BF16_ROWS, tn)
                term = jnp.where(rank2 < row(cnt_ref, h, l0 + li), e2z * row(e1_ref, h, l0 + li), 0.0)
                gate = term if gate is None else gate + term
            ps.append(gate.reshape(PEER_NKEYS, tn) * act_rows(li))
        return jnp.concatenate(ps, axis=0)

    @pl.when(s == 0)
    def _():
        acc_scr[...] = jnp.zeros_like(acc_scr)
        p_prev[...] = jnp.zeros_like(p_prev)
        act_next[...] = activation(u_first_ref)

    out_prev = _dot(vt_prev_ref[...], p_prev[...])
    act_b = activation(u_b_ref)
    p_a = gates(0, lambda li: act_next[pl.ds(li * PEER_NKEYS, PEER_NKEYS), :])
    out_a = _dot(vt_a_ref[...], p_a)
    act_next[...] = activation(u_next_ref)
    p_prev[...] = gates(PEER_I1_PER_BLOCK, lambda li: act_b[li * PEER_NKEYS:(li + 1) * PEER_NKEYS, :])
    acc_scr[...] += out_prev + out_a

    @pl.when(s == pl.num_programs(1) - 1)
    def _():
        o_ref[...] = x1_ref[...] + (acc_scr[...] + _dot(vt_last_ref[...], p_prev[...])).T


def _experts(hnt, u_b, vt_b, rank2, cnt, e1, e2z, x1, tn):
    d, t = hnt.shape
    ne = u_b.shape[0]
    eb = PEER_I1_PER_BLOCK * PEER_NKEYS
    nblk = ne // eb
    once = pl.Buffered(1)
    full = pl.BlockSpec((PEER_HEADS, PEER_NKEYS, tn), lambda i, s: (0, 0, i))
    part = pl.BlockSpec((PEER_HEADS, 2 * PEER_I1_PER_BLOCK, tn), lambda i, s: (0, s, i))
    return pl.pallas_call(
        _expert_body,
        grid=(t // tn, nblk // 2),
        in_specs=[
            pl.BlockSpec((d, tn), lambda i, s: (0, i)),
            pl.BlockSpec((eb, d), lambda i, s: (0, 0), pipeline_mode=once),
            pl.BlockSpec((eb, d), lambda i, s: (2 * s + 1, 0)),
            pl.BlockSpec((eb, d), lambda i, s: (jnp.minimum(2 * s + 2, nblk - 1), 0)),
            pl.BlockSpec((d, eb), lambda i, s: (0, jnp.maximum(2 * s - 1, 0))),
            pl.BlockSpec((d, eb), lambda i, s: (0, 2 * s)),
            pl.BlockSpec((d, eb), lambda i, s: (0, nblk - 1), pipeline_mode=once),
            full, part, part, full,
            pl.BlockSpec((tn, d), lambda i, s: (i, 0), pipeline_mode=once),
        ],
        out_specs=pl.BlockSpec((tn, d), lambda i, s: (i, 0)),
        out_shape=jax.ShapeDtypeStruct((t, d), F32),
        scratch_shapes=[pltpu.VMEM((d, tn), F32), pltpu.VMEM((eb, tn), BF16), pltpu.VMEM((eb, tn), BF16)],
        compiler_params=_cparams(("parallel", "arbitrary")),
        name="peer_experts",
    )(hnt, u_b, u_b, u_b, vt_b, vt_b, vt_b, rank2, cnt, e1, e2z, x1)


def _suffix_sum_matrix(tk):
    r = jnp.arange(2 * tk)[:, None] % tk
    c = jnp.arange(2 * tk)[None, :]
    return jnp.where(c < tk, r > c, True).astype(BF16)


def _head_mean_matrix():
    r = (jnp.arange(2 * LANES)[:, None] % LANES) // SB_HEAD_DIM
    c = jnp.arange(LANES)[None, :] // SB_HEAD_DIM
    return jnp.where(r == c, 1.0 / SB_HEAD_DIM, 0.0).astype(BF16)


class _Tiles(NamedTuple):
    proj_tokens: int
    sb_q: int
    sb_k: int
    gdn_seq: int
    route_tokens: int
    expert_tokens: int


def _tiles(t, s):
    sb_k = 128
    return _Tiles(min(512, t), min(2 * sb_k, s), sb_k, min(512, s), min(256, t), min(512, t))


def _layer(x, norm1_g, w_in, sb_q_g, sb_k_g, sb_o_g, conv_w, a_log, dt_bias, gdn_o_g, w_out, norm2_g,
           w_q, keys1, keys2, u_tab, v_tab):
    b, s, d = x.shape
    t = b * s
    tiles = _tiles(t, s)
    n_main = 3 * SB_WIDTH + 4 * GDN_WIDTH
    x2 = x.reshape(t, d)
    w_main = w_in[:, :n_main].astype(BF16)
    w_ab = jnp.pad(w_in[:, n_main:], ((0, 0), (0, LANES - 2 * GDN_HEADS))).astype(BF16)
    proj, ab = _inproj(x2, norm1_g.reshape(1, d), w_main, w_ab, tiles.proj_tokens)
    proj3 = proj.reshape(b, s, n_main)
    ab3 = ab.reshape(b, s, LANES)

    tile2 = lambda g: jnp.tile(g, 2).reshape(1, LANES)
    sb_o = _sb_attention(proj3, tile2(sb_q_g), tile2(sb_k_g), tile2(sb_o_g), _suffix_sum_matrix(tiles.sb_k),
                         _head_mean_matrix(), tiles.sb_q, tiles.sb_k)

    on_lanes = lambda p: jnp.pad(p, (0, LANES - GDN_HEADS)).reshape(1, LANES)
    gd_o = _gdn(proj3, ab3, conv_w, on_lanes(a_log), on_lanes(dt_bias), gdn_o_g.reshape(1, LANES), tiles.gdn_seq)

    wo = w_out.astype(BF16)
    x1, hnt, pqt = _outproj(sb_o.reshape(t, SB_WIDTH), gd_o.reshape(t, GDN_WIDTH), x2, wo[:SB_WIDTH],
                            wo[SB_WIDTH:], norm2_g.reshape(1, d), w_q.T.astype(BF16), tiles.proj_tokens)

    rank2, cnt, e1, e2z = _route(pqt, keys1.astype(BF16), keys2.astype(BF16), tiles.route_tokens)
    y = _experts(hnt, u_tab.astype(BF16), v_tab.T.astype(BF16), rank2, cnt, e1, e2z, x1, tiles.expert_tokens)
    return y.reshape(b, s, d)


def kernel(x, norm1_g, w_in, sb_q_norm_g, sb_k_norm_g, sb_out_norm_g, gdn_conv_w, gdn_a_log, gdn_dt_bias,
           gdn_out_norm_g, w_out, norm2_g, peer_w_q, peer_keys1, peer_keys2, peer_u, peer_v):
    for layer in range(norm1_g.shape[0]):
        x = _layer(x, norm1_g[layer], w_in[layer], sb_q_norm_g[layer], sb_k_norm_g[layer],
                   sb_out_norm_g[layer], gdn_conv_w[layer], gdn_a_log[layer], gdn_dt_bias[layer],
                   gdn_out_norm_g[layer], w_out[layer], norm2_g[layer], peer_w_q[layer],
                   peer_keys1[layer], peer_keys2[layer], peer_u[layer], peer_v[layer])
    return x
```

```python
import functools
from typing import NamedTuple

import jax
import jax.numpy as jnp
from jax import lax
from jax.experimental import pallas as pl
from jax.experimental.pallas import tpu as pltpu

F32 = jnp.float32
BF16 = jnp.bfloat16
EPS = 1e-6

SB_HEADS = 8
SB_HEAD_DIM = 64
SB_WIDTH = SB_HEADS * SB_HEAD_DIM
GDN_HEADS = 4
GDN_HEAD_DIM = 128
GDN_WIDTH = GDN_HEADS * GDN_HEAD_DIM
GDN_CONV = 4
PEER_HEADS = 8
PEER_NKEYS = 128
PEER_HALF = 128
PEER_TOPK = 16
LANES = 128

VMEM_LIMIT = 56 * 1024 * 1024


def _cparams(sem):
    return pltpu.CompilerParams(dimension_semantics=sem, vmem_limit_bytes=VMEM_LIMIT)


def _dot(a, b):
    return jnp.dot(a, b, preferred_element_type=F32)


def _dot_nt(a, b):
    return lax.dot_general(a, b, (((1,), (1,)), ((), ())), preferred_element_type=F32)


def _dot_tn(a, b):
    return lax.dot_general(a, b, (((0,), (0,)), ((), ())), preferred_element_type=F32)


def _split(a):
    hi = a.astype(BF16)
    return hi, (a - hi.astype(F32)).astype(BF16)


def _dot_exact_lhs(a, b):
    ab = a.astype(BF16)
    hi = b.astype(BF16)
    rest = b - hi.astype(F32)
    mid = rest.astype(BF16)
    lo = (rest - mid.astype(F32)).astype(BF16)
    return _dot(jnp.concatenate([ab, ab, ab], axis=1), jnp.concatenate([hi, mid, lo], axis=0))


def _dot3(a, b):
    ah, al = _split(a)
    bh, bl = _split(b)
    return _dot(jnp.concatenate([ah, ah, al], axis=1), jnp.concatenate([bh, bl, bh], axis=0))


def _softplus(x):
    return jnp.maximum(x, 0.0) + jnp.log1p(jnp.exp(-jnp.abs(x)))


def _sigmoid(x):
    return 1.0 / (1.0 + jnp.exp(-x))


def _inproj_body(x_ref, g_ref, w_ref, wab_ref, proj_ref, ab_ref):
    x = x_ref[...]
    ms = jnp.mean(x * x, axis=-1, keepdims=True)
    h = (x * lax.rsqrt(ms + EPS) * g_ref[...]).astype(BF16)
    proj_ref[...] = _dot(h, w_ref[...])
    ab_ref[...] = _dot(h, wab_ref[...])


def _inproj(x2, g, w_main, w_ab, tm):
    t, d = x2.shape
    n = w_main.shape[1]
    return pl.pallas_call(
        _inproj_body,
        grid=(t // tm,),
        in_specs=[
            pl.BlockSpec((tm, d), lambda i: (i, 0)),
            pl.BlockSpec((1, d), lambda i: (0, 0)),
            pl.BlockSpec((d, n), lambda i: (0, 0)),
            pl.BlockSpec((d, LANES), lambda i: (0, 0)),
        ],
        out_specs=[
            pl.BlockSpec((tm, n), lambda i: (i, 0)),
            pl.BlockSpec((tm, LANES), lambda i: (i, 0)),
        ],
        out_shape=[
            jax.ShapeDtypeStruct((t, n), F32),
            jax.ShapeDtypeStruct((t, LANES), F32),
        ],
        compiler_params=_cparams(("parallel",)),
        name="inproj",
    )(x2, g, w_main, w_ab)


SB_DEAD_LOG = -104.0


def _sb_body(q_ref, k_ref, v_ref, gq_ref, gk_ref, go_ref, m2_ref, o_ref, kn_scr, vb_scr, *, tq, tk):
    i = pl.program_id(2)
    lane = lax.broadcasted_iota(jnp.int32, (1, LANES), 1)
    is0 = lane < SB_HEAD_DIM

    def headnorm(x, g):
        x2 = x * x
        s0 = jnp.sum(jnp.where(is0, x2, 0.0), axis=-1, keepdims=True)
        s1 = jnp.sum(jnp.where(is0, 0.0, x2), axis=-1, keepdims=True)
        ms = jnp.where(is0, s0, s1) * (1.0 / SB_HEAD_DIM)
        return x * lax.rsqrt(ms + EPS) * g

    @pl.when(i == 0)
    def _():
        kn_scr[...] = headnorm(k_ref[...], gk_ref[...]).astype(BF16)
        vb_scr[...] = v_ref[...].astype(BF16)

    qn = headnorm(q_ref[...], gq_ref[...]) * (SB_HEAD_DIM ** -0.5)
    q2h = jnp.concatenate([jnp.where(is0, qn, 0.0), jnp.where(is0, 0.0, qn)], axis=0).astype(BF16)
    halves = lambda x: [x[:tq], x[tq:]]
    m2 = m2_ref[...]
    row = i * tq + lax.broadcasted_iota(jnp.int32, (tq, tk), 0)
    col0 = lax.broadcasted_iota(jnp.int32, (tq, tk), 1)
    nkb = (i + 1) * (tq // tk)

    def scores(j):
        kj = kn_scr[pl.ds(pl.multiple_of(j * tk, tk), tk), :]
        return halves(_dot_nt(q2h, kj))

    def logs(j, zs, on_diagonal):
        causal = (col0 + j * tk) < row
        zls, cats = [], []
        for z in zs:
            lk = -(jnp.maximum(z, 0.0) + jnp.log(1.0 + jnp.exp(-jnp.abs(z))))
            zl = z + lk
            if on_diagonal:
                lk = jnp.where(causal, lk, 0.0)
                zl = jnp.where(causal, zl, -jnp.inf)
            hi = lk.astype(BF16)
            lo = (lk - hi.astype(F32)).astype(BF16)
            zls.append(zl)
            cats.append(jnp.concatenate([hi, lo], axis=1))
        return zls, cats

    def sums(cats):
        return halves(_dot(jnp.concatenate(cats, axis=0), m2))

    def pair(n, carry, on_diagonal):
        j = nkb - 1 - 2 * n
        zs_a = scores(j)
        zs_b = scores(j - 1)
        zl_a, cats_a = logs(j, zs_a, on_diagonal)
        rt_a = sums(cats_a)
        zl_b, cats_b = logs(j - 1, zs_b, on_diagonal)
        rt_b = sums(cats_b)
        v2 = vb_scr[pl.ds(pl.multiple_of((j - 1) * tk, tk), 2 * tk), :]
        ws, rests = [], []
        for h in range(2):
            rest = carry[2 * h + 1]
            rest_mid = rest + rt_a[h][:, tk:]
            w_a = jnp.exp(zl_a[h] + (rest + rt_a[h][:, :tk])).astype(BF16)
            w_b = jnp.exp(zl_b[h] + (rest_mid + rt_b[h][:, :tk])).astype(BF16)
            ws.append(jnp.concatenate([w_b, w_a], axis=1))
            rests.append(rest_mid + rt_b[h][:, tk:])
        pv = halves(_dot(jnp.concatenate(ws, axis=0), v2))
        out = [carry[0] + pv[0], rests[0], carry[2] + pv[1], rests[1]]
        alive = (jnp.max(jnp.maximum(out[1], out[3])) > SB_DEAD_LOG).astype(jnp.int32)
        return n + 1, alive, tuple(out)

    def cond(state):
        return jnp.logical_and(state[0] < nkb // 2, state[1] > 0)

    assert tq == 2 * tk
    zero = jnp.zeros((tq, LANES), F32)
    first = pair(0, (zero, zero, zero, zero), True)
    res = lax.while_loop(cond, lambda state: pair(state[0], state[2], False), first)[2]
    o = jnp.where(is0, res[0], res[2])
    o_ref[...] = headnorm(o, go_ref[...])


def _sb_attention(proj3, gq, gk, go, m2, tq, tk):
    b, s, _ = proj3.shape
    hp = SB_HEADS // 2
    return pl.pallas_call(
        functools.partial(_sb_body, tq=tq, tk=tk),
        grid=(b, hp, s // tq),
        in_specs=[
            pl.BlockSpec((None, tq, LANES), lambda bi, h, i: (bi, i, h)),
            pl.BlockSpec((None, s, LANES), lambda bi, h, i: (bi, 0, hp + h)),
            pl.BlockSpec((None, s, LANES), lambda bi, h, i: (bi, 0, 2 * hp + h)),
            pl.BlockSpec((1, LANES), lambda bi, h, i: (0, 0)),
            pl.BlockSpec((1, LANES), lambda bi, h, i: (0, 0)),
            pl.BlockSpec((1, LANES), lambda bi, h, i: (0, 0)),
            pl.BlockSpec((2 * tk, 2 * tk), lambda bi, h, i: (0, 0)),
        ],
        out_specs=pl.BlockSpec((None, tq, LANES), lambda bi, h, i: (bi, i, h)),
        out_shape=jax.ShapeDtypeStruct((b, s, SB_WIDTH), F32),
        scratch_shapes=[pltpu.VMEM((s, LANES), BF16), pltpu.VMEM((s, LANES), BF16)],
        compiler_params=_cparams(("parallel", "parallel", "arbitrary")),
        name="sb_attention",
    )(proj3, proj3, proj3, gq, gk, go, m2)


GDN_BLOCK = 128
GDN_HALO = 8


def _gdn_body(x_ref, halo_ref, z_ref, ab_ref, cw_ref, alog_ref, dtb_ref, og_ref, o_ref,
              q_scr, k_scr, v_scr, g_scr, beta_scr, state_scr, *, ts):
    t = pl.program_id(1)
    c = GDN_BLOCK
    nh = GDN_HEADS
    lane = lax.broadcasted_iota(jnp.int32, (1, LANES), 1)

    @pl.when(t == 0)
    def _():
        state_scr[...] = jnp.zeros_like(state_scr)

    halo = jnp.where(t > 0, halo_ref[...], 0.0)
    xe = jnp.concatenate([halo, x_ref[...]], axis=0)
    cw = cw_ref[...]
    y = xe * cw[3:4, :]
    for d in (1, 2, 3):
        y = y + pltpu.roll(xe, d, 0) * cw[3 - d:4 - d, :]
    y = y[GDN_HALO:, :]
    y = y * _sigmoid(y)

    def l2n(x):
        return x * lax.rsqrt(jnp.sum(x * x, axis=-1, keepdims=True) + EPS)

    ab = ab_ref[...]
    g_all = -jnp.exp(alog_ref[...]) * _softplus(ab + dtb_ref[...])
    beta_all = _sigmoid(ab)
    for h in range(nh):
        sl = pl.ds(h * LANES, LANES)
        q_scr[:, sl] = l2n(y[:, h * LANES:(h + 1) * LANES]) * (GDN_HEAD_DIM ** -0.5)
        k_scr[:, sl] = l2n(y[:, GDN_WIDTH + h * LANES:GDN_WIDTH + (h + 1) * LANES])
        g_col = jnp.sum(jnp.where(lane == h, g_all, 0.0), axis=-1, keepdims=True)
        beta_col = jnp.sum(jnp.where(lane == h + nh, beta_all, 0.0), axis=-1, keepdims=True)
        g_scr[:, sl] = jnp.broadcast_to(g_col, (ts, LANES))
        beta_scr[:, sl] = jnp.broadcast_to(beta_col, (ts, LANES))
    v_scr[...] = y[:, 2 * GDN_WIDTH:]

    ri = lax.broadcasted_iota(jnp.int32, (c, nh * c), 0)
    ci = lax.broadcasted_iota(jnp.int32, (c, nh * c), 1) % c
    lower_incl = ci <= ri
    lower_strict = ci < ri
    eye = (ci == ri).astype(F32)
    ltri = lower_incl[:, :c].astype(F32)
    ones = jnp.ones((c, c), F32)
    og = og_ref[...]
    heads = range(nh)
    hs = lambda m, h: m[:, h * c:(h + 1) * c]

    def blk(n, states):
        r0 = pl.multiple_of(n * c, c)
        q = q_scr[pl.ds(r0, c), :]
        k = k_scr[pl.ds(r0, c), :]
        v = v_scr[pl.ds(r0, c), :]
        g = g_scr[pl.ds(r0, c), :]
        beta = beta_scr[pl.ds(r0, c), :]
        gc = _dot_exact_lhs(ltri, g)
        gc_row = _dot_exact_lhs(ones, gc * eye)
        decay = jnp.exp(jnp.where(lower_incl, gc - gc_row, -jnp.inf))
        kb = k * beta
        kbf = k.astype(BF16)
        kbb = kb.astype(BF16)
        qbf = q.astype(BF16)
        kq = [_dot_nt(jnp.concatenate([hs(kbb, h), hs(qbf, h)], axis=0), hs(kbf, h)) for h in heads]
        kk = jnp.concatenate([x[:c] for x in kq], axis=1)
        qk = jnp.concatenate([x[c:] for x in kq], axis=1)
        a = jnp.where(lower_strict, kk * decay, 0.0)
        attn = jnp.where(lower_incl, qk * decay, 0.0).astype(BF16)
        nmat = [-hs(a, h) for h in heads]
        xp = [_dot3(hs(a, h), hs(a, h)) for h in heads]
        for _ in range(5):
            both = [_dot3(jnp.concatenate([nm, x], axis=0), x) for nm, x in zip(nmat, xp)]
            nmat = [nm + x + nx[:c] for nm, x, nx in zip(nmat, xp, both)]
            xp = [nx[c:] for nx in both]
        prod = [_dot3(nm, x) for nm, x in zip(nmat, xp)]
        nmat = [nm + x + p for nm, x, p in zip(nmat, xp, prod)]
        eg = jnp.exp(gc)
        vb = v * beta
        kbd = kb * eg
        gl = gc[c - 1:c, :]
        qd = (q * eg).astype(BF16)
        kd = (k * jnp.exp(gl - gc)).astype(BF16)
        dl = jnp.exp(gl)
        uw_rhs = [jnp.concatenate([hs(vb, h), hs(kbd, h)], axis=1) for h in heads]
        uw = [r + _dot(nm.astype(BF16), r.astype(BF16)) for nm, r in zip(nmat, uw_rhs)]
        u = [x[:, :c] for x in uw]
        w = [x[:, c:].astype(BF16) for x in uw]
        sb = [st.astype(BF16) for st in states]
        ws = [_dot(jnp.concatenate([w[h], hs(qd, h)], axis=0), sb[h]) for h in heads]
        v_new = [u[h] - ws[h][:c] for h in heads]
        vnb = [vn.astype(BF16) for vn in v_new]
        o = [ws[h][c:] + _dot(hs(attn, h), vnb[h]) for h in heads]
        new_states = tuple(states[h] * hs(dl, h) + _dot_tn(hs(kd, h), vnb[h]) for h in heads)
        on = jnp.concatenate(
            [x * lax.rsqrt(jnp.mean(x * x, axis=-1, keepdims=True) + EPS) * og for x in o], axis=1)
        z = z_ref[pl.ds(r0, c), :]
        o_ref[pl.ds(r0, c), :] = on * (z * _sigmoid(z))
        return new_states

    states = lax.fori_loop(0, ts // c, blk, tuple(state_scr[h] for h in heads))
    for h in heads:
        state_scr[h] = states[h]


def _gdn(proj3, ab3, conv_w, alog_b, dtb_b, og, ts):
    b, s, _ = proj3.shape
    w3 = 3 * GDN_WIDTH
    assert 3 * SB_WIDTH == w3 and 2 * w3 % GDN_WIDTH == 0
    per = ts // GDN_HALO
    par = pl.BlockSpec((1, LANES), lambda bi, t: (0, 0))
    return pl.pallas_call(
        functools.partial(_gdn_body, ts=ts),
        grid=(b, s // ts),
        in_specs=[
            pl.BlockSpec((None, ts, w3), lambda bi, t: (bi, t, 1)),
            pl.BlockSpec((None, GDN_HALO, w3), lambda bi, t: (bi, jnp.maximum(t * per - 1, 0), 1)),
            pl.BlockSpec((None, ts, GDN_WIDTH), lambda bi, t: (bi, t, 2 * w3 // GDN_WIDTH)),
            pl.BlockSpec((None, ts, LANES), lambda bi, t: (bi, t, 0)),
            pl.BlockSpec((GDN_CONV, w3), lambda bi, t: (0, 0)),
            par, par,
            pl.BlockSpec((1, LANES), lambda bi, t: (0, 0)),
        ],
        out_specs=pl.BlockSpec((None, ts, GDN_WIDTH), lambda bi, t: (bi, t, 0)),
        out_shape=jax.ShapeDtypeStruct((b, s, GDN_WIDTH), F32),
        scratch_shapes=[pltpu.VMEM((ts, GDN_WIDTH), F32) for _ in range(5)]
        + [pltpu.VMEM((GDN_HEADS, GDN_HEAD_DIM, GDN_HEAD_DIM), F32)],
        compiler_params=_cparams(("parallel", "arbitrary")),
        name="gdn",
    )(proj3, proj3, proj3, ab3, conv_w, alog_b, dtb_b, og)


def _outproj_body(sb_ref, gd_ref, x_ref, wo1_ref, wo2_ref, g2_ref, wqt_ref, x1_ref, hnt_ref, pqt_ref):
    mix = _dot(sb_ref[...].astype(BF16), wo1_ref[...]) + _dot(gd_ref[...].astype(BF16), wo2_ref[...])
    x1 = x_ref[...] + mix
    x1_ref[...] = x1
    ms = jnp.mean(x1 * x1, axis=-1, keepdims=True)
    hn = x1 * lax.rsqrt(ms + EPS) * g2_ref[...]
    hnt = hn.T.astype(BF16)
    hnt_ref[...] = hnt
    pqt_ref[...] = _dot(wqt_ref[...], hnt)


def _outproj(sb_o, gd_o, x2, wo1, wo2, g2, wqt, tm):
    t, d = x2.shape
    nq = wqt.shape[0]
    return pl.pallas_call(
        _outproj_body,
        grid=(t // tm,),
        in_specs=[
            pl.BlockSpec((tm, SB_WIDTH), lambda i: (i, 0)),
            pl.BlockSpec((tm, GDN_WIDTH), lambda i: (i, 0)),
            pl.BlockSpec((tm, d), lambda i: (i, 0)),
            pl.BlockSpec((SB_WIDTH, d), lambda i: (0, 0)),
            pl.BlockSpec((GDN_WIDTH, d), lambda i: (0, 0)),
            pl.BlockSpec((1, d), lambda i: (0, 0)),
            pl.BlockSpec((nq, d), lambda i: (0, 0)),
        ],
        out_specs=[
            pl.BlockSpec((tm, d), lambda i: (i, 0)),
            pl.BlockSpec((d, tm), lambda i: (0, i)),
            pl.BlockSpec((nq, tm), lambda i: (0, i)),
        ],
        out_shape=[
            jax.ShapeDtypeStruct((t, d), F32),
            jax.ShapeDtypeStruct((d, t), BF16),
            jax.ShapeDtypeStruct((nq, t), F32),
        ],
        compiler_params=_cparams(("parallel",)),
        name="outproj",
    )(sb_o, gd_o, x2, wo1, wo2, g2, wqt)


def _extract_topk(s, ids, k, want_rank):
    del want_rank
    big = jnp.int32(2 ** 30)
    work = s
    vals = []
    rank = jnp.full(s.shape, float(k), F32)
    for r in range(k):
        m = jnp.max(work, axis=0, keepdims=True)
        first = jnp.min(jnp.where(work == m, ids, big), axis=0, keepdims=True)
        hit = ids == first
        rank = jnp.where(hit, float(r), rank)
        work = jnp.where(hit, -jnp.inf, work)
        vals.append(m)
    return vals, rank, rank < float(k), None


def _extract_topk_untied(s, ids, k, want_rank):
    del ids
    work = s
    vals = []
    rank = jnp.full(s.shape, float(k), F32) if want_rank else None
    for r in range(k):
        m = jnp.max(work, axis=0, keepdims=True)
        hit = work == m
        if want_rank:
            rank = jnp.where(hit, float(r), rank)
        work = jnp.where(hit, -jnp.inf, work)
        vals.append(m)
    taken = rank < float(k) if want_rank else jnp.logical_and(work == -jnp.inf, s > -jnp.inf)
    count = jnp.sum(taken.astype(F32), axis=0, keepdims=True)
    return vals, rank, taken, count == float(k)


def _route_body(pqt_ref, k1_ref, k2_ref, rank2_ref, cnt_ref, e1_ref, e2z_ref):
    kk = PEER_TOPK
    tn = pqt_ref.shape[1]
    key_ids = lax.broadcasted_iota(jnp.int32, (PEER_NKEYS, tn), 0)
    i8 = lax.broadcasted_iota(jnp.int32, (8, tn), 0)
    i16 = lax.broadcasted_iota(jnp.int32, (kk, tn), 0)
    cand_ids = jnp.concatenate(
        [i16] + [i8 + a * kk for a in (1, 2, 3)]
        + [jnp.where(i8 >= 4, i8 * kk + b, kk * kk + i8 * kk + b) for b in (0, 1, 2)] + [(i8 + 8) * kk], axis=0)
    def route_head(h, s1, s2, extract, tie_rule):
        v1, rank1, _, ok1 = extract(s1, key_ids, kk, tie_rule)
        v2, rank2, _, ok2 = extract(s2, key_ids, kk, True)
        v1a = jnp.concatenate(v1, axis=0)
        v2a = jnp.concatenate(v2, axis=0)
        cand = jnp.concatenate(
            [v1[0] + v2a] + [v1[a] + v2a[:8] for a in (1, 2, 3)]
            + [jnp.where(i8 >= 4, v1a[:8] + v2[b], -jnp.inf) for b in (0, 1, 2)] + [v1a[8:] + v2[0]], axis=0)
        top, _, taken, ok3 = extract(cand, cand_ids, kk, False)
        zsum = jnp.ones_like(top[0])
        for r in range(1, kk):
            zsum = zsum + jnp.exp(top[r] - top[0])
        sel = taken.astype(F32)
        low = [jnp.sum(sel[0:16], axis=0, keepdims=True)] + [
            jnp.sum(sel[8 + 8 * a:16 + 8 * a], axis=0, keepdims=True) for a in (1, 2, 3)]
        mid = sel[40:48] + sel[48:56] + sel[56:64]
        high = sel[64:72]
        cnt = jnp.zeros(s1.shape, F32)
        for a in range(kk):
            n_a = low[a] if a < 4 else (mid[a:a + 1] if a < 8 else high[a - 8:a - 7])
            cnt = jnp.where(rank1 == float(a) if tie_rule else s1 == v1[a], n_a, cnt)
        rank2_ref[h] = rank2.astype(BF16)
        cnt_ref[h] = cnt
        e1_ref[h] = jnp.exp(s1 - v1[0])
        e2z_ref[h] = (jnp.exp(s2 - v2[0]) * (0.5 / zsum)).astype(BF16)
        return None if ok1 is None else jnp.logical_and(jnp.logical_and(ok1, ok2), ok3)

    for h in range(PEER_HEADS):
        q1 = pqt_ref[pl.ds(h * 2 * PEER_HALF, PEER_HALF), :].astype(BF16)
        q2 = pqt_ref[pl.ds(h * 2 * PEER_HALF + PEER_HALF, PEER_HALF), :].astype(BF16)
        s1 = _dot(k1_ref[h], q1)
        s2 = _dot(k2_ref[h], q2)
        ok = route_head(h, s1, s2, _extract_topk_untied, False)
        tied = jnp.max(jnp.where(ok, 0.0, 1.0)) > 0.0

        @pl.when(tied)
        def _():
            route_head(h, s1, s2, _extract_topk, True)


def _route(pqt, k1, k2, tn):
    nq, t = pqt.shape
    hk = (PEER_HEADS, PEER_NKEYS, PEER_HALF)
    out = lambda dt: jax.ShapeDtypeStruct((PEER_HEADS, PEER_NKEYS, t), dt)
    ospec = pl.BlockSpec((PEER_HEADS, PEER_NKEYS, tn), lambda i: (0, 0, i))
    return pl.pallas_call(
        _route_body,
        grid=(t // tn,),
        in_specs=[
            pl.BlockSpec((nq, tn), lambda i: (0, i)),
            pl.BlockSpec(hk, lambda i: (0, 0, 0)),
            pl.BlockSpec(hk, lambda i: (0, 0, 0)),
        ],
        out_specs=[ospec, ospec, ospec, ospec],
        out_shape=[out(BF16), out(F32), out(F32), out(BF16)],
        compiler_params=_cparams(("parallel",)),
        name="peer_route",
    )(pqt, k1, k2)


PEER_I1_PER_BLOCK = 8
BF16_ROWS = 16


def _expert_body(hnt_ref, u_first_ref, u_b_ref, u_next_ref, vt_prev_ref, vt_a_ref, vt_last_ref, rank2_ref,
                 cnt_ref, e1_ref, e2z_ref, x1_ref, o_ref, acc_scr, act_next, p_prev):
    s = pl.program_id(1)
    hnt = hnt_ref[...]
    tn = hnt.shape[1]
    tiles = PEER_NKEYS // BF16_ROWS

    def activation(u_ref):
        pre = _dot(u_ref[...], hnt)
        return (pre * (1.0 + lax.erf(pre * (2.0 ** -0.5)))).astype(BF16)

    def row(ref, h, l):
        return jnp.broadcast_to(ref[h, l:l + 1, :], (BF16_ROWS, tn)).astype(BF16)[None]

    def gates(l0, act_rows):
        ps = []
        for li in range(PEER_I1_PER_BLOCK):
            gate = None
            for h in range(PEER_HEADS):
                rank2 = rank2_ref[h].reshape(tiles, BF16_ROWS, tn)
                e2z = e2z_ref[h].reshape(tiles, BF16_ROWS, tn)
                term = jnp.where(rank2 < row(cnt_ref, h, l0 + li), e2z * row(e1_ref, h, l0 + li), 0.0)
                gate = term if gate is None else gate + term
            ps.append(gate.reshape(PEER_NKEYS, tn) * act_rows(li))
        return jnp.concatenate(ps, axis=0)

    @pl.when(s == 0)
    def _():
        acc_scr[...] = jnp.zeros_like(acc_scr)
        p_prev[...] = jnp.zeros_like(p_prev)
        act_next[...] = activation(u_first_ref)

    out_prev = _dot(vt_prev_ref[...], p_prev[...])
    act_b = activation(u_b_ref)
    p_a = gates(0, lambda li: act_next[pl.ds(li * PEER_NKEYS, PEER_NKEYS), :])
    out_a = _dot(vt_a_ref[...], p_a)
    act_next[...] = activation(u_next_ref)
    p_prev[...] = gates(PEER_I1_PER_BLOCK, lambda li: act_b[li * PEER_NKEYS:(li + 1) * PEER_NKEYS, :])
    acc_scr[...] += out_prev + out_a

    @pl.when(s == pl.num_programs(1) - 1)
    def _():
        o_ref[...] = x1_ref[...] + (acc_scr[...] + _dot(vt_last_ref[...], p_prev[...])).T


def _experts(hnt, u_b, vt_b, rank2, cnt, e1, e2z, x1, tn):
    d, t = hnt.shape
    ne = u_b.shape[0]
    eb = PEER_I1_PER_BLOCK * PEER_NKEYS
    nblk = ne // eb
    once = pl.Buffered(1)
    full = pl.BlockSpec((PEER_HEADS, PEER_NKEYS, tn), lambda i, s: (0, 0, i))
    part = pl.BlockSpec((PEER_HEADS, 2 * PEER_I1_PER_BLOCK, tn), lambda i, s: (0, s, i))
    return pl.pallas_call(
        _expert_body,
        grid=(t // tn, nblk // 2),
        in_specs=[
            pl.BlockSpec((d, tn), lambda i, s: (0, i)),
            pl.BlockSpec((eb, d), lambda i, s: (0, 0), pipeline_mode=once),
            pl.BlockSpec((eb, d), lambda i, s: (2 * s + 1, 0)),
            pl.BlockSpec((eb, d), lambda i, s: (jnp.minimum(2 * s + 2, nblk - 1), 0)),
            pl.BlockSpec((d, eb), lambda i, s: (0, jnp.maximum(2 * s - 1, 0))),
            pl.BlockSpec((d, eb), lambda i, s: (0, 2 * s)),
            pl.BlockSpec((d, eb), lambda i, s: (0, nblk - 1), pipeline_mode=once),
            full, part, part, full,
            pl.BlockSpec((tn, d), lambda i, s: (i, 0), pipeline_mode=once),
        ],
        out_specs=pl.BlockSpec((tn, d), lambda i, s: (i, 0)),
        out_shape=jax.ShapeDtypeStruct((t, d), F32),
        scratch_shapes=[pltpu.VMEM((d, tn), F32), pltpu.VMEM((eb, tn), BF16), pltpu.VMEM((eb, tn), BF16)],
        compiler_params=_cparams(("parallel", "arbitrary")),
        name="peer_experts",
    )(hnt, u_b, u_b, u_b, vt_b, vt_b, vt_b, rank2, cnt, e1, e2z, x1)


def _suffix_sum_matrix(tk):
    r = jnp.arange(2 * tk)[:, None] % tk
    c = jnp.arange(2 * tk)[None, :]
    return jnp.where(c < tk, r > c, True).astype(BF16)


class _Tiles(NamedTuple):
    proj_tokens: int
    sb_q: int
    sb_k: int
    gdn_seq: int
    route_tokens: int
    expert_tokens: int


def _tiles(t, s):
    sb_k = 128
    return _Tiles(min(512, t), min(2 * sb_k, s), sb_k, min(512, s), min(256, t), min(512, t))


def _layer(x, norm1_g, w_in, sb_q_g, sb_k_g, sb_o_g, conv_w, a_log, dt_bias, gdn_o_g, w_out, norm2_g,
           w_q, keys1, keys2, u_tab, v_tab):
    b, s, d = x.shape
    t = b * s
    tiles = _tiles(t, s)
    n_main = 3 * SB_WIDTH + 4 * GDN_WIDTH
    x2 = x.reshape(t, d)
    w_main = w_in[:, :n_main].astype(BF16)
    w_ab = jnp.pad(w_in[:, n_main:], ((0, 0), (0, LANES - 2 * GDN_HEADS))).astype(BF16)
    proj, ab = _inproj(x2, norm1_g.reshape(1, d), w_main, w_ab, tiles.proj_tokens)
    proj3 = proj.reshape(b, s, n_main)
    ab3 = ab.reshape(b, s, LANES)

    tile2 = lambda g: jnp.tile(g, 2).reshape(1, LANES)
    sb_o = _sb_attention(proj3, tile2(sb_q_g), tile2(sb_k_g), tile2(sb_o_g), _suffix_sum_matrix(tiles.sb_k),
                         tiles.sb_q, tiles.sb_k)

    on_lanes = lambda p: jnp.pad(p, (0, LANES - GDN_HEADS)).reshape(1, LANES)
    gd_o = _gdn(proj3, ab3, conv_w, on_lanes(a_log), on_lanes(dt_bias), gdn_o_g.reshape(1, LANES), tiles.gdn_seq)

    wo = w_out.astype(BF16)
    x1, hnt, pqt = _outproj(sb_o.reshape(t, SB_WIDTH), gd_o.reshape(t, GDN_WIDTH), x2, wo[:SB_WIDTH],
                            wo[SB_WIDTH:], norm2_g.reshape(1, d), w_q.T.astype(BF16), tiles.proj_tokens)

    rank2, cnt, e1, e2z = _route(pqt, keys1.astype(BF16), keys2.astype(BF16), tiles.route_tokens)
    y = _experts(hnt, u_tab.astype(BF16), v_tab.T.astype(BF16), rank2, cnt, e1, e2z, x1, tiles.expert_tokens)
    return y.reshape(b, s, d)


def kernel(x, norm1_g, w_in, sb_q_norm_g, sb_k_norm_g, sb_out_norm_g, gdn_conv_w, gdn_a_log, gdn_dt_bias,
           gdn_out_norm_g, w_out, norm2_g, peer_w_q, peer_keys1, peer_keys2, peer_u, peer_v):
    for layer in range(norm1_g.shape[0]):
        x = _layer(x, norm1_g[layer], w_in[layer], sb_q_norm_g[layer], sb_k_norm_g[layer],
                   sb_out_norm_g[layer], gdn_conv_w[layer], gdn_a_log[layer], gdn_dt_bias[layer],
                   gdn_out_norm_g[layer], w_out[layer], norm2_g[layer], peer_w_q[layer],
                   peer_keys1[layer], peer_keys2[layer], peer_u[layer], peer_v[layer])
    return x
```

```python
import functools
from typing import NamedTuple

import jax
import jax.numpy as jnp
from jax import lax
from jax.experimental import pallas as pl
from jax.experimental.pallas import tpu as pltpu

F32 = jnp.float32
BF16 = jnp.bfloat16
EPS = 1e-6

SB_HEADS = 8
SB_HEAD_DIM = 64
SB_WIDTH = SB_HEADS * SB_HEAD_DIM
GDN_HEADS = 4
GDN_HEAD_DIM = 128
GDN_WIDTH = GDN_HEADS * GDN_HEAD_DIM
GDN_CONV = 4
PEER_HEADS = 8
PEER_NKEYS = 128
PEER_HALF = 128
PEER_TOPK = 16
PEER_HEADS_PER_PASS = 2
LANES = 128

VMEM_LIMIT = 56 * 1024 * 1024


def _cparams(sem):
    return pltpu.CompilerParams(dimension_semantics=sem, vmem_limit_bytes=VMEM_LIMIT)


def _dot(a, b):
    return jnp.dot(a, b, preferred_element_type=F32)


def _dot_nt(a, b):
    return lax.dot_general(a, b, (((1,), (1,)), ((), ())), preferred_element_type=F32)


def _dot_tn(a, b):
    return lax.dot_general(a, b, (((0,), (0,)), ((), ())), preferred_element_type=F32)


def _split(a):
    hi = a.astype(BF16)
    return hi, (a - hi.astype(F32)).astype(BF16)


def _dot_exact_lhs(a, b):
    ab = a.astype(BF16)
    hi = b.astype(BF16)
    rest = b - hi.astype(F32)
    mid = rest.astype(BF16)
    lo = (rest - mid.astype(F32)).astype(BF16)
    return _dot(jnp.concatenate([ab, ab, ab], axis=1), jnp.concatenate([hi, mid, lo], axis=0))


def _dot3(a, b):
    ah, al = _split(a)
    bh, bl = _split(b)
    return _dot(jnp.concatenate([ah, ah, al], axis=1), jnp.concatenate([bh, bl, bh], axis=0))


def _softplus(x):
    return jnp.maximum(x, 0.0) + jnp.log1p(jnp.exp(-jnp.abs(x)))


def _sigmoid(x):
    return 1.0 / (1.0 + jnp.exp(-x))


def _inproj_body(x_ref, g_ref, w_ref, wab_ref, proj_ref, ab_ref):
    x = x_ref[...]
    ms = jnp.mean(x * x, axis=-1, keepdims=True)
    h = (x * lax.rsqrt(ms + EPS) * g_ref[...]).astype(BF16)
    proj_ref[...] = _dot(h, w_ref[...])
    ab_ref[...] = _dot(h, wab_ref[...])


def _inproj(x2, g, w_main, w_ab, tm):
    t, d = x2.shape
    n = w_main.shape[1]
    return pl.pallas_call(
        _inproj_body,
        grid=(t // tm,),
        in_specs=[
            pl.BlockSpec((tm, d), lambda i: (i, 0)),
            pl.BlockSpec((1, d), lambda i: (0, 0)),
            pl.BlockSpec((d, n), lambda i: (0, 0)),
            pl.BlockSpec((d, LANES), lambda i: (0, 0)),
        ],
        out_specs=[
            pl.BlockSpec((tm, n), lambda i: (i, 0)),
            pl.BlockSpec((tm, LANES), lambda i: (i, 0)),
        ],
        out_shape=[
            jax.ShapeDtypeStruct((t, n), F32),
            jax.ShapeDtypeStruct((t, LANES), F32),
        ],
        compiler_params=_cparams(("parallel",)),
        name="inproj",
    )(x2, g, w_main, w_ab)


SB_DEAD_LOG = -104.0


def _sb_body(q_ref, k_ref, v_ref, gq_ref, gk_ref, go_ref, m2_ref, o_ref, kn_scr, vb_scr, *, tq, tk):
    i = pl.program_id(2)
    lane = lax.broadcasted_iota(jnp.int32, (1, LANES), 1)
    is0 = lane < SB_HEAD_DIM

    def headnorm(x, g):
        x2 = x * x
        s0 = jnp.sum(jnp.where(is0, x2, 0.0), axis=-1, keepdims=True)
        s1 = jnp.sum(jnp.where(is0, 0.0, x2), axis=-1, keepdims=True)
        ms = jnp.where(is0, s0, s1) * (1.0 / SB_HEAD_DIM)
        return x * lax.rsqrt(ms + EPS) * g

    @pl.when(i == 0)
    def _():
        kn_scr[...] = headnorm(k_ref[...], gk_ref[...]).astype(BF16)
        vb_scr[...] = v_ref[...].astype(BF16)

    qn = headnorm(q_ref[...], gq_ref[...]) * (SB_HEAD_DIM ** -0.5)
    q2h = jnp.concatenate([jnp.where(is0, qn, 0.0), jnp.where(is0, 0.0, qn)], axis=0).astype(BF16)
    halves = lambda x: [x[:tq], x[tq:]]
    m2 = m2_ref[...]
    row = i * tq + lax.broadcasted_iota(jnp.int32, (tq, tk), 0)
    col0 = lax.broadcasted_iota(jnp.int32, (tq, tk), 1)
    nkb = (i + 1) * (tq // tk)

    def scores(j):
        kj = kn_scr[pl.ds(pl.multiple_of(j * tk, tk), tk), :]
        return halves(_dot_nt(q2h, kj))

    def logs(j, zs, on_diagonal):
        causal = (col0 + j * tk) < row
        zls, cats = [], []
        for z in zs:
            lk = -(jnp.maximum(z, 0.0) + jnp.log(1.0 + jnp.exp(-jnp.abs(z))))
            zl = z + lk
            if on_diagonal:
                lk = jnp.where(causal, lk, 0.0)
                zl = jnp.where(causal, zl, -jnp.inf)
            hi = lk.astype(BF16)
            lo = (lk - hi.astype(F32)).astype(BF16)
            zls.append(zl)
            cats.append(jnp.concatenate([hi, lo], axis=1))
        return zls, cats

    def sums(cats):
        return halves(_dot(jnp.concatenate(cats, axis=0), m2))

    def pair(n, carry, on_diagonal):
        j = nkb - 1 - 2 * n
        zs_a = scores(j)
        zs_b = scores(j - 1)
        zl_a, cats_a = logs(j, zs_a, on_diagonal)
        rt_a = sums(cats_a)
        zl_b, cats_b = logs(j - 1, zs_b, on_diagonal)
        rt_b = sums(cats_b)
        v2 = vb_scr[pl.ds(pl.multiple_of((j - 1) * tk, tk), 2 * tk), :]
        ws, rests = [], []
        for h in range(2):
            rest = carry[2 * h + 1]
            rest_mid = rest + rt_a[h][:, tk:]
            w_a = jnp.exp(zl_a[h] + (rest + rt_a[h][:, :tk])).astype(BF16)
            w_b = jnp.exp(zl_b[h] + (rest_mid + rt_b[h][:, :tk])).astype(BF16)
            ws.append(jnp.concatenate([w_b, w_a], axis=1))
            rests.append(rest_mid + rt_b[h][:, tk:])
        pv = halves(_dot(jnp.concatenate(ws, axis=0), v2))
        out = [carry[0] + pv[0], rests[0], carry[2] + pv[1], rests[1]]
        alive = (jnp.max(jnp.maximum(out[1], out[3])) > SB_DEAD_LOG).astype(jnp.int32)
        return n + 1, alive, tuple(out)

    def cond(state):
        return jnp.logical_and(state[0] < nkb // 2, state[1] > 0)

    assert tq == 2 * tk
    zero = jnp.zeros((tq, LANES), F32)
    first = pair(0, (zero, zero, zero, zero), True)
    res = lax.while_loop(cond, lambda state: pair(state[0], state[2], False), first)[2]
    o = jnp.where(is0, res[0], res[2])
    o_ref[...] = headnorm(o, go_ref[...])


def _sb_attention(proj3, gq, gk, go, m2, tq, tk):
    b, s, _ = proj3.shape
    hp = SB_HEADS // 2
    return pl.pallas_call(
        functools.partial(_sb_body, tq=tq, tk=tk),
        grid=(b, hp, s // tq),
        in_specs=[
            pl.BlockSpec((None, tq, LANES), lambda bi, h, i: (bi, i, h)),
            pl.BlockSpec((None, s, LANES), lambda bi, h, i: (bi, 0, hp + h)),
            pl.BlockSpec((None, s, LANES), lambda bi, h, i: (bi, 0, 2 * hp + h)),
            pl.BlockSpec((1, LANES), lambda bi, h, i: (0, 0)),
            pl.BlockSpec((1, LANES), lambda bi, h, i: (0, 0)),
            pl.BlockSpec((1, LANES), lambda bi, h, i: (0, 0)),
            pl.BlockSpec((2 * tk, 2 * tk), lambda bi, h, i: (0, 0)),
        ],
        out_specs=pl.BlockSpec((None, tq, LANES), lambda bi, h, i: (bi, i, h)),
        out_shape=jax.ShapeDtypeStruct((b, s, SB_WIDTH), F32),
        scratch_shapes=[pltpu.VMEM((s, LANES), BF16), pltpu.VMEM((s, LANES), BF16)],
        compiler_params=_cparams(("parallel", "parallel", "arbitrary")),
        name="sb_attention",
    )(proj3, proj3, proj3, gq, gk, go, m2)


GDN_BLOCK = 128
GDN_HALO = 8


def _gdn_body(x_ref, halo_ref, z_ref, ab_ref, cw_ref, alog_ref, dtb_ref, og_ref, o_ref,
              q_scr, k_scr, v_scr, g_scr, beta_scr, state_scr, *, ts):
    t = pl.program_id(1)
    c = GDN_BLOCK
    nh = GDN_HEADS
    lane = lax.broadcasted_iota(jnp.int32, (1, LANES), 1)

    @pl.when(t == 0)
    def _():
        state_scr[...] = jnp.zeros_like(state_scr)

    halo = jnp.where(t > 0, halo_ref[...], 0.0)
    xe = jnp.concatenate([halo, x_ref[...]], axis=0)
    cw = cw_ref[...]
    y = xe * cw[3:4, :]
    for d in (1, 2, 3):
        y = y + pltpu.roll(xe, d, 0) * cw[3 - d:4 - d, :]
    y = y[GDN_HALO:, :]
    y = y * _sigmoid(y)

    def l2n(x):
        return x * lax.rsqrt(jnp.sum(x * x, axis=-1, keepdims=True) + EPS)

    ab = ab_ref[...]
    g_all = -jnp.exp(alog_ref[...]) * _softplus(ab + dtb_ref[...])
    beta_all = _sigmoid(ab)
    for h in range(nh):
        sl = pl.ds(h * LANES, LANES)
        q_scr[:, sl] = l2n(y[:, h * LANES:(h + 1) * LANES]) * (GDN_HEAD_DIM ** -0.5)
        k_scr[:, sl] = l2n(y[:, GDN_WIDTH + h * LANES:GDN_WIDTH + (h + 1) * LANES])
        g_col = jnp.sum(jnp.where(lane == h, g_all, 0.0), axis=-1, keepdims=True)
        beta_col = jnp.sum(jnp.where(lane == h + nh, beta_all, 0.0), axis=-1, keepdims=True)
        g_scr[:, sl] = jnp.broadcast_to(g_col, (ts, LANES))
        beta_scr[:, sl] = jnp.broadcast_to(beta_col, (ts, LANES))
    v_scr[...] = y[:, 2 * GDN_WIDTH:]

    ri = lax.broadcasted_iota(jnp.int32, (c, nh * c), 0)
    ci = lax.broadcasted_iota(jnp.int32, (c, nh * c), 1) % c
    lower_incl = ci <= ri
    lower_strict = ci < ri
    eye = (ci == ri).astype(F32)
    ltri = lower_incl[:, :c].astype(F32)
    ones = jnp.ones((c, c), F32)
    og = og_ref[...]
    heads = range(nh)
    hs = lambda m, h: m[:, h * c:(h + 1) * c]

    def blk(n, states):
        r0 = pl.multiple_of(n * c, c)
        q = q_scr[pl.ds(r0, c), :]
        k = k_scr[pl.ds(r0, c), :]
        v = v_scr[pl.ds(r0, c), :]
        g = g_scr[pl.ds(r0, c), :]
        beta = beta_scr[pl.ds(r0, c), :]
        gc = _dot_exact_lhs(ltri, g)
        gc_row = _dot_exact_lhs(ones, gc * eye)
        decay = jnp.exp(jnp.where(lower_incl, gc - gc_row, -jnp.inf))
        kb = k * beta
        kbf = k.astype(BF16)
        kbb = kb.astype(BF16)
        qbf = q.astype(BF16)
        kq = [_dot_nt(jnp.concatenate([hs(kbb, h), hs(qbf, h)], axis=0), hs(kbf, h)) for h in heads]
        kk = jnp.concatenate([x[:c] for x in kq], axis=1)
        qk = jnp.concatenate([x[c:] for x in kq], axis=1)
        a = jnp.where(lower_strict, kk * decay, 0.0)
        attn = jnp.where(lower_incl, qk * decay, 0.0).astype(BF16)
        nmat = [-hs(a, h) for h in heads]
        xp = [_dot3(hs(a, h), hs(a, h)) for h in heads]
        for _ in range(5):
            both = [_dot3(jnp.concatenate([nm, x], axis=0), x) for nm, x in zip(nmat, xp)]
            nmat = [nm + x + nx[:c] for nm, x, nx in zip(nmat, xp, both)]
            xp = [nx[c:] for nx in both]
        prod = [_dot3(nm, x) for nm, x in zip(nmat, xp)]
        nmat = [nm + x + p for nm, x, p in zip(nmat, xp, prod)]
        eg = jnp.exp(gc)
        vb = v * beta
        kbd = kb * eg
        gl = gc[c - 1:c, :]
        qd = (q * eg).astype(BF16)
        kd = (k * jnp.exp(gl - gc)).astype(BF16)
        dl = jnp.exp(gl)
        uw_rhs = [jnp.concatenate([hs(vb, h), hs(kbd, h)], axis=1) for h in heads]
        uw = [r + _dot(nm.astype(BF16), r.astype(BF16)) for nm, r in zip(nmat, uw_rhs)]
        u = [x[:, :c] for x in uw]
        w = [x[:, c:].astype(BF16) for x in uw]
        sb = [st.astype(BF16) for st in states]
        ws = [_dot(jnp.concatenate([w[h], hs(qd, h)], axis=0), sb[h]) for h in heads]
        v_new = [u[h] - ws[h][:c] for h in heads]
        vnb = [vn.astype(BF16) for vn in v_new]
        o = [ws[h][c:] + _dot(hs(attn, h), vnb[h]) for h in heads]
        new_states = tuple(states[h] * hs(dl, h) + _dot_tn(hs(kd, h), vnb[h]) for h in heads)
        on = jnp.concatenate(
            [x * lax.rsqrt(jnp.mean(x * x, axis=-1, keepdims=True) + EPS) * og for x in o], axis=1)
        z = z_ref[pl.ds(r0, c), :]
        o_ref[pl.ds(r0, c), :] = on * (z * _sigmoid(z))
        return new_states

    states = lax.fori_loop(0, ts // c, blk, tuple(state_scr[h] for h in heads))
    for h in heads:
        state_scr[h] = states[h]


def _gdn(proj3, ab3, conv_w, alog_b, dtb_b, og, ts):
    b, s, _ = proj3.shape
    w3 = 3 * GDN_WIDTH
    assert 3 * SB_WIDTH == w3 and 2 * w3 % GDN_WIDTH == 0
    per = ts // GDN_HALO
    par = pl.BlockSpec((1, LANES), lambda bi, t: (0, 0))
    return pl.pallas_call(
        functools.partial(_gdn_body, ts=ts),
        grid=(b, s // ts),
        in_specs=[
            pl.BlockSpec((None, ts, w3), lambda bi, t: (bi, t, 1)),
            pl.BlockSpec((None, GDN_HALO, w3), lambda bi, t: (bi, jnp.maximum(t * per - 1, 0), 1)),
            pl.BlockSpec((None, ts, GDN_WIDTH), lambda bi, t: (bi, t, 2 * w3 // GDN_WIDTH)),
            pl.BlockSpec((None, ts, LANES), lambda bi, t: (bi, t, 0)),
            pl.BlockSpec((GDN_CONV, w3), lambda bi, t: (0, 0)),
            par, par,
            pl.BlockSpec((1, LANES), lambda bi, t: (0, 0)),
        ],
        out_specs=pl.BlockSpec((None, ts, GDN_WIDTH), lambda bi, t: (bi, t, 0)),
        out_shape=jax.ShapeDtypeStruct((b, s, GDN_WIDTH), F32),
        scratch_shapes=[pltpu.VMEM((ts, GDN_WIDTH), F32) for _ in range(5)]
        + [pltpu.VMEM((GDN_HEADS, GDN_HEAD_DIM, GDN_HEAD_DIM), F32)],
        compiler_params=_cparams(("parallel", "arbitrary")),
        name="gdn",
    )(proj3, proj3, proj3, ab3, conv_w, alog_b, dtb_b, og)


def _outproj_body(sb_ref, gd_ref, x_ref, wo1_ref, wo2_ref, g2_ref, wqt_ref, x1_ref, hnt_ref, pqt_ref):
    mix = _dot(sb_ref[...].astype(BF16), wo1_ref[...]) + _dot(gd_ref[...].astype(BF16), wo2_ref[...])
    x1 = x_ref[...] + mix
    x1_ref[...] = x1
    ms = jnp.mean(x1 * x1, axis=-1, keepdims=True)
    hn = x1 * lax.rsqrt(ms + EPS) * g2_ref[...]
    hnt = hn.T.astype(BF16)
    hnt_ref[...] = hnt
    pqt_ref[...] = _dot(wqt_ref[...], hnt)


def _outproj(sb_o, gd_o, x2, wo1, wo2, g2, wqt, tm):
    t, d = x2.shape
    nq = wqt.shape[0]
    return pl.pallas_call(
        _outproj_body,
        grid=(t // tm,),
        in_specs=[
            pl.BlockSpec((tm, SB_WIDTH), lambda i: (i, 0)),
            pl.BlockSpec((tm, GDN_WIDTH), lambda i: (i, 0)),
            pl.BlockSpec((tm, d), lambda i: (i, 0)),
            pl.BlockSpec((SB_WIDTH, d), lambda i: (0, 0)),
            pl.BlockSpec((GDN_WIDTH, d), lambda i: (0, 0)),
            pl.BlockSpec((1, d), lambda i: (0, 0)),
            pl.BlockSpec((nq, d), lambda i: (0, 0)),
        ],
        out_specs=[
            pl.BlockSpec((tm, d), lambda i: (i, 0)),
            pl.BlockSpec((d, tm), lambda i: (0, i)),
            pl.BlockSpec((nq, tm), lambda i: (0, i)),
        ],
        out_shape=[
            jax.ShapeDtypeStruct((t, d), F32),
            jax.ShapeDtypeStruct((d, t), BF16),
            jax.ShapeDtypeStruct((nq, t), F32),
        ],
        compiler_params=_cparams(("parallel",)),
        name="outproj",
    )(sb_o, gd_o, x2, wo1, wo2, g2, wqt)


def _extract_topk(arrays, ids, k, want_rank):
    del want_rank
    big = jnp.int32(2 ** 30)
    works = list(arrays)
    vals = [[] for _ in arrays]
    ranks = [jnp.full(s.shape, float(k), F32) for s in arrays]
    for r in range(k):
        ms = [jnp.max(w, axis=0, keepdims=True) for w in works]
        firsts = [jnp.min(jnp.where(w == m, i, big), axis=0, keepdims=True) for w, m, i in zip(works, ms, ids)]
        hits = [i == f for i, f in zip(ids, firsts)]
        ranks = [jnp.where(hit, float(r), rank) for hit, rank in zip(hits, ranks)]
        works = [jnp.where(hit, -jnp.inf, w) for hit, w in zip(hits, works)]
        for v, m in zip(vals, ms):
            v.append(m)
    return [(v, rank, rank < float(k), None) for v, rank in zip(vals, ranks)]


def _extract_topk_untied(arrays, ids, k, want_rank):
    del ids
    works = list(arrays)
    vals = [[] for _ in arrays]
    ranks = [jnp.full(s.shape, float(k), F32) if want else None for s, want in zip(arrays, want_rank)]
    for r in range(k):
        ms = [jnp.max(w, axis=0, keepdims=True) for w in works]
        hits = [w == m for w, m in zip(works, ms)]
        ranks = [None if rank is None else jnp.where(hit, float(r), rank) for hit, rank in zip(hits, ranks)]
        works = [jnp.where(hit, -jnp.inf, w) for hit, w in zip(hits, works)]
        for v, m in zip(vals, ms):
            v.append(m)
    out = []
    for s, w, v, rank in zip(arrays, works, vals, ranks):
        taken = jnp.logical_and(w == -jnp.inf, s > -jnp.inf) if rank is None else rank < float(k)
        count = jnp.sum(taken.astype(F32), axis=0, keepdims=True)
        out.append((v, rank, taken, count == float(k)))
    return out


def _route_body(pqt_ref, k1_ref, k2_ref, rank2_ref, cnt_ref, e1_ref, e2z_ref):
    kk = PEER_TOPK
    tn = pqt_ref.shape[1]
    key_ids = lax.broadcasted_iota(jnp.int32, (PEER_NKEYS, tn), 0)
    i8 = lax.broadcasted_iota(jnp.int32, (8, tn), 0)
    i16 = lax.broadcasted_iota(jnp.int32, (kk, tn), 0)
    cand_ids = jnp.concatenate(
        [i16] + [i8 + a * kk for a in (1, 2, 3)]
        + [jnp.where(i8 >= 4, i8 * kk + b, kk * kk + i8 * kk + b) for b in (0, 1, 2)] + [(i8 + 8) * kk], axis=0)
    def candidates(v1, v2):
        v1a = jnp.concatenate(v1, axis=0)
        v2a = jnp.concatenate(v2, axis=0)
        return jnp.concatenate(
            [v1[0] + v2a] + [v1[a] + v2a[:8] for a in (1, 2, 3)]
            + [jnp.where(i8 >= 4, v1a[:8] + v2[b], -jnp.inf) for b in (0, 1, 2)] + [v1a[8:] + v2[0]], axis=0)

    def route_heads(heads, s1s, s2s, extract, tie_rule):
        n = len(heads)
        first = extract(s1s + s2s, [key_ids] * (2 * n), kk, [tie_rule] * n + [True] * n)
        second = extract([candidates(first[i][0], first[n + i][0]) for i in range(n)], [cand_ids] * n, kk,
                         [False] * n)
        return [finish(heads[i], s1s[i], s2s[i], first[i], first[n + i], second[i], tie_rule) for i in range(n)]

    def finish(h, s1, s2, first1, first2, second, tie_rule):
        v1, rank1, _, ok1 = first1
        v2, rank2, _, ok2 = first2
        top, _, taken, ok3 = second
        zsum = jnp.ones_like(top[0])
        for r in range(1, kk):
            zsum = zsum + jnp.exp(top[r] - top[0])
        sel = taken.astype(F32)
        low = [jnp.sum(sel[0:16], axis=0, keepdims=True)] + [
            jnp.sum(sel[8 + 8 * a:16 + 8 * a], axis=0, keepdims=True) for a in (1, 2, 3)]
        mid = sel[40:48] + sel[48:56] + sel[56:64]
        high = sel[64:72]
        cnt = jnp.zeros(s1.shape, F32)
        for a in range(kk):
            n_a = low[a] if a < 4 else (mid[a:a + 1] if a < 8 else high[a - 8:a - 7])
            cnt = jnp.where(rank1 == float(a) if tie_rule else s1 == v1[a], n_a, cnt)
        rank2_ref[h] = rank2.astype(BF16)
        cnt_ref[h] = cnt
        e1_ref[h] = jnp.exp(s1 - v1[0])
        e2z_ref[h] = (jnp.exp(s2 - v2[0]) * (0.5 / zsum)).astype(BF16)
        return None if ok1 is None else jnp.logical_and(jnp.logical_and(ok1, ok2), ok3)

    def scores(keys_ref, h, half):
        q = pqt_ref[pl.ds((2 * h + half) * PEER_HALF, PEER_HALF), :].astype(BF16)
        return _dot(keys_ref[h], q)

    for h0 in range(0, PEER_HEADS, PEER_HEADS_PER_PASS):
        heads = list(range(h0, h0 + PEER_HEADS_PER_PASS))
        s1s = [scores(k1_ref, h, 0) for h in heads]
        s2s = [scores(k2_ref, h, 1) for h in heads]
        oks = route_heads(heads, s1s, s2s, _extract_topk_untied, False)
        for h, s1, s2, ok in zip(heads, s1s, s2s, oks):
            tied = jnp.max(jnp.where(ok, 0.0, 1.0)) > 0.0

            @pl.when(tied)
            def _(h=h, s1=s1, s2=s2):
                route_heads([h], [s1], [s2], _extract_topk, True)


def _route(pqt, k1, k2, tn):
    nq, t = pqt.shape
    hk = (PEER_HEADS, PEER_NKEYS, PEER_HALF)
    out = lambda dt: jax.ShapeDtypeStruct((PEER_HEADS, PEER_NKEYS, t), dt)
    ospec = pl.BlockSpec((PEER_HEADS, PEER_NKEYS, tn), lambda i: (0, 0, i))
    return pl.pallas_call(
        _route_body,
        grid=(t // tn,),
        in_specs=[
            pl.BlockSpec((nq, tn), lambda i: (0, i)),
            pl.BlockSpec(hk, lambda i: (0, 0, 0)),
            pl.BlockSpec(hk, lambda i: (0, 0, 0)),
        ],
        out_specs=[ospec, ospec, ospec, ospec],
        out_shape=[out(BF16), out(F32), out(F32), out(BF16)],
        compiler_params=_cparams(("parallel",)),
        name="peer_route",
    )(pqt, k1, k2)


PEER_I1_PER_BLOCK = 8
BF16_ROWS = 16


def _expert_body(hnt_ref, u_first_ref, u_b_ref, u_next_ref, vt_prev_ref, vt_a_ref, vt_last_ref, rank2_ref,
                 cnt_ref, e1_ref, e2z_ref, x1_ref, o_ref, acc_scr, act_next, p_prev):
    s = pl.program_id(1)
    hnt = hnt_ref[...]
    tn = hnt.shape[1]
    tiles = PEER_NKEYS // BF16_ROWS

    def activation(u_ref):
        pre = _dot(u_ref[...], hnt)
        return (pre * (1.0 + lax.erf(pre * (2.0 ** -0.5)))).astype(BF16)

    def row(ref, h, l):
        return jnp.broadcast_to(ref[h, l:l + 1, :], (BF16_ROWS, tn)).astype(BF16)[None]

    def gates(l0, act_rows):
        ps = []
        for li in range(PEER_I1_PER_BLOCK):
            gate = None
            for h in range(PEER_HEADS):
                rank2 = rank2_ref[h].reshape(tiles, BF16_ROWS, tn)
                e2z = e2z_ref[h].reshape(tiles, BF16_ROWS, tn)
                term = jnp.where(rank2 < row(cnt_ref, h, l0 + li), e2z * row(e1_ref, h, l0 + li), 0.0)
                gate = term if gate is None else gate + term
            ps.append(gate.reshape(PEER_NKEYS, tn) * act_rows(li))
        return jnp.concatenate(ps, axis=0)

    @pl.when(s == 0)
    def _():
        acc_scr[...] = jnp.zeros_like(acc_scr)
        p_prev[...] = jnp.zeros_like(p_prev)
        act_next[...] = activation(u_first_ref)

    out_prev = _dot(vt_prev_ref[...], p_prev[...])
    act_b = activation(u_b_ref)
    p_a = gates(0, lambda li: act_next[pl.ds(li * PEER_NKEYS, PEER_NKEYS), :])
    out_a = _dot(vt_a_ref[...], p_a)
    act_next[...] = activation(u_next_ref)
    p_prev[...] = gates(PEER_I1_PER_BLOCK, lambda li: act_b[li * PEER_NKEYS:(li + 1) * PEER_NKEYS, :])
    acc_scr[...] += out_prev + out_a

    @pl.when(s == pl.num_programs(1) - 1)
    def _():
        o_ref[...] = x1_ref[...] + (acc_scr[...] + _dot(vt_last_ref[...], p_prev[...])).T


def _experts(hnt, u_b, vt_b, rank2, cnt, e1, e2z, x1, tn):
    d, t = hnt.shape
    ne = u_b.shape[0]
    eb = PEER_I1_PER_BLOCK * PEER_NKEYS
    nblk = ne // eb
    once = pl.Buffered(1)
    full = pl.BlockSpec((PEER_HEADS, PEER_NKEYS, tn), lambda i, s: (0, 0, i))
    part = pl.BlockSpec((PEER_HEADS, 2 * PEER_I1_PER_BLOCK, tn), lambda i, s: (0, s, i))
    return pl.pallas_call(
        _expert_body,
        grid=(t // tn, nblk // 2),
        in_specs=[
            pl.BlockSpec((d, tn), lambda i, s: (0, i)),
            pl.BlockSpec((eb, d), lambda i, s: (0, 0), pipeline_mode=once),
            pl.BlockSpec((eb, d), lambda i, s: (2 * s + 1, 0)),
            pl.BlockSpec((eb, d), lambda i, s: (jnp.minimum(2 * s + 2, nblk - 1), 0)),
            pl.BlockSpec((d, eb), lambda i, s: (0, jnp.maximum(2 * s - 1, 0))),
            pl.BlockSpec((d, eb), lambda i, s: (0, 2 * s)),
            pl.BlockSpec((d, eb), lambda i, s: (0, nblk - 1), pipeline_mode=once),
            full, part, part, full,
            pl.BlockSpec((tn, d), lambda i, s: (i, 0), pipeline_mode=once),
        ],
        out_specs=pl.BlockSpec((tn, d), lambda i, s: (i, 0)),
        out_shape=jax.ShapeDtypeStruct((t, d), F32),
        scratch_shapes=[pltpu.VMEM((d, tn), F32), pltpu.VMEM((eb, tn), BF16), pltpu.VMEM((eb, tn), BF16)],
        compiler_params=_cparams(("parallel", "arbitrary")),
        name="peer_experts",
    )(hnt, u_b, u_b, u_b, vt_b, vt_b, vt_b, rank2, cnt, e1, e2z, x1)


def _suffix_sum_matrix(tk):
    r = jnp.arange(2 * tk)[:, None] % tk
    c = jnp.arange(2 * tk)[None, :]
    return jnp.where(c < tk, r > c, True).astype(BF16)


class _Tiles(NamedTuple):
    proj_tokens: int
    sb_q: int
    sb_k: int
    gdn_seq: int
    route_tokens: int
    expert_tokens: int


def _tiles(t, s):
    sb_k = 128
    return _Tiles(min(512, t), min(2 * sb_k, s), sb_k, min(512, s), min(256, t), min(512, t))


def _layer(x, norm1_g, w_in, sb_q_g, sb_k_g, sb_o_g, conv_w, a_log, dt_bias, gdn_o_g, w_out, norm2_g,
           w_q, keys1, keys2, u_tab, v_tab):
    b, s, d = x.shape
    t = b * s
    tiles = _tiles(t, s)
    n_main = 3 * SB_WIDTH + 4 * GDN_WIDTH
    x2 = x.reshape(t, d)
    w_main = w_in[:, :n_main].astype(BF16)
    w_ab = jnp.pad(w_in[:, n_main:], ((0, 0), (0, LANES - 2 * GDN_HEADS))).astype(BF16)
    proj, ab = _inproj(x2, norm1_g.reshape(1, d), w_main, w_ab, tiles.proj_tokens)
    proj3 = proj.reshape(b, s, n_main)
    ab3 = ab.reshape(b, s, LANES)

    tile2 = lambda g: jnp.tile(g, 2).reshape(1, LANES)
    sb_o = _sb_attention(proj3, tile2(sb_q_g), tile2(sb_k_g), tile2(sb_o_g), _suffix_sum_matrix(tiles.sb_k),
                         tiles.sb_q, tiles.sb_k)

    on_lanes = lambda p: jnp.pad(p, (0, LANES - GDN_HEADS)).reshape(1, LANES)
    gd_o = _gdn(proj3, ab3, conv_w, on_lanes(a_log), on_lanes(dt_bias), gdn_o_g.reshape(1, LANES), tiles.gdn_seq)

    wo = w_out.astype(BF16)
    x1, hnt, pqt = _outproj(sb_o.reshape(t, SB_WIDTH), gd_o.reshape(t, GDN_WIDTH), x2, wo[:SB_WIDTH],
                            wo[SB_WIDTH:], norm2_g.reshape(1, d), w_q.T.astype(BF16), tiles.proj_tokens)

    rank2, cnt, e1, e2z = _route(pqt, keys1.astype(BF16), keys2.astype(BF16), tiles.route_tokens)
    y = _experts(hnt, u_tab.astype(BF16), v_tab.T.astype(BF16), rank2, cnt, e1, e2z, x1, tiles.expert_tokens)
    return y.reshape(b, s, d)


def kernel(x, norm1_g, w_in, sb_q_norm_g, sb_k_norm_g, sb_out_norm_g, gdn_conv_w, gdn_a_log, gdn_dt_bias,
           gdn_out_norm_g, w_out, norm2_g, peer_w_q, peer_keys1, peer_keys2, peer_u, peer_v):
    for layer in range(norm1_g.shape[0]):
        x = _layer(x, norm1_g[layer], w_in[layer], sb_q_norm_g[layer], sb_k_norm_g[layer],
                   sb_out_norm_g[layer], gdn_conv_w[layer], gdn_a_log[layer], gdn_dt_bias[layer],
                   gdn_out_norm_g[layer], w_out[layer], norm2_g[layer], peer_w_q[layer],
                   peer_keys1[layer], peer_keys2[layer], peer_u[layer], peer_v[layer])
    return x
```

```python
import functools
from typing import NamedTuple

import jax
import jax.numpy as jnp
from jax import lax
from jax.experimental import pallas as pl
from jax.experimental.pallas import tpu as pltpu

F32 = jnp.float32
BF16 = jnp.bfloat16
EPS = 1e-6

SB_HEADS = 8
SB_HEAD_DIM = 64
SB_WIDTH = SB_HEADS * SB_HEAD_DIM
GDN_HEADS = 4
GDN_HEAD_DIM = 128
GDN_WIDTH = GDN_HEADS * GDN_HEAD_DIM
GDN_CONV = 4
PEER_HEADS = 8
PEER_NKEYS = 128
PEER_HALF = 128
PEER_TOPK = 16
PEER_HEADS_PER_PASS = 4
LANES = 128

VMEM_LIMIT = 56 * 1024 * 1024


def _cparams(sem):
    return pltpu.CompilerParams(dimension_semantics=sem, vmem_limit_bytes=VMEM_LIMIT)


def _dot(a, b):
    return jnp.dot(a, b, preferred_element_type=F32)


def _dot_nt(a, b):
    return lax.dot_general(a, b, (((1,), (1,)), ((), ())), preferred_element_type=F32)


def _dot_tn(a, b):
    return lax.dot_general(a, b, (((0,), (0,)), ((), ())), preferred_element_type=F32)


def _split(a):
    hi = a.astype(BF16)
    return hi, (a - hi.astype(F32)).astype(BF16)


def _dot_exact_lhs(a, b):
    ab = a.astype(BF16)
    hi = b.astype(BF16)
    rest = b - hi.astype(F32)
    mid = rest.astype(BF16)
    lo = (rest - mid.astype(F32)).astype(BF16)
    return _dot(jnp.concatenate([ab, ab, ab], axis=1), jnp.concatenate([hi, mid, lo], axis=0))


def _dot3(a, b):
    ah, al = _split(a)
    bh, bl = _split(b)
    return _dot(jnp.concatenate([ah, ah, al], axis=1), jnp.concatenate([bh, bl, bh], axis=0))


def _softplus(x):
    return jnp.maximum(x, 0.0) + jnp.log1p(jnp.exp(-jnp.abs(x)))


def _sigmoid(x):
    return 1.0 / (1.0 + jnp.exp(-x))


def _inproj_body(x_ref, g_ref, w_ref, wab_ref, proj_ref, ab_ref):
    x = x_ref[...]
    ms = jnp.mean(x * x, axis=-1, keepdims=True)
    h = (x * lax.rsqrt(ms + EPS) * g_ref[...]).astype(BF16)
    proj_ref[...] = _dot(h, w_ref[...])
    ab_ref[...] = _dot(h, wab_ref[...])


def _inproj(x2, g, w_main, w_ab, tm):
    t, d = x2.shape
    n = w_main.shape[1]
    return pl.pallas_call(
        _inproj_body,
        grid=(t // tm,),
        in_specs=[
            pl.BlockSpec((tm, d), lambda i: (i, 0)),
            pl.BlockSpec((1, d), lambda i: (0, 0)),
            pl.BlockSpec((d, n), lambda i: (0, 0)),
            pl.BlockSpec((d, LANES), lambda i: (0, 0)),
        ],
        out_specs=[
            pl.BlockSpec((tm, n), lambda i: (i, 0)),
            pl.BlockSpec((tm, LANES), lambda i: (i, 0)),
        ],
        out_shape=[
            jax.ShapeDtypeStruct((t, n), F32),
            jax.ShapeDtypeStruct((t, LANES), F32),
        ],
        compiler_params=_cparams(("parallel",)),
        name="inproj",
    )(x2, g, w_main, w_ab)


SB_DEAD_LOG = -104.0


def _sb_body(q_ref, k_ref, v_ref, gq_ref, gk_ref, go_ref, m2_ref, o_ref, kn_scr, vb_scr, *, tq, tk):
    i = pl.program_id(2)
    lane = lax.broadcasted_iota(jnp.int32, (1, LANES), 1)
    is0 = lane < SB_HEAD_DIM

    def headnorm(x, g):
        x2 = x * x
        s0 = jnp.sum(jnp.where(is0, x2, 0.0), axis=-1, keepdims=True)
        s1 = jnp.sum(jnp.where(is0, 0.0, x2), axis=-1, keepdims=True)
        ms = jnp.where(is0, s0, s1) * (1.0 / SB_HEAD_DIM)
        return x * lax.rsqrt(ms + EPS) * g

    @pl.when(i == 0)
    def _():
        kn_scr[...] = headnorm(k_ref[...], gk_ref[...]).astype(BF16)
        vb_scr[...] = v_ref[...].astype(BF16)

    qn = headnorm(q_ref[...], gq_ref[...]) * (SB_HEAD_DIM ** -0.5)
    q2h = jnp.concatenate([jnp.where(is0, qn, 0.0), jnp.where(is0, 0.0, qn)], axis=0).astype(BF16)
    halves = lambda x: [x[:tq], x[tq:]]
    m2 = m2_ref[...]
    row = i * tq + lax.broadcasted_iota(jnp.int32, (tq, tk), 0)
    col0 = lax.broadcasted_iota(jnp.int32, (tq, tk), 1)
    nkb = (i + 1) * (tq // tk)

    def scores(j):
        kj = kn_scr[pl.ds(pl.multiple_of(j * tk, tk), tk), :]
        return halves(_dot_nt(q2h, kj))

    def logs(j, zs, on_diagonal):
        causal = (col0 + j * tk) < row
        zls, cats = [], []
        for z in zs:
            lk = -(jnp.maximum(z, 0.0) + jnp.log(1.0 + jnp.exp(-jnp.abs(z))))
            zl = z + lk
            if on_diagonal:
                lk = jnp.where(causal, lk, 0.0)
                zl = jnp.where(causal, zl, -jnp.inf)
            hi = lk.astype(BF16)
            lo = (lk - hi.astype(F32)).astype(BF16)
            zls.append(zl)
            cats.append(jnp.concatenate([hi, lo], axis=1))
        return zls, cats

    def sums(cats):
        return halves(_dot(jnp.concatenate(cats, axis=0), m2))

    def pair(n, carry, on_diagonal):
        j = nkb - 1 - 2 * n
        zs_a = scores(j)
        zs_b = scores(j - 1)
        zl_a, cats_a = logs(j, zs_a, on_diagonal)
        rt_a = sums(cats_a)
        zl_b, cats_b = logs(j - 1, zs_b, on_diagonal)
        rt_b = sums(cats_b)
        v2 = vb_scr[pl.ds(pl.multiple_of((j - 1) * tk, tk), 2 * tk), :]
        ws, rests = [], []
        for h in range(2):
            rest = carry[2 * h + 1]
            rest_mid = rest + rt_a[h][:, tk:]
            w_a = jnp.exp(zl_a[h] + (rest + rt_a[h][:, :tk])).astype(BF16)
            w_b = jnp.exp(zl_b[h] + (rest_mid + rt_b[h][:, :tk])).astype(BF16)
            ws.append(jnp.concatenate([w_b, w_a], axis=1))
            rests.append(rest_mid + rt_b[h][:, tk:])
        pv = halves(_dot(jnp.concatenate(ws, axis=0), v2))
        out = [carry[0] + pv[0], rests[0], carry[2] + pv[1], rests[1]]
        alive = (jnp.max(jnp.maximum(out[1], out[3])) > SB_DEAD_LOG).astype(jnp.int32)
        return n + 1, alive, tuple(out)

    def cond(state):
        return jnp.logical_and(state[0] < nkb // 2, state[1] > 0)

    assert tq == 2 * tk
    zero = jnp.zeros((tq, LANES), F32)
    first = pair(0, (zero, zero, zero, zero), True)
    res = lax.while_loop(cond, lambda state: pair(state[0], state[2], False), first)[2]
    o = jnp.where(is0, res[0], res[2])
    o_ref[...] = headnorm(o, go_ref[...])


def _sb_attention(proj3, gq, gk, go, m2, tq, tk):
    b, s, _ = proj3.shape
    hp = SB_HEADS // 2
    return pl.pallas_call(
        functools.partial(_sb_body, tq=tq, tk=tk),
        grid=(b, hp, s // tq),
        in_specs=[
            pl.BlockSpec((None, tq, LANES), lambda bi, h, i: (bi, i, h)),
            pl.BlockSpec((None, s, LANES), lambda bi, h, i: (bi, 0, hp + h)),
            pl.BlockSpec((None, s, LANES), lambda bi, h, i: (bi, 0, 2 * hp + h)),
            pl.BlockSpec((1, LANES), lambda bi, h, i: (0, 0)),
            pl.BlockSpec((1, LANES), lambda bi, h, i: (0, 0)),
            pl.BlockSpec((1, LANES), lambda bi, h, i: (0, 0)),
            pl.BlockSpec((2 * tk, 2 * tk), lambda bi, h, i: (0, 0)),
        ],
        out_specs=pl.BlockSpec((None, tq, LANES), lambda bi, h, i: (bi, i, h)),
        out_shape=jax.ShapeDtypeStruct((b, s, SB_WIDTH), F32),
        scratch_shapes=[pltpu.VMEM((s, LANES), BF16), pltpu.VMEM((s, LANES), BF16)],
        compiler_params=_cparams(("parallel", "parallel", "arbitrary")),
        name="sb_attention",
    )(proj3, proj3, proj3, gq, gk, go, m2)


GDN_BLOCK = 128
GDN_HALO = 8


def _gdn_body(x_ref, halo_ref, z_ref, ab_ref, cw_ref, alog_ref, dtb_ref, og_ref, o_ref,
              q_scr, k_scr, v_scr, g_scr, beta_scr, state_scr, *, ts):
    t = pl.program_id(1)
    c = GDN_BLOCK
    nh = GDN_HEADS
    lane = lax.broadcasted_iota(jnp.int32, (1, LANES), 1)

    @pl.when(t == 0)
    def _():
        state_scr[...] = jnp.zeros_like(state_scr)

    halo = jnp.where(t > 0, halo_ref[...], 0.0)
    xe = jnp.concatenate([halo, x_ref[...]], axis=0)
    cw = cw_ref[...]
    y = xe * cw[3:4, :]
    for d in (1, 2, 3):
        y = y + pltpu.roll(xe, d, 0) * cw[3 - d:4 - d, :]
    y = y[GDN_HALO:, :]
    y = y * _sigmoid(y)

    def l2n(x):
        return x * lax.rsqrt(jnp.sum(x * x, axis=-1, keepdims=True) + EPS)

    ab = ab_ref[...]
    g_all = -jnp.exp(alog_ref[...]) * _softplus(ab + dtb_ref[...])
    beta_all = _sigmoid(ab)
    for h in range(nh):
        sl = pl.ds(h * LANES, LANES)
        q_scr[:, sl] = l2n(y[:, h * LANES:(h + 1) * LANES]) * (GDN_HEAD_DIM ** -0.5)
        k_scr[:, sl] = l2n(y[:, GDN_WIDTH + h * LANES:GDN_WIDTH + (h + 1) * LANES])
        g_col = jnp.sum(jnp.where(lane == h, g_all, 0.0), axis=-1, keepdims=True)
        beta_col = jnp.sum(jnp.where(lane == h + nh, beta_all, 0.0), axis=-1, keepdims=True)
        g_scr[:, sl] = jnp.broadcast_to(g_col, (ts, LANES))
        beta_scr[:, sl] = jnp.broadcast_to(beta_col, (ts, LANES))
    v_scr[...] = y[:, 2 * GDN_WIDTH:]

    ri = lax.broadcasted_iota(jnp.int32, (c, nh * c), 0)
    ci = lax.broadcasted_iota(jnp.int32, (c, nh * c), 1) % c
    lower_incl = ci <= ri
    lower_strict = ci < ri
    eye = (ci == ri).astype(F32)
    ltri = lower_incl[:, :c].astype(F32)
    ones = jnp.ones((c, c), F32)
    og = og_ref[...]
    heads = range(nh)
    hs = lambda m, h: m[:, h * c:(h + 1) * c]

    def blk(n, states):
        r0 = pl.multiple_of(n * c, c)
        q = q_scr[pl.ds(r0, c), :]
        k = k_scr[pl.ds(r0, c), :]
        v = v_scr[pl.ds(r0, c), :]
        g = g_scr[pl.ds(r0, c), :]
        beta = beta_scr[pl.ds(r0, c), :]
        gc = _dot_exact_lhs(ltri, g)
        gc_row = _dot_exact_lhs(ones, gc * eye)
        decay = jnp.exp(jnp.where(lower_incl, gc - gc_row, -jnp.inf))
        kb = k * beta
        kbf = k.astype(BF16)
        kbb = kb.astype(BF16)
        qbf = q.astype(BF16)
        kq = [_dot_nt(jnp.concatenate([hs(kbb, h), hs(qbf, h)], axis=0), hs(kbf, h)) for h in heads]
        kk = jnp.concatenate([x[:c] for x in kq], axis=1)
        qk = jnp.concatenate([x[c:] for x in kq], axis=1)
        a = jnp.where(lower_strict, kk * decay, 0.0)
        attn = jnp.where(lower_incl, qk * decay, 0.0).astype(BF16)
        nmat = [-hs(a, h) for h in heads]
        xp = [_dot3(hs(a, h), hs(a, h)) for h in heads]
        for _ in range(5):
            both = [_dot3(jnp.concatenate([nm, x], axis=0), x) for nm, x in zip(nmat, xp)]
            nmat = [nm + x + nx[:c] for nm, x, nx in zip(nmat, xp, both)]
            xp = [nx[c:] for nx in both]
        prod = [_dot3(nm, x) for nm, x in zip(nmat, xp)]
        nmat = [nm + x + p for nm, x, p in zip(nmat, xp, prod)]
        eg = jnp.exp(gc)
        vb = v * beta
        kbd = kb * eg
        gl = gc[c - 1:c, :]
        qd = (q * eg).astype(BF16)
        kd = (k * jnp.exp(gl - gc)).astype(BF16)
        dl = jnp.exp(gl)
        uw_rhs = [jnp.concatenate([hs(vb, h), hs(kbd, h)], axis=1) for h in heads]
        uw = [r + _dot(nm.astype(BF16), r.astype(BF16)) for nm, r in zip(nmat, uw_rhs)]
        u = [x[:, :c] for x in uw]
        w = [x[:, c:].astype(BF16) for x in uw]
        sb = [st.astype(BF16) for st in states]
        ws = [_dot(jnp.concatenate([w[h], hs(qd, h)], axis=0), sb[h]) for h in heads]
        v_new = [u[h] - ws[h][:c] for h in heads]
        vnb = [vn.astype(BF16) for vn in v_new]
        o = [ws[h][c:] + _dot(hs(attn, h), vnb[h]) for h in heads]
        new_states = tuple(states[h] * hs(dl, h) + _dot_tn(hs(kd, h), vnb[h]) for h in heads)
        on = jnp.concatenate(
            [x * lax.rsqrt(jnp.mean(x * x, axis=-1, keepdims=True) + EPS) * og for x in o], axis=1)
        z = z_ref[pl.ds(r0, c), :]
        o_ref[pl.ds(r0, c), :] = on * (z * _sigmoid(z))
        return new_states

    states = lax.fori_loop(0, ts // c, blk, tuple(state_scr[h] for h in heads))
    for h in heads:
        state_scr[h] = states[h]


def _gdn(proj3, ab3, conv_w, alog_b, dtb_b, og, ts):
    b, s, _ = proj3.shape
    w3 = 3 * GDN_WIDTH
    assert 3 * SB_WIDTH == w3 and 2 * w3 % GDN_WIDTH == 0
    per = ts // GDN_HALO
    par = pl.BlockSpec((1, LANES), lambda bi, t: (0, 0))
    return pl.pallas_call(
        functools.partial(_gdn_body, ts=ts),
        grid=(b, s // ts),
        in_specs=[
            pl.BlockSpec((None, ts, w3), lambda bi, t: (bi, t, 1)),
            pl.BlockSpec((None, GDN_HALO, w3), lambda bi, t: (bi, jnp.maximum(t * per - 1, 0), 1)),
            pl.BlockSpec((None, ts, GDN_WIDTH), lambda bi, t: (bi, t, 2 * w3 // GDN_WIDTH)),
            pl.BlockSpec((None, ts, LANES), lambda bi, t: (bi, t, 0)),
            pl.BlockSpec((GDN_CONV, w3), lambda bi, t: (0, 0)),
            par, par,
            pl.BlockSpec((1, LANES), lambda bi, t: (0, 0)),
        ],
        out_specs=pl.BlockSpec((None, ts, GDN_WIDTH), lambda bi, t: (bi, t, 0)),
        out_shape=jax.ShapeDtypeStruct((b, s, GDN_WIDTH), F32),
        scratch_shapes=[pltpu.VMEM((ts, GDN_WIDTH), F32) for _ in range(5)]
        + [pltpu.VMEM((GDN_HEADS, GDN_HEAD_DIM, GDN_HEAD_DIM), F32)],
        compiler_params=_cparams(("parallel", "arbitrary")),
        name="gdn",
    )(proj3, proj3, proj3, ab3, conv_w, alog_b, dtb_b, og)


def _outproj_body(sb_ref, gd_ref, x_ref, wo1_ref, wo2_ref, g2_ref, wqt_ref, x1_ref, hnt_ref, pqt_ref):
    mix = _dot(sb_ref[...].astype(BF16), wo1_ref[...]) + _dot(gd_ref[...].astype(BF16), wo2_ref[...])
    x1 = x_ref[...] + mix
    x1_ref[...] = x1
    ms = jnp.mean(x1 * x1, axis=-1, keepdims=True)
    hn = x1 * lax.rsqrt(ms + EPS) * g2_ref[...]
    hnt = hn.T.astype(BF16)
    hnt_ref[...] = hnt
    pqt_ref[...] = _dot(wqt_ref[...], hnt)


def _outproj(sb_o, gd_o, x2, wo1, wo2, g2, wqt, tm):
    t, d = x2.shape
    nq = wqt.shape[0]
    return pl.pallas_call(
        _outproj_body,
        grid=(t // tm,),
        in_specs=[
            pl.BlockSpec((tm, SB_WIDTH), lambda i: (i, 0)),
            pl.BlockSpec((tm, GDN_WIDTH), lambda i: (i, 0)),
            pl.BlockSpec((tm, d), lambda i: (i, 0)),
            pl.BlockSpec((SB_WIDTH, d), lambda i: (0, 0)),
            pl.BlockSpec((GDN_WIDTH, d), lambda i: (0, 0)),
            pl.BlockSpec((1, d), lambda i: (0, 0)),
            pl.BlockSpec((nq, d), lambda i: (0, 0)),
        ],
        out_specs=[
            pl.BlockSpec((tm, d), lambda i: (i, 0)),
            pl.BlockSpec((d, tm), lambda i: (0, i)),
            pl.BlockSpec((nq, tm), lambda i: (0, i)),
        ],
        out_shape=[
            jax.ShapeDtypeStruct((t, d), F32),
            jax.ShapeDtypeStruct((d, t), BF16),
            jax.ShapeDtypeStruct((nq, t), F32),
        ],
        compiler_params=_cparams(("parallel",)),
        name="outproj",
    )(sb_o, gd_o, x2, wo1, wo2, g2, wqt)


def _extract_topk(arrays, ids, k, want_rank):
    del want_rank
    big = jnp.int32(2 ** 30)
    works = list(arrays)
    vals = [[] for _ in arrays]
    ranks = [jnp.full(s.shape, float(k), F32) for s in arrays]
    for r in range(k):
        ms = [jnp.max(w, axis=0, keepdims=True) for w in works]
        firsts = [jnp.min(jnp.where(w == m, i, big), axis=0, keepdims=True) for w, m, i in zip(works, ms, ids)]
        hits = [i == f for i, f in zip(ids, firsts)]
        ranks = [jnp.where(hit, float(r), rank) for hit, rank in zip(hits, ranks)]
        works = [jnp.where(hit, -jnp.inf, w) for hit, w in zip(hits, works)]
        for v, m in zip(vals, ms):
            v.append(m)
    return [(v, rank, rank < float(k), None) for v, rank in zip(vals, ranks)]


def _extract_topk_untied(arrays, ids, k, want_rank):
    del ids
    works = list(arrays)
    vals = [[] for _ in arrays]
    ranks = [jnp.full(s.shape, float(k), F32) if want else None for s, want in zip(arrays, want_rank)]
    for r in range(k):
        ms = [jnp.max(w, axis=0, keepdims=True) for w in works]
        hits = [w == m for w, m in zip(works, ms)]
        ranks = [None if rank is None else jnp.where(hit, float(r), rank) for hit, rank in zip(hits, ranks)]
        works = [jnp.where(hit, -jnp.inf, w) for hit, w in zip(hits, works)]
        for v, m in zip(vals, ms):
            v.append(m)
    out = []
    for s, w, v, rank in zip(arrays, works, vals, ranks):
        taken = jnp.logical_and(w == -jnp.inf, s > -jnp.inf) if rank is None else rank < float(k)
        count = jnp.sum(taken.astype(F32), axis=0, keepdims=True)
        out.append((v, rank, taken, count == float(k)))
    return out


def _route_body(pqt_ref, k1_ref, k2_ref, rank2_ref, cnt_ref, e1_ref, e2z_ref):
    kk = PEER_TOPK
    tn = pqt_ref.shape[1]
    key_ids = lax.broadcasted_iota(jnp.int32, (PEER_NKEYS, tn), 0)
    i8 = lax.broadcasted_iota(jnp.int32, (8, tn), 0)
    i16 = lax.broadcasted_iota(jnp.int32, (kk, tn), 0)
    cand_ids = jnp.concatenate(
        [i16] + [i8 + a * kk for a in (1, 2, 3)]
        + [jnp.where(i8 >= 4, i8 * kk + b, kk * kk + i8 * kk + b) for b in (0, 1, 2)] + [(i8 + 8) * kk], axis=0)
    def candidates(v1, v2):
        v1a = jnp.concatenate(v1, axis=0)
        v2a = jnp.concatenate(v2, axis=0)
        return jnp.concatenate(
            [v1[0] + v2a] + [v1[a] + v2a[:8] for a in (1, 2, 3)]
            + [jnp.where(i8 >= 4, v1a[:8] + v2[b], -jnp.inf) for b in (0, 1, 2)] + [v1a[8:] + v2[0]], axis=0)

    def route_heads(heads, s1s, s2s, extract, tie_rule):
        n = len(heads)
        first = extract(s1s + s2s, [key_ids] * (2 * n), kk, [tie_rule] * n + [True] * n)
        second = extract([candidates(first[i][0], first[n + i][0]) for i in range(n)], [cand_ids] * n, kk,
                         [False] * n)
        return [finish(heads[i], s1s[i], s2s[i], first[i], first[n + i], second[i], tie_rule) for i in range(n)]

    def finish(h, s1, s2, first1, first2, second, tie_rule):
        v1, rank1, _, ok1 = first1
        v2, rank2, _, ok2 = first2
        top, _, taken, ok3 = second
        zsum = jnp.ones_like(top[0])
        for r in range(1, kk):
            zsum = zsum + jnp.exp(top[r] - top[0])
        sel = taken.astype(F32)
        low = [jnp.sum(sel[0:16], axis=0, keepdims=True)] + [
            jnp.sum(sel[8 + 8 * a:16 + 8 * a], axis=0, keepdims=True) for a in (1, 2, 3)]
        mid = sel[40:48] + sel[48:56] + sel[56:64]
        high = sel[64:72]
        cnt = jnp.zeros(s1.shape, F32)
        for a in range(kk):
            n_a = low[a] if a < 4 else (mid[a:a + 1] if a < 8 else high[a - 8:a - 7])
            cnt = jnp.where(rank1 == float(a) if tie_rule else s1 == v1[a], n_a, cnt)
        rank2_ref[h] = rank2.astype(BF16)
        cnt_ref[h] = cnt
        e1_ref[h] = jnp.exp(s1 - v1[0])
        e2z_ref[h] = (jnp.exp(s2 - v2[0]) * (0.5 / zsum)).astype(BF16)
        return None if ok1 is None else jnp.logical_and(jnp.logical_and(ok1, ok2), ok3)

    def scores(keys_ref, h, half):
        q = pqt_ref[pl.ds((2 * h + half) * PEER_HALF, PEER_HALF), :].astype(BF16)
        return _dot(keys_ref[h], q)

    for h0 in range(0, PEER_HEADS, PEER_HEADS_PER_PASS):
        heads = list(range(h0, h0 + PEER_HEADS_PER_PASS))
        s1s = [scores(k1_ref, h, 0) for h in heads]
        s2s = [scores(k2_ref, h, 1) for h in heads]
        oks = route_heads(heads, s1s, s2s, _extract_topk_untied, False)
        for h, s1, s2, ok in zip(heads, s1s, s2s, oks):
            tied = jnp.max(jnp.where(ok, 0.0, 1.0)) > 0.0

            @pl.when(tied)
            def _(h=h, s1=s1, s2=s2):
                route_heads([h], [s1], [s2], _extract_topk, True)


def _route(pqt, k1, k2, tn):
    nq, t = pqt.shape
    hk = (PEER_HEADS, PEER_NKEYS, PEER_HALF)
    out = lambda dt: jax.ShapeDtypeStruct((PEER_HEADS, PEER_NKEYS, t), dt)
    ospec = pl.BlockSpec((PEER_HEADS, PEER_NKEYS, tn), lambda i: (0, 0, i))
    return pl.pallas_call(
        _route_body,
        grid=(t // tn,),
        in_specs=[
            pl.BlockSpec((nq, tn), lambda i: (0, i)),
            pl.BlockSpec(hk, lambda i: (0, 0, 0)),
            pl.BlockSpec(hk, lambda i: (0, 0, 0)),
        ],
        out_specs=[ospec, ospec, ospec, ospec],
        out_shape=[out(BF16), out(F32), out(F32), out(BF16)],
        compiler_params=_cparams(("parallel",)),
        name="peer_route",
    )(pqt, k1, k2)


PEER_I1_PER_BLOCK = 8
BF16_ROWS = 16


def _expert_body(hnt_ref, u_first_ref, u_b_ref, u_next_ref, vt_prev_ref, vt_a_ref, vt_last_ref, rank2_ref,
                 cnt_ref, e1_ref, e2z_ref, x1_ref, o_ref, acc_scr, act_next, p_prev):
    s = pl.program_id(1)
    hnt = hnt_ref[...]
    tn = hnt.shape[1]
    tiles = PEER_NKEYS // BF16_ROWS

    def activation(u_ref):
        pre = _dot(u_ref[...], hnt)
        return (pre * (1.0 + lax.erf(pre * (2.0 ** -0.5)))).astype(BF16)

    def row(ref, h, l):
        return jnp.broadcast_to(ref[h, l:l + 1, :], (BF16_ROWS, tn)).astype(BF16)[None]

    def gates(l0, act_rows):
        ps = []
        for li in range(PEER_I1_PER_BLOCK):
            gate = None
            for h in range(PEER_HEADS):
                rank2 = rank2_ref[h].reshape(tiles, BF16_ROWS, tn)
                e2z = e2z_ref[h].reshape(tiles, BF16_ROWS, tn)
                term = jnp.where(rank2 < row(cnt_ref, h, l0 + li), e2z * row(e1_ref, h, l0 + li), 0.0)
                gate = term if gate is None else gate + term
            ps.append(gate.reshape(PEER_NKEYS, tn) * act_rows(li))
        return jnp.concatenate(ps, axis=0)

    @pl.when(s == 0)
    def _():
        acc_scr[...] = jnp.zeros_like(acc_scr)
        p_prev[...] = jnp.zeros_like(p_prev)
        act_next[...] = activation(u_first_ref)

    out_prev = _dot(vt_prev_ref[...], p_prev[...])
    act_b = activation(u_b_ref)
    p_a = gates(0, lambda li: act_next[pl.ds(li * PEER_NKEYS, PEER_NKEYS), :])
    out_a = _dot(vt_a_ref[...], p_a)
    act_next[...] = activation(u_next_ref)
    p_prev[...] = gates(PEER_I1_PER_BLOCK, lambda li: act_b[li * PEER_NKEYS:(li + 1) * PEER_NKEYS, :])
    acc_scr[...] += out_prev + out_a

    @pl.when(s == pl.num_programs(1) - 1)
    def _():
        o_ref[...] = x1_ref[...] + (acc_scr[...] + _dot(vt_last_ref[...], p_prev[...])).T


def _experts(hnt, u_b, vt_b, rank2, cnt, e1, e2z, x1, tn):
    d, t = hnt.shape
    ne = u_b.shape[0]
    eb = PEER_I1_PER_BLOCK * PEER_NKEYS
    nblk = ne // eb
    once = pl.Buffered(1)
    full = pl.BlockSpec((PEER_HEADS, PEER_NKEYS, tn), lambda i, s: (0, 0, i))
    part = pl.BlockSpec((PEER_HEADS, 2 * PEER_I1_PER_BLOCK, tn), lambda i, s: (0, s, i))
    return pl.pallas_call(
        _expert_body,
        grid=(t // tn, nblk // 2),
        in_specs=[
            pl.BlockSpec((d, tn), lambda i, s: (0, i)),
            pl.BlockSpec((eb, d), lambda i, s: (0, 0), pipeline_mode=once),
            pl.BlockSpec((eb, d), lambda i, s: (2 * s + 1, 0)),
            pl.BlockSpec((eb, d), lambda i, s: (jnp.minimum(2 * s + 2, nblk - 1), 0)),
            pl.BlockSpec((d, eb), lambda i, s: (0, jnp.maximum(2 * s - 1, 0))),
            pl.BlockSpec((d, eb), lambda i, s: (0, 2 * s)),
            pl.BlockSpec((d, eb), lambda i, s: (0, nblk - 1), pipeline_mode=once),
            full, part, part, full,
            pl.BlockSpec((tn, d), lambda i, s: (i, 0), pipeline_mode=once),
        ],
        out_specs=pl.BlockSpec((tn, d), lambda i, s: (i, 0)),
        out_shape=jax.ShapeDtypeStruct((t, d), F32),
        scratch_shapes=[pltpu.VMEM((d, tn), F32), pltpu.VMEM((eb, tn), BF16), pltpu.VMEM((eb, tn), BF16)],
        compiler_params=_cparams(("parallel", "arbitrary")),
        name="peer_experts",
    )(hnt, u_b, u_b, u_b, vt_b, vt_b, vt_b, rank2, cnt, e1, e2z, x1)


def _suffix_sum_matrix(tk):
    r = jnp.arange(2 * tk)[:, None] % tk
    c = jnp.arange(2 * tk)[None, :]
    return jnp.where(c < tk, r > c, True).astype(BF16)


class _Tiles(NamedTuple):
    proj_tokens: int
    sb_q: int
    sb_k: int
    gdn_seq: int
    route_tokens: int
    expert_tokens: int


def _tiles(t, s):
    sb_k = 128
    return _Tiles(min(512, t), min(2 * sb_k, s), sb_k, min(512, s), min(256, t), min(512, t))


def _layer(x, norm1_g, w_in, sb_q_g, sb_k_g, sb_o_g, conv_w, a_log, dt_bias, gdn_o_g, w_out, norm2_g,
           w_q, keys1, keys2, u_tab, v_tab):
    b, s, d = x.shape
    t = b * s
    tiles = _tiles(t, s)
    n_main = 3 * SB_WIDTH + 4 * GDN_WIDTH
    x2 = x.reshape(t, d)
    w_main = w_in[:, :n_main].astype(BF16)
    w_ab = jnp.pad(w_in[:, n_main:], ((0, 0), (0, LANES - 2 * GDN_HEADS))).astype(BF16)
    proj, ab = _inproj(x2, norm1_g.reshape(1, d), w_main, w_ab, tiles.proj_tokens)
    proj3 = proj.reshape(b, s, n_main)
    ab3 = ab.reshape(b, s, LANES)

    tile2 = lambda g: jnp.tile(g, 2).reshape(1, LANES)
    sb_o = _sb_attention(proj3, tile2(sb_q_g), tile2(sb_k_g), tile2(sb_o_g), _suffix_sum_matrix(tiles.sb_k),
                         tiles.sb_q, tiles.sb_k)

    on_lanes = lambda p: jnp.pad(p, (0, LANES - GDN_HEADS)).reshape(1, LANES)
    gd_o = _gdn(proj3, ab3, conv_w, on_lanes(a_log), on_lanes(dt_bias), gdn_o_g.reshape(1, LANES), tiles.gdn_seq)

    wo = w_out.astype(BF16)
    x1, hnt, pqt = _outproj(sb_o.reshape(t, SB_WIDTH), gd_o.reshape(t, GDN_WIDTH), x2, wo[:SB_WIDTH],
                            wo[SB_WIDTH:], norm2_g.reshape(1, d), w_q.T.astype(BF16), tiles.proj_tokens)

    rank2, cnt, e1, e2z = _route(pqt, keys1.astype(BF16), keys2.astype(BF16), tiles.route_tokens)
    y = _experts(hnt, u_tab.astype(BF16), v_tab.T.astype(BF16), rank2, cnt, e1, e2z, x1, tiles.expert_tokens)
    return y.reshape(b, s, d)


def kernel(x, norm1_g, w_in, sb_q_norm_g, sb_k_norm_g, sb_out_norm_g, gdn_conv_w, gdn_a_log, gdn_dt_bias,
           gdn_out_norm_g, w_out, norm2_g, peer_w_q, peer_keys1, peer_keys2, peer_u, peer_v):
    for layer in range(norm1_g.shape[0]):
        x = _layer(x, norm1_g[layer], w_in[layer], sb_q_norm_g[layer], sb_k_norm_g[layer],
                   sb_out_norm_g[layer], gdn_conv_w[layer], gdn_a_log[layer], gdn_dt_bias[layer],
                   gdn_out_norm_g[layer], w_out[layer], norm2_g[layer], peer_w_q[layer],
                   peer_keys1[layer], peer_keys2[layer], peer_u[layer], peer_v[layer])
    return x
```

```python
import functools
from typing import NamedTuple

import jax
import jax.numpy as jnp
from jax import lax
from jax.experimental import pallas as pl
from jax.experimental.pallas import tpu as pltpu

F32 = jnp.float32
BF16 = jnp.bfloat16
EPS = 1e-6

SB_HEADS = 8
SB_HEAD_DIM = 64
SB_WIDTH = SB_HEADS * SB_HEAD_DIM
GDN_HEADS = 4
GDN_HEAD_DIM = 128
GDN_WIDTH = GDN_HEADS * GDN_HEAD_DIM
GDN_CONV = 4
PEER_HEADS = 8
PEER_NKEYS = 128
PEER_HALF = 128
PEER_TOPK = 16
PEER_HEADS_PER_PASS = 8
LANES = 128

VMEM_LIMIT = 56 * 1024 * 1024


def _cparams(sem):
    return pltpu.CompilerParams(dimension_semantics=sem, vmem_limit_bytes=VMEM_LIMIT)


def _dot(a, b):
    return jnp.dot(a, b, preferred_element_type=F32)


def _dot_nt(a, b):
    return lax.dot_general(a, b, (((1,), (1,)), ((), ())), preferred_element_type=F32)


def _dot_tn(a, b):
    return lax.dot_general(a, b, (((0,), (0,)), ((), ())), preferred_element_type=F32)


def _split(a):
    hi = a.astype(BF16)
    return hi, (a - hi.astype(F32)).astype(BF16)


def _dot_exact_lhs(a, b):
    ab = a.astype(BF16)
    hi = b.astype(BF16)
    rest = b - hi.astype(F32)
    mid = rest.astype(BF16)
    lo = (rest - mid.astype(F32)).astype(BF16)
    return _dot(jnp.concatenate([ab, ab, ab], axis=1), jnp.concatenate([hi, mid, lo], axis=0))


def _dot3(a, b):
    ah, al = _split(a)
    bh, bl = _split(b)
    return _dot(jnp.concatenate([ah, ah, al], axis=1), jnp.concatenate([bh, bl, bh], axis=0))


def _softplus(x):
    return jnp.maximum(x, 0.0) + jnp.log1p(jnp.exp(-jnp.abs(x)))


def _sigmoid(x):
    return 1.0 / (1.0 + jnp.exp(-x))


def _inproj_body(x_ref, g_ref, w_ref, wab_ref, proj_ref, ab_ref):
    x = x_ref[...]
    ms = jnp.mean(x * x, axis=-1, keepdims=True)
    h = (x * lax.rsqrt(ms + EPS) * g_ref[...]).astype(BF16)
    proj_ref[...] = _dot(h, w_ref[...])
    ab_ref[...] = _dot(h, wab_ref[...])


def _inproj(x2, g, w_main, w_ab, tm):
    t, d = x2.shape
    n = w_main.shape[1]
    return pl.pallas_call(
        _inproj_body,
        grid=(t // tm,),
        in_specs=[
            pl.BlockSpec((tm, d), lambda i: (i, 0)),
            pl.BlockSpec((1, d), lambda i: (0, 0)),
            pl.BlockSpec((d, n), lambda i: (0, 0)),
            pl.BlockSpec((d, LANES), lambda i: (0, 0)),
        ],
        out_specs=[
            pl.BlockSpec((tm, n), lambda i: (i, 0)),
            pl.BlockSpec((tm, LANES), lambda i: (i, 0)),
        ],
        out_shape=[
            jax.ShapeDtypeStruct((t, n), F32),
            jax.ShapeDtypeStruct((t, LANES), F32),
        ],
        compiler_params=_cparams(("parallel",)),
        name="inproj",
    )(x2, g, w_main, w_ab)


SB_DEAD_LOG = -104.0


def _sb_body(q_ref, k_ref, v_ref, gq_ref, gk_ref, go_ref, m2_ref, o_ref, kn_scr, vb_scr, *, tq, tk):
    i = pl.program_id(2)
    lane = lax.broadcasted_iota(jnp.int32, (1, LANES), 1)
    is0 = lane < SB_HEAD_DIM

    def headnorm(x, g):
        x2 = x * x
        s0 = jnp.sum(jnp.where(is0, x2, 0.0), axis=-1, keepdims=True)
        s1 = jnp.sum(jnp.where(is0, 0.0, x2), axis=-1, keepdims=True)
        ms = jnp.where(is0, s0, s1) * (1.0 / SB_HEAD_DIM)
        return x * lax.rsqrt(ms + EPS) * g

    @pl.when(i == 0)
    def _():
        kn_scr[...] = headnorm(k_ref[...], gk_ref[...]).astype(BF16)
        vb_scr[...] = v_ref[...].astype(BF16)

    qn = headnorm(q_ref[...], gq_ref[...]) * (SB_HEAD_DIM ** -0.5)
    q2h = jnp.concatenate([jnp.where(is0, qn, 0.0), jnp.where(is0, 0.0, qn)], axis=0).astype(BF16)
    halves = lambda x: [x[:tq], x[tq:]]
    m2 = m2_ref[...]
    row = i * tq + lax.broadcasted_iota(jnp.int32, (tq, tk), 0)
    col0 = lax.broadcasted_iota(jnp.int32, (tq, tk), 1)
    nkb = (i + 1) * (tq // tk)

    def scores(j):
        kj = kn_scr[pl.ds(pl.multiple_of(j * tk, tk), tk), :]
        return halves(_dot_nt(q2h, kj))

    def logs(j, zs, on_diagonal):
        causal = (col0 + j * tk) < row
        zls, cats = [], []
        for z in zs:
            lk = -(jnp.maximum(z, 0.0) + jnp.log(1.0 + jnp.exp(-jnp.abs(z))))
            zl = z + lk
            if on_diagonal:
                lk = jnp.where(causal, lk, 0.0)
                zl = jnp.where(causal, zl, -jnp.inf)
            hi = lk.astype(BF16)
            lo = (lk - hi.astype(F32)).astype(BF16)
            zls.append(zl)
            cats.append(jnp.concatenate([hi, lo], axis=1))
        return zls, cats

    def sums(cats):
        return halves(_dot(jnp.concatenate(cats, axis=0), m2))

    def pair(n, carry, on_diagonal):
        j = nkb - 1 - 2 * n
        zs_a = scores(j)
        zs_b = scores(j - 1)
        zl_a, cats_a = logs(j, zs_a, on_diagonal)
        rt_a = sums(cats_a)
        zl_b, cats_b = logs(j - 1, zs_b, on_diagonal)
        rt_b = sums(cats_b)
        v2 = vb_scr[pl.ds(pl.multiple_of((j - 1) * tk, tk), 2 * tk), :]
        ws, rests = [], []
        for h in range(2):
            rest = carry[2 * h + 1]
            rest_mid = rest + rt_a[h][:, tk:]
            w_a = jnp.exp(zl_a[h] + (rest + rt_a[h][:, :tk])).astype(BF16)
            w_b = jnp.exp(zl_b[h] + (rest_mid + rt_b[h][:, :tk])).astype(BF16)
            ws.append(jnp.concatenate([w_b, w_a], axis=1))
            rests.append(rest_mid + rt_b[h][:, tk:])
        pv = halves(_dot(jnp.concatenate(ws, axis=0), v2))
        out = [carry[0] + pv[0], rests[0], carry[2] + pv[1], rests[1]]
        alive = (jnp.max(jnp.maximum(out[1], out[3])) > SB_DEAD_LOG).astype(jnp.int32)
        return n + 1, alive, tuple(out)

    def cond(state):
        return jnp.logical_and(state[0] < nkb // 2, state[1] > 0)

    assert tq == 2 * tk
    zero = jnp.zeros((tq, LANES), F32)
    first = pair(0, (zero, zero, zero, zero), True)
    res = lax.while_loop(cond, lambda state: pair(state[0], state[2], False), first)[2]
    o = jnp.where(is0, res[0], res[2])
    o_ref[...] = headnorm(o, go_ref[...])


def _sb_attention(proj3, gq, gk, go, m2, tq, tk):
    b, s, _ = proj3.shape
    hp = SB_HEADS // 2
    return pl.pallas_call(
        functools.partial(_sb_body, tq=tq, tk=tk),
        grid=(b, hp, s // tq),
        in_specs=[
            pl.BlockSpec((None, tq, LANES), lambda bi, h, i: (bi, i, h)),
            pl.BlockSpec((None, s, LANES), lambda bi, h, i: (bi, 0, hp + h)),
            pl.BlockSpec((None, s, LANES), lambda bi, h, i: (bi, 0, 2 * hp + h)),
            pl.BlockSpec((1, LANES), lambda bi, h, i: (0, 0)),
            pl.BlockSpec((1, LANES), lambda bi, h, i: (0, 0)),
            pl.BlockSpec((1, LANES), lambda bi, h, i: (0, 0)),
            pl.BlockSpec((2 * tk, 2 * tk), lambda bi, h, i: (0, 0)),
        ],
        out_specs=pl.BlockSpec((None, tq, LANES), lambda bi, h, i: (bi, i, h)),
        out_shape=jax.ShapeDtypeStruct((b, s, SB_WIDTH), F32),
        scratch_shapes=[pltpu.VMEM((s, LANES), BF16), pltpu.VMEM((s, LANES), BF16)],
        compiler_params=_cparams(("parallel", "parallel", "arbitrary")),
        name="sb_attention",
    )(proj3, proj3, proj3, gq, gk, go, m2)


GDN_BLOCK = 128
GDN_HALO = 8


def _gdn_body(x_ref, halo_ref, z_ref, ab_ref, cw_ref, alog_ref, dtb_ref, og_ref, o_ref,
              q_scr, k_scr, v_scr, g_scr, beta_scr, state_scr, *, ts):
    t = pl.program_id(1)
    c = GDN_BLOCK
    nh = GDN_HEADS
    lane = lax.broadcasted_iota(jnp.int32, (1, LANES), 1)

    @pl.when(t == 0)
    def _():
        state_scr[...] = jnp.zeros_like(state_scr)

    halo = jnp.where(t > 0, halo_ref[...], 0.0)
    xe = jnp.concatenate([halo, x_ref[...]], axis=0)
    cw = cw_ref[...]
    y = xe * cw[3:4, :]
    for d in (1, 2, 3):
        y = y + pltpu.roll(xe, d, 0) * cw[3 - d:4 - d, :]
    y = y[GDN_HALO:, :]
    y = y * _sigmoid(y)

    def l2n(x):
        return x * lax.rsqrt(jnp.sum(x * x, axis=-1, keepdims=True) + EPS)

    ab = ab_ref[...]
    g_all = -jnp.exp(alog_ref[...]) * _softplus(ab + dtb_ref[...])
    beta_all = _sigmoid(ab)
    for h in range(nh):
        sl = pl.ds(h * LANES, LANES)
        q_scr[:, sl] = l2n(y[:, h * LANES:(h + 1) * LANES]) * (GDN_HEAD_DIM ** -0.5)
        k_scr[:, sl] = l2n(y[:, GDN_WIDTH + h * LANES:GDN_WIDTH + (h + 1) * LANES])
        g_col = jnp.sum(jnp.where(lane == h, g_all, 0.0), axis=-1, keepdims=True)
        beta_col = jnp.sum(jnp.where(lane == h + nh, beta_all, 0.0), axis=-1, keepdims=True)
        g_scr[:, sl] = jnp.broadcast_to(g_col, (ts, LANES))
        beta_scr[:, sl] = jnp.broadcast_to(beta_col, (ts, LANES))
    v_scr[...] = y[:, 2 * GDN_WIDTH:]

    ri = lax.broadcasted_iota(jnp.int32, (c, nh * c), 0)
    ci = lax.broadcasted_iota(jnp.int32, (c, nh * c), 1) % c
    lower_incl = ci <= ri
    lower_strict = ci < ri
    eye = (ci == ri).astype(F32)
    ltri = lower_incl[:, :c].astype(F32)
    ones = jnp.ones((c, c), F32)
    og = og_ref[...]
    heads = range(nh)
    hs = lambda m, h: m[:, h * c:(h + 1) * c]

    def blk(n, states):
        r0 = pl.multiple_of(n * c, c)
        q = q_scr[pl.ds(r0, c), :]
        k = k_scr[pl.ds(r0, c), :]
        v = v_scr[pl.ds(r0, c), :]
        g = g_scr[pl.ds(r0, c), :]
        beta = beta_scr[pl.ds(r0, c), :]
        gc = _dot_exact_lhs(ltri, g)
        gc_row = _dot_exact_lhs(ones, gc * eye)
        decay = jnp.exp(jnp.where(lower_incl, gc - gc_row, -jnp.inf))
        kb = k * beta
        kbf = k.astype(BF16)
        kbb = kb.astype(BF16)
        qbf = q.astype(BF16)
        kq = [_dot_nt(jnp.concatenate([hs(kbb, h), hs(qbf, h)], axis=0), hs(kbf, h)) for h in heads]
        kk = jnp.concatenate([x[:c] for x in kq], axis=1)
        qk = jnp.concatenate([x[c:] for x in kq], axis=1)
        a = jnp.where(lower_strict, kk * decay, 0.0)
        attn = jnp.where(lower_incl, qk * decay, 0.0).astype(BF16)
        nmat = [-hs(a, h) for h in heads]
        xp = [_dot3(hs(a, h), hs(a, h)) for h in heads]
        for _ in range(5):
            both = [_dot3(jnp.concatenate([nm, x], axis=0), x) for nm, x in zip(nmat, xp)]
            nmat = [nm + x + nx[:c] for nm, x, nx in zip(nmat, xp, both)]
            xp = [nx[c:] for nx in both]
        prod = [_dot3(nm, x) for nm, x in zip(nmat, xp)]
        nmat = [nm + x + p for nm, x, p in zip(nmat, xp, prod)]
        eg = jnp.exp(gc)
        vb = v * beta
        kbd = kb * eg
        gl = gc[c - 1:c, :]
        qd = (q * eg).astype(BF16)
        kd = (k * jnp.exp(gl - gc)).astype(BF16)
        dl = jnp.exp(gl)
        uw_rhs = [jnp.concatenate([hs(vb, h), hs(kbd, h)], axis=1) for h in heads]
        uw = [r + _dot(nm.astype(BF16), r.astype(BF16)) for nm, r in zip(nmat, uw_rhs)]
        u = [x[:, :c] for x in uw]
        w = [x[:, c:].astype(BF16) for x in uw]
        sb = [st.astype(BF16) for st in states]
        ws = [_dot(jnp.concatenate([w[h], hs(qd, h)], axis=0), sb[h]) for h in heads]
        v_new = [u[h] - ws[h][:c] for h in heads]
        vnb = [vn.astype(BF16) for vn in v_new]
        o = [ws[h][c:] + _dot(hs(attn, h), vnb[h]) for h in heads]
        new_states = tuple(states[h] * hs(dl, h) + _dot_tn(hs(kd, h), vnb[h]) for h in heads)
        on = jnp.concatenate(
            [x * lax.rsqrt(jnp.mean(x * x, axis=-1, keepdims=True) + EPS) * og for x in o], axis=1)
        z = z_ref[pl.ds(r0, c), :]
        o_ref[pl.ds(r0, c), :] = on * (z * _sigmoid(z))
        return new_states

    states = lax.fori_loop(0, ts // c, blk, tuple(state_scr[h] for h in heads))
    for h in heads:
        state_scr[h] = states[h]


def _gdn(proj3, ab3, conv_w, alog_b, dtb_b, og, ts):
    b, s, _ = proj3.shape
    w3 = 3 * GDN_WIDTH
    assert 3 * SB_WIDTH == w3 and 2 * w3 % GDN_WIDTH == 0
    per = ts // GDN_HALO
    par = pl.BlockSpec((1, LANES), lambda bi, t: (0, 0))
    return pl.pallas_call(
        functools.partial(_gdn_body, ts=ts),
        grid=(b, s // ts),
        in_specs=[
            pl.BlockSpec((None, ts, w3), lambda bi, t: (bi, t, 1)),
            pl.BlockSpec((None, GDN_HALO, w3), lambda bi, t: (bi, jnp.maximum(t * per - 1, 0), 1)),
            pl.BlockSpec((None, ts, GDN_WIDTH), lambda bi, t: (bi, t, 2 * w3 // GDN_WIDTH)),
            pl.BlockSpec((None, ts, LANES), lambda bi, t: (bi, t, 0)),
            pl.BlockSpec((GDN_CONV, w3), lambda bi, t: (0, 0)),
            par, par,
            pl.BlockSpec((1, LANES), lambda bi, t: (0, 0)),
        ],
        out_specs=pl.BlockSpec((None, ts, GDN_WIDTH), lambda bi, t: (bi, t, 0)),
        out_shape=jax.ShapeDtypeStruct((b, s, GDN_WIDTH), F32),
        scratch_shapes=[pltpu.VMEM((ts, GDN_WIDTH), F32) for _ in range(5)]
        + [pltpu.VMEM((GDN_HEADS, GDN_HEAD_DIM, GDN_HEAD_DIM), F32)],
        compiler_params=_cparams(("parallel", "arbitrary")),
        name="gdn",
    )(proj3, proj3, proj3, ab3, conv_w, alog_b, dtb_b, og)


def _outproj_body(sb_ref, gd_ref, x_ref, wo1_ref, wo2_ref, g2_ref, wqt_ref, x1_ref, hnt_ref, pqt_ref):
    mix = _dot(sb_ref[...].astype(BF16), wo1_ref[...]) + _dot(gd_ref[...].astype(BF16), wo2_ref[...])
    x1 = x_ref[...] + mix
    x1_ref[...] = x1
    ms = jnp.mean(x1 * x1, axis=-1, keepdims=True)
    hn = x1 * lax.rsqrt(ms + EPS) * g2_ref[...]
    hnt = hn.T.astype(BF16)
    hnt_ref[...] = hnt
    pqt_ref[...] = _dot(wqt_ref[...], hnt)


def _outproj(sb_o, gd_o, x2, wo1, wo2, g2, wqt, tm):
    t, d = x2.shape
    nq = wqt.shape[0]
    return pl.pallas_call(
        _outproj_body,
        grid=(t // tm,),
        in_specs=[
            pl.BlockSpec((tm, SB_WIDTH), lambda i: (i, 0)),
            pl.BlockSpec((tm, GDN_WIDTH), lambda i: (i, 0)),
            pl.BlockSpec((tm, d), lambda i: (i, 0)),
            pl.BlockSpec((SB_WIDTH, d), lambda i: (0, 0)),
            pl.BlockSpec((GDN_WIDTH, d), lambda i: (0, 0)),
            pl.BlockSpec((1, d), lambda i: (0, 0)),
            pl.BlockSpec((nq, d), lambda i: (0, 0)),
        ],
        out_specs=[
            pl.BlockSpec((tm, d), lambda i: (i, 0)),
            pl.BlockSpec((d, tm), lambda i: (0, i)),
            pl.BlockSpec((nq, tm), lambda i: (0, i)),
        ],
        out_shape=[
            jax.ShapeDtypeStruct((t, d), F32),
            jax.ShapeDtypeStruct((d, t), BF16),
            jax.ShapeDtypeStruct((nq, t), F32),
        ],
        compiler_params=_cparams(("parallel",)),
        name="outproj",
    )(sb_o, gd_o, x2, wo1, wo2, g2, wqt)


def _extract_topk(arrays, ids, k, want_rank):
    del want_rank
    big = jnp.int32(2 ** 30)
    works = list(arrays)
    vals = [[] for _ in arrays]
    ranks = [jnp.full(s.shape, float(k), F32) for s in arrays]
    for r in range(k):
        ms = [jnp.max(w, axis=0, keepdims=True) for w in works]
        firsts = [jnp.min(jnp.where(w == m, i, big), axis=0, keepdims=True) for w, m, i in zip(works, ms, ids)]
        hits = [i == f for i, f in zip(ids, firsts)]
        ranks = [jnp.where(hit, float(r), rank) for hit, rank in zip(hits, ranks)]
        works = [jnp.where(hit, -jnp.inf, w) for hit, w in zip(hits, works)]
        for v, m in zip(vals, ms):
            v.append(m)
    return [(v, rank, rank < float(k), None) for v, rank in zip(vals, ranks)]


def _extract_topk_untied(arrays, ids, k, want_rank):
    del ids
    works = list(arrays)
    vals = [[] for _ in arrays]
    ranks = [jnp.full(s.shape, float(k), F32) if want else None for s, want in zip(arrays, want_rank)]
    for r in range(k):
        ms = [jnp.max(w, axis=0, keepdims=True) for w in works]
        hits = [w == m for w, m in zip(works, ms)]
        ranks = [None if rank is None else jnp.where(hit, float(r), rank) for hit, rank in zip(hits, ranks)]
        works = [jnp.where(hit, -jnp.inf, w) for hit, w in zip(hits, works)]
        for v, m in zip(vals, ms):
            v.append(m)
    out = []
    for s, w, v, rank in zip(arrays, works, vals, ranks):
        taken = jnp.logical_and(w == -jnp.inf, s > -jnp.inf) if rank is None else rank < float(k)
        count = jnp.sum(taken.astype(F32), axis=0, keepdims=True)
        out.append((v, rank, taken, count == float(k)))
    return out


def _route_body(pqt_ref, k1_ref, k2_ref, rank2_ref, cnt_ref, e1_ref, e2z_ref):
    kk = PEER_TOPK
    tn = pqt_ref.shape[1]
    key_ids = lax.broadcasted_iota(jnp.int32, (PEER_NKEYS, tn), 0)
    i8 = lax.broadcasted_iota(jnp.int32, (8, tn), 0)
    i16 = lax.broadcasted_iota(jnp.int32, (kk, tn), 0)
    cand_ids = jnp.concatenate(
        [i16] + [i8 + a * kk for a in (1, 2, 3)]
        + [jnp.where(i8 >= 4, i8 * kk + b, kk * kk + i8 * kk + b) for b in (0, 1, 2)] + [(i8 + 8) * kk], axis=0)
    def candidates(v1, v2):
        v1a = jnp.concatenate(v1, axis=0)
        v2a = jnp.concatenate(v2, axis=0)
        return jnp.concatenate(
            [v1[0] + v2a] + [v1[a] + v2a[:8] for a in (1, 2, 3)]
            + [jnp.where(i8 >= 4, v1a[:8] + v2[b], -jnp.inf) for b in (0, 1, 2)] + [v1a[8:] + v2[0]], axis=0)

    def route_heads(heads, s1s, s2s, extract, tie_rule):
        n = len(heads)
        first = extract(s1s + s2s, [key_ids] * (2 * n), kk, [tie_rule] * n + [True] * n)
        second = extract([candidates(first[i][0], first[n + i][0]) for i in range(n)], [cand_ids] * n, kk,
                         [False] * n)
        return [finish(heads[i], s1s[i], s2s[i], first[i], first[n + i], second[i], tie_rule) for i in range(n)]

    def finish(h, s1, s2, first1, first2, second, tie_rule):
        v1, rank1, _, ok1 = first1
        v2, rank2, _, ok2 = first2
        top, _, taken, ok3 = second
        zsum = jnp.ones_like(top[0])
        for r in range(1, kk):
            zsum = zsum + jnp.exp(top[r] - top[0])
        sel = taken.astype(F32)
        low = [jnp.sum(sel[0:16], axis=0, keepdims=True)] + [
            jnp.sum(sel[8 + 8 * a:16 + 8 * a], axis=0, keepdims=True) for a in (1, 2, 3)]
        mid = sel[40:48] + sel[48:56] + sel[56:64]
        high = sel[64:72]
        cnt = jnp.zeros(s1.shape, F32)
        for a in range(kk):
            n_a = low[a] if a < 4 else (mid[a:a + 1] if a < 8 else high[a - 8:a - 7])
            cnt = jnp.where(rank1 == float(a) if tie_rule else s1 == v1[a], n_a, cnt)
        rank2_ref[h] = rank2.astype(BF16)
        cnt_ref[h] = cnt
        e1_ref[h] = jnp.exp(s1 - v1[0])
        e2z_ref[h] = (jnp.exp(s2 - v2[0]) * (0.5 / zsum)).astype(BF16)
        return None if ok1 is None else jnp.logical_and(jnp.logical_and(ok1, ok2), ok3)

    def scores(keys_ref, h, half):
        q = pqt_ref[pl.ds((2 * h + half) * PEER_HALF, PEER_HALF), :].astype(BF16)
        return _dot(keys_ref[h], q)

    for h0 in range(0, PEER_HEADS, PEER_HEADS_PER_PASS):
        heads = list(range(h0, h0 + PEER_HEADS_PER_PASS))
        s1s = [scores(k1_ref, h, 0) for h in heads]
        s2s = [scores(k2_ref, h, 1) for h in heads]
        oks = route_heads(heads, s1s, s2s, _extract_topk_untied, False)
        for h, s1, s2, ok in zip(heads, s1s, s2s, oks):
            tied = jnp.max(jnp.where(ok, 0.0, 1.0)) > 0.0

            @pl.when(tied)
            def _(h=h, s1=s1, s2=s2):
                route_heads([h], [s1], [s2], _extract_topk, True)


def _route(pqt, k1, k2, tn):
    nq, t = pqt.shape
    hk = (PEER_HEADS, PEER_NKEYS, PEER_HALF)
    out = lambda dt: jax.ShapeDtypeStruct((PEER_HEADS, PEER_NKEYS, t), dt)
    ospec = pl.BlockSpec((PEER_HEADS, PEER_NKEYS, tn), lambda i: (0, 0, i))
    return pl.pallas_call(
        _route_body,
        grid=(t // tn,),
        in_specs=[
            pl.BlockSpec((nq, tn), lambda i: (0, i)),
            pl.BlockSpec(hk, lambda i: (0, 0, 0)),
            pl.BlockSpec(hk, lambda i: (0, 0, 0)),
        ],
        out_specs=[ospec, ospec, ospec, ospec],
        out_shape=[out(BF16), out(F32), out(F32), out(BF16)],
        compiler_params=_cparams(("parallel",)),
        name="peer_route",
    )(pqt, k1, k2)


PEER_I1_PER_BLOCK = 8
BF16_ROWS = 16


def _expert_body(hnt_ref, u_first_ref, u_b_ref, u_next_ref, vt_prev_ref, vt_a_ref, vt_last_ref, rank2_ref,
                 cnt_ref, e1_ref, e2z_ref, x1_ref, o_ref, acc_scr, act_next, p_prev):
    s = pl.program_id(1)
    hnt = hnt_ref[...]
    tn = hnt.shape[1]
    tiles = PEER_NKEYS // BF16_ROWS

    def activation(u_ref):
        pre = _dot(u_ref[...], hnt)
        return (pre * (1.0 + lax.erf(pre * (2.0 ** -0.5)))).astype(BF16)

    def row(ref, h, l):
        return jnp.broadcast_to(ref[h, l:l + 1, :], (BF16_ROWS, tn)).astype(BF16)[None]

    def gates(l0, act_rows):
        ps = []
        for li in range(PEER_I1_PER_BLOCK):
            gate = None
            for h in range(PEER_HEADS):
                rank2 = rank2_ref[h].reshape(tiles, BF16_ROWS, tn)
                e2z = e2z_ref[h].reshape(tiles, BF16_ROWS, tn)
                term = jnp.where(rank2 < row(cnt_ref, h, l0 + li), e2z * row(e1_ref, h, l0 + li), 0.0)
                gate = term if gate is None else gate + term
            ps.append(gate.reshape(PEER_NKEYS, tn) * act_rows(li))
        return jnp.concatenate(ps, axis=0)

    @pl.when(s == 0)
    def _():
        acc_scr[...] = jnp.zeros_like(acc_scr)
        p_prev[...] = jnp.zeros_like(p_prev)
        act_next[...] = activation(u_first_ref)

    out_prev = _dot(vt_prev_ref[...], p_prev[...])
    act_b = activation(u_b_ref)
    p_a = gates(0, lambda li: act_next[pl.ds(li * PEER_NKEYS, PEER_NKEYS), :])
    out_a = _dot(vt_a_ref[...], p_a)
    act_next[...] = activation(u_next_ref)
    p_prev[...] = gates(PEER_I1_PER_BLOCK, lambda li: act_b[li * PEER_NKEYS:(li + 1) * PEER_NKEYS, :])
    acc_scr[...] += out_prev + out_a

    @pl.when(s == pl.num_programs(1) - 1)
    def _():
        o_ref[...] = x1_ref[...] + (acc_scr[...] + _dot(vt_last_ref[...], p_prev[...])).T


def _experts(hnt, u_b, vt_b, rank2, cnt, e1, e2z, x1, tn):
    d, t = hnt.shape
    ne = u_b.shape[0]
    eb = PEER_I1_PER_BLOCK * PEER_NKEYS
    nblk = ne // eb
    once = pl.Buffered(1)
    full = pl.BlockSpec((PEER_HEADS, PEER_NKEYS, tn), lambda i, s: (0, 0, i))
    part = pl.BlockSpec((PEER_HEADS, 2 * PEER_I1_PER_BLOCK, tn), lambda i, s: (0, s, i))
    return pl.pallas_call(
        _expert_body,
        grid=(t // tn, nblk // 2),
        in_specs=[
            pl.BlockSpec((d, tn), lambda i, s: (0, i)),
            pl.BlockSpec((eb, d), lambda i, s: (0, 0), pipeline_mode=once),
            pl.BlockSpec((eb, d), lambda i, s: (2 * s + 1, 0)),
            pl.BlockSpec((eb, d), lambda i, s: (jnp.minimum(2 * s + 2, nblk - 1), 0)),
            pl.BlockSpec((d, eb), lambda i, s: (0, jnp.maximum(2 * s - 1, 0))),
            pl.BlockSpec((d, eb), lambda i, s: (0, 2 * s)),
            pl.BlockSpec((d, eb), lambda i, s: (0, nblk - 1), pipeline_mode=once),
            full, part, part, full,
            pl.BlockSpec((tn, d), lambda i, s: (i, 0), pipeline_mode=once),
        ],
        out_specs=pl.BlockSpec((tn, d), lambda i, s: (i, 0)),
        out_shape=jax.ShapeDtypeStruct((t, d), F32),
        scratch_shapes=[pltpu.VMEM((d, tn), F32), pltpu.VMEM((eb, tn), BF16), pltpu.VMEM((eb, tn), BF16)],
        compiler_params=_cparams(("parallel", "arbitrary")),
        name="peer_experts",
    )(hnt, u_b, u_b, u_b, vt_b, vt_b, vt_b, rank2, cnt, e1, e2z, x1)


def _suffix_sum_matrix(tk):
    r = jnp.arange(2 * tk)[:, None] % tk
    c = jnp.arange(2 * tk)[None, :]
    return jnp.where(c < tk, r > c, True).astype(BF16)


class _Tiles(NamedTuple):
    proj_tokens: int
    sb_q: int
    sb_k: int
    gdn_seq: int
    route_tokens: int
    expert_tokens: int


def _tiles(t, s):
    sb_k = 128
    return _Tiles(min(512, t), min(2 * sb_k, s), sb_k, min(512, s), min(256, t), min(512, t))


def _layer(x, norm1_g, w_in, sb_q_g, sb_k_g, sb_o_g, conv_w, a_log, dt_bias, gdn_o_g, w_out, norm2_g,
           w_q, keys1, keys2, u_tab, v_tab):
    b, s, d = x.shape
    t = b * s
    tiles = _tiles(t, s)
    n_main = 3 * SB_WIDTH + 4 * GDN_WIDTH
    x2 = x.reshape(t, d)
    w_main = w_in[:, :n_main].astype(BF16)
    w_ab = jnp.pad(w_in[:, n_main:], ((0, 0), (0, LANES - 2 * GDN_HEADS))).astype(BF16)
    proj, ab = _inproj(x2, norm1_g.reshape(1, d), w_main, w_ab, tiles.proj_tokens)
    proj3 = proj.reshape(b, s, n_main)
    ab3 = ab.reshape(b, s, LANES)

    tile2 = lambda g: jnp.tile(g, 2).reshape(1, LANES)
    sb_o = _sb_attention(proj3, tile2(sb_q_g), tile2(sb_k_g), tile2(sb_o_g), _suffix_sum_matrix(tiles.sb_k),
                         tiles.sb_q, tiles.sb_k)

    on_lanes = lambda p: jnp.pad(p, (0, LANES - GDN_HEADS)).reshape(1, LANES)
    gd_o = _gdn(proj3, ab3, conv_w, on_lanes(a_log), on_lanes(dt_bias), gdn_o_g.reshape(1, LANES), tiles.gdn_seq)

    wo = w_out.astype(BF16)
    x1, hnt, pqt = _outproj(sb_o.reshape(t, SB_WIDTH), gd_o.reshape(t, GDN_WIDTH), x2, wo[:SB_WIDTH],
                            wo[SB_WIDTH:], norm2_g.reshape(1, d), w_q.T.astype(BF16), tiles.proj_tokens)

    rank2, cnt, e1, e2z = _route(pqt, keys1.astype(BF16), keys2.astype(BF16), tiles.route_tokens)
    y = _experts(hnt, u_tab.astype(BF16), v_tab.T.astype(BF16), rank2, cnt, e1, e2z, x1, tiles.expert_tokens)
    return y.reshape(b, s, d)


def kernel(x, norm1_g, w_in, sb_q_norm_g, sb_k_norm_g, sb_out_norm_g, gdn_conv_w, gdn_a_log, gdn_dt_bias,
           gdn_out_norm_g, w_out, norm2_g, peer_w_q, peer_keys1, peer_keys2, peer_u, peer_v):
    for layer in range(norm1_g.shape[0]):
        x = _layer(x, norm1_g[layer], w_in[layer], sb_q_norm_g[layer], sb_k_norm_g[layer],
                   sb_out_norm_g[layer], gdn_conv_w[layer], gdn_a_log[layer], gdn_dt_bias[layer],
                   gdn_out_norm_g[layer], w_out[layer], norm2_g[layer], peer_w_q[layer],
                   peer_keys1[layer], peer_keys2[layer], peer_u[layer], peer_v[layer])
    return x
```
